```python
import math
import jax
import jax.numpy as jnp
from jax import lax
import numpy as np

D_MODEL = 1024
BATCH = 8
SEQ = 2048
DEPTH = 4
DEC_BATCH = 16
DEC_SEQ = 2048
PAST_LEN = 128

N_MIXERS = 3
N_A_LAYERS = (DEPTH + 2) // 3
N_B_LAYERS = (DEPTH + 1) // 3
N_C_LAYERS = DEPTH // 3
ADA_CHUNKS = 6
NORM_EPS = 1e-6

RWKV_HEAD = 64
RWKV_HEADS = D_MODEL // RWKV_HEAD
DECAY_LORA = 64
AAA_LORA = 64
MV_LORA = 32
GATE_LORA = 128
RWKV_GN_EPS = 64e-5

SGU_CHUNK = 128
SGU_DIM = 2 * D_MODEL
SGU_GROUPS = 8
SGU_GC = SGU_DIM // SGU_GROUPS

DIFF_HEADS = 8
DIFF_HEAD = D_MODEL // (2 * DIFF_HEADS)
Q_BLOCK = 128
NUM_BUCKETS = 32
MAX_DISTANCE = 128

MOE_GROUPS = 4
EXPERTS_PER_GROUP = 8
N_EXPERTS = MOE_GROUPS * EXPERTS_PER_GROUP
MOE_TOP_K = 2
EXPERT_DIM = D_MODEL // 2
EXPERT_BLOCK = 128

kernel_name = 'hybrid_rwkv7_gmlp_diffattn_hmoe_encoder'


def rmsnorm(x, g, eps=NORM_EPS):
    xf = x.astype(jnp.float32)
    y = xf * lax.rsqrt(jnp.mean(xf * xf, axis=-1, keepdims=True) + eps)
    return (y * g.astype(jnp.float32)).astype(x.dtype)


def layernorm(x, g, b, eps=1e-5):
    xf = x.astype(jnp.float32)
    xc = xf - jnp.mean(xf, axis=-1, keepdims=True)
    y = xc * lax.rsqrt(jnp.mean(xc * xc, axis=-1, keepdims=True) + eps)
    return (y * g.astype(jnp.float32) + b.astype(jnp.float32)).astype(x.dtype)


def wkv7_scan(r, w, k, v, kk, a, reverse):
    B, S, H, N = r.shape

    def step(state, inp):
        r_t, w_t, k_t, v_t, kk_t, a_t = inp
        sa = jnp.einsum('bhvk,bhk->bhv', state, -kk_t)
        state = (state * w_t[:, :, None, :]
                 + sa[..., None] * (kk_t * a_t)[:, :, None, :]
                 + v_t[..., None] * k_t[:, :, None, :])
        return state, jnp.einsum('bhvk,bhk->bhv', state, r_t)

    xs = tuple(jnp.moveaxis(t, 1, 0) for t in (r, w, k, v, kk, a))
    s0 = jnp.zeros((B, H, N, N), jnp.float32)
    _, y = lax.scan(step, s0, xs, reverse=reverse)
    return jnp.moveaxis(y, 0, 1)


def rwkv7_mix(x, v_first, mu, wr, wk, wv, wo, w0, w1, w2, a0, a1, a2,
              g1, g2, k_k, k_a, r_k, ln_w, ln_b, vres):
    B, S, D = x.shape
    H, N = RWKV_HEADS, RWKV_HEAD
    pad = jnp.zeros_like(x[:, :1])
    xx = 0.5 * (jnp.concatenate([pad, x[:, :-1]], 1) + jnp.concatenate([x[:, 1:], pad], 1)) - x
    xr, xw, xk, xv, xa, xg = [x + xx * mu[n] for n in range(6)]
    r = xr @ wr
    k = xk @ wk
    v = xv @ wv
    if vres is None:
        v_first = v
    else:
        v0, v1, v2 = vres
        v = v + (v_first - v) * jax.nn.sigmoid(v0 + (xv @ v1) @ v2)
    g = jax.nn.sigmoid(xg @ g1) @ g2

    def hd(t):
        return t.reshape(B, S, H, N).astype(jnp.float32)

    kk = hd(k * k_k)
    kk = kk / jnp.maximum(jnp.sqrt(jnp.sum(kk * kk, axis=-1, keepdims=True)), 1e-12)
    rf, vf, kf = hd(r), hd(v), hd(k)
    k_a_h = k_a.reshape(H, N).astype(jnp.float32)
    y = jnp.zeros_like(rf)
    k_bonus = jnp.zeros_like(rf)
    for d in range(2):
        w_raw = -jax.nn.softplus(-(w0[d] + jnp.tanh(xw @ w1[d]) @ w2[d])) - 0.5
        decay = jnp.exp(-jnp.exp(hd(w_raw)))
        a = hd(jax.nn.sigmoid(a0[d] + (xa @ a1[d]) @ a2[d]))
        kd = kf * (1.0 + (a - 1.0) * k_a_h)
        y = y + wkv7_scan(rf, decay, kd, vf, kk, a, reverse=(d == 1))
        k_bonus = k_bonus + 0.5 * kd
    yc = y - jnp.mean(y, axis=-1, keepdims=True)
    yn = yc * lax.rsqrt(jnp.mean(yc * yc, axis=-1, keepdims=True) + RWKV_GN_EPS)
    yn = yn.reshape(B, S, D) * ln_w.astype(jnp.float32) + ln_b.astype(jnp.float32)
    bonus = jnp.sum(rf * k_bonus * r_k.astype(jnp.float32), axis=-1, keepdims=True) * vf
    out = (yn + bonus.reshape(B, S, D)).astype(x.dtype) * g
    return out @ wo, v_first


def sgu_mix(x, w_in, b_in, n_g, n_b, w_s, b_s, w_out, b_out):
    B, S, D = x.shape
    z = jax.nn.gelu(x @ w_in + b_in, approximate=False)
    u, v = jnp.split(z, 2, axis=-1)
    v = layernorm(v, n_g, n_b)
    nc = S // SGU_CHUNK
    v = v.reshape(B, nc, SGU_CHUNK, SGU_GROUPS, SGU_GC)
    s = jnp.einsum('gpq,bnqgc->bnpgc', w_s, v) + jnp.transpose(b_s)[:, :, None]
    s = s.reshape(B, S, SGU_DIM)
    return (u * s) @ w_out + b_out


def t5_bucket(rel):
    nb = NUM_BUCKETS // 2
    max_exact = nb // 2
    ret = jnp.where(rel > 0, nb, 0)
    n = jnp.abs(rel)
    nf = jnp.maximum(n, 1).astype(jnp.float32)
    large = max_exact + (jnp.log(nf / max_exact) / math.log(MAX_DISTANCE / max_exact)
                         * (nb - max_exact)).astype(jnp.int32)
    large = jnp.minimum(large, nb - 1)
    return ret + jnp.where(n < max_exact, n, large)


def diff_attn(x, w_qkv, w_o, lq1, lk1, lq2, lk2, subln_g, rel_bias, lambda_init):
    B, S, D = x.shape
    q, k, v = jnp.split(x @ w_qkv, 3, axis=-1)
    q = q.reshape(B, S, DIFF_HEADS, 2, DIFF_HEAD) * (DIFF_HEAD ** -0.5)
    k = k.reshape(B, S, DIFF_HEADS, 2, DIFF_HEAD)
    v = v.reshape(B, S, DIFF_HEADS, 2 * DIFF_HEAD)
    lam = (jnp.exp(jnp.sum(lq1.astype(jnp.float32) * lk1.astype(jnp.float32)))
           - jnp.exp(jnp.sum(lq2.astype(jnp.float32) * lk2.astype(jnp.float32))) + lambda_init)
    nb = S // Q_BLOCK
    qb = jnp.moveaxis(q.reshape(B, nb, Q_BLOCK, DIFF_HEADS, 2, DIFF_HEAD), 1, 0)
    k_pos = jnp.arange(S, dtype=jnp.int32)

    def block(args):
        q_blk, start = args
        s = jnp.einsum('bqhcd,bkhcd->bhcqk', q_blk, k).astype(jnp.float32)
        q_pos = start + jnp.arange(Q_BLOCK, dtype=jnp.int32)
        bucket = t5_bucket(k_pos[None, :] - q_pos[:, None])
        bias = jnp.transpose(rel_bias[bucket], (2, 0, 1)).astype(jnp.float32)
        p = jax.nn.softmax(s + bias[None, :, None], axis=-1)
        attn = p[:, :, 0] - lam * p[:, :, 1]
        return jnp.einsum('bhqk,bkhe->bqhe', attn.astype(v.dtype), v)

    o = lax.map(block, (qb, jnp.arange(nb, dtype=jnp.int32) * Q_BLOCK))
    o = jnp.moveaxis(o, 0, 1).reshape(B, S, DIFF_HEADS, 2 * DIFF_HEAD)
    o = rmsnorm(o, subln_g, eps=1e-5) * (1.0 - lambda_init)
    return o.reshape(B, S, D) @ w_o


def moe_dispatch(xt, expert, gate, w_g, w_u, w_d):
    T, D = xt.shape
    A = T * MOE_TOP_K
    e = expert.reshape(A)
    tok = jnp.repeat(jnp.arange(T, dtype=jnp.int32), MOE_TOP_K)
    wgt = gate.reshape(A)
    order = jnp.argsort(e)
    e_sorted = e[order]
    counts = jnp.bincount(e, length=N_EXPERTS)
    padded = (counts + EXPERT_BLOCK - 1) // EXPERT_BLOCK * EXPERT_BLOCK
    start_unpad = jnp.cumsum(counts) - counts
    end_pad = jnp.cumsum(padded)
    start_pad = end_pad - padded
    dest = start_pad[e_sorted] + jnp.arange(A) - start_unpad[e_sorted]
    n_blocks = (A + N_EXPERTS * (EXPERT_BLOCK - 1) + EXPERT_BLOCK - 1) // EXPERT_BLOCK
    P = n_blocks * EXPERT_BLOCK
    row_tok = jnp.zeros((P,), jnp.int32).at[dest].set(tok[order])
    row_w = jnp.zeros((P,), wgt.dtype).at[dest].set(wgt[order])
    block_e = jnp.searchsorted(end_pad, jnp.arange(n_blocks) * EXPERT_BLOCK, side='right')
    block_e = jnp.minimum(block_e, N_EXPERTS - 1)
    xs = xt[row_tok].reshape(n_blocks, EXPERT_BLOCK, D)

    def run(args):
        xb, eid = args
        return (jax.nn.silu(xb @ w_g[eid]) * (xb @ w_u[eid])) @ w_d[eid]

    ys = lax.map(run, (xs, block_e)).reshape(P, D)
    return jnp.zeros_like(xt).at[row_tok].add(ys * row_w[:, None].astype(ys.dtype))


def hier_moe(h, w_rg, b_rg, w_re, b_re, w_g, w_u, w_d):
    B, S, D = h.shape
    T = B * S
    xt = h.reshape(T, D)
    rows = jnp.arange(T)
    g_logits = (xt @ w_rg + b_rg).astype(jnp.float32)
    grp = jnp.argmax(g_logits, axis=-1)
    p_grp = jax.nn.softmax(g_logits, axis=-1)[rows, grp]
    e_logits = (xt @ w_re + b_re).astype(jnp.float32).reshape(T, MOE_GROUPS, EXPERTS_PER_GROUP)
    e_in = e_logits[rows, grp]
    top_v, top_i = lax.top_k(e_in, MOE_TOP_K)
    gate = p_grp[:, None] * jax.nn.softmax(top_v, axis=-1)
    expert = grp[:, None].astype(jnp.int32) * EXPERTS_PER_GROUP + top_i.astype(jnp.int32)
    return moe_dispatch(xt, expert, gate, w_g, w_u, w_d).reshape(B, S, D)


def setup_inputs(seed: int = 0) -> dict:
    key = jax.random.key(seed)
    ks = iter(jax.random.split(key, 64))

    def nrm(shape, scale):
        return scale * jax.random.normal(next(ks), shape, jnp.float32)

    def unif(shape, lo, hi):
        return jax.random.uniform(next(ks), shape, jnp.float32, lo, hi)

    D = D_MODEL
    NA, NB, NC = N_A_LAYERS, N_B_LAYERS, N_C_LAYERS
    return {
        'x_prompt': nrm((BATCH, SEQ, D), 1.0),
        'x_sample': nrm((DEC_BATCH, DEC_SEQ, D), 1.0),
        'c_prompt': nrm((BATCH, D), 1.0),
        'c_sample': nrm((DEC_BATCH, D), 1.0),
        'ada_w': nrm((DEPTH, D, ADA_CHUNKS * D), 0.5 * D ** -0.5),
        'ada_b': nrm((DEPTH, ADA_CHUNKS * D), 0.02),
        'norm_g': 1.0 + nrm((DEPTH, 2, D), 0.02),
        'final_g': 1.0 + nrm((D,), 0.02),
        'rwkv_mu': unif((NA, 6, D), 0.0, 1.0),
        'rwkv_wr': nrm((NA, D, D), D ** -0.5),
        'rwkv_wk': nrm((NA, D, D), D ** -0.5),
        'rwkv_wv': nrm((NA, D, D), D ** -0.5),
        'rwkv_wo': nrm((NA, D, D), D ** -0.5),
        'rwkv_w0': unif((NA, 2, D), -4.0, 2.0),
        'rwkv_w1': nrm((NA, 2, D, DECAY_LORA), D ** -0.5),
        'rwkv_w2': nrm((NA, 2, DECAY_LORA, D), 0.5 * DECAY_LORA ** -0.5),
        'rwkv_a0': nrm((NA, 2, D), 0.1),
        'rwkv_a1': nrm((NA, 2, D, AAA_LORA), D ** -0.5),
        'rwkv_a2': nrm((NA, 2, AAA_LORA, D), 0.5 * AAA_LORA ** -0.5),
        'rwkv_v0': nrm((NA - 1, D), 0.1),
        'rwkv_v1': nrm((NA - 1, D, MV_LORA), D ** -0.5),
        'rwkv_v2': nrm((NA - 1, MV_LORA, D), MV_LORA ** -0.5),
        'rwkv_g1': nrm((NA, D, GATE_LORA), D ** -0.5),
        'rwkv_g2': nrm((NA, GATE_LORA, D), GATE_LORA ** -0.5),
        'rwkv_kk': 0.85 + nrm((NA, D), 0.02),
        'rwkv_ka': 1.0 + nrm((NA, D), 0.02),
        'rwkv_rk': nrm((NA, RWKV_HEADS, RWKV_HEAD), 0.1),
        'rwkv_lnw': 1.0 + nrm((NA, D), 0.02),
        'rwkv_lnb': nrm((NA, D), 0.02),
        'sgu_win': nrm((NB, D, 2 * SGU_DIM), D ** -0.5),
        'sgu_bin': nrm((NB, 2 * SGU_DIM), 0.02),
        'sgu_ng': 1.0 + nrm((NB, SGU_DIM), 0.02),
        'sgu_nb': nrm((NB, SGU_DIM), 0.02),
        'sgu_ws': nrm((NB, SGU_GROUPS, SGU_CHUNK, SGU_CHUNK), SGU_CHUNK ** -0.5),
        'sgu_bs': nrm((NB, SGU_GROUPS, SGU_CHUNK), 0.02),
        'sgu_wout': nrm((NB, SGU_DIM, D), SGU_DIM ** -0.5),
        'sgu_bout': nrm((NB, D), 0.02),
        'diff_wqkv': nrm((NC, D, 3 * D), D ** -0.5),
        'diff_wo': nrm((NC, D, D), D ** -0.5),
        'diff_lq1': nrm((NC, DIFF_HEAD), 0.1),
        'diff_lk1': nrm((NC, DIFF_HEAD), 0.1),
        'diff_lq2': nrm((NC, DIFF_HEAD), 0.1),
        'diff_lk2': nrm((NC, DIFF_HEAD), 0.1),
        'diff_subln': 1.0 + nrm((NC, 2 * DIFF_HEAD), 0.02),
        'rel_bias': nrm((NUM_BUCKETS, DIFF_HEADS), 0.5),
        'moe_wrg': nrm((DEPTH, D, MOE_GROUPS), D ** -0.5),
        'moe_brg': nrm((DEPTH, MOE_GROUPS), 0.01),
        'moe_wre': nrm((DEPTH, D, N_EXPERTS), D ** -0.5),
        'moe_bre': nrm((DEPTH, N_EXPERTS), 0.01),
        'moe_wg': nrm((DEPTH, N_EXPERTS, D, EXPERT_DIM), D ** -0.5),
        'moe_wu': nrm((DEPTH, N_EXPERTS, D, EXPERT_DIM), D ** -0.5),
        'moe_wd': nrm((DEPTH, N_EXPERTS, EXPERT_DIM, D), EXPERT_DIM ** -0.5),
    }


def reference(x_prompt, x_sample, c_prompt, c_sample, ada_w, ada_b, norm_g, final_g,
              rwkv_mu, rwkv_wr, rwkv_wk, rwkv_wv, rwkv_wo, rwkv_w0, rwkv_w1, rwkv_w2,
              rwkv_a0, rwkv_a1, rwkv_a2, rwkv_v0, rwkv_v1, rwkv_v2, rwkv_g1, rwkv_g2,
              rwkv_kk, rwkv_ka, rwkv_rk, rwkv_lnw, rwkv_lnb,
              sgu_win, sgu_bin, sgu_ng, sgu_nb, sgu_ws, sgu_bs, sgu_wout, sgu_bout,
              diff_wqkv, diff_wo, diff_lq1, diff_lk1, diff_lq2, diff_lk2, diff_subln, rel_bias,
              moe_wrg, moe_brg, moe_wre, moe_bre, moe_wg, moe_wu, moe_wd):

    def trunk(x, c):
        v_first = None
        for i in range(DEPTH):
            j = i // N_MIXERS
            mod = jax.nn.silu(c) @ ada_w[i] + ada_b[i]
            sh1, sc1, gt1, sh2, sc2, gt2 = jnp.split(mod[:, None, :], ADA_CHUNKS, axis=-1)
            h = rmsnorm(x, norm_g[i, 0]) * (1.0 + sc1) + sh1
            if i % N_MIXERS == 0:
                vres = None if j == 0 else (rwkv_v0[j - 1], rwkv_v1[j - 1], rwkv_v2[j - 1])
                out, v_first = rwkv7_mix(h, v_first, rwkv_mu[j], rwkv_wr[j], rwkv_wk[j], rwkv_wv[j],
                                         rwkv_wo[j], rwkv_w0[j], rwkv_w1[j], rwkv_w2[j],
                                         rwkv_a0[j], rwkv_a1[j], rwkv_a2[j], rwkv_g1[j], rwkv_g2[j],
                                         rwkv_kk[j], rwkv_ka[j], rwkv_rk[j], rwkv_lnw[j], rwkv_lnb[j],
                                         vres)
            elif i % N_MIXERS == 1:
                out = sgu_mix(h, sgu_win[j], sgu_bin[j], sgu_ng[j], sgu_nb[j], sgu_ws[j], sgu_bs[j],
                              sgu_wout[j], sgu_bout[j])
            else:
                lambda_init = 0.8 - 0.6 * math.exp(-0.3 * i)
                out = diff_attn(h, diff_wqkv[j], diff_wo[j], diff_lq1[j], diff_lk1[j], diff_lq2[j],
                                diff_lk2[j], diff_subln[j], rel_bias, lambda_init)
            x = x + gt1 * out
            h = rmsnorm(x, norm_g[i, 1]) * (1.0 + sc2) + sh2
            x = x + gt2 * hier_moe(h, moe_wrg[i], moe_brg[i], moe_wre[i], moe_bre[i],
                                   moe_wg[i], moe_wu[i], moe_wd[i])
        return rmsnorm(x, final_g)

    y_prompt = trunk(x_prompt, c_prompt)
    y_sample = trunk(x_sample, c_sample)
    return (y_prompt, y_sample)
```

```python
import functools
import math

import numpy as np
import jax
import jax.numpy as jnp
from jax import lax
from jax.experimental import pallas as pl
from jax.experimental.pallas import tpu as pltpu

F32 = jnp.float32
BF16 = jnp.bfloat16
HIGHEST = lax.Precision.HIGHEST

D_MODEL = 1024
DEPTH = 4
N_MIXERS = 3
NORM_EPS = 1e-6
LANES = 128
VMEM_LIMIT_BYTES = 56 * 1024 * 1024

RWKV_HEAD = 64
RWKV_HEADS = D_MODEL // RWKV_HEAD
RWKV_PAIRS = D_MODEL // LANES
RWKV_GN_EPS = 64e-5
WKV_CHUNK = 64

SGU_CHUNK = 128
SGU_DIM = 2 * D_MODEL
SGU_GROUPS = 8
SGU_GC = SGU_DIM // SGU_GROUPS

DIFF_HEADS = 8
DIFF_HEAD = 64
NUM_BUCKETS = 32
MAX_DISTANCE = 128
ATTN_TQ = 256

MOE_GROUPS = 4
EXPERTS_PER_GROUP = 8
N_EXPERTS = MOE_GROUPS * EXPERTS_PER_GROUP
MOE_TOP_K = 2
EXPERT_DIM = D_MODEL // 2
MOE_BLOCK = 256
ROUTER_COLS = LANES

TOKEN_TILE = 256


def _params(*semantics):
    return pltpu.CompilerParams(dimension_semantics=semantics, vmem_limit_bytes=VMEM_LIMIT_BYTES)


def _resident(shape):
    zeros = (0,) * len(shape)
    return pl.BlockSpec(shape, lambda *_: zeros, pipeline_mode=pl.Buffered(1))


def _dot(a, b, precision=None):
    return jnp.dot(a, b, preferred_element_type=F32, precision=precision)


def _dot_nt(a, b, precision=None):
    return lax.dot_general(a, b, (((1,), (1,)), ((), ())), preferred_element_type=F32,
                           precision=precision)


def _dot_tn(a, b, precision=None):
    return lax.dot_general(a, b, (((0,), (0,)), ((), ())), preferred_element_type=F32,
                           precision=precision)


def _sigmoid(x):
    return 1.0 / (1.0 + jnp.exp(-x))


def _norm_mod(x, g, scale, shift, eps=NORM_EPS):
    y = x * lax.rsqrt(jnp.mean(x * x, axis=-1, keepdims=True) + eps)
    return (y * g) * (1.0 + scale) + shift


def _mod_kernel(c_ref, w_ref, b_ref, o_ref):
    c = c_ref[...]
    o_ref[...] = _dot(c * _sigmoid(c), w_ref[...], HIGHEST) + b_ref[...]


def _modulation(c, ada_w, ada_b):
    nb = c.shape[0]
    depth, d, n = ada_w.shape
    out = pl.pallas_call(
        _mod_kernel,
        grid=(depth, n // d),
        in_specs=[pl.BlockSpec((nb, d), lambda l, j: (0, 0)),
                  pl.BlockSpec((None, d, d), lambda l, j: (l, 0, j)),
                  pl.BlockSpec((None, 1, d), lambda l, j: (l, 0, j))],
        out_specs=pl.BlockSpec((None, nb, d), lambda l, j: (l, 0, j)),
        out_shape=jax.ShapeDtypeStruct((depth, nb, n), F32),
        compiler_params=_params("parallel", "parallel"),
        name="adaln_modulation",
    )(c, ada_w, ada_b.reshape(depth, 1, n))
    return out.reshape(depth, nb, n // d, d)


def _rwkv_pre_kernel(*refs, tm, tiles_per_seq, has_vres):
    if has_vres:
        (x_ref, xp_ref, xn_ref, mod_ref, ng_ref, mu_ref, wr_ref, wk_ref, wv_ref, wl1_ref, wl2_ref,
         al1_ref, al2_ref, g1_ref, g2_ref, w0_ref, a0_ref, vf_ref, v0_ref, v1_ref, v2_ref,
         r_ref, k_ref, v_ref, g_ref, wlf_ref, wlb_ref, af_ref, ab_ref) = refs
    else:
        (x_ref, xp_ref, xn_ref, mod_ref, ng_ref, mu_ref, wr_ref, wk_ref, wv_ref, wl1_ref, wl2_ref,
         al1_ref, al2_ref, g1_ref, g2_ref, w0_ref, a0_ref,
         r_ref, k_ref, v_ref, g_ref, wlf_ref, wlb_ref, af_ref, ab_ref) = refs
    d = D_MODEL
    i = pl.program_id(0)
    pos = i % tiles_per_seq
    shift, scale = mod_ref[0:1, :], mod_ref[1:2, :]
    ng = ng_ref[...]
    h = _norm_mod(x_ref[...], ng, scale, shift)
    h_prev = jnp.where(pos == 0, 0.0, _norm_mod(xp_ref[7:8, :], ng, scale, shift))
    h_next = jnp.where(pos == tiles_per_seq - 1, 0.0, _norm_mod(xn_ref[0:1, :], ng, scale, shift))
    row = lax.broadcasted_iota(jnp.int32, (tm, 1), 0)
    h_up = jnp.where(row == 0, h_prev, pltpu.roll(h, 1, 0))
    h_dn = jnp.where(row == tm - 1, h_next, pltpu.roll(h, tm - 1, 0))
    xx = 0.5 * (h_up + h_dn) - h

    def mix(n):
        return (h + xx * mu_ref[n:n + 1, :]).astype(BF16)

    xr, xw, xk, xv, xa, xg = [mix(n) for n in range(6)]
    r_ref[...] = _dot(xr, wr_ref[...])
    k_ref[...] = _dot(xk, wk_ref[...])
    v = _dot(xv, wv_ref[...])
    if has_vres:
        lv = _dot(_dot(xv, v1_ref[...]).astype(BF16), v2_ref[...])
        v = v + (vf_ref[...] - v) * _sigmoid(v0_ref[...] + lv)
    v_ref[...] = v
    g_ref[...] = _dot(_sigmoid(_dot(xg, g1_ref[...])).astype(BF16), g2_ref[...])
    wl = w0_ref[...] + _dot(jnp.tanh(_dot(xw, wl1_ref[...])).astype(BF16), wl2_ref[...])
    w_raw = jnp.minimum(wl, 0.0) - jnp.log1p(jnp.exp(-jnp.abs(wl))) - 0.5
    logw = -jnp.exp(w_raw)
    wlf_ref[...] = logw[:, :d]
    wlb_ref[...] = logw[:, d:]
    a = _sigmoid(a0_ref[...] + _dot(_dot(xa, al1_ref[...]).astype(BF16), al2_ref[...]))
    af_ref[...] = a[:, :d]
    ab_ref[...] = a[:, d:]


def _block_diag2(m0, m1):
    z = jnp.zeros_like(m0)
    return jnp.concatenate([jnp.concatenate([m0, z], 1), jnp.concatenate([z, m1], 1)], 0)


def _rwkv_pre(x, mod_l, norm_g, p, v_first, seq_len):
    t, d = x.shape
    tm = TOKEN_TILE
    tiles_per_seq = seq_len // tm
    n_tiles = t // tm
    has_vres = v_first is not None
    tile = pl.BlockSpec((tm, d), lambda i: (i, 0))
    rows8 = tm // 8
    last8 = t // 8 - 1
    in_specs = [
        tile,
        pl.BlockSpec((8, d), lambda i: (jnp.maximum(i * rows8 - 1, 0), 0)),
        pl.BlockSpec((8, d), lambda i: (jnp.minimum((i + 1) * rows8, last8), 0)),
        pl.BlockSpec((None, 6, d), lambda i: (i // tiles_per_seq, 0, 0)),
        _resident((1, d)), _resident((6, d)),
        _resident((d, d)), _resident((d, d)), _resident((d, d)),
        _resident((d, LANES)), _resident((LANES, 2 * d)),
        _resident((d, LANES)), _resident((LANES, 2 * d)),
        _resident((d, LANES)), _resident((LANES, d)),
        _resident((1, 2 * d)), _resident((1, 2 * d)),
    ]
    args = [x, x, x, mod_l, norm_g.reshape(1, d), p["mu"],
            p["wr"], p["wk"], p["wv"], p["wl1"], p["wl2"], p["al1"], p["al2"], p["g1"], p["g2"],
            p["w0"], p["a0"]]
    if has_vres:
        in_specs += [tile, _resident((1, d)), _resident((d, LANES)), _resident((LANES, d))]
        args += [v_first, p["v0"], p["v1"], p["v2"]]
    outs = pl.pallas_call(
        functools.partial(_rwkv_pre_kernel, tm=tm, tiles_per_seq=tiles_per_seq, has_vres=has_vres),
        grid=(n_tiles,),
        in_specs=in_specs,
        out_specs=[tile] * 8,
        out_shape=[jax.ShapeDtypeStruct((t, d), F32)] * 8,
        compiler_params=_params("parallel"),
        name="rwkv_pre",
    )(*args)
    return outs


def _wkv_kernel(r_ref, k_ref, v_ref, wlf_ref, wlb_ref, af_ref, ab_ref, kk_ref, ka_ref, y_ref,
                state_ref, *, seq_len, chunk):
    n_chunks = seq_len // chunk
    two = 2 * chunk
    lane = lax.broadcasted_iota(jnp.int32, (1, LANES), 1)
    head0 = lane < RWKV_HEAD
    ri = lax.broadcasted_iota(jnp.int32, (chunk, chunk), 0)
    ci = lax.broadcasted_iota(jnp.int32, (chunk, chunk), 1)
    si = lax.broadcasted_iota(jnp.int32, (two, two), 0)
    sj = lax.broadcasted_iota(jnp.int32, (two, two), 1)
    same_head = (si < chunk) == (sj < chunk)
    st, su = si & (chunk - 1), sj & (chunk - 1)
    eye = (si == sj).astype(F32)
    li = lax.broadcasted_iota(jnp.int32, (LANES, LANES), 0)
    lj = lax.broadcasted_iota(jnp.int32, (LANES, LANES), 1)
    head_ones = ((li < RWKV_HEAD) == (lj < RWKV_HEAD)).astype(F32)
    kscale = kk_ref[...]
    ka = ka_ref[...]

    def stack(x):
        return jnp.concatenate([jnp.where(head0, x, 0.0), jnp.where(head0, 0.0, x)], axis=0)

    def one_chunk(rows, wl_ref, a_ref, slot, reverse):
        r = r_ref[rows, :]
        k = k_ref[rows, :]
        v = v_ref[rows, :]
        lw = wl_ref[rows, :]
        a = a_ref[rows, :]
        kkr = k * kscale
        norm = jnp.sqrt(_dot(kkr * kkr, head_ones, HIGHEST))
        kk = kkr / jnp.maximum(norm, 1e-12)
        kd = k * (1.0 + (a - 1.0) * ka)
        if reverse:
            cum = (ri <= ci).astype(F32)
            strict = same_head & (st < su)
            incl = same_head & (st <= su)
        else:
            cum = (ri >= ci).astype(F32)
            strict = same_head & (st > su)
            incl = same_head & (st >= su)
        logp = _dot(cum, lw, HIGHEST)
        p_in = jnp.exp(logp)
        p_out = jnp.exp(-logp)
        xa = stack(-kk * jnp.exp(logp - lw)).astype(BF16)
        xr = stack(r * p_in).astype(BF16)
        yb = stack(kk * a * p_out).astype(BF16)
        yk = stack(kd * p_out).astype(BF16)
        vs = stack(v).astype(BF16)
        s0 = state_ref[slot]
        s0b = s0.astype(BF16)
        sc = _dot_nt(jnp.concatenate([xa, xr], 0), jnp.concatenate([yb, yk], 0))
        m_ab = jnp.where(strict, sc[:two, :two], 0.0)
        m_ak = jnp.where(strict, sc[:two, two:], 0.0)
        n_rb = jnp.where(incl, sc[two:, :two], 0.0)
        n_rk = jnp.where(incl, sc[two:, two:], 0.0)
        z = _dot_nt(xa, s0b) + _dot(m_ak.astype(BF16), vs)
        inv = eye + m_ab
        mp = m_ab
        for _ in range(int(math.log2(chunk)) - 1):
            mp = _dot(mp, mp, HIGHEST)
            inv = inv + _dot(mp, inv, HIGHEST)
        u = _dot(inv, z, HIGHEST)
        ub = u.astype(BF16)
        ys = _dot_nt(xr, s0b) + _dot(n_rb.astype(BF16), ub) + _dot(n_rk.astype(BF16), vs)
        y_ref[rows, :] += ys[:chunk] + ys[chunk:]
        total = jnp.sum(lw, axis=0, keepdims=True)
        state_ref[slot] = (s0 + _dot_tn(ub, yb) + _dot_tn(vs, yk)) * jnp.exp(total)

    y_ref[...] = jnp.zeros_like(y_ref)
    state_ref[...] = jnp.zeros_like(state_ref)

    def body(c, carry):
        rows_f = pl.ds(pl.multiple_of(c * chunk, chunk), chunk)
        rows_b = pl.ds(pl.multiple_of((n_chunks - 1 - c) * chunk, chunk), chunk)
        one_chunk(rows_f, wlf_ref, af_ref, 0, False)
        one_chunk(rows_b, wlb_ref, ab_ref, 1, True)
        return carry

    lax.fori_loop(0, n_chunks, body, 0)


def _wkv(r, k, v, wlf, wlb, af, ab, k_k, k_a, seq_len):
    t, d = r.shape
    nb = t // seq_len
    seq = pl.BlockSpec((seq_len, LANES), lambda b, p: (b, p))
    par = pl.BlockSpec((1, LANES), lambda b, p: (0, p))
    return pl.pallas_call(
        functools.partial(_wkv_kernel, seq_len=seq_len, chunk=WKV_CHUNK),
        grid=(nb, d // LANES),
        in_specs=[seq] * 7 + [par, par],
        out_specs=seq,
        out_shape=jax.ShapeDtypeStruct((t, d), F32),
        scratch_shapes=[pltpu.VMEM((2, LANES, LANES), F32)],
        compiler_params=_params("parallel", "parallel"),
        name="wkv7_chunked",
    )(r, k, v, wlf, wlb, af, ab, k_k.reshape(1, d), k_a.reshape(1, d))


def _rwkv_mid_kernel(y_ref, r_ref, k_ref, v_ref, af_ref, ab_ref, g_ref, ka_ref, rk_ref, lnw_ref,
                     lnb_ref, hsum_ref, hexp_ref, o_ref):
    hsum = hsum_ref[...]
    hexp = hexp_ref[...]

    def head_sum(x):
        return _dot(_dot(x, hsum, HIGHEST), hexp, HIGHEST)

    y = y_ref[...]
    yc = y - head_sum(y) * (1.0 / RWKV_HEAD)
    var = head_sum(yc * yc) * (1.0 / RWKV_HEAD)
    yn = yc * lax.rsqrt(var + RWKV_GN_EPS) * lnw_ref[...] + lnb_ref[...]
    k_bonus = k_ref[...] * (1.0 + (0.5 * (af_ref[...] + ab_ref[...]) - 1.0) * ka_ref[...])
    bonus = head_sum(r_ref[...] * k_bonus * rk_ref[...]) * v_ref[...]
    o_ref[...] = ((yn + bonus) * g_ref[...]).astype(BF16)


def _rwkv_mid(y, r, k, v, af, ab, g, p):
    t, d = y.shape
    tm = TOKEN_TILE
    tile = pl.BlockSpec((tm, d), lambda i: (i, 0))
    vec = _resident((1, d))
    head_of_lane = np.arange(d) // RWKV_HEAD
    hsum = jnp.asarray(head_of_lane[:, None] == np.arange(RWKV_HEADS)[None, :], F32)
    return pl.pallas_call(
        _rwkv_mid_kernel,
        grid=(t // tm,),
        in_specs=[tile] * 7 + [vec] * 4 + [_resident((d, RWKV_HEADS)), _resident((RWKV_HEADS, d))],
        out_specs=tile,
        out_shape=jax.ShapeDtypeStruct((t, d), BF16),
        compiler_params=_params("parallel"),
        name="rwkv_mid",
    )(y, r, k, v, af, ab, g, p["ka"], p["rk"], p["lnw"], p["lnb"], hsum, hsum.T)


def _sgu_kernel(x_ref, mod_ref, ng_ref, win_ref, bin_ref, lg_ref, lb_ref, ws_ref, bs_ref, o_ref,
                *, tm):
    h = _norm_mod(x_ref[...], ng_ref[...], mod_ref[1:2, :], mod_ref[0:1, :]).astype(BF16)
    z = _dot(h, win_ref[...]) + bin_ref[...]
    z = 0.5 * z * (1.0 + lax.erf(z * (1.0 / math.sqrt(2.0))))
    u = z[:, :SGU_DIM]
    v = z[:, SGU_DIM:]
    vc = v - jnp.mean(v, axis=-1, keepdims=True)
    vn = vc * lax.rsqrt(jnp.mean(vc * vc, axis=-1, keepdims=True) + 1e-5) * lg_ref[...] + lb_ref[...]
    vb = vn.astype(BF16)
    for c in range(tm // SGU_CHUNK):
        rows = slice(c * SGU_CHUNK, (c + 1) * SGU_CHUNK)
        for g in range(SGU_GROUPS):
            cols = slice(g * SGU_GC, (g + 1) * SGU_GC)
            s = _dot(ws_ref[g], vb[rows, cols]) + bs_ref[:, cols]
            o_ref[rows, cols] = (u[rows, cols] * s).astype(BF16)


def _sgu(x, mod_l, norm_g, p, seq_len):
    t, d = x.shape
    tm = TOKEN_TILE
    tiles_per_seq = seq_len // tm
    return pl.pallas_call(
        functools.partial(_sgu_kernel, tm=tm),
        grid=(t // tm,),
        in_specs=[pl.BlockSpec((tm, d), lambda i: (i, 0)),
                  pl.BlockSpec((None, 6, d), lambda i: (i // tiles_per_seq, 0, 0)),
                  _resident((1, d)),
                  _resident((d, 2 * SGU_DIM)), _resident((1, 2 * SGU_DIM)),
                  _resident((1, SGU_DIM)), _resident((1, SGU_DIM)),
                  _resident((SGU_GROUPS, SGU_CHUNK, SGU_CHUNK)), _resident((SGU_CHUNK, SGU_DIM))],
        out_specs=pl.BlockSpec((tm, SGU_DIM), lambda i: (i, 0)),
        out_shape=jax.ShapeDtypeStruct((t, SGU_DIM), BF16),
        compiler_params=_params("parallel"),
        name="sgu",
    )(x, mod_l, norm_g.reshape(1, d), p["win"], p["bin"], p["ng"], p["nb"], p["ws"], p["bs"])


def _qkv_kernel(x_ref, mod_ref, ng_ref, w_ref, q_ref, k_ref, v_ref):
    d = D_MODEL
    h = _norm_mod(x_ref[...], ng_ref[...], mod_ref[1:2, :], mod_ref[0:1, :]).astype(BF16)
    qkv = _dot(h, w_ref[...])
    q_ref[...] = (qkv[:, :d] * (DIFF_HEAD ** -0.5)).astype(BF16)
    k_ref[...] = qkv[:, d:2 * d].astype(BF16)
    v_ref[...] = qkv[:, 2 * d:].astype(BF16)


def _qkv(x, mod_l, norm_g, w_qkv, seq_len):
    t, d = x.shape
    tm = TOKEN_TILE
    tiles_per_seq = seq_len // tm
    tile = pl.BlockSpec((tm, d), lambda i: (i, 0))
    return pl.pallas_call(
        _qkv_kernel,
        grid=(t // tm,),
        in_specs=[tile, pl.BlockSpec((None, 6, d), lambda i: (i // tiles_per_seq, 0, 0)),
                  _resident((1, d)), _resident((d, 3 * d))],
        out_specs=[tile] * 3,
        out_shape=[jax.ShapeDtypeStruct((t, d), BF16)] * 3,
        compiler_params=_params("parallel"),
        name="diff_qkv",
    )(x, mod_l, norm_g.reshape(1, d), w_qkv)


def _attn_kernel(q_ref, k_ref, v_ref, win_ref, lam_ref, sg_ref, o_ref, *, tq, seq_len, out_scale):
    q = q_ref[...]
    lane = lax.broadcasted_iota(jnp.int32, (1, LANES), 1)
    zero = jnp.zeros_like(q)
    qs = jnp.concatenate([jnp.where(lane < DIFF_HEAD, q, zero),
                          jnp.where(lane < DIFF_HEAD, zero, q)], axis=0)
    s = _dot_nt(qs, k_ref[...])
    width = seq_len + tq
    window = jnp.broadcast_to(win_ref[...], (tq, width))
    bias = pltpu.roll(window, width - tq + 1, 1, stride=1, stride_axis=0)[:, :seq_len]

    def softmax(x):
        e = jnp.exp(x - jnp.max(x, axis=-1, keepdims=True))
        return e / jnp.sum(e, axis=-1, keepdims=True)

    p1 = softmax(s[:tq] + bias)
    p2 = softmax(s[tq:] + bias)
    attn = (p1 - lam_ref[...] * p2).astype(BF16)
    o = _dot(attn, v_ref[...])
    o = o * lax.rsqrt(jnp.mean(o * o, axis=-1, keepdims=True) + 1e-5) * sg_ref[...]
    o_ref[...] = (o * out_scale).astype(BF16)


def _t5_bucket(rel):
    nb = NUM_BUCKETS // 2
    max_exact = nb // 2
    ret = jnp.where(rel > 0, nb, 0)
    n = jnp.abs(rel)
    nf = jnp.maximum(n, 1).astype(F32)
    large = max_exact + (jnp.log(nf / max_exact) / math.log(MAX_DISTANCE / max_exact)
                         * (nb - max_exact)).astype(jnp.int32)
    large = jnp.minimum(large, nb - 1)
    return ret + jnp.where(n < max_exact, n, large)


def _bias_windows(rel_bias, seq_len, tq):
    nqb = seq_len // tq
    j = jnp.arange(seq_len + tq, dtype=jnp.int32)[None, :]
    q_hi = (jnp.arange(nqb, dtype=jnp.int32)[:, None] + 1) * tq
    bucket = _t5_bucket(j - q_hi + 1)
    return jnp.transpose(rel_bias[bucket], (0, 2, 1))[:, :, None, :].astype(F32)


def _diff_attention(q, k, v, windows, lam, subln_g, lambda_init, seq_len):
    t, d = q.shape
    nb = t // seq_len
    tq = ATTN_TQ
    nqb = seq_len // tq
    kv = pl.BlockSpec((seq_len, LANES), lambda b, h, i: (b, h))
    qo = pl.BlockSpec((tq, LANES), lambda b, h, i: (b * nqb + i, h))
    return pl.pallas_call(
        functools.partial(_attn_kernel, tq=tq, seq_len=seq_len, out_scale=1.0 - lambda_init),
        grid=(nb, DIFF_HEADS, nqb),
        in_specs=[qo, kv, kv,
                  pl.BlockSpec((None, None, 1, seq_len + tq), lambda b, h, i: (i, h, 0, 0)),
                  pl.BlockSpec((1, 1), lambda b, h, i: (0, 0)),
                  pl.BlockSpec((1, LANES), lambda b, h, i: (0, 0))],
        out_specs=qo,
        out_shape=jax.ShapeDtypeStruct((t, d), BF16),
        compiler_params=_params("parallel", "parallel", "arbitrary"),
        name="diff_attention",
    )(q, k, v, windows, lam.reshape(1, 1), subln_g.reshape(1, LANES))


def _post_kernel(pre_ref, w_ref, b_ref, x_ref, mod_ref, ng_ref, wr_ref, br_ref,
                 x1_ref, h2_ref, lg_ref):
    out = _dot(pre_ref[...], w_ref[...]) + b_ref[...]
    x1 = x_ref[...] + mod_ref[2:3, :] * out
    x1_ref[...] = x1
    h2 = _norm_mod(x1, ng_ref[...], mod_ref[4:5, :], mod_ref[3:4, :])
    h2_ref[...] = h2.astype(BF16)
    lg_ref[...] = _dot(h2, wr_ref[...], HIGHEST) + br_ref[...]


def _post(pre, w, b, x, mod_l, norm_g2, w_router, b_router, seq_len):
    t, d = x.shape
    din = pre.shape[1]
    tm = TOKEN_TILE
    tiles_per_seq = seq_len // tm
    tile = pl.BlockSpec((tm, d), lambda i: (i, 0))
    return pl.pallas_call(
        _post_kernel,
        grid=(t // tm,),
        in_specs=[pl.BlockSpec((tm, din), lambda i: (i, 0)), _resident((din, d)), _resident((1, d)),
                  tile, pl.BlockSpec((None, 6, d), lambda i: (i // tiles_per_seq, 0, 0)),
                  _resident((1, d)), _resident((d, ROUTER_COLS)), _resident((1, ROUTER_COLS))],
        out_specs=[tile, tile, pl.BlockSpec((tm, ROUTER_COLS), lambda i: (i, 0))],
        out_shape=[jax.ShapeDtypeStruct((t, d), F32), jax.ShapeDtypeStruct((t, d), BF16),
                   jax.ShapeDtypeStruct((t, ROUTER_COLS), F32)],
        compiler_params=_params("parallel"),
        name="post_router",
    )(pre, w, b, x, mod_l, norm_g2.reshape(1, d), w_router, b_router)


def _expert_kernel(be_ref, nu_ref, x_ref, wg_ref, wu_ref, wd_ref, o_ref):
    i = pl.program_id(0)

    @pl.when(i < nu_ref[0])
    def _():
        x = x_ref[...]
        hg = _dot(x, wg_ref[...])
        hu = _dot(x, wu_ref[...])
        act = hg * _sigmoid(hg) * hu
        o_ref[...] = _dot(act.astype(BF16), wd_ref[...])

    @pl.when(i >= nu_ref[0])
    def _():
        o_ref[...] = jnp.zeros_like(o_ref)


def _experts(xs, block_e, n_used, wg, wu, wd):
    p_rows, d = xs.shape
    n_blocks = p_rows // MOE_BLOCK
    grid_spec = pltpu.PrefetchScalarGridSpec(
        num_scalar_prefetch=2,
        grid=(n_blocks,),
        in_specs=[pl.BlockSpec((MOE_BLOCK, d), lambda i, be, nu: (i, 0)),
                  pl.BlockSpec((None, d, EXPERT_DIM), lambda i, be, nu: (be[i], 0, 0)),
                  pl.BlockSpec((None, d, EXPERT_DIM), lambda i, be, nu: (be[i], 0, 0)),
                  pl.BlockSpec((None, EXPERT_DIM, d), lambda i, be, nu: (be[i], 0, 0))],
        out_specs=pl.BlockSpec((MOE_BLOCK, d), lambda i, be, nu: (i, 0)),
    )
    return pl.pallas_call(
        _expert_kernel,
        grid_spec=grid_spec,
        out_shape=jax.ShapeDtypeStruct((p_rows, d), F32),
        compiler_params=_params("arbitrary"),
        name="moe_experts",
    )(block_e, n_used, xs, wg, wu, wd)


def _combine_kernel(x_ref, y0_ref, y1_ref, gate_ref, mod_ref, o_ref):
    gate = gate_ref[...]
    moe = gate[:, 0:1] * y0_ref[...] + gate[:, 1:2] * y1_ref[...]
    o_ref[...] = x_ref[...] + mod_ref[5:6, :] * moe


def _combine(x1, y0, y1, gate, mod_l, seq_len):
    t, d = x1.shape
    tm = TOKEN_TILE
    tiles_per_seq = seq_len // tm
    tile = pl.BlockSpec((tm, d), lambda i: (i, 0))
    return pl.pallas_call(
        _combine_kernel,
        grid=(t // tm,),
        in_specs=[tile, tile, tile, pl.BlockSpec((tm, MOE_TOP_K), lambda i: (i, 0)),
                  pl.BlockSpec((None, 6, d), lambda i: (i // tiles_per_seq, 0, 0))],
        out_specs=tile,
        out_shape=jax.ShapeDtypeStruct((t, d), F32),
        compiler_params=_params("parallel"),
        name="moe_combine",
    )(x1, y0, y1, gate, mod_l)


def _route(logits):
    t = logits.shape[0]
    g_logits = logits[:, :MOE_GROUPS]
    e_logits = logits[:, MOE_GROUPS:MOE_GROUPS + N_EXPERTS].reshape(t, MOE_GROUPS, EXPERTS_PER_GROUP)
    grp = jnp.argmax(g_logits, axis=-1)
    p_grp = 1.0 / jnp.sum(jnp.exp(g_logits - jnp.max(g_logits, axis=-1, keepdims=True)), axis=-1)
    e_in = jnp.take_along_axis(e_logits, grp[:, None, None], axis=1)[:, 0]
    top_v, top_i = lax.top_k(e_in, MOE_TOP_K)
    gate = p_grp[:, None] * jax.nn.softmax(top_v, axis=-1)
    expert = grp[:, None].astype(jnp.int32) * EXPERTS_PER_GROUP + top_i.astype(jnp.int32)
    return expert, gate


def _dispatch_plan(expert):
    t = expert.shape[0]
    a = t * MOE_TOP_K
    e = expert.reshape(a)
    onehot = (e[:, None] == jnp.arange(N_EXPERTS, dtype=jnp.int32)[None, :]).astype(jnp.int32)
    csum = jnp.cumsum(onehot, axis=0)
    rank = jnp.take_along_axis(csum, e[:, None], axis=1)[:, 0] - 1
    counts = csum[-1]
    padded = (counts + MOE_BLOCK - 1) // MOE_BLOCK * MOE_BLOCK
    end_pad = jnp.cumsum(padded)
    start_pad = end_pad - padded
    dest = (start_pad[e] + rank).astype(jnp.int32)
    n_blocks = (a + N_EXPERTS * (MOE_BLOCK - 1) + MOE_BLOCK - 1) // MOE_BLOCK
    row_tok = jnp.zeros((n_blocks * MOE_BLOCK,), jnp.int32).at[dest].set(
        jnp.arange(a, dtype=jnp.int32) // MOE_TOP_K)
    block_e = jnp.searchsorted(end_pad, jnp.arange(n_blocks, dtype=jnp.int32) * MOE_BLOCK,
                               side="right")
    block_e = jnp.minimum(block_e, N_EXPERTS - 1).astype(jnp.int32)
    n_used = (end_pad[-1] // MOE_BLOCK).astype(jnp.int32).reshape(1)
    return dest.reshape(t, MOE_TOP_K), row_tok, block_e, n_used


def _moe(x1, h2, logits, mod_l, wg, wu, wd, seq_len):
    expert, gate = _route(logits)
    dest, row_tok, block_e, n_used = _dispatch_plan(expert)
    ys = _experts(h2[row_tok], block_e, n_used, wg, wu, wd)
    return _combine(x1, ys[dest[:, 0]], ys[dest[:, 1]], gate, mod_l, seq_len)


def _final_kernel(x_ref, g_ref, o_ref):
    x = x_ref[...]
    o_ref[...] = x * lax.rsqrt(jnp.mean(x * x, axis=-1, keepdims=True) + NORM_EPS) * g_ref[...]


def _final_norm(x, g):
    t, d = x.shape
    tm = TOKEN_TILE
    tile = pl.BlockSpec((tm, d), lambda i: (i, 0))
    return pl.pallas_call(
        _final_kernel,
        grid=(t // tm,),
        in_specs=[tile, _resident((1, d))],
        out_specs=tile,
        out_shape=jax.ShapeDtypeStruct((t, d), F32),
        compiler_params=_params("parallel"),
        name="final_norm",
    )(x, g.reshape(1, d))


def _pad_cols(w, n):
    return jnp.pad(w, ((0, 0), (0, n - w.shape[1])))


def _pad_rows(w, n):
    return jnp.pad(w, ((0, n - w.shape[0]), (0, 0)))


def _rwkv_params(j, mu, wr, wk, wv, w0, w1, w2, a0, a1, a2, v0, v1, v2, g1, g2, kk, ka, rk,
                 lnw, lnb):
    d = D_MODEL
    p = {
        "mu": mu[j],
        "wr": wr[j].astype(BF16), "wk": wk[j].astype(BF16), "wv": wv[j].astype(BF16),
        "wl1": jnp.concatenate([w1[j, 0], w1[j, 1]], axis=1).astype(BF16),
        "wl2": _block_diag2(w2[j, 0], w2[j, 1]).astype(BF16),
        "al1": jnp.concatenate([a1[j, 0], a1[j, 1]], axis=1).astype(BF16),
        "al2": _block_diag2(a2[j, 0], a2[j, 1]).astype(BF16),
        "g1": g1[j].astype(BF16), "g2": g2[j].astype(BF16),
        "w0": w0[j].reshape(1, 2 * d), "a0": a0[j].reshape(1, 2 * d),
        "kk": kk[j], "ka": ka[j].reshape(1, d), "rk": rk[j].reshape(1, d),
        "lnw": lnw[j].reshape(1, d), "lnb": lnb[j].reshape(1, d),
    }
    if j > 0:
        p["v0"] = v0[j - 1].reshape(1, d)
        p["v1"] = _pad_cols(v1[j - 1], LANES).astype(BF16)
        p["v2"] = _pad_rows(v2[j - 1], LANES).astype(BF16)
    return p


def kernel(x_prompt, x_sample, c_prompt, c_sample, ada_w, ada_b, norm_g, final_g, rwkv_mu, rwkv_wr, rwkv_wk, rwkv_wv, rwkv_wo, rwkv_w0, rwkv_w1, rwkv_w2, rwkv_a0, rwkv_a1, rwkv_a2, rwkv_v0, rwkv_v1, rwkv_v2, rwkv_g1, rwkv_g2, rwkv_kk, rwkv_ka, rwkv_rk, rwkv_lnw, rwkv_lnb, sgu_win, sgu_bin, sgu_ng, sgu_nb, sgu_ws, sgu_bs, sgu_wout, sgu_bout, diff_wqkv, diff_wo, diff_lq1, diff_lk1, diff_lq2, diff_lk2, diff_subln, rel_bias, moe_wrg, moe_brg, moe_wre, moe_bre, moe_wg, moe_wu, moe_wd):
    d = D_MODEL
    nb_p, seq_len, _ = x_prompt.shape
    assert x_sample.shape[1] == seq_len
    t_p = nb_p * seq_len
    x = jnp.concatenate([x_prompt.reshape(-1, d), x_sample.reshape(-1, d)], axis=0)
    c = jnp.concatenate([c_prompt, c_sample], axis=0)
    mod = _modulation(c, ada_w, ada_b)
    zero_bias = jnp.zeros((1, d), F32)
    v_first = None
    for i in range(DEPTH):
        j = i // N_MIXERS
        mod_l = mod[i]
        if i % N_MIXERS == 0:
            p = _rwkv_params(j, rwkv_mu, rwkv_wr, rwkv_wk, rwkv_wv, rwkv_w0, rwkv_w1, rwkv_w2,
                             rwkv_a0, rwkv_a1, rwkv_a2, rwkv_v0, rwkv_v1, rwkv_v2, rwkv_g1, rwkv_g2,
                             rwkv_kk, rwkv_ka, rwkv_rk, rwkv_lnw, rwkv_lnb)
            r, k, v, g, wlf, wlb, af, ab = _rwkv_pre(x, mod_l, norm_g[i, 0], p, v_first, seq_len)
            if v_first is None:
                v_first = v
            y = _wkv(r, k, v, wlf, wlb, af, ab, p["kk"], p["ka"], seq_len)
            pre = _rwkv_mid(y, r, k, v, af, ab, g, p)
            w_out, b_out = rwkv_wo[j].astype(BF16), zero_bias
        elif i % N_MIXERS == 1:
            p = {"win": sgu_win[j].astype(BF16), "bin": sgu_bin[j].reshape(1, -1),
                 "ng": sgu_ng[j].reshape(1, -1), "nb": sgu_nb[j].reshape(1, -1),
                 "ws": sgu_ws[j].astype(BF16),
                 "bs": jnp.repeat(jnp.transpose(sgu_bs[j]), SGU_GC, axis=1)}
            pre = _sgu(x, mod_l, norm_g[i, 0], p, seq_len)
            w_out, b_out = sgu_wout[j].astype(BF16), sgu_bout[j].reshape(1, d)
        else:
            lambda_init = 0.8 - 0.6 * math.exp(-0.3 * i)
            lam = (jnp.exp(jnp.sum(diff_lq1[j] * diff_lk1[j])) - jnp.exp(jnp.sum(diff_lq2[j] * diff_lk2[j]))
                   + lambda_init)
            q, k, v = _qkv(x, mod_l, norm_g[i, 0], diff_wqkv[j].astype(BF16), seq_len)
            windows = _bias_windows(rel_bias, seq_len, ATTN_TQ)
            pre = _diff_attention(q, k, v, windows, lam, diff_subln[j], lambda_init, seq_len)
            w_out, b_out = diff_wo[j].astype(BF16), zero_bias
        w_router = _pad_cols(jnp.concatenate([moe_wrg[i], moe_wre[i]], axis=1), ROUTER_COLS)
        b_router = _pad_cols(jnp.concatenate([moe_brg[i], moe_bre[i]])[None, :], ROUTER_COLS)
        x1, h2, logits = _post(pre, w_out, b_out, x, mod_l, norm_g[i, 1], w_router, b_router, seq_len)
        x = _moe(x1, h2, logits, mod_l, moe_wg[i].astype(BF16), moe_wu[i].astype(BF16),
                 moe_wd[i].astype(BF16), seq_len)
    y = _final_norm(x, final_g)
    return (y[:t_p].reshape(x_prompt.shape), y[t_p:].reshape(x_sample.shape))
```

```python
import functools
import math

import numpy as np
import jax
import jax.numpy as jnp
from jax import lax
from jax.experimental import pallas as pl
from jax.experimental.pallas import tpu as pltpu

F32 = jnp.float32
BF16 = jnp.bfloat16
HIGHEST = lax.Precision.HIGHEST

D_MODEL = 1024
DEPTH = 4
N_MIXERS = 3
NORM_EPS = 1e-6
LANES = 128
VMEM_LIMIT_BYTES = 56 * 1024 * 1024

RWKV_HEAD = 64
RWKV_HEADS = D_MODEL // RWKV_HEAD
RWKV_PAIRS = D_MODEL // LANES
RWKV_GN_EPS = 64e-5
WKV_CHUNK = 64
WKV_PAIRS_PER_STEP = 2
WKV_CHUNKS_PER_STEP = 2

SGU_CHUNK = 128
SGU_DIM = 2 * D_MODEL
SGU_GROUPS = 8
SGU_GC = SGU_DIM // SGU_GROUPS

DIFF_HEADS = 8
DIFF_HEAD = 64
NUM_BUCKETS = 32
MAX_DISTANCE = 128
ATTN_TQ = 256

MOE_GROUPS = 4
EXPERTS_PER_GROUP = 8
N_EXPERTS = MOE_GROUPS * EXPERTS_PER_GROUP
MOE_TOP_K = 2
EXPERT_DIM = D_MODEL // 2
MOE_BLOCK = 256
ROUTER_COLS = LANES

TOKEN_TILE = 256


def _params(*semantics):
    return pltpu.CompilerParams(dimension_semantics=semantics, vmem_limit_bytes=VMEM_LIMIT_BYTES)


def _resident(shape):
    zeros = (0,) * len(shape)
    return pl.BlockSpec(shape, lambda *_: zeros, pipeline_mode=pl.Buffered(1))


def _dot(a, b, precision=None):
    return jnp.dot(a, b, preferred_element_type=F32, precision=precision)


def _dot_nt(a, b, precision=None):
    return lax.dot_general(a, b, (((1,), (1,)), ((), ())), preferred_element_type=F32,
                           precision=precision)


def _dot_tn(a, b, precision=None):
    return lax.dot_general(a, b, (((0,), (0,)), ((), ())), preferred_element_type=F32,
                           precision=precision)


def _sigmoid(x):
    return 1.0 / (1.0 + jnp.exp(-x))


def _norm_mod(x, g, scale, shift, eps=NORM_EPS):
    y = x * lax.rsqrt(jnp.mean(x * x, axis=-1, keepdims=True) + eps)
    return (y * g) * (1.0 + scale) + shift


def _mod_kernel(c_ref, w_ref, b_ref, o_ref):
    c = c_ref[...]
    o_ref[...] = _dot(c * _sigmoid(c), w_ref[...], HIGHEST) + b_ref[...]


def _modulation(c, ada_w, ada_b):
    nb = c.shape[0]
    depth, d, n = ada_w.shape
    out = pl.pallas_call(
        _mod_kernel,
        grid=(depth, n // d),
        in_specs=[pl.BlockSpec((nb, d), lambda l, j: (0, 0)),
                  pl.BlockSpec((None, d, d), lambda l, j: (l, 0, j)),
                  pl.BlockSpec((None, 1, d), lambda l, j: (l, 0, j))],
        out_specs=pl.BlockSpec((None, nb, d), lambda l, j: (l, 0, j)),
        out_shape=jax.ShapeDtypeStruct((depth, nb, n), F32),
        compiler_params=_params("parallel", "parallel"),
        name="adaln_modulation",
    )(c, ada_w, ada_b.reshape(depth, 1, n))
    return out.reshape(depth, nb, n // d, d)


def _rwkv_pre_kernel(*refs, tm, tiles_per_seq, has_vres):
    if has_vres:
        (x_ref, xp_ref, xn_ref, mod_ref, ng_ref, mu_ref, wr_ref, wk_ref, wv_ref, wl1_ref, wl2_ref,
         al1_ref, al2_ref, g1_ref, g2_ref, w0_ref, a0_ref, kscale_ref, hsum_ref, hexp_ref,
         vf_ref, v0_ref, v1_ref, v2_ref,
         r_ref, k_ref, v_ref, g_ref, kk_ref, wlf_ref, wlb_ref, af_ref, ab_ref) = refs
    else:
        (x_ref, xp_ref, xn_ref, mod_ref, ng_ref, mu_ref, wr_ref, wk_ref, wv_ref, wl1_ref, wl2_ref,
         al1_ref, al2_ref, g1_ref, g2_ref, w0_ref, a0_ref, kscale_ref, hsum_ref, hexp_ref,
         r_ref, k_ref, v_ref, g_ref, kk_ref, wlf_ref, wlb_ref, af_ref, ab_ref) = refs
    d = D_MODEL
    i = pl.program_id(0)
    pos = i % tiles_per_seq
    shift, scale = mod_ref[0:1, :], mod_ref[1:2, :]
    ng = ng_ref[...]
    h = _norm_mod(x_ref[...], ng, scale, shift)
    h_prev = jnp.where(pos == 0, 0.0, _norm_mod(xp_ref[7:8, :], ng, scale, shift))
    h_next = jnp.where(pos == tiles_per_seq - 1, 0.0, _norm_mod(xn_ref[0:1, :], ng, scale, shift))
    row = lax.broadcasted_iota(jnp.int32, (tm, 1), 0)
    h_up = jnp.where(row == 0, h_prev, pltpu.roll(h, 1, 0))
    h_dn = jnp.where(row == tm - 1, h_next, pltpu.roll(h, tm - 1, 0))
    xx = 0.5 * (h_up + h_dn) - h

    def mix(n):
        return (h + xx * mu_ref[n:n + 1, :]).astype(BF16)

    xr, xw, xk, xv, xa, xg = [mix(n) for n in range(6)]
    r_ref[...] = _dot(xr, wr_ref[...])
    k = _dot(xk, wk_ref[...])
    k_ref[...] = k
    kkr = k * kscale_ref[...]
    norm = jnp.sqrt(_dot(_dot(kkr * kkr, hsum_ref[...], HIGHEST), hexp_ref[...], HIGHEST))
    kk_ref[...] = kkr / jnp.maximum(norm, 1e-12)
    v = _dot(xv, wv_ref[...])
    if has_vres:
        lv = _dot(_dot(xv, v1_ref[...]).astype(BF16), v2_ref[...])
        v = v + (vf_ref[...] - v) * _sigmoid(v0_ref[...] + lv)
    v_ref[...] = v
    g_ref[...] = _dot(_sigmoid(_dot(xg, g1_ref[...])).astype(BF16), g2_ref[...])
    wl = w0_ref[...] + _dot(jnp.tanh(_dot(xw, wl1_ref[...])).astype(BF16), wl2_ref[...])
    w_raw = jnp.minimum(wl, 0.0) - jnp.log1p(jnp.exp(-jnp.abs(wl))) - 0.5
    logw = -jnp.exp(w_raw)
    wlf_ref[...] = logw[:, :d]
    wlb_ref[...] = logw[:, d:]
    a = _sigmoid(a0_ref[...] + _dot(_dot(xa, al1_ref[...]).astype(BF16), al2_ref[...]))
    af_ref[...] = a[:, :d]
    ab_ref[...] = a[:, d:]


def _head_sum_matrix():
    head_of_lane = np.arange(D_MODEL) // RWKV_HEAD
    return jnp.asarray(head_of_lane[:, None] == np.arange(RWKV_HEADS)[None, :], F32)


def _block_diag2(m0, m1):
    z = jnp.zeros_like(m0)
    return jnp.concatenate([jnp.concatenate([m0, z], 1), jnp.concatenate([z, m1], 1)], 0)


def _rwkv_pre(x, mod_l, norm_g, p, v_first, seq_len):
    t, d = x.shape
    tm = TOKEN_TILE
    tiles_per_seq = seq_len // tm
    n_tiles = t // tm
    has_vres = v_first is not None
    tile = pl.BlockSpec((tm, d), lambda i: (i, 0))
    rows8 = tm // 8
    last8 = t // 8 - 1
    in_specs = [
        tile,
        pl.BlockSpec((8, d), lambda i: (jnp.maximum(i * rows8 - 1, 0), 0)),
        pl.BlockSpec((8, d), lambda i: (jnp.minimum((i + 1) * rows8, last8), 0)),
        pl.BlockSpec((None, 6, d), lambda i: (i // tiles_per_seq, 0, 0)),
        _resident((1, d)), _resident((6, d)),
        _resident((d, d)), _resident((d, d)), _resident((d, d)),
        _resident((d, LANES)), _resident((LANES, 2 * d)),
        _resident((d, LANES)), _resident((LANES, 2 * d)),
        _resident((d, LANES)), _resident((LANES, d)),
        _resident((1, 2 * d)), _resident((1, 2 * d)),
        _resident((1, d)), _resident((d, RWKV_HEADS)), _resident((RWKV_HEADS, d)),
    ]
    hsum = _head_sum_matrix()
    args = [x, x, x, mod_l, norm_g.reshape(1, d), p["mu"],
            p["wr"], p["wk"], p["wv"], p["wl1"], p["wl2"], p["al1"], p["al2"], p["g1"], p["g2"],
            p["w0"], p["a0"], p["kk"], hsum, hsum.T]
    if has_vres:
        in_specs += [tile, _resident((1, d)), _resident((d, LANES)), _resident((LANES, d))]
        args += [v_first, p["v0"], p["v1"], p["v2"]]
    outs = pl.pallas_call(
        functools.partial(_rwkv_pre_kernel, tm=tm, tiles_per_seq=tiles_per_seq, has_vres=has_vres),
        grid=(n_tiles,),
        in_specs=in_specs,
        out_specs=[tile] * 9,
        out_shape=[jax.ShapeDtypeStruct((t, d), F32)] * 9,
        compiler_params=_params("parallel"),
        name="rwkv_pre",
    )(*args)
    return outs


def _split_bf16(x):
    hi = x.astype(BF16)
    return hi, (x - hi.astype(F32)).astype(BF16)


def _wkv_kernel(r_ref, k_ref, v_ref, kk_ref, wlf_ref, wlb_ref, af_ref, ab_ref, ka_ref,
                lvl_ref, y_ref, state_ref, inv_ref, zp_ref, yp_ref, sp_ref, xa_ref, xr_ref, yb_ref,
                nrb_ref, dec_ref, *, seq_len, chunk, pairs, group):
    n_chunks = seq_len // chunk
    n_levels = int(math.log2(chunk))
    two = 2 * chunk
    lane = lax.broadcasted_iota(jnp.int32, (1, LANES), 1)
    head0 = lane < RWKV_HEAD
    ri = lax.broadcasted_iota(jnp.int32, (chunk, chunk), 0)
    ci = lax.broadcasted_iota(jnp.int32, (chunk, chunk), 1)
    si = lax.broadcasted_iota(jnp.int32, (two, two), 0)
    sj = lax.broadcasted_iota(jnp.int32, (two, two), 1)
    same_head = (si < chunk) == (sj < chunk)
    st, su = si & (chunk - 1), sj & (chunk - 1)
    eye = (si == sj).astype(F32)
    chain_defs = [(pair, reverse) for pair in range(pairs) for reverse in (False, True)]

    def rows_of(idx, reverse):
        pos = (n_chunks - 1 - idx) if reverse else idx
        return pl.ds(pl.multiple_of(pos * chunk, chunk), chunk)

    def stack(x):
        return jnp.concatenate([jnp.where(head0, x, 0.0), jnp.where(head0, 0.0, x)], axis=0)

    def a_load(q, g, idx):
        pair, reverse = chain_defs[q]
        rows = rows_of(idx, reverse)
        lanes = slice(pair * LANES, (pair + 1) * LANES)
        a = dict(q=q, g=g, rows=rows, lanes=lanes, reverse=reverse)
        a["lw"] = lw = (wlb_ref if reverse else wlf_ref)[rows, lanes]
        a["a"] = (ab_ref if reverse else af_ref)[rows, lanes]
        cum = ((ri <= ci) if reverse else (ri >= ci)).astype(BF16)
        lw_hi = lw.astype(BF16)
        lw_mid, lw_lo = _split_bf16(lw - lw_hi.astype(F32))
        a["logp"] = _dot(cum, lw_hi) + (_dot(cum, lw_mid) + _dot(cum, lw_lo))
        return a

    def a_scale(a):
        rows, lanes, logp, lw, av = a["rows"], a["lanes"], a["logp"], a["lw"], a["a"]
        k = k_ref[rows, lanes]
        kk = kk_ref[rows, lanes]
        kd = k * (1.0 + (av - 1.0) * ka_ref[:, lanes])
        p_in = jnp.exp(logp)
        p_out = jnp.exp(-logp)
        a["xa"] = stack(-kk * jnp.exp(logp - lw)).astype(BF16)
        a["xr"] = stack(r_ref[rows, lanes] * p_in).astype(BF16)
        a["yb"] = stack(kk * av * p_out).astype(BF16)
        a["yk"] = stack(kd * p_out).astype(BF16)
        a["vs"] = stack(v_ref[rows, lanes]).astype(BF16)
        a["decay"] = jnp.exp(jnp.sum(lw, axis=0, keepdims=True))

    def a_scores(a):
        if a["reverse"]:
            strict, incl = same_head & (st < su), same_head & (st <= su)
        else:
            strict, incl = same_head & (st > su), same_head & (st >= su)
        sc = _dot_nt(jnp.concatenate([a["xa"], a["xr"]], 0),
                     jnp.concatenate([a["yb"], a["yk"]], 0))
        a["m"] = sc[:two, :two].astype(BF16)
        m_ak = jnp.where(strict, sc[:two, two:], 0.0).astype(BF16)
        n_rk = jnp.where(incl, sc[two:, two:], 0.0).astype(BF16)
        a["n_rb"] = jnp.where(incl, sc[two:, :two], 0.0).astype(BF16)
        a["inv"] = eye + (a["m"] * lvl_ref[int(a["reverse"]), 0]).astype(F32)
        a["zp"] = _dot(m_ak, a["vs"])
        a["yp"] = _dot(n_rk, a["vs"])
        a["sp"] = _dot_tn(a["vs"], a["yk"])

    def a_double(a, level):
        t = a["inv"].astype(BF16)
        m_n = a["m"] * lvl_ref[int(a["reverse"]), level]
        a["inv"] = a["inv"] + _dot(_dot(t, m_n).astype(BF16), t)

    def a_store(a, slot):
        at = (slot, a["g"], a["q"])
        inv_ref[at] = a["inv"].astype(BF16)
        zp_ref[at] = a["zp"]
        yp_ref[at] = a["yp"]
        sp_ref[at] = a["sp"]
        xa_ref[at] = a["xa"]
        xr_ref[at] = a["xr"]
        yb_ref[at] = a["yb"]
        nrb_ref[at] = a["n_rb"]
        dec_ref[at] = a["decay"]

    a_stage_list = [a_scale, a_scores] + [functools.partial(a_double, level=level)
                                          for level in range(1, n_levels)]

    def b_load(b):
        s0 = state_ref[b["q"]]
        b["s0"], b["s0b"] = s0, s0.astype(BF16)
        b["z"] = _dot_nt(xa_ref[b["at"]], b["s0b"]) + zp_ref[b["at"]]

    def b_solve(b):
        b["ub"] = _dot(inv_ref[b["at"]], b["z"].astype(BF16)).astype(BF16)

    def b_out(b):
        at = b["at"]
        pair, reverse = chain_defs[b["q"]]
        ys = _dot_nt(xr_ref[at], b["s0b"]) + yp_ref[at] + _dot(nrb_ref[at], b["ub"])
        y_ref[rows_of(b["idx"], reverse), pair * LANES:(pair + 1) * LANES] += ys[:chunk] + ys[chunk:]
        state_ref[b["q"]] = (b["s0"] + sp_ref[at] + _dot_tn(b["ub"], yb_ref[at])) * dec_ref[at]

    b_stage_list = [b_load, b_solve, b_out]

    y_ref[...] = jnp.zeros_like(y_ref)
    state_ref[...] = jnp.zeros_like(state_ref)
    n_chains = len(chain_defs)
    first = [a_load(q, g, g) for g in range(group) for q in range(n_chains)]
    for stage in a_stage_list:
        for a in first:
            stage(a)
    for a in first:
        a_store(a, 0)

    def body(step, carry):
        slot = step & 1
        base = step * group
        ahead = [a_load(q, g, jnp.minimum(base + group + g, n_chunks - 1))
                 for g in range(group) for q in range(n_chains)]
        b_work = []
        for g in range(group):
            now = [dict(q=q, at=(slot, g, q), idx=base + g) for q in range(n_chains)]
            b_work += [(stage, now) for stage in b_stage_list]
        a_work = [(stage, ahead) for stage in a_stage_list]
        per_a = -(-len(b_work) // len(a_work))
        while a_work or b_work:
            for stage, items in b_work[:per_a]:
                for b in items:
                    stage(b)
            b_work = b_work[per_a:]
            if a_work:
                stage, items = a_work.pop(0)
                for a in items:
                    stage(a)
        for a in ahead:
            a_store(a, 1 - slot)
        return carry

    lax.fori_loop(0, n_chunks // group, body, 0)


def _wkv_level_masks(chunk):
    s = np.arange(2 * chunk)
    head, t = s // chunk, s % chunk
    n_levels = int(math.log2(chunk))
    masks = np.zeros((2, n_levels, 2 * chunk, 2 * chunk), np.float32)
    for d in range(2):
        pos = t if d == 0 else chunk - 1 - t
        for j in range(n_levels):
            blk = pos >> j
            masks[d, j] = ((head[:, None] == head[None, :]) & (blk[:, None] % 2 == 1)
                           & (blk[None, :] == blk[:, None] - 1))
    return masks


def _wkv(r, k, v, kk, wlf, wlb, af, ab, k_a, seq_len):
    t, d = r.shape
    nb = t // seq_len
    pairs = WKV_PAIRS_PER_STEP
    width = pairs * LANES
    n_chains = 2 * pairs
    seq = pl.BlockSpec((seq_len, width), lambda b, p: (b, p))
    par = pl.BlockSpec((1, width), lambda b, p: (0, p))
    levels = jnp.asarray(_wkv_level_masks(WKV_CHUNK), BF16)
    group = WKV_CHUNKS_PER_STEP
    parked_f32 = pltpu.VMEM((2, group, n_chains, LANES, LANES), F32)
    parked_bf16 = pltpu.VMEM((2, group, n_chains, LANES, LANES), BF16)
    return pl.pallas_call(
        functools.partial(_wkv_kernel, seq_len=seq_len, chunk=WKV_CHUNK, pairs=pairs, group=group),
        grid=(nb, d // width),
        in_specs=[seq] * 8 + [par, _resident(levels.shape)],
        out_specs=seq,
        out_shape=jax.ShapeDtypeStruct((t, d), F32),
        scratch_shapes=[pltpu.VMEM((n_chains, LANES, LANES), F32), parked_bf16] + [parked_f32] * 3
        + [parked_bf16] * 4 + [pltpu.VMEM((2, group, n_chains, 1, LANES), F32)],
        compiler_params=_params("parallel", "parallel"),
        name="wkv7_chunked",
    )(r, k, v, kk, wlf, wlb, af, ab, k_a.reshape(1, d), levels)


def _rwkv_mid_kernel(y_ref, r_ref, k_ref, v_ref, af_ref, ab_ref, g_ref, ka_ref, rk_ref, lnw_ref,
                     lnb_ref, hsum_ref, hexp_ref, o_ref):
    hsum = hsum_ref[...]
    hexp = hexp_ref[...]

    def head_sum(x):
        return _dot(_dot(x, hsum, HIGHEST), hexp, HIGHEST)

    y = y_ref[...]
    yc = y - head_sum(y) * (1.0 / RWKV_HEAD)
    var = head_sum(yc * yc) * (1.0 / RWKV_HEAD)
    yn = yc * lax.rsqrt(var + RWKV_GN_EPS) * lnw_ref[...] + lnb_ref[...]
    k_bonus = k_ref[...] * (1.0 + (0.5 * (af_ref[...] + ab_ref[...]) - 1.0) * ka_ref[...])
    bonus = head_sum(r_ref[...] * k_bonus * rk_ref[...]) * v_ref[...]
    o_ref[...] = ((yn + bonus) * g_ref[...]).astype(BF16)


def _rwkv_mid(y, r, k, v, af, ab, g, p):
    t, d = y.shape
    tm = TOKEN_TILE
    tile = pl.BlockSpec((tm, d), lambda i: (i, 0))
    vec = _resident((1, d))
    hsum = _head_sum_matrix()
    return pl.pallas_call(
        _rwkv_mid_kernel,
        grid=(t // tm,),
        in_specs=[tile] * 7 + [vec] * 4 + [_resident((d, RWKV_HEADS)), _resident((RWKV_HEADS, d))],
        out_specs=tile,
        out_shape=jax.ShapeDtypeStruct((t, d), BF16),
        compiler_params=_params("parallel"),
        name="rwkv_mid",
    )(y, r, k, v, af, ab, g, p["ka"], p["rk"], p["lnw"], p["lnb"], hsum, hsum.T)


def _sgu_kernel(x_ref, mod_ref, ng_ref, win_ref, bin_ref, lg_ref, lb_ref, ws_ref, bs_ref, o_ref,
                *, tm):
    h = _norm_mod(x_ref[...], ng_ref[...], mod_ref[1:2, :], mod_ref[0:1, :]).astype(BF16)
    z = _dot(h, win_ref[...]) + bin_ref[...]
    z = 0.5 * z * (1.0 + lax.erf(z * (1.0 / math.sqrt(2.0))))
    u = z[:, :SGU_DIM]
    v = z[:, SGU_DIM:]
    vc = v - jnp.mean(v, axis=-1, keepdims=True)
    vn = vc * lax.rsqrt(jnp.mean(vc * vc, axis=-1, keepdims=True) + 1e-5) * lg_ref[...] + lb_ref[...]
    vb = vn.astype(BF16)
    for c in range(tm // SGU_CHUNK):
        rows = slice(c * SGU_CHUNK, (c + 1) * SGU_CHUNK)
        for g in range(SGU_GROUPS):
            cols = slice(g * SGU_GC, (g + 1) * SGU_GC)
            s = _dot(ws_ref[g], vb[rows, cols]) + bs_ref[:, cols]
            o_ref[rows, cols] = (u[rows, cols] * s).astype(BF16)


def _sgu(x, mod_l, norm_g, p, seq_len):
    t, d = x.shape
    tm = TOKEN_TILE
    tiles_per_seq = seq_len // tm
    return pl.pallas_call(
        functools.partial(_sgu_kernel, tm=tm),
        grid=(t // tm,),
        in_specs=[pl.BlockSpec((tm, d), lambda i: (i, 0)),
                  pl.BlockSpec((None, 6, d), lambda i: (i // tiles_per_seq, 0, 0)),
                  _resident((1, d)),
                  _resident((d, 2 * SGU_DIM)), _resident((1, 2 * SGU_DIM)),
                  _resident((1, SGU_DIM)), _resident((1, SGU_DIM)),
                  _resident((SGU_GROUPS, SGU_CHUNK, SGU_CHUNK)), _resident((SGU_CHUNK, SGU_DIM))],
        out_specs=pl.BlockSpec((tm, SGU_DIM), lambda i: (i, 0)),
        out_shape=jax.ShapeDtypeStruct((t, SGU_DIM), BF16),
        compiler_params=_params("parallel"),
        name="sgu",
    )(x, mod_l, norm_g.reshape(1, d), p["win"], p["bin"], p["ng"], p["nb"], p["ws"], p["bs"])


def _qkv_kernel(x_ref, mod_ref, ng_ref, w_ref, q_ref, k_ref, v_ref):
    d = D_MODEL
    h = _norm_mod(x_ref[...], ng_ref[...], mod_ref[1:2, :], mod_ref[0:1, :]).astype(BF16)
    qkv = _dot(h, w_ref[...])
    q_ref[...] = (qkv[:, :d] * (DIFF_HEAD ** -0.5)).astype(BF16)
    k_ref[...] = qkv[:, d:2 * d].astype(BF16)
    v_ref[...] = qkv[:, 2 * d:].astype(BF16)


def _qkv(x, mod_l, norm_g, w_qkv, seq_len):
    t, d = x.shape
    tm = TOKEN_TILE
    tiles_per_seq = seq_len // tm
    tile = pl.BlockSpec((tm, d), lambda i: (i, 0))
    return pl.pallas_call(
        _qkv_kernel,
        grid=(t // tm,),
        in_specs=[tile, pl.BlockSpec((None, 6, d), lambda i: (i // tiles_per_seq, 0, 0)),
                  _resident((1, d)), _resident((d, 3 * d))],
        out_specs=[tile] * 3,
        out_shape=[jax.ShapeDtypeStruct((t, d), BF16)] * 3,
        compiler_params=_params("parallel"),
        name="diff_qkv",
    )(x, mod_l, norm_g.reshape(1, d), w_qkv)


def _attn_kernel(q_ref, k_ref, v_ref, win_ref, lam_ref, sg_ref, o_ref, *, tq, seq_len, out_scale):
    q = q_ref[...]
    lane = lax.broadcasted_iota(jnp.int32, (1, LANES), 1)
    zero = jnp.zeros_like(q)
    qs = jnp.concatenate([jnp.where(lane < DIFF_HEAD, q, zero),
                          jnp.where(lane < DIFF_HEAD, zero, q)], axis=0)
    s = _dot_nt(qs, k_ref[...])
    width = seq_len + tq
    window = jnp.broadcast_to(win_ref[...], (tq, width))
    bias = pltpu.roll(window, width - tq + 1, 1, stride=1, stride_axis=0)[:, :seq_len]

    def softmax(x):
        e = jnp.exp(x - jnp.max(x, axis=-1, keepdims=True))
        return e / jnp.sum(e, axis=-1, keepdims=True)

    p1 = softmax(s[:tq] + bias)
    p2 = softmax(s[tq:] + bias)
    attn = (p1 - lam_ref[...] * p2).astype(BF16)
    o = _dot(attn, v_ref[...])
    o = o * lax.rsqrt(jnp.mean(o * o, axis=-1, keepdims=True) + 1e-5) * sg_ref[...]
    o_ref[...] = (o * out_scale).astype(BF16)


def _t5_bucket(rel):
    nb = NUM_BUCKETS // 2
    max_exact = nb // 2
    ret = jnp.where(rel > 0, nb, 0)
    n = jnp.abs(rel)
    nf = jnp.maximum(n, 1).astype(F32)
    large = max_exact + (jnp.log(nf / max_exact) / math.log(MAX_DISTANCE / max_exact)
                         * (nb - max_exact)).astype(jnp.int32)
    large = jnp.minimum(large, nb - 1)
    return ret + jnp.where(n < max_exact, n, large)


def _bias_windows(rel_bias, seq_len, tq):
    nqb = seq_len // tq
    j = jnp.arange(seq_len + tq, dtype=jnp.int32)[None, :]
    q_hi = (jnp.arange(nqb, dtype=jnp.int32)[:, None] + 1) * tq
    bucket = _t5_bucket(j - q_hi + 1)
    return jnp.transpose(rel_bias[bucket], (0, 2, 1))[:, :, None, :].astype(F32)


def _diff_attention(q, k, v, windows, lam, subln_g, lambda_init, seq_len):
    t, d = q.shape
    nb = t // seq_len
    tq = ATTN_TQ
    nqb = seq_len // tq
    kv = pl.BlockSpec((seq_len, LANES), lambda b, h, i: (b, h))
    qo = pl.BlockSpec((tq, LANES), lambda b, h, i: (b * nqb + i, h))
    return pl.pallas_call(
        functools.partial(_attn_kernel, tq=tq, seq_len=seq_len, out_scale=1.0 - lambda_init),
        grid=(nb, DIFF_HEADS, nqb),
        in_specs=[qo, kv, kv,
                  pl.BlockSpec((None, None, 1, seq_len + tq), lambda b, h, i: (i, h, 0, 0)),
                  pl.BlockSpec((1, 1), lambda b, h, i: (0, 0)),
                  pl.BlockSpec((1, LANES), lambda b, h, i: (0, 0))],
        out_specs=qo,
        out_shape=jax.ShapeDtypeStruct((t, d), BF16),
        compiler_params=_params("parallel", "parallel", "arbitrary"),
        name="diff_attention",
    )(q, k, v, windows, lam.reshape(1, 1), subln_g.reshape(1, LANES))


def _post_kernel(pre_ref, w_ref, b_ref, x_ref, mod_ref, ng_ref, wr_ref, br_ref,
                 x1_ref, h2_ref, lg_ref):
    out = _dot(pre_ref[...], w_ref[...]) + b_ref[...]
    x1 = x_ref[...] + mod_ref[2:3, :] * out
    x1_ref[...] = x1
    h2 = _norm_mod(x1, ng_ref[...], mod_ref[4:5, :], mod_ref[3:4, :])
    h2_ref[...] = h2.astype(BF16)
    lg_ref[...] = _dot(h2, wr_ref[...], HIGHEST) + br_ref[...]


def _post(pre, w, b, x, mod_l, norm_g2, w_router, b_router, seq_len):
    t, d = x.shape
    din = pre.shape[1]
    tm = TOKEN_TILE
    tiles_per_seq = seq_len // tm
    tile = pl.BlockSpec((tm, d), lambda i: (i, 0))
    return pl.pallas_call(
        _post_kernel,
        grid=(t // tm,),
        in_specs=[pl.BlockSpec((tm, din), lambda i: (i, 0)), _resident((din, d)), _resident((1, d)),
                  tile, pl.BlockSpec((None, 6, d), lambda i: (i // tiles_per_seq, 0, 0)),
                  _resident((1, d)), _resident((d, ROUTER_COLS)), _resident((1, ROUTER_COLS))],
        out_specs=[tile, tile, pl.BlockSpec((tm, ROUTER_COLS), lambda i: (i, 0))],
        out_shape=[jax.ShapeDtypeStruct((t, d), F32), jax.ShapeDtypeStruct((t, d), BF16),
                   jax.ShapeDtypeStruct((t, ROUTER_COLS), F32)],
        compiler_params=_params("parallel"),
        name="post_router",
    )(pre, w, b, x, mod_l, norm_g2.reshape(1, d), w_router, b_router)


def _expert_kernel(be_ref, nu_ref, x_ref, wg_ref, wu_ref, wd_ref, o_ref):
    i = pl.program_id(0)

    @pl.when(i < nu_ref[0])
    def _():
        x = x_ref[...]
        hg = _dot(x, wg_ref[...])
        hu = _dot(x, wu_ref[...])
        act = hg * _sigmoid(hg) * hu
        o_ref[...] = _dot(act.astype(BF16), wd_ref[...])

    @pl.when(i >= nu_ref[0])
    def _():
        o_ref[...] = jnp.zeros_like(o_ref)


def _experts(xs, block_e, n_used, wg, wu, wd):
    p_rows, d = xs.shape
    n_blocks = p_rows // MOE_BLOCK
    grid_spec = pltpu.PrefetchScalarGridSpec(
        num_scalar_prefetch=2,
        grid=(n_blocks,),
        in_specs=[pl.BlockSpec((MOE_BLOCK, d), lambda i, be, nu: (i, 0)),
                  pl.BlockSpec((None, d, EXPERT_DIM), lambda i, be, nu: (be[i], 0, 0)),
                  pl.BlockSpec((None, d, EXPERT_DIM), lambda i, be, nu: (be[i], 0, 0)),
                  pl.BlockSpec((None, EXPERT_DIM, d), lambda i, be, nu: (be[i], 0, 0))],
        out_specs=pl.BlockSpec((MOE_BLOCK, d), lambda i, be, nu: (i, 0)),
    )
    return pl.pallas_call(
        _expert_kernel,
        grid_spec=grid_spec,
        out_shape=jax.ShapeDtypeStruct((p_rows, d), F32),
        compiler_params=_params("arbitrary"),
        name="moe_experts",
    )(block_e, n_used, xs, wg, wu, wd)


def _combine_kernel(x_ref, y0_ref, y1_ref, gate_ref, mod_ref, o_ref):
    gate = gate_ref[...]
    moe = gate[:, 0:1] * y0_ref[...] + gate[:, 1:2] * y1_ref[...]
    o_ref[...] = x_ref[...] + mod_ref[5:6, :] * moe


def _combine(x1, y0, y1, gate, mod_l, seq_len):
    t, d = x1.shape
    tm = TOKEN_TILE
    tiles_per_seq = seq_len // tm
    tile = pl.BlockSpec((tm, d), lambda i: (i, 0))
    return pl.pallas_call(
        _combine_kernel,
        grid=(t // tm,),
        in_specs=[tile, tile, tile, pl.BlockSpec((tm, MOE_TOP_K), lambda i: (i, 0)),
                  pl.BlockSpec((None, 6, d), lambda i: (i // tiles_per_seq, 0, 0))],
        out_specs=tile,
        out_shape=jax.ShapeDtypeStruct((t, d), F32),
        compiler_params=_params("parallel"),
        name="moe_combine",
    )(x1, y0, y1, gate, mod_l)


def _route(logits):
    t = logits.shape[0]
    g_logits = logits[:, :MOE_GROUPS]
    e_logits = logits[:, MOE_GROUPS:MOE_GROUPS + N_EXPERTS].reshape(t, MOE_GROUPS, EXPERTS_PER_GROUP)
    grp = jnp.argmax(g_logits, axis=-1)
    p_grp = 1.0 / jnp.sum(jnp.exp(g_logits - jnp.max(g_logits, axis=-1, keepdims=True)), axis=-1)
    e_in = jnp.take_along_axis(e_logits, grp[:, None, None], axis=1)[:, 0]
    top_v, top_i = lax.top_k(e_in, MOE_TOP_K)
    gate = p_grp[:, None] * jax.nn.softmax(top_v, axis=-1)
    expert = grp[:, None].astype(jnp.int32) * EXPERTS_PER_GROUP + top_i.astype(jnp.int32)
    return expert, gate


def _dispatch_plan(expert):
    t = expert.shape[0]
    a = t * MOE_TOP_K
    e = expert.reshape(a)
    onehot = (e[:, None] == jnp.arange(N_EXPERTS, dtype=jnp.int32)[None, :]).astype(jnp.int32)
    csum = jnp.cumsum(onehot, axis=0)
    rank = jnp.take_along_axis(csum, e[:, None], axis=1)[:, 0] - 1
    counts = csum[-1]
    padded = (counts + MOE_BLOCK - 1) // MOE_BLOCK * MOE_BLOCK
    end_pad = jnp.cumsum(padded)
    start_pad = end_pad - padded
    dest = (start_pad[e] + rank).astype(jnp.int32)
    n_blocks = (a + N_EXPERTS * (MOE_BLOCK - 1) + MOE_BLOCK - 1) // MOE_BLOCK
    row_tok = jnp.zeros((n_blocks * MOE_BLOCK,), jnp.int32).at[dest].set(
        jnp.arange(a, dtype=jnp.int32) // MOE_TOP_K)
    block_e = jnp.searchsorted(end_pad, jnp.arange(n_blocks, dtype=jnp.int32) * MOE_BLOCK,
                               side="right")
    block_e = jnp.minimum(block_e, N_EXPERTS - 1).astype(jnp.int32)
    n_used = (end_pad[-1] // MOE_BLOCK).astype(jnp.int32).reshape(1)
    return dest.reshape(t, MOE_TOP_K), row_tok, block_e, n_used


def _moe(x1, h2, logits, mod_l, wg, wu, wd, seq_len):
    expert, gate = _route(logits)
    dest, row_tok, block_e, n_used = _dispatch_plan(expert)
    ys = _experts(h2[row_tok], block_e, n_used, wg, wu, wd)
    return _combine(x1, ys[dest[:, 0]], ys[dest[:, 1]], gate, mod_l, seq_len)


def _final_kernel(x_ref, g_ref, o_ref):
    x = x_ref[...]
    o_ref[...] = x * lax.rsqrt(jnp.mean(x * x, axis=-1, keepdims=True) + NORM_EPS) * g_ref[...]


def _final_norm(x, g):
    t, d = x.shape
    tm = TOKEN_TILE
    tile = pl.BlockSpec((tm, d), lambda i: (i, 0))
    return pl.pallas_call(
        _final_kernel,
        grid=(t // tm,),
        in_specs=[tile, _resident((1, d))],
        out_specs=tile,
        out_shape=jax.ShapeDtypeStruct((t, d), F32),
        compiler_params=_params("parallel"),
        name="final_norm",
    )(x, g.reshape(1, d))


def _pad_cols(w, n):
    return jnp.pad(w, ((0, 0), (0, n - w.shape[1])))


def _pad_rows(w, n):
    return jnp.pad(w, ((0, n - w.shape[0]), (0, 0)))


def _rwkv_params(j, mu, wr, wk, wv, w0, w1, w2, a0, a1, a2, v0, v1, v2, g1, g2, kk, ka, rk,
                 lnw, lnb):
    d = D_MODEL
    p = {
        "mu": mu[j],
        "wr": wr[j].astype(BF16), "wk": wk[j].astype(BF16), "wv": wv[j].astype(BF16),
        "wl1": jnp.concatenate([w1[j, 0], w1[j, 1]], axis=1).astype(BF16),
        "wl2": _block_diag2(w2[j, 0], w2[j, 1]).astype(BF16),
        "al1": jnp.concatenate([a1[j, 0], a1[j, 1]], axis=1).astype(BF16),
        "al2": _block_diag2(a2[j, 0], a2[j, 1]).astype(BF16),
        "g1": g1[j].astype(BF16), "g2": g2[j].astype(BF16),
        "w0": w0[j].reshape(1, 2 * d), "a0": a0[j].reshape(1, 2 * d),
        "kk": kk[j].reshape(1, d), "ka": ka[j].reshape(1, d), "rk": rk[j].reshape(1, d),
        "lnw": lnw[j].reshape(1, d), "lnb": lnb[j].reshape(1, d),
    }
    if j > 0:
        p["v0"] = v0[j - 1].reshape(1, d)
        p["v1"] = _pad_cols(v1[j - 1], LANES).astype(BF16)
        p["v2"] = _pad_rows(v2[j - 1], LANES).astype(BF16)
    return p


def kernel(x_prompt, x_sample, c_prompt, c_sample, ada_w, ada_b, norm_g, final_g, rwkv_mu, rwkv_wr, rwkv_wk, rwkv_wv, rwkv_wo, rwkv_w0, rwkv_w1, rwkv_w2, rwkv_a0, rwkv_a1, rwkv_a2, rwkv_v0, rwkv_v1, rwkv_v2, rwkv_g1, rwkv_g2, rwkv_kk, rwkv_ka, rwkv_rk, rwkv_lnw, rwkv_lnb, sgu_win, sgu_bin, sgu_ng, sgu_nb, sgu_ws, sgu_bs, sgu_wout, sgu_bout, diff_wqkv, diff_wo, diff_lq1, diff_lk1, diff_lq2, diff_lk2, diff_subln, rel_bias, moe_wrg, moe_brg, moe_wre, moe_bre, moe_wg, moe_wu, moe_wd):
    d = D_MODEL
    nb_p, seq_len, _ = x_prompt.shape
    assert x_sample.shape[1] == seq_len
    t_p = nb_p * seq_len
    x = jnp.concatenate([x_prompt.reshape(-1, d), x_sample.reshape(-1, d)], axis=0)
    c = jnp.concatenate([c_prompt, c_sample], axis=0)
    mod = _modulation(c, ada_w, ada_b)
    zero_bias = jnp.zeros((1, d), F32)
    v_first = None
    for i in range(DEPTH):
        j = i // N_MIXERS
        mod_l = mod[i]
        if i % N_MIXERS == 0:
            p = _rwkv_params(j, rwkv_mu, rwkv_wr, rwkv_wk, rwkv_wv, rwkv_w0, rwkv_w1, rwkv_w2,
                             rwkv_a0, rwkv_a1, rwkv_a2, rwkv_v0, rwkv_v1, rwkv_v2, rwkv_g1, rwkv_g2,
                             rwkv_kk, rwkv_ka, rwkv_rk, rwkv_lnw, rwkv_lnb)
            r, k, v, g, kk, wlf, wlb, af, ab = _rwkv_pre(x, mod_l, norm_g[i, 0], p, v_first, seq_len)
            if v_first is None:
                v_first = v
            y = _wkv(r, k, v, kk, wlf, wlb, af, ab, p["ka"], seq_len)
            pre = _rwkv_mid(y, r, k, v, af, ab, g, p)
            w_out, b_out = rwkv_wo[j].astype(BF16), zero_bias
        elif i % N_MIXERS == 1:
            p = {"win": sgu_win[j].astype(BF16), "bin": sgu_bin[j].reshape(1, -1),
                 "ng": sgu_ng[j].reshape(1, -1), "nb": sgu_nb[j].reshape(1, -1),
                 "ws": sgu_ws[j].astype(BF16),
                 "bs": jnp.repeat(jnp.transpose(sgu_bs[j]), SGU_GC, axis=1)}
            pre = _sgu(x, mod_l, norm_g[i, 0], p, seq_len)
            w_out, b_out = sgu_wout[j].astype(BF16), sgu_bout[j].reshape(1, d)
        else:
            lambda_init = 0.8 - 0.6 * math.exp(-0.3 * i)
            lam = (jnp.exp(jnp.sum(diff_lq1[j] * diff_lk1[j])) - jnp.exp(jnp.sum(diff_lq2[j] * diff_lk2[j]))
                   + lambda_init)
            q, k, v = _qkv(x, mod_l, norm_g[i, 0], diff_wqkv[j].astype(BF16), seq_len)
            windows = _bias_windows(rel_bias, seq_len, ATTN_TQ)
            pre = _diff_attention(q, k, v, windows, lam, diff_subln[j], lambda_init, seq_len)
            w_out, b_out = diff_wo[j].astype(BF16), zero_bias
        w_router = _pad_cols(jnp.concatenate([moe_wrg[i], moe_wre[i]], axis=1), ROUTER_COLS)
        b_router = _pad_cols(jnp.concatenate([moe_brg[i], moe_bre[i]])[None, :], ROUTER_COLS)
        x1, h2, logits = _post(pre, w_out, b_out, x, mod_l, norm_g[i, 1], w_router, b_router, seq_len)
        x = _moe(x1, h2, logits, mod_l, moe_wg[i].astype(BF16), moe_wu[i].astype(BF16),
                 moe_wd[i].astype(BF16), seq_len)
    y = _final_norm(x, final_g)
    return (y[:t_p].reshape(x_prompt.shape), y[t_p:].reshape(x_sample.shape))
```

```python
import functools
import math

import numpy as np
import jax
import jax.numpy as jnp
from jax import lax
from jax.experimental import pallas as pl
from jax.experimental.pallas import tpu as pltpu

F32 = jnp.float32
BF16 = jnp.bfloat16
HIGHEST = lax.Precision.HIGHEST

D_MODEL = 1024
DEPTH = 4
N_MIXERS = 3
NORM_EPS = 1e-6
LANES = 128
VMEM_LIMIT_BYTES = 56 * 1024 * 1024

RWKV_HEAD = 64
RWKV_HEADS = D_MODEL // RWKV_HEAD
RWKV_PAIRS = D_MODEL // LANES
RWKV_GN_EPS = 64e-5
WKV_CHUNK = 64
WKV_PAIRS_PER_STEP = 2
WKV_CHUNKS_PER_STEP = 2

SGU_CHUNK = 128
SGU_DIM = 2 * D_MODEL
SGU_GROUPS = 8
SGU_GC = SGU_DIM // SGU_GROUPS

DIFF_HEADS = 8
DIFF_HEAD = 64
NUM_BUCKETS = 32
MAX_DISTANCE = 128
ATTN_TQ = 256

MOE_GROUPS = 4
EXPERTS_PER_GROUP = 8
N_EXPERTS = MOE_GROUPS * EXPERTS_PER_GROUP
MOE_TOP_K = 2
EXPERT_DIM = D_MODEL // 2
MOE_BLOCK = 256
ROUTER_COLS = LANES

TOKEN_TILE = 256


def _params(*semantics):
    return pltpu.CompilerParams(dimension_semantics=semantics, vmem_limit_bytes=VMEM_LIMIT_BYTES)


def _resident(shape):
    zeros = (0,) * len(shape)
    return pl.BlockSpec(shape, lambda *_: zeros, pipeline_mode=pl.Buffered(1))


def _dot(a, b, precision=None):
    return jnp.dot(a, b, preferred_element_type=F32, precision=precision)


def _dot_nt(a, b, precision=None):
    return lax.dot_general(a, b, (((1,), (1,)), ((), ())), preferred_element_type=F32,
                           precision=precision)


def _dot_tn(a, b, precision=None):
    return lax.dot_general(a, b, (((0,), (0,)), ((), ())), preferred_element_type=F32,
                           precision=precision)


def _split_bf16(x):
    hi = x.astype(BF16)
    return hi, (x - hi.astype(F32)).astype(BF16)


def _head_sums(x, hsum, hexp):
    hi, lo = _split_bf16(x)
    shi, slo = _split_bf16(_dot(hi, hsum) + _dot(lo, hsum))
    return _dot(shi, hexp) + _dot(slo, hexp)


def _sigmoid(x):
    return 1.0 / (1.0 + jnp.exp(-x))


def _norm_mod(x, g, scale, shift, eps=NORM_EPS):
    y = x * lax.rsqrt(jnp.mean(x * x, axis=-1, keepdims=True) + eps)
    return (y * g) * (1.0 + scale) + shift


def _mod_kernel(c_ref, w_ref, b_ref, o_ref):
    c = c_ref[...]
    o_ref[...] = _dot(c * _sigmoid(c), w_ref[...], HIGHEST) + b_ref[...]


def _modulation(c, ada_w, ada_b):
    nb = c.shape[0]
    depth, d, n = ada_w.shape
    out = pl.pallas_call(
        _mod_kernel,
        grid=(depth, n // d),
        in_specs=[pl.BlockSpec((nb, d), lambda l, j: (0, 0)),
                  pl.BlockSpec((None, d, d), lambda l, j: (l, 0, j)),
                  pl.BlockSpec((None, 1, d), lambda l, j: (l, 0, j))],
        out_specs=pl.BlockSpec((None, nb, d), lambda l, j: (l, 0, j)),
        out_shape=jax.ShapeDtypeStruct((depth, nb, n), F32),
        compiler_params=_params("parallel", "parallel"),
        name="adaln_modulation",
    )(c, ada_w, ada_b.reshape(depth, 1, n))
    return out.reshape(depth, nb, n // d, d)


def _rwkv_pre_kernel(*refs, tm, tiles_per_seq, has_vres):
    if has_vres:
        (x_ref, xp_ref, xn_ref, mod_ref, ng_ref, mu_ref, wr_ref, wk_ref, wv_ref, wl1_ref, wl2_ref,
         al1_ref, al2_ref, g1_ref, g2_ref, w0_ref, a0_ref, kscale_ref, hsum_ref, hexp_ref,
         vf_ref, v0_ref, v1_ref, v2_ref,
         r_ref, k_ref, v_ref, g_ref, kk_ref, wlf_ref, wlb_ref, af_ref, ab_ref) = refs
    else:
        (x_ref, xp_ref, xn_ref, mod_ref, ng_ref, mu_ref, wr_ref, wk_ref, wv_ref, wl1_ref, wl2_ref,
         al1_ref, al2_ref, g1_ref, g2_ref, w0_ref, a0_ref, kscale_ref, hsum_ref, hexp_ref,
         r_ref, k_ref, v_ref, g_ref, kk_ref, wlf_ref, wlb_ref, af_ref, ab_ref) = refs
    d = D_MODEL
    i = pl.program_id(0)
    pos = i % tiles_per_seq
    shift, scale = mod_ref[0:1, :], mod_ref[1:2, :]
    ng = ng_ref[...]
    h = _norm_mod(x_ref[...], ng, scale, shift)
    h_prev = jnp.where(pos == 0, 0.0, _norm_mod(xp_ref[7:8, :], ng, scale, shift))
    h_next = jnp.where(pos == tiles_per_seq - 1, 0.0, _norm_mod(xn_ref[0:1, :], ng, scale, shift))
    row = lax.broadcasted_iota(jnp.int32, (tm, 1), 0)
    h_up = jnp.where(row == 0, h_prev, pltpu.roll(h, 1, 0))
    h_dn = jnp.where(row == tm - 1, h_next, pltpu.roll(h, tm - 1, 0))
    xx = 0.5 * (h_up + h_dn) - h

    def mix(n):
        return (h + xx * mu_ref[n:n + 1, :]).astype(BF16)

    xr, xw, xk, xv, xa, xg = [mix(n) for n in range(6)]
    r_ref[...] = _dot(xr, wr_ref[...])
    k = _dot(xk, wk_ref[...])
    k_ref[...] = k
    kkr = k * kscale_ref[...]
    norm = jnp.sqrt(_head_sums(kkr * kkr, hsum_ref[...], hexp_ref[...]))
    kk_ref[...] = kkr / jnp.maximum(norm, 1e-12)
    v = _dot(xv, wv_ref[...])
    if has_vres:
        lv = _dot(_dot(xv, v1_ref[...]).astype(BF16), v2_ref[...])
        v = v + (vf_ref[...] - v) * _sigmoid(v0_ref[...] + lv)
    v_ref[...] = v
    g_ref[...] = _dot(_sigmoid(_dot(xg, g1_ref[...])).astype(BF16), g2_ref[...])
    wl = w0_ref[...] + _dot(jnp.tanh(_dot(xw, wl1_ref[...])).astype(BF16), wl2_ref[...])
    w_raw = jnp.minimum(wl, 0.0) - jnp.log1p(jnp.exp(-jnp.abs(wl))) - 0.5
    logw = -jnp.exp(w_raw)
    wlf_ref[...] = logw[:, :d]
    wlb_ref[...] = logw[:, d:]
    a = _sigmoid(a0_ref[...] + _dot(_dot(xa, al1_ref[...]).astype(BF16), al2_ref[...]))
    af_ref[...] = a[:, :d]
    ab_ref[...] = a[:, d:]


def _head_sum_matrix():
    head_of_lane = np.arange(D_MODEL) // RWKV_HEAD
    return jnp.asarray(head_of_lane[:, None] == np.arange(RWKV_HEADS)[None, :], BF16)


def _block_diag2(m0, m1):
    z = jnp.zeros_like(m0)
    return jnp.concatenate([jnp.concatenate([m0, z], 1), jnp.concatenate([z, m1], 1)], 0)


def _rwkv_pre(x, mod_l, norm_g, p, v_first, seq_len):
    t, d = x.shape
    tm = TOKEN_TILE
    tiles_per_seq = seq_len // tm
    n_tiles = t // tm
    has_vres = v_first is not None
    tile = pl.BlockSpec((tm, d), lambda i: (i, 0))
    rows8 = tm // 8
    last8 = t // 8 - 1
    in_specs = [
        tile,
        pl.BlockSpec((8, d), lambda i: (jnp.maximum(i * rows8 - 1, 0), 0)),
        pl.BlockSpec((8, d), lambda i: (jnp.minimum((i + 1) * rows8, last8), 0)),
        pl.BlockSpec((None, 6, d), lambda i: (i // tiles_per_seq, 0, 0)),
        _resident((1, d)), _resident((6, d)),
        _resident((d, d)), _resident((d, d)), _resident((d, d)),
        _resident((d, LANES)), _resident((LANES, 2 * d)),
        _resident((d, LANES)), _resident((LANES, 2 * d)),
        _resident((d, LANES)), _resident((LANES, d)),
        _resident((1, 2 * d)), _resident((1, 2 * d)),
        _resident((1, d)), _resident((d, RWKV_HEADS)), _resident((RWKV_HEADS, d)),
    ]
    hsum = _head_sum_matrix()
    args = [x, x, x, mod_l, norm_g.reshape(1, d), p["mu"],
            p["wr"], p["wk"], p["wv"], p["wl1"], p["wl2"], p["al1"], p["al2"], p["g1"], p["g2"],
            p["w0"], p["a0"], p["kk"], hsum, hsum.T]
    if has_vres:
        in_specs += [tile, _resident((1, d)), _resident((d, LANES)), _resident((LANES, d))]
        args += [v_first, p["v0"], p["v1"], p["v2"]]
    outs = pl.pallas_call(
        functools.partial(_rwkv_pre_kernel, tm=tm, tiles_per_seq=tiles_per_seq, has_vres=has_vres),
        grid=(n_tiles,),
        in_specs=in_specs,
        out_specs=[tile] * 9,
        out_shape=[jax.ShapeDtypeStruct((t, d), F32)] * 9,
        compiler_params=_params("parallel"),
        name="rwkv_pre",
    )(*args)
    return outs


def _wkv_kernel(r_ref, k_ref, v_ref, kk_ref, wlf_ref, wlb_ref, af_ref, ab_ref, ka_ref,
                lvl_ref, y_ref, state_ref, inv_ref, zp_ref, yp_ref, sp_ref, xa_ref, xr_ref, yb_ref,
                nrb_ref, dec_ref, *, seq_len, chunk, pairs, group):
    n_chunks = seq_len // chunk
    n_levels = int(math.log2(chunk))
    two = 2 * chunk
    lane = lax.broadcasted_iota(jnp.int32, (1, LANES), 1)
    head0 = lane < RWKV_HEAD
    ri = lax.broadcasted_iota(jnp.int32, (chunk, chunk), 0)
    ci = lax.broadcasted_iota(jnp.int32, (chunk, chunk), 1)
    si = lax.broadcasted_iota(jnp.int32, (two, two), 0)
    sj = lax.broadcasted_iota(jnp.int32, (two, two), 1)
    same_head = (si < chunk) == (sj < chunk)
    st, su = si & (chunk - 1), sj & (chunk - 1)
    eye = (si == sj).astype(F32)
    chain_defs = [(pair, reverse) for pair in range(pairs) for reverse in (False, True)]

    def rows_of(idx, reverse):
        pos = (n_chunks - 1 - idx) if reverse else idx
        return pl.ds(pl.multiple_of(pos * chunk, chunk), chunk)

    def stack(x):
        return jnp.concatenate([jnp.where(head0, x, 0.0), jnp.where(head0, 0.0, x)], axis=0)

    def a_load(q, g, idx):
        pair, reverse = chain_defs[q]
        rows = rows_of(idx, reverse)
        lanes = slice(pair * LANES, (pair + 1) * LANES)
        a = dict(q=q, g=g, rows=rows, lanes=lanes, reverse=reverse)
        a["lw"] = lw = (wlb_ref if reverse else wlf_ref)[rows, lanes]
        a["a"] = (ab_ref if reverse else af_ref)[rows, lanes]
        cum = ((ri <= ci) if reverse else (ri >= ci)).astype(BF16)
        lw_hi = lw.astype(BF16)
        lw_mid, lw_lo = _split_bf16(lw - lw_hi.astype(F32))
        a["logp"] = _dot(cum, lw_hi) + (_dot(cum, lw_mid) + _dot(cum, lw_lo))
        return a

    def a_scale(a):
        rows, lanes, logp, lw, av = a["rows"], a["lanes"], a["logp"], a["lw"], a["a"]
        k = k_ref[rows, lanes]
        kk = kk_ref[rows, lanes]
        kd = k * (1.0 + (av - 1.0) * ka_ref[:, lanes])
        p_in = jnp.exp(logp)
        p_out = jnp.exp(-logp)
        a["xa"] = stack(-kk * jnp.exp(logp - lw)).astype(BF16)
        a["xr"] = stack(r_ref[rows, lanes] * p_in).astype(BF16)
        a["yb"] = stack(kk * av * p_out).astype(BF16)
        a["yk"] = stack(kd * p_out).astype(BF16)
        a["vs"] = stack(v_ref[rows, lanes]).astype(BF16)
        a["decay"] = jnp.exp(jnp.sum(lw, axis=0, keepdims=True))

    def a_scores(a):
        if a["reverse"]:
            strict, incl = same_head & (st < su), same_head & (st <= su)
        else:
            strict, incl = same_head & (st > su), same_head & (st >= su)
        sc = _dot_nt(jnp.concatenate([a["xa"], a["xr"]], 0),
                     jnp.concatenate([a["yb"], a["yk"]], 0))
        a["m"] = sc[:two, :two].astype(BF16)
        m_ak = jnp.where(strict, sc[:two, two:], 0.0).astype(BF16)
        n_rk = jnp.where(incl, sc[two:, two:], 0.0).astype(BF16)
        a["n_rb"] = jnp.where(incl, sc[two:, :two], 0.0).astype(BF16)
        a["inv"] = eye + (a["m"] * lvl_ref[int(a["reverse"]), 0]).astype(F32)
        a["zp"] = _dot(m_ak, a["vs"])
        a["yp"] = _dot(n_rk, a["vs"])
        a["sp"] = _dot_tn(a["vs"], a["yk"])

    def a_double(a, level):
        t = a["inv"].astype(BF16)
        m_n = a["m"] * lvl_ref[int(a["reverse"]), level]
        a["inv"] = a["inv"] + _dot(_dot(t, m_n).astype(BF16), t)

    def a_store(a, slot):
        at = (slot, a["g"], a["q"])
        inv_ref[at] = a["inv"].astype(BF16)
        zp_ref[at] = a["zp"]
        yp_ref[at] = a["yp"]
        sp_ref[at] = a["sp"]
        xa_ref[at] = a["xa"]
        xr_ref[at] = a["xr"]
        yb_ref[at] = a["yb"]
        nrb_ref[at] = a["n_rb"]
        dec_ref[at] = a["decay"]

    a_stage_list = [a_scale, a_scores] + [functools.partial(a_double, level=level)
                                          for level in range(1, n_levels)]

    def b_load(b):
        s0 = state_ref[b["q"]]
        b["s0"], b["s0b"] = s0, s0.astype(BF16)
        b["z"] = _dot_nt(xa_ref[b["at"]], b["s0b"]) + zp_ref[b["at"]]

    def b_solve(b):
        b["ub"] = _dot(inv_ref[b["at"]], b["z"].astype(BF16)).astype(BF16)

    def b_out(b):
        at = b["at"]
        pair, reverse = chain_defs[b["q"]]
        ys = _dot_nt(xr_ref[at], b["s0b"]) + yp_ref[at] + _dot(nrb_ref[at], b["ub"])
        y_ref[rows_of(b["idx"], reverse), pair * LANES:(pair + 1) * LANES] += ys[:chunk] + ys[chunk:]
        state_ref[b["q"]] = (b["s0"] + sp_ref[at] + _dot_tn(b["ub"], yb_ref[at])) * dec_ref[at]

    b_stage_list = [b_load, b_solve, b_out]

    y_ref[...] = jnp.zeros_like(y_ref)
    state_ref[...] = jnp.zeros_like(state_ref)
    n_chains = len(chain_defs)
    first = [a_load(q, g, g) for g in range(group) for q in range(n_chains)]
    for stage in a_stage_list:
        for a in first:
            stage(a)
    for a in first:
        a_store(a, 0)

    def body(step, carry):
        slot = step & 1
        base = step * group
        ahead = [a_load(q, g, jnp.minimum(base + group + g, n_chunks - 1))
                 for g in range(group) for q in range(n_chains)]
        b_work = []
        for g in range(group):
            now = [dict(q=q, at=(slot, g, q), idx=base + g) for q in range(n_chains)]
            b_work += [(stage, now) for stage in b_stage_list]
        a_work = [(stage, ahead) for stage in a_stage_list]
        per_a = -(-len(b_work) // len(a_work))
        while a_work or b_work:
            for stage, items in b_work[:per_a]:
                for b in items:
                    stage(b)
            b_work = b_work[per_a:]
            if a_work:
                stage, items = a_work.pop(0)
                for a in items:
                    stage(a)
        for a in ahead:
            a_store(a, 1 - slot)
        return carry

    lax.fori_loop(0, n_chunks // group, body, 0)


def _wkv_level_masks(chunk):
    s = np.arange(2 * chunk)
    head, t = s // chunk, s % chunk
    n_levels = int(math.log2(chunk))
    masks = np.zeros((2, n_levels, 2 * chunk, 2 * chunk), np.float32)
    for d in range(2):
        pos = t if d == 0 else chunk - 1 - t
        for j in range(n_levels):
            blk = pos >> j
            masks[d, j] = ((head[:, None] == head[None, :]) & (blk[:, None] % 2 == 1)
                           & (blk[None, :] == blk[:, None] - 1))
    return masks


def _wkv(r, k, v, kk, wlf, wlb, af, ab, k_a, seq_len):
    t, d = r.shape
    nb = t // seq_len
    pairs = WKV_PAIRS_PER_STEP
    width = pairs * LANES
    n_chains = 2 * pairs
    seq = pl.BlockSpec((seq_len, width), lambda b, p: (b, p))
    par = pl.BlockSpec((1, width), lambda b, p: (0, p))
    levels = jnp.asarray(_wkv_level_masks(WKV_CHUNK), BF16)
    group = WKV_CHUNKS_PER_STEP
    parked_f32 = pltpu.VMEM((2, group, n_chains, LANES, LANES), F32)
    parked_bf16 = pltpu.VMEM((2, group, n_chains, LANES, LANES), BF16)
    return pl.pallas_call(
        functools.partial(_wkv_kernel, seq_len=seq_len, chunk=WKV_CHUNK, pairs=pairs, group=group),
        grid=(nb, d // width),
        in_specs=[seq] * 8 + [par, _resident(levels.shape)],
        out_specs=seq,
        out_shape=jax.ShapeDtypeStruct((t, d), F32),
        scratch_shapes=[pltpu.VMEM((n_chains, LANES, LANES), F32), parked_bf16] + [parked_f32] * 3
        + [parked_bf16] * 4 + [pltpu.VMEM((2, group, n_chains, 1, LANES), F32)],
        compiler_params=_params("parallel", "parallel"),
        name="wkv7_chunked",
    )(r, k, v, kk, wlf, wlb, af, ab, k_a.reshape(1, d), levels)


def _rwkv_mid_kernel(y_ref, r_ref, k_ref, v_ref, af_ref, ab_ref, g_ref, ka_ref, rk_ref, lnw_ref,
                     lnb_ref, hsum_ref, hexp_ref, o_ref):
    hsum = hsum_ref[...]
    hexp = hexp_ref[...]

    def head_sum(x):
        return _head_sums(x, hsum, hexp)

    y = y_ref[...]
    yc = y - head_sum(y) * (1.0 / RWKV_HEAD)
    var = head_sum(yc * yc) * (1.0 / RWKV_HEAD)
    yn = yc * lax.rsqrt(var + RWKV_GN_EPS) * lnw_ref[...] + lnb_ref[...]
    k_bonus = k_ref[...] * (1.0 + (0.5 * (af_ref[...] + ab_ref[...]) - 1.0) * ka_ref[...])
    bonus = head_sum(r_ref[...] * k_bonus * rk_ref[...]) * v_ref[...]
    o_ref[...] = ((yn + bonus) * g_ref[...]).astype(BF16)


def _rwkv_mid(y, r, k, v, af, ab, g, p):
    t, d = y.shape
    tm = TOKEN_TILE
    tile = pl.BlockSpec((tm, d), lambda i: (i, 0))
    vec = _resident((1, d))
    hsum = _head_sum_matrix()
    return pl.pallas_call(
        _rwkv_mid_kernel,
        grid=(t // tm,),
        in_specs=[tile] * 7 + [vec] * 4 + [_resident((d, RWKV_HEADS)), _resident((RWKV_HEADS, d))],
        out_specs=tile,
        out_shape=jax.ShapeDtypeStruct((t, d), BF16),
        compiler_params=_params("parallel"),
        name="rwkv_mid",
    )(y, r, k, v, af, ab, g, p["ka"], p["rk"], p["lnw"], p["lnb"], hsum, hsum.T)


def _sgu_kernel(x_ref, mod_ref, ng_ref, win_ref, bin_ref, lg_ref, lb_ref, ws_ref, bs_ref, o_ref,
                *, tm):
    h = _norm_mod(x_ref[...], ng_ref[...], mod_ref[1:2, :], mod_ref[0:1, :]).astype(BF16)
    z = _dot(h, win_ref[...]) + bin_ref[...]
    z = 0.5 * z * (1.0 + lax.erf(z * (1.0 / math.sqrt(2.0))))
    u = z[:, :SGU_DIM]
    v = z[:, SGU_DIM:]
    vc = v - jnp.mean(v, axis=-1, keepdims=True)
    vn = vc * lax.rsqrt(jnp.mean(vc * vc, axis=-1, keepdims=True) + 1e-5) * lg_ref[...] + lb_ref[...]
    vb = vn.astype(BF16)
    for c in range(tm // SGU_CHUNK):
        rows = slice(c * SGU_CHUNK, (c + 1) * SGU_CHUNK)
        for g in range(SGU_GROUPS):
            cols = slice(g * SGU_GC, (g + 1) * SGU_GC)
            s = _dot(ws_ref[g], vb[rows, cols]) + bs_ref[:, cols]
            o_ref[rows, cols] = (u[rows, cols] * s).astype(BF16)


def _sgu(x, mod_l, norm_g, p, seq_len):
    t, d = x.shape
    tm = TOKEN_TILE
    tiles_per_seq = seq_len // tm
    return pl.pallas_call(
        functools.partial(_sgu_kernel, tm=tm),
        grid=(t // tm,),
        in_specs=[pl.BlockSpec((tm, d), lambda i: (i, 0)),
                  pl.BlockSpec((None, 6, d), lambda i: (i // tiles_per_seq, 0, 0)),
                  _resident((1, d)),
                  _resident((d, 2 * SGU_DIM)), _resident((1, 2 * SGU_DIM)),
                  _resident((1, SGU_DIM)), _resident((1, SGU_DIM)),
                  _resident((SGU_GROUPS, SGU_CHUNK, SGU_CHUNK)), _resident((SGU_CHUNK, SGU_DIM))],
        out_specs=pl.BlockSpec((tm, SGU_DIM), lambda i: (i, 0)),
        out_shape=jax.ShapeDtypeStruct((t, SGU_DIM), BF16),
        compiler_params=_params("parallel"),
        name="sgu",
    )(x, mod_l, norm_g.reshape(1, d), p["win"], p["bin"], p["ng"], p["nb"], p["ws"], p["bs"])


def _qkv_kernel(x_ref, mod_ref, ng_ref, w_ref, q_ref, k_ref, v_ref):
    d = D_MODEL
    h = _norm_mod(x_ref[...], ng_ref[...], mod_ref[1:2, :], mod_ref[0:1, :]).astype(BF16)
    qkv = _dot(h, w_ref[...])
    q_ref[...] = (qkv[:, :d] * (DIFF_HEAD ** -0.5)).astype(BF16)
    k_ref[...] = qkv[:, d:2 * d].astype(BF16)
    v_ref[...] = qkv[:, 2 * d:].astype(BF16)


def _qkv(x, mod_l, norm_g, w_qkv, seq_len):
    t, d = x.shape
    tm = TOKEN_TILE
    tiles_per_seq = seq_len // tm
    tile = pl.BlockSpec((tm, d), lambda i: (i, 0))
    return pl.pallas_call(
        _qkv_kernel,
        grid=(t // tm,),
        in_specs=[tile, pl.BlockSpec((None, 6, d), lambda i: (i // tiles_per_seq, 0, 0)),
                  _resident((1, d)), _resident((d, 3 * d))],
        out_specs=[tile] * 3,
        out_shape=[jax.ShapeDtypeStruct((t, d), BF16)] * 3,
        compiler_params=_params("parallel"),
        name="diff_qkv",
    )(x, mod_l, norm_g.reshape(1, d), w_qkv)


def _attn_kernel(q_ref, k_ref, v_ref, win_ref, lam_ref, sg_ref, o_ref, vext_ref, *, tq, seq_len,
                 out_scale):
    @pl.when(pl.program_id(2) == 0)
    def _():
        col = lax.broadcasted_iota(jnp.int32, (seq_len, LANES), 1)
        vext_ref[:, :LANES] = v_ref[...]
        vext_ref[:, LANES:] = jnp.where(col == 0, 1.0, 0.0).astype(BF16)

    q = q_ref[...]
    lane = lax.broadcasted_iota(jnp.int32, (1, LANES), 1)
    zero = jnp.zeros_like(q)
    qs = jnp.concatenate([jnp.where(lane < DIFF_HEAD, q, zero),
                          jnp.where(lane < DIFF_HEAD, zero, q)], axis=0)
    s = _dot_nt(qs, k_ref[...])
    width = seq_len + tq
    window = jnp.broadcast_to(win_ref[...], (tq, width))
    bias = pltpu.roll(window, width - tq + 1, 1, stride=1, stride_axis=0)[:, :seq_len]

    def unnormalised(x):
        e = jnp.exp(x - jnp.max(x, axis=-1, keepdims=True)).astype(BF16)
        pv = _dot(e, vext_ref[...])
        return pv[:, :LANES], pv[:, LANES:LANES + 1]

    o1, l1 = unnormalised(s[:tq] + bias)
    o2, l2 = unnormalised(s[tq:] + bias)
    o = o1 / l1 - lam_ref[...] * (o2 / l2)
    o = o * lax.rsqrt(jnp.mean(o * o, axis=-1, keepdims=True) + 1e-5) * sg_ref[...]
    o_ref[...] = (o * out_scale).astype(BF16)


def _t5_bucket(rel):
    nb = NUM_BUCKETS // 2
    max_exact = nb // 2
    ret = jnp.where(rel > 0, nb, 0)
    n = jnp.abs(rel)
    nf = jnp.maximum(n, 1).astype(F32)
    large = max_exact + (jnp.log(nf / max_exact) / math.log(MAX_DISTANCE / max_exact)
                         * (nb - max_exact)).astype(jnp.int32)
    large = jnp.minimum(large, nb - 1)
    return ret + jnp.where(n < max_exact, n, large)


def _bias_windows(rel_bias, seq_len, tq):
    nqb = seq_len // tq
    j = jnp.arange(seq_len + tq, dtype=jnp.int32)[None, :]
    q_hi = (jnp.arange(nqb, dtype=jnp.int32)[:, None] + 1) * tq
    bucket = _t5_bucket(j - q_hi + 1)
    return jnp.transpose(rel_bias[bucket], (0, 2, 1))[:, :, None, :].astype(F32)


def _diff_attention(q, k, v, windows, lam, subln_g, lambda_init, seq_len):
    t, d = q.shape
    nb = t // seq_len
    tq = ATTN_TQ
    nqb = seq_len // tq
    kv = pl.BlockSpec((seq_len, LANES), lambda b, h, i: (b, h))
    qo = pl.BlockSpec((tq, LANES), lambda b, h, i: (b * nqb + i, h))
    return pl.pallas_call(
        functools.partial(_attn_kernel, tq=tq, seq_len=seq_len, out_scale=1.0 - lambda_init),
        grid=(nb, DIFF_HEADS, nqb),
        in_specs=[qo, kv, kv,
                  pl.BlockSpec((None, None, 1, seq_len + tq), lambda b, h, i: (i, h, 0, 0)),
                  pl.BlockSpec((1, 1), lambda b, h, i: (0, 0)),
                  pl.BlockSpec((1, LANES), lambda b, h, i: (0, 0))],
        out_specs=qo,
        out_shape=jax.ShapeDtypeStruct((t, d), BF16),
        scratch_shapes=[pltpu.VMEM((seq_len, 2 * LANES), BF16)],
        compiler_params=_params("parallel", "parallel", "arbitrary"),
        name="diff_attention",
    )(q, k, v, windows, lam.reshape(1, 1), subln_g.reshape(1, LANES))


def _post_kernel(pre_ref, w_ref, b_ref, x_ref, mod_ref, ng_ref, wrh_ref, wrl_ref, br_ref,
                 x1_ref, h2_ref, lg_ref):
    out = _dot(pre_ref[...], w_ref[...]) + b_ref[...]
    x1 = x_ref[...] + mod_ref[2:3, :] * out
    x1_ref[...] = x1
    h2 = _norm_mod(x1, ng_ref[...], mod_ref[4:5, :], mod_ref[3:4, :])
    h2_hi, h2_lo = _split_bf16(h2)
    h2_ref[...] = h2_hi
    lg_ref[...] = (_dot(h2_hi, wrh_ref[...]) + (_dot(h2_hi, wrl_ref[...]) + _dot(h2_lo, wrh_ref[...]))
                   + br_ref[...])


def _post(pre, w, b, x, mod_l, norm_g2, w_router, b_router, seq_len):
    t, d = x.shape
    din = pre.shape[1]
    tm = TOKEN_TILE
    tiles_per_seq = seq_len // tm
    tile = pl.BlockSpec((tm, d), lambda i: (i, 0))
    return pl.pallas_call(
        _post_kernel,
        grid=(t // tm,),
        in_specs=[pl.BlockSpec((tm, din), lambda i: (i, 0)), _resident((din, d)), _resident((1, d)),
                  tile, pl.BlockSpec((None, 6, d), lambda i: (i // tiles_per_seq, 0, 0)),
                  _resident((1, d)), _resident((d, ROUTER_COLS)), _resident((d, ROUTER_COLS)),
                  _resident((1, ROUTER_COLS))],
        out_specs=[tile, tile, pl.BlockSpec((tm, ROUTER_COLS), lambda i: (i, 0))],
        out_shape=[jax.ShapeDtypeStruct((t, d), F32), jax.ShapeDtypeStruct((t, d), BF16),
                   jax.ShapeDtypeStruct((t, ROUTER_COLS), F32)],
        compiler_params=_params("parallel"),
        name="post_router",
    )(pre, w, b, x, mod_l, norm_g2.reshape(1, d), *_split_bf16(w_router), b_router)


def _expert_kernel(be_ref, nu_ref, x_ref, wg_ref, wu_ref, wd_ref, o_ref, wgb_ref, wub_ref, wdb_ref):
    i = pl.program_id(0)
    used = i < nu_ref[0]
    new_expert = (i == 0) | (be_ref[i] != be_ref[jnp.maximum(i - 1, 0)])

    @pl.when(used & new_expert)
    def _():
        wgb_ref[...] = wg_ref[...].astype(BF16)
        wub_ref[...] = wu_ref[...].astype(BF16)
        wdb_ref[...] = wd_ref[...].astype(BF16)

    @pl.when(used)
    def _():
        x = x_ref[...]
        hg = _dot(x, wgb_ref[...])
        hu = _dot(x, wub_ref[...])
        act = hg * _sigmoid(hg) * hu
        o_ref[...] = _dot(act.astype(BF16), wdb_ref[...])

    @pl.when(i >= nu_ref[0])
    def _():
        o_ref[...] = jnp.zeros_like(o_ref)


def _experts(xs, block_e, n_used, wg, wu, wd):
    p_rows, d = xs.shape
    n_blocks = p_rows // MOE_BLOCK
    grid_spec = pltpu.PrefetchScalarGridSpec(
        num_scalar_prefetch=2,
        grid=(n_blocks,),
        in_specs=[pl.BlockSpec((MOE_BLOCK, d), lambda i, be, nu: (i, 0)),
                  pl.BlockSpec((None, d, EXPERT_DIM), lambda i, be, nu: (be[i], 0, 0)),
                  pl.BlockSpec((None, d, EXPERT_DIM), lambda i, be, nu: (be[i], 0, 0)),
                  pl.BlockSpec((None, EXPERT_DIM, d), lambda i, be, nu: (be[i], 0, 0))],
        out_specs=pl.BlockSpec((MOE_BLOCK, d), lambda i, be, nu: (i, 0)),
        scratch_shapes=[pltpu.VMEM((d, EXPERT_DIM), BF16), pltpu.VMEM((d, EXPERT_DIM), BF16),
                        pltpu.VMEM((EXPERT_DIM, d), BF16)],
    )
    return pl.pallas_call(
        _expert_kernel,
        grid_spec=grid_spec,
        out_shape=jax.ShapeDtypeStruct((p_rows, d), F32),
        compiler_params=_params("arbitrary"),
        name="moe_experts",
    )(block_e, n_used, xs, wg, wu, wd)


def _combine_kernel(x_ref, y0_ref, y1_ref, gate_ref, mod_ref, o_ref):
    gate = gate_ref[...]
    moe = gate[:, 0:1] * y0_ref[...] + gate[:, 1:2] * y1_ref[...]
    o_ref[...] = x_ref[...] + mod_ref[5:6, :] * moe


def _combine(x1, y0, y1, gate, mod_l, seq_len):
    t, d = x1.shape
    tm = TOKEN_TILE
    tiles_per_seq = seq_len // tm
    tile = pl.BlockSpec((tm, d), lambda i: (i, 0))
    return pl.pallas_call(
        _combine_kernel,
        grid=(t // tm,),
        in_specs=[tile, tile, tile, pl.BlockSpec((tm, MOE_TOP_K), lambda i: (i, 0)),
                  pl.BlockSpec((None, 6, d), lambda i: (i // tiles_per_seq, 0, 0))],
        out_specs=tile,
        out_shape=jax.ShapeDtypeStruct((t, d), F32),
        compiler_params=_params("parallel"),
        name="moe_combine",
    )(x1, y0, y1, gate, mod_l)


def _route(logits):
    t = logits.shape[0]
    g_logits = logits[:, :MOE_GROUPS]
    e_logits = logits[:, MOE_GROUPS:MOE_GROUPS + N_EXPERTS].reshape(t, MOE_GROUPS, EXPERTS_PER_GROUP)
    grp = jnp.argmax(g_logits, axis=-1)
    p_grp = 1.0 / jnp.sum(jnp.exp(g_logits - jnp.max(g_logits, axis=-1, keepdims=True)), axis=-1)
    e_in = jnp.take_along_axis(e_logits, grp[:, None, None], axis=1)[:, 0]
    top_v, top_i = lax.top_k(e_in, MOE_TOP_K)
    gate = p_grp[:, None] * jax.nn.softmax(top_v, axis=-1)
    expert = grp[:, None].astype(jnp.int32) * EXPERTS_PER_GROUP + top_i.astype(jnp.int32)
    return expert, gate


def _dispatch_plan(expert):
    t = expert.shape[0]
    a = t * MOE_TOP_K
    e = expert.reshape(a)
    onehot = (e[:, None] == jnp.arange(N_EXPERTS, dtype=jnp.int32)[None, :]).astype(jnp.int32)
    csum = jnp.cumsum(onehot, axis=0)
    rank = jnp.take_along_axis(csum, e[:, None], axis=1)[:, 0] - 1
    counts = csum[-1]
    padded = (counts + MOE_BLOCK - 1) // MOE_BLOCK * MOE_BLOCK
    end_pad = jnp.cumsum(padded)
    start_pad = end_pad - padded
    dest = (start_pad[e] + rank).astype(jnp.int32)
    n_blocks = (a + N_EXPERTS * (MOE_BLOCK - 1) + MOE_BLOCK - 1) // MOE_BLOCK
    row_tok = jnp.zeros((n_blocks * MOE_BLOCK,), jnp.int32).at[dest].set(
        jnp.arange(a, dtype=jnp.int32) // MOE_TOP_K)
    block_e = jnp.searchsorted(end_pad, jnp.arange(n_blocks, dtype=jnp.int32) * MOE_BLOCK,
                               side="right")
    block_e = jnp.minimum(block_e, N_EXPERTS - 1).astype(jnp.int32)
    n_used = (end_pad[-1] // MOE_BLOCK).astype(jnp.int32).reshape(1)
    return dest.reshape(t, MOE_TOP_K), row_tok, block_e, n_used


def _moe(x1, h2, logits, mod_l, wg, wu, wd, seq_len):
    expert, gate = _route(logits)
    dest, row_tok, block_e, n_used = _dispatch_plan(expert)
    ys = _experts(h2[row_tok], block_e, n_used, wg, wu, wd)
    return _combine(x1, ys[dest[:, 0]], ys[dest[:, 1]], gate, mod_l, seq_len)


def _final_kernel(x_ref, g_ref, o_ref):
    x = x_ref[...]
    o_ref[...] = x * lax.rsqrt(jnp.mean(x * x, axis=-1, keepdims=True) + NORM_EPS) * g_ref[...]


def _final_norm(x, g):
    t, d = x.shape
    tm = TOKEN_TILE
    tile = pl.BlockSpec((tm, d), lambda i: (i, 0))
    return pl.pallas_call(
        _final_kernel,
        grid=(t // tm,),
        in_specs=[tile, _resident((1, d))],
        out_specs=tile,
        out_shape=jax.ShapeDtypeStruct((t, d), F32),
        compiler_params=_params("parallel"),
        name="final_norm",
    )(x, g.reshape(1, d))


def _pad_cols(w, n):
    return jnp.pad(w, ((0, 0), (0, n - w.shape[1])))


def _pad_rows(w, n):
    return jnp.pad(w, ((0, n - w.shape[0]), (0, 0)))


def _rwkv_params(j, mu, wr, wk, wv, w0, w1, w2, a0, a1, a2, v0, v1, v2, g1, g2, kk, ka, rk,
                 lnw, lnb):
    d = D_MODEL
    p = {
        "mu": mu[j],
        "wr": wr[j].astype(BF16), "wk": wk[j].astype(BF16), "wv": wv[j].astype(BF16),
        "wl1": jnp.concatenate([w1[j, 0], w1[j, 1]], axis=1).astype(BF16),
        "wl2": _block_diag2(w2[j, 0], w2[j, 1]).astype(BF16),
        "al1": jnp.concatenate([a1[j, 0], a1[j, 1]], axis=1).astype(BF16),
        "al2": _block_diag2(a2[j, 0], a2[j, 1]).astype(BF16),
        "g1": g1[j].astype(BF16), "g2": g2[j].astype(BF16),
        "w0": w0[j].reshape(1, 2 * d), "a0": a0[j].reshape(1, 2 * d),
        "kk": kk[j].reshape(1, d), "ka": ka[j].reshape(1, d), "rk": rk[j].reshape(1, d),
        "lnw": lnw[j].reshape(1, d), "lnb": lnb[j].reshape(1, d),
    }
    if j > 0:
        p["v0"] = v0[j - 1].reshape(1, d)
        p["v1"] = _pad_cols(v1[j - 1], LANES).astype(BF16)
        p["v2"] = _pad_rows(v2[j - 1], LANES).astype(BF16)
    return p


def kernel(x_prompt, x_sample, c_prompt, c_sample, ada_w, ada_b, norm_g, final_g, rwkv_mu, rwkv_wr, rwkv_wk, rwkv_wv, rwkv_wo, rwkv_w0, rwkv_w1, rwkv_w2, rwkv_a0, rwkv_a1, rwkv_a2, rwkv_v0, rwkv_v1, rwkv_v2, rwkv_g1, rwkv_g2, rwkv_kk, rwkv_ka, rwkv_rk, rwkv_lnw, rwkv_lnb, sgu_win, sgu_bin, sgu_ng, sgu_nb, sgu_ws, sgu_bs, sgu_wout, sgu_bout, diff_wqkv, diff_wo, diff_lq1, diff_lk1, diff_lq2, diff_lk2, diff_subln, rel_bias, moe_wrg, moe_brg, moe_wre, moe_bre, moe_wg, moe_wu, moe_wd):
    d = D_MODEL
    nb_p, seq_len, _ = x_prompt.shape
    assert x_sample.shape[1] == seq_len
    t_p = nb_p * seq_len
    x = jnp.concatenate([x_prompt.reshape(-1, d), x_sample.reshape(-1, d)], axis=0)
    c = jnp.concatenate([c_prompt, c_sample], axis=0)
    mod = _modulation(c, ada_w, ada_b)
    zero_bias = jnp.zeros((1, d), F32)
    v_first = None
    for i in range(DEPTH):
        j = i // N_MIXERS
        mod_l = mod[i]
        if i % N_MIXERS == 0:
            p = _rwkv_params(j, rwkv_mu, rwkv_wr, rwkv_wk, rwkv_wv, rwkv_w0, rwkv_w1, rwkv_w2,
                             rwkv_a0, rwkv_a1, rwkv_a2, rwkv_v0, rwkv_v1, rwkv_v2, rwkv_g1, rwkv_g2,
                             rwkv_kk, rwkv_ka, rwkv_rk, rwkv_lnw, rwkv_lnb)
            r, k, v, g, kk, wlf, wlb, af, ab = _rwkv_pre(x, mod_l, norm_g[i, 0], p, v_first, seq_len)
            if v_first is None:
                v_first = v
            y = _wkv(r, k, v, kk, wlf, wlb, af, ab, p["ka"], seq_len)
            pre = _rwkv_mid(y, r, k, v, af, ab, g, p)
            w_out, b_out = rwkv_wo[j].astype(BF16), zero_bias
        elif i % N_MIXERS == 1:
            p = {"win": sgu_win[j].astype(BF16), "bin": sgu_bin[j].reshape(1, -1),
                 "ng": sgu_ng[j].reshape(1, -1), "nb": sgu_nb[j].reshape(1, -1),
                 "ws": sgu_ws[j].astype(BF16),
                 "bs": jnp.repeat(jnp.transpose(sgu_bs[j]), SGU_GC, axis=1)}
            pre = _sgu(x, mod_l, norm_g[i, 0], p, seq_len)
            w_out, b_out = sgu_wout[j].astype(BF16), sgu_bout[j].reshape(1, d)
        else:
            lambda_init = 0.8 - 0.6 * math.exp(-0.3 * i)
            lam = (jnp.exp(jnp.sum(diff_lq1[j] * diff_lk1[j])) - jnp.exp(jnp.sum(diff_lq2[j] * diff_lk2[j]))
                   + lambda_init)
            q, k, v = _qkv(x, mod_l, norm_g[i, 0], diff_wqkv[j].astype(BF16), seq_len)
            windows = _bias_windows(rel_bias, seq_len, ATTN_TQ)
            pre = _diff_attention(q, k, v, windows, lam, diff_subln[j], lambda_init, seq_len)
            w_out, b_out = diff_wo[j].astype(BF16), zero_bias
        w_router = _pad_cols(jnp.concatenate([moe_wrg[i], moe_wre[i]], axis=1), ROUTER_COLS)
        b_router = _pad_cols(jnp.concatenate([moe_brg[i], moe_bre[i]])[None, :], ROUTER_COLS)
        x1, h2, logits = _post(pre, w_out, b_out, x, mod_l, norm_g[i, 1], w_router, b_router, seq_len)
        x = _moe(x1, h2, logits, mod_l, moe_wg[i], moe_wu[i], moe_wd[i], seq_len)
    y = _final_norm(x, final_g)
    return (y[:t_p].reshape(x_prompt.shape), y[t_p:].reshape(x_sample.shape))
```

```python
import functools
import math

import numpy as np
import jax
import jax.numpy as jnp
from jax import lax
from jax.experimental import pallas as pl
from jax.experimental.pallas import tpu as pltpu

F32 = jnp.float32
BF16 = jnp.bfloat16
HIGHEST = lax.Precision.HIGHEST

D_MODEL = 1024
DEPTH = 4
N_MIXERS = 3
NORM_EPS = 1e-6
LANES = 128
VMEM_LIMIT_BYTES = 56 * 1024 * 1024

RWKV_HEAD = 64
RWKV_HEADS = D_MODEL // RWKV_HEAD
RWKV_PAIRS = D_MODEL // LANES
RWKV_GN_EPS = 64e-5
WKV_CHUNK = 64
WKV_PAIRS_PER_STEP = 2
WKV_CHUNKS_PER_STEP = 2

SGU_CHUNK = 128
SGU_DIM = 2 * D_MODEL
SGU_GROUPS = 8
SGU_GC = SGU_DIM // SGU_GROUPS

DIFF_HEADS = 8
DIFF_HEAD = 64
NUM_BUCKETS = 32
MAX_DISTANCE = 128
ATTN_TQ = 256

MOE_GROUPS = 4
EXPERTS_PER_GROUP = 8
N_EXPERTS = MOE_GROUPS * EXPERTS_PER_GROUP
MOE_TOP_K = 2
EXPERT_DIM = D_MODEL // 2
MOE_BLOCK = 256
ROUTER_COLS = LANES
RANK_TILE = 512

TOKEN_TILE = 256


def _params(*semantics):
    return pltpu.CompilerParams(dimension_semantics=semantics, vmem_limit_bytes=VMEM_LIMIT_BYTES)


def _resident(shape):
    zeros = (0,) * len(shape)
    return pl.BlockSpec(shape, lambda *_: zeros, pipeline_mode=pl.Buffered(1))


def _dot(a, b, precision=None):
    return jnp.dot(a, b, preferred_element_type=F32, precision=precision)


def _dot_nt(a, b, precision=None):
    return lax.dot_general(a, b, (((1,), (1,)), ((), ())), preferred_element_type=F32,
                           precision=precision)


def _dot_tn(a, b, precision=None):
    return lax.dot_general(a, b, (((0,), (0,)), ((), ())), preferred_element_type=F32,
                           precision=precision)


def _split_bf16(x):
    hi = x.astype(BF16)
    return hi, (x - hi.astype(F32)).astype(BF16)


def _head_sums(x, hsum, hexp):
    hi, lo = _split_bf16(x)
    shi, slo = _split_bf16(_dot(hi, hsum) + _dot(lo, hsum))
    return _dot(shi, hexp) + _dot(slo, hexp)


def _sigmoid(x):
    return 1.0 / (1.0 + jnp.exp(-x))


def _norm_mod(x, g, scale, shift, eps=NORM_EPS):
    y = x * lax.rsqrt(jnp.mean(x * x, axis=-1, keepdims=True) + eps)
    return (y * g) * (1.0 + scale) + shift


def _mod_kernel(c_ref, w_ref, b_ref, o_ref):
    c = c_ref[...]
    o_ref[...] = _dot(c * _sigmoid(c), w_ref[...], HIGHEST) + b_ref[...]


def _modulation(c, ada_w, ada_b):
    nb = c.shape[0]
    depth, d, n = ada_w.shape
    out = pl.pallas_call(
        _mod_kernel,
        grid=(depth, n // d),
        in_specs=[pl.BlockSpec((nb, d), lambda l, j: (0, 0)),
                  pl.BlockSpec((None, d, d), lambda l, j: (l, 0, j)),
                  pl.BlockSpec((None, 1, d), lambda l, j: (l, 0, j))],
        out_specs=pl.BlockSpec((None, nb, d), lambda l, j: (l, 0, j)),
        out_shape=jax.ShapeDtypeStruct((depth, nb, n), F32),
        compiler_params=_params("parallel", "parallel"),
        name="adaln_modulation",
    )(c, ada_w, ada_b.reshape(depth, 1, n))
    return out.reshape(depth, nb, n // d, d)


def _rwkv_pre_kernel(*refs, tm, tiles_per_seq, has_vres):
    if has_vres:
        (x_ref, xp_ref, xn_ref, mod_ref, ng_ref, mu_ref, wr_ref, wk_ref, wv_ref, wl1_ref, wl2_ref,
         al1_ref, al2_ref, g1_ref, g2_ref, w0_ref, a0_ref, kscale_ref, ka_ref, rk_ref, hsum_ref,
         hexp_ref, vf_ref, v0_ref, v1_ref, v2_ref,
         r_ref, k_ref, v_ref, g_ref, kk_ref, wlf_ref, wlb_ref, af_ref, ab_ref, bonus_ref) = refs
    else:
        (x_ref, xp_ref, xn_ref, mod_ref, ng_ref, mu_ref, wr_ref, wk_ref, wv_ref, wl1_ref, wl2_ref,
         al1_ref, al2_ref, g1_ref, g2_ref, w0_ref, a0_ref, kscale_ref, ka_ref, rk_ref, hsum_ref,
         hexp_ref,
         r_ref, k_ref, v_ref, g_ref, kk_ref, wlf_ref, wlb_ref, af_ref, ab_ref, bonus_ref) = refs
    d = D_MODEL
    i = pl.program_id(0)
    pos = i % tiles_per_seq
    shift, scale = mod_ref[0:1, :], mod_ref[1:2, :]
    ng = ng_ref[...]
    h = _norm_mod(x_ref[...], ng, scale, shift)
    h_prev = jnp.where(pos == 0, 0.0, _norm_mod(xp_ref[7:8, :], ng, scale, shift))
    h_next = jnp.where(pos == tiles_per_seq - 1, 0.0, _norm_mod(xn_ref[0:1, :], ng, scale, shift))
    row = lax.broadcasted_iota(jnp.int32, (tm, 1), 0)
    h_up = jnp.where(row == 0, h_prev, pltpu.roll(h, 1, 0))
    h_dn = jnp.where(row == tm - 1, h_next, pltpu.roll(h, tm - 1, 0))
    xx = 0.5 * (h_up + h_dn) - h

    def mix(n):
        return (h + xx * mu_ref[n:n + 1, :]).astype(BF16)

    xr, xw, xk, xv, xa, xg = [mix(n) for n in range(6)]
    r = _dot(xr, wr_ref[...])
    r_ref[...] = r.astype(BF16)
    k = _dot(xk, wk_ref[...])
    k_ref[...] = k.astype(BF16)
    kkr = k * kscale_ref[...]
    norm = jnp.sqrt(_head_sums(kkr * kkr, hsum_ref[...], hexp_ref[...]))
    kk_ref[...] = (kkr / jnp.maximum(norm, 1e-12)).astype(BF16)
    v = _dot(xv, wv_ref[...])
    if has_vres:
        lv = _dot(_dot(xv, v1_ref[...]).astype(BF16), v2_ref[...])
        v = v + (vf_ref[...].astype(F32) - v) * _sigmoid(v0_ref[...] + lv)
    v_ref[...] = v.astype(BF16)
    g_ref[...] = _dot(_sigmoid(_dot(xg, g1_ref[...])).astype(BF16), g2_ref[...]).astype(BF16)
    wl = w0_ref[...] + _dot(jnp.tanh(_dot(xw, wl1_ref[...])).astype(BF16), wl2_ref[...])
    w_raw = jnp.minimum(wl, 0.0) - jnp.log1p(jnp.exp(-jnp.abs(wl))) - 0.5
    logw = -jnp.exp(w_raw)
    wlf_ref[...] = logw[:, :d]
    wlb_ref[...] = logw[:, d:]
    a = _sigmoid(a0_ref[...] + _dot(_dot(xa, al1_ref[...]).astype(BF16), al2_ref[...]))
    af_ref[...] = a[:, :d].astype(BF16)
    ab_ref[...] = a[:, d:].astype(BF16)
    k_bonus = k * (1.0 + (0.5 * (a[:, :d] + a[:, d:]) - 1.0) * ka_ref[...])
    hi, lo = _split_bf16(r * k_bonus * rk_ref[...])
    bonus_ref[...] = _dot(hi, hsum_ref[...]) + _dot(lo, hsum_ref[...])


def _head_sum_matrix():
    head_of_lane = np.arange(D_MODEL) // RWKV_HEAD
    return jnp.asarray(head_of_lane[:, None] == np.arange(RWKV_HEADS)[None, :], BF16)


def _block_diag2(m0, m1):
    z = jnp.zeros_like(m0)
    return jnp.concatenate([jnp.concatenate([m0, z], 1), jnp.concatenate([z, m1], 1)], 0)


def _rwkv_pre(x, mod_l, norm_g, p, v_first, seq_len):
    t, d = x.shape
    tm = TOKEN_TILE
    tiles_per_seq = seq_len // tm
    n_tiles = t // tm
    has_vres = v_first is not None
    tile = pl.BlockSpec((tm, d), lambda i: (i, 0))
    rows8 = tm // 8
    last8 = t // 8 - 1
    in_specs = [
        tile,
        pl.BlockSpec((8, d), lambda i: (jnp.maximum(i * rows8 - 1, 0), 0)),
        pl.BlockSpec((8, d), lambda i: (jnp.minimum((i + 1) * rows8, last8), 0)),
        pl.BlockSpec((None, 6, d), lambda i: (i // tiles_per_seq, 0, 0)),
        _resident((1, d)), _resident((6, d)),
        _resident((d, d)), _resident((d, d)), _resident((d, d)),
        _resident((d, LANES)), _resident((LANES, 2 * d)),
        _resident((d, LANES)), _resident((LANES, 2 * d)),
        _resident((d, LANES)), _resident((LANES, d)),
        _resident((1, 2 * d)), _resident((1, 2 * d)),
        _resident((1, d)), _resident((1, d)), _resident((1, d)),
        _resident((d, RWKV_HEADS)), _resident((RWKV_HEADS, d)),
    ]
    hsum = _head_sum_matrix()
    args = [x, x, x, mod_l, norm_g.reshape(1, d), p["mu"],
            p["wr"], p["wk"], p["wv"], p["wl1"], p["wl2"], p["al1"], p["al2"], p["g1"], p["g2"],
            p["w0"], p["a0"], p["kk"], p["ka"], p["rk"], hsum, hsum.T]
    if has_vres:
        in_specs += [tile, _resident((1, d)), _resident((d, LANES)), _resident((LANES, d))]
        args += [v_first, p["v0"], p["v1"], p["v2"]]
    dtypes = [BF16] * 5 + [F32] * 2 + [BF16] * 2
    outs = pl.pallas_call(
        functools.partial(_rwkv_pre_kernel, tm=tm, tiles_per_seq=tiles_per_seq, has_vres=has_vres),
        grid=(n_tiles,),
        in_specs=in_specs,
        out_specs=[tile] * 9 + [pl.BlockSpec((tm, RWKV_HEADS), lambda i: (i, 0))],
        out_shape=[jax.ShapeDtypeStruct((t, d), dt) for dt in dtypes]
        + [jax.ShapeDtypeStruct((t, RWKV_HEADS), F32)],
        compiler_params=_params("parallel"),
        name="rwkv_pre",
    )(*args)
    return outs


def _wkv_kernel(r_ref, k_ref, v_ref, kk_ref, wlf_ref, wlb_ref, af_ref, ab_ref, ka_ref,
                lvl_ref, y_ref, state_ref, inv_ref, zp_ref, yp_ref, sp_ref, xa_ref, xr_ref, yb_ref,
                nrb_ref, dec_ref, *, seq_len, chunk, pairs, group):
    n_chunks = seq_len // chunk
    n_levels = int(math.log2(chunk))
    two = 2 * chunk
    lane = lax.broadcasted_iota(jnp.int32, (1, LANES), 1)
    head0 = lane < RWKV_HEAD
    ri = lax.broadcasted_iota(jnp.int32, (chunk, chunk), 0)
    ci = lax.broadcasted_iota(jnp.int32, (chunk, chunk), 1)
    si = lax.broadcasted_iota(jnp.int32, (two, two), 0)
    sj = lax.broadcasted_iota(jnp.int32, (two, two), 1)
    same_head = (si < chunk) == (sj < chunk)
    st, su = si & (chunk - 1), sj & (chunk - 1)
    eye = (si == sj).astype(F32)
    chain_defs = [(pair, reverse) for pair in range(pairs) for reverse in (False, True)]

    def rows_of(idx, reverse):
        pos = (n_chunks - 1 - idx) if reverse else idx
        return pl.ds(pl.multiple_of(pos * chunk, chunk), chunk)

    def stack(x):
        return jnp.concatenate([jnp.where(head0, x, 0.0), jnp.where(head0, 0.0, x)], axis=0)

    def a_load(q, g, idx):
        pair, reverse = chain_defs[q]
        rows = rows_of(idx, reverse)
        lanes = slice(pair * LANES, (pair + 1) * LANES)
        a = dict(q=q, g=g, rows=rows, lanes=lanes, reverse=reverse)
        a["lw"] = lw = (wlb_ref if reverse else wlf_ref)[rows, lanes]
        a["a"] = (ab_ref if reverse else af_ref)[rows, lanes].astype(F32)
        cum = ((ri <= ci) if reverse else (ri >= ci)).astype(BF16)
        lw_hi = lw.astype(BF16)
        lw_mid, lw_lo = _split_bf16(lw - lw_hi.astype(F32))
        a["logp"] = _dot(cum, lw_hi) + (_dot(cum, lw_mid) + _dot(cum, lw_lo))
        return a

    def a_scale(a):
        rows, lanes, logp, lw, av = a["rows"], a["lanes"], a["logp"], a["lw"], a["a"]
        k = k_ref[rows, lanes].astype(F32)
        kk = kk_ref[rows, lanes].astype(F32)
        kd = k * (1.0 + (av - 1.0) * ka_ref[:, lanes])
        p_in = jnp.exp(logp)
        p_out = jnp.exp(-logp)
        a["xa"] = stack(-kk * jnp.exp(logp - lw)).astype(BF16)
        a["xr"] = stack(r_ref[rows, lanes].astype(F32) * p_in).astype(BF16)
        a["yb"] = stack(kk * av * p_out).astype(BF16)
        a["yk"] = stack(kd * p_out).astype(BF16)
        a["vs"] = stack(v_ref[rows, lanes])
        a["decay"] = jnp.exp(jnp.sum(lw, axis=0, keepdims=True))

    def a_scores(a):
        if a["reverse"]:
            strict, incl = same_head & (st < su), same_head & (st <= su)
        else:
            strict, incl = same_head & (st > su), same_head & (st >= su)
        sc = _dot_nt(jnp.concatenate([a["xa"], a["xr"]], 0),
                     jnp.concatenate([a["yb"], a["yk"]], 0))
        a["m"] = sc[:two, :two].astype(BF16)
        m_ak = jnp.where(strict, sc[:two, two:], 0.0).astype(BF16)
        n_rk = jnp.where(incl, sc[two:, two:], 0.0).astype(BF16)
        a["n_rb"] = jnp.where(incl, sc[two:, :two], 0.0).astype(BF16)
        a["inv"] = eye + (a["m"] * lvl_ref[int(a["reverse"]), 0]).astype(F32)
        a["zp"] = _dot(m_ak, a["vs"])
        a["yp"] = _dot(n_rk, a["vs"])
        a["sp"] = _dot_tn(a["vs"], a["yk"])

    def a_double(a, level):
        t = a["inv"].astype(BF16)
        m_n = a["m"] * lvl_ref[int(a["reverse"]), level]
        a["inv"] = a["inv"] + _dot(_dot(t, m_n).astype(BF16), t)

    def a_store(a, slot):
        at = (slot, a["g"], a["q"])
        inv_ref[at] = a["inv"].astype(BF16)
        zp_ref[at] = a["zp"]
        yp_ref[at] = a["yp"]
        sp_ref[at] = a["sp"]
        xa_ref[at] = a["xa"]
        xr_ref[at] = a["xr"]
        yb_ref[at] = a["yb"]
        nrb_ref[at] = a["n_rb"]
        dec_ref[at] = a["decay"]

    a_stage_list = [a_scale, a_scores] + [functools.partial(a_double, level=level)
                                          for level in range(1, n_levels)]

    def b_load(b):
        s0 = state_ref[b["q"]]
        b["s0"], b["s0b"] = s0, s0.astype(BF16)
        b["z"] = _dot_nt(xa_ref[b["at"]], b["s0b"]) + zp_ref[b["at"]]

    def b_solve(b):
        b["ub"] = _dot(inv_ref[b["at"]], b["z"].astype(BF16)).astype(BF16)

    def b_out(b):
        at = b["at"]
        pair, reverse = chain_defs[b["q"]]
        ys = _dot_nt(xr_ref[at], b["s0b"]) + yp_ref[at] + _dot(nrb_ref[at], b["ub"])
        y_ref[rows_of(b["idx"], reverse), pair * LANES:(pair + 1) * LANES] += ys[:chunk] + ys[chunk:]
        state_ref[b["q"]] = (b["s0"] + sp_ref[at] + _dot_tn(b["ub"], yb_ref[at])) * dec_ref[at]

    b_stage_list = [b_load, b_solve, b_out]

    y_ref[...] = jnp.zeros_like(y_ref)
    state_ref[...] = jnp.zeros_like(state_ref)
    n_chains = len(chain_defs)
    first = [a_load(q, g, g) for g in range(group) for q in range(n_chains)]
    for stage in a_stage_list:
        for a in first:
            stage(a)
    for a in first:
        a_store(a, 0)

    def body(step, carry):
        slot = step & 1
        base = step * group
        ahead = [a_load(q, g, jnp.minimum(base + group + g, n_chunks - 1))
                 for g in range(group) for q in range(n_chains)]
        b_work = []
        for g in range(group):
            now = [dict(q=q, at=(slot, g, q), idx=base + g) for q in range(n_chains)]
            b_work += [(stage, now) for stage in b_stage_list]
        a_work = [(stage, ahead) for stage in a_stage_list]
        per_a = -(-len(b_work) // len(a_work))
        while a_work or b_work:
            for stage, items in b_work[:per_a]:
                for b in items:
                    stage(b)
            b_work = b_work[per_a:]
            if a_work:
                stage, items = a_work.pop(0)
                for a in items:
                    stage(a)
        for a in ahead:
            a_store(a, 1 - slot)
        return carry

    lax.fori_loop(0, n_chunks // group, body, 0)


def _wkv_level_masks(chunk):
    s = np.arange(2 * chunk)
    head, t = s // chunk, s % chunk
    n_levels = int(math.log2(chunk))
    masks = np.zeros((2, n_levels, 2 * chunk, 2 * chunk), np.float32)
    for d in range(2):
        pos = t if d == 0 else chunk - 1 - t
        for j in range(n_levels):
            blk = pos >> j
            masks[d, j] = ((head[:, None] == head[None, :]) & (blk[:, None] % 2 == 1)
                           & (blk[None, :] == blk[:, None] - 1))
    return masks


def _wkv(r, k, v, kk, wlf, wlb, af, ab, k_a, seq_len):
    t, d = r.shape
    nb = t // seq_len
    pairs = WKV_PAIRS_PER_STEP
    width = pairs * LANES
    n_chains = 2 * pairs
    seq = pl.BlockSpec((seq_len, width), lambda b, p: (b, p))
    par = pl.BlockSpec((1, width), lambda b, p: (0, p))
    levels = jnp.asarray(_wkv_level_masks(WKV_CHUNK), BF16)
    group = WKV_CHUNKS_PER_STEP
    parked_f32 = pltpu.VMEM((2, group, n_chains, LANES, LANES), F32)
    parked_bf16 = pltpu.VMEM((2, group, n_chains, LANES, LANES), BF16)
    return pl.pallas_call(
        functools.partial(_wkv_kernel, seq_len=seq_len, chunk=WKV_CHUNK, pairs=pairs, group=group),
        grid=(nb, d // width),
        in_specs=[seq] * 8 + [par, _resident(levels.shape)],
        out_specs=seq,
        out_shape=jax.ShapeDtypeStruct((t, d), F32),
        scratch_shapes=[pltpu.VMEM((n_chains, LANES, LANES), F32), parked_bf16] + [parked_f32] * 3
        + [parked_bf16] * 4 + [pltpu.VMEM((2, group, n_chains, 1, LANES), F32)],
        compiler_params=_params("parallel", "parallel"),
        name="wkv7_chunked",
    )(r, k, v, kk, wlf, wlb, af, ab, k_a.reshape(1, d), levels)


def _rwkv_mid_kernel(y_ref, v_ref, g_ref, bonus_ref, lnw_ref, lnb_ref, hsum_ref, hexp_ref, o_ref):
    hsum = hsum_ref[...]
    hexp = hexp_ref[...]

    def head_sum(x):
        return _head_sums(x, hsum, hexp)

    y = y_ref[...]
    yc = y - head_sum(y) * (1.0 / RWKV_HEAD)
    var = head_sum(yc * yc) * (1.0 / RWKV_HEAD)
    yn = yc * lax.rsqrt(var + RWKV_GN_EPS) * lnw_ref[...] + lnb_ref[...]
    b_hi, b_lo = _split_bf16(bonus_ref[...])
    bonus = (_dot(b_hi, hexp) + _dot(b_lo, hexp)) * v_ref[...].astype(F32)
    o_ref[...] = ((yn + bonus) * g_ref[...].astype(F32)).astype(BF16)


def _rwkv_mid(y, v, g, bonus, p):
    t, d = y.shape
    tm = TOKEN_TILE
    tile = pl.BlockSpec((tm, d), lambda i: (i, 0))
    vec = _resident((1, d))
    hsum = _head_sum_matrix()
    return pl.pallas_call(
        _rwkv_mid_kernel,
        grid=(t // tm,),
        in_specs=[tile] * 3 + [pl.BlockSpec((tm, RWKV_HEADS), lambda i: (i, 0))] + [vec] * 2
        + [_resident((d, RWKV_HEADS)), _resident((RWKV_HEADS, d))],
        out_specs=tile,
        out_shape=jax.ShapeDtypeStruct((t, d), BF16),
        compiler_params=_params("parallel"),
        name="rwkv_mid",
    )(y, v, g, bonus, p["lnw"], p["lnb"], hsum, hsum.T)


def _sgu_kernel(x_ref, mod_ref, ng_ref, win_ref, bin_ref, lg_ref, lb_ref, ws_ref, bs_ref, o_ref,
                *, tm):
    h = _norm_mod(x_ref[...], ng_ref[...], mod_ref[1:2, :], mod_ref[0:1, :]).astype(BF16)
    z = _dot(h, win_ref[...]) + bin_ref[...]
    z = 0.5 * z * (1.0 + lax.erf(z * (1.0 / math.sqrt(2.0))))
    u = z[:, :SGU_DIM]
    v = z[:, SGU_DIM:]
    vc = v - jnp.mean(v, axis=-1, keepdims=True)
    vn = vc * lax.rsqrt(jnp.mean(vc * vc, axis=-1, keepdims=True) + 1e-5) * lg_ref[...] + lb_ref[...]
    vb = vn.astype(BF16)
    for c in range(tm // SGU_CHUNK):
        rows = slice(c * SGU_CHUNK, (c + 1) * SGU_CHUNK)
        for g in range(SGU_GROUPS):
            cols = slice(g * SGU_GC, (g + 1) * SGU_GC)
            s = _dot(ws_ref[g], vb[rows, cols]) + bs_ref[:, cols]
            o_ref[rows, cols] = (u[rows, cols] * s).astype(BF16)


def _sgu(x, mod_l, norm_g, p, seq_len):
    t, d = x.shape
    tm = TOKEN_TILE
    tiles_per_seq = seq_len // tm
    return pl.pallas_call(
        functools.partial(_sgu_kernel, tm=tm),
        grid=(t // tm,),
        in_specs=[pl.BlockSpec((tm, d), lambda i: (i, 0)),
                  pl.BlockSpec((None, 6, d), lambda i: (i // tiles_per_seq, 0, 0)),
                  _resident((1, d)),
                  _resident((d, 2 * SGU_DIM)), _resident((1, 2 * SGU_DIM)),
                  _resident((1, SGU_DIM)), _resident((1, SGU_DIM)),
                  _resident((SGU_GROUPS, SGU_CHUNK, SGU_CHUNK)), _resident((SGU_CHUNK, SGU_DIM))],
        out_specs=pl.BlockSpec((tm, SGU_DIM), lambda i: (i, 0)),
        out_shape=jax.ShapeDtypeStruct((t, SGU_DIM), BF16),
        compiler_params=_params("parallel"),
        name="sgu",
    )(x, mod_l, norm_g.reshape(1, d), p["win"], p["bin"], p["ng"], p["nb"], p["ws"], p["bs"])


def _qkv_kernel(x_ref, mod_ref, ng_ref, w_ref, q_ref, k_ref, v_ref):
    d = D_MODEL
    h = _norm_mod(x_ref[...], ng_ref[...], mod_ref[1:2, :], mod_ref[0:1, :]).astype(BF16)
    qkv = _dot(h, w_ref[...])
    q_ref[...] = (qkv[:, :d] * (DIFF_HEAD ** -0.5)).astype(BF16)
    k_ref[...] = qkv[:, d:2 * d].astype(BF16)
    v_ref[...] = qkv[:, 2 * d:].astype(BF16)


def _qkv(x, mod_l, norm_g, w_qkv, seq_len):
    t, d = x.shape
    tm = TOKEN_TILE
    tiles_per_seq = seq_len // tm
    tile = pl.BlockSpec((tm, d), lambda i: (i, 0))
    return pl.pallas_call(
        _qkv_kernel,
        grid=(t // tm,),
        in_specs=[tile, pl.BlockSpec((None, 6, d), lambda i: (i // tiles_per_seq, 0, 0)),
                  _resident((1, d)), _resident((d, 3 * d))],
        out_specs=[tile] * 3,
        out_shape=[jax.ShapeDtypeStruct((t, d), BF16)] * 3,
        compiler_params=_params("parallel"),
        name="diff_qkv",
    )(x, mod_l, norm_g.reshape(1, d), w_qkv)


def _attn_kernel(q_ref, k_ref, v_ref, win_ref, lam_ref, sg_ref, o_ref, vext_ref, *, tq, seq_len,
                 out_scale):
    @pl.when(pl.program_id(2) == 0)
    def _():
        col = lax.broadcasted_iota(jnp.int32, (seq_len, LANES), 1)
        vext_ref[:, :LANES] = v_ref[...]
        vext_ref[:, LANES:] = jnp.where(col == 0, 1.0, 0.0).astype(BF16)

    q = q_ref[...]
    lane = lax.broadcasted_iota(jnp.int32, (1, LANES), 1)
    zero = jnp.zeros_like(q)
    qs = jnp.concatenate([jnp.where(lane < DIFF_HEAD, q, zero),
                          jnp.where(lane < DIFF_HEAD, zero, q)], axis=0)
    s = _dot_nt(qs, k_ref[...])
    width = seq_len + tq
    window = jnp.broadcast_to(win_ref[...], (tq, width))
    bias = pltpu.roll(window, width - tq + 1, 1, stride=1, stride_axis=0)[:, :seq_len]

    def unnormalised(x):
        e = jnp.exp(x - jnp.max(x, axis=-1, keepdims=True)).astype(BF16)
        pv = _dot(e, vext_ref[...])
        return pv[:, :LANES], pv[:, LANES:LANES + 1]

    o1, l1 = unnormalised(s[:tq] + bias)
    o2, l2 = unnormalised(s[tq:] + bias)
    o = o1 / l1 - lam_ref[...] * (o2 / l2)
    o = o * lax.rsqrt(jnp.mean(o * o, axis=-1, keepdims=True) + 1e-5) * sg_ref[...]
    o_ref[...] = (o * out_scale).astype(BF16)


def _t5_bucket(rel):
    nb = NUM_BUCKETS // 2
    max_exact = nb // 2
    ret = jnp.where(rel > 0, nb, 0)
    n = jnp.abs(rel)
    nf = jnp.maximum(n, 1).astype(F32)
    large = max_exact + (jnp.log(nf / max_exact) / math.log(MAX_DISTANCE / max_exact)
                         * (nb - max_exact)).astype(jnp.int32)
    large = jnp.minimum(large, nb - 1)
    return ret + jnp.where(n < max_exact, n, large)


def _bias_windows(rel_bias, seq_len, tq):
    nqb = seq_len // tq
    j = jnp.arange(seq_len + tq, dtype=jnp.int32)[None, :]
    q_hi = (jnp.arange(nqb, dtype=jnp.int32)[:, None] + 1) * tq
    bucket = _t5_bucket(j - q_hi + 1)
    return jnp.transpose(rel_bias[bucket], (0, 2, 1))[:, :, None, :].astype(F32)


def _diff_attention(q, k, v, windows, lam, subln_g, lambda_init, seq_len):
    t, d = q.shape
    nb = t // seq_len
    tq = ATTN_TQ
    nqb = seq_len // tq
    kv = pl.BlockSpec((seq_len, LANES), lambda b, h, i: (b, h))
    qo = pl.BlockSpec((tq, LANES), lambda b, h, i: (b * nqb + i, h))
    return pl.pallas_call(
        functools.partial(_attn_kernel, tq=tq, seq_len=seq_len, out_scale=1.0 - lambda_init),
        grid=(nb, DIFF_HEADS, nqb),
        in_specs=[qo, kv, kv,
                  pl.BlockSpec((None, None, 1, seq_len + tq), lambda b, h, i: (i, h, 0, 0)),
                  pl.BlockSpec((1, 1), lambda b, h, i: (0, 0)),
                  pl.BlockSpec((1, LANES), lambda b, h, i: (0, 0))],
        out_specs=qo,
        out_shape=jax.ShapeDtypeStruct((t, d), BF16),
        scratch_shapes=[pltpu.VMEM((seq_len, 2 * LANES), BF16)],
        compiler_params=_params("parallel", "parallel", "arbitrary"),
        name="diff_attention",
    )(q, k, v, windows, lam.reshape(1, 1), subln_g.reshape(1, LANES))


def _post_kernel(pre_ref, w_ref, b_ref, x_ref, mod_ref, ng_ref, wrh_ref, wrl_ref, br_ref,
                 x1_ref, h2_ref, lg_ref):
    out = _dot(pre_ref[...], w_ref[...]) + b_ref[...]
    x1 = x_ref[...] + mod_ref[2:3, :] * out
    x1_ref[...] = x1
    h2 = _norm_mod(x1, ng_ref[...], mod_ref[4:5, :], mod_ref[3:4, :])
    h2_hi, h2_lo = _split_bf16(h2)
    h2_ref[...] = h2_hi
    lg_ref[...] = (_dot(h2_hi, wrh_ref[...]) + (_dot(h2_hi, wrl_ref[...]) + _dot(h2_lo, wrh_ref[...]))
                   + br_ref[...])


def _post(pre, w, b, x, mod_l, norm_g2, w_router, b_router, seq_len):
    t, d = x.shape
    din = pre.shape[1]
    tm = TOKEN_TILE
    tiles_per_seq = seq_len // tm
    tile = pl.BlockSpec((tm, d), lambda i: (i, 0))
    return pl.pallas_call(
        _post_kernel,
        grid=(t // tm,),
        in_specs=[pl.BlockSpec((tm, din), lambda i: (i, 0)), _resident((din, d)), _resident((1, d)),
                  tile, pl.BlockSpec((None, 6, d), lambda i: (i // tiles_per_seq, 0, 0)),
                  _resident((1, d)), _resident((d, ROUTER_COLS)), _resident((d, ROUTER_COLS)),
                  _resident((1, ROUTER_COLS))],
        out_specs=[tile, tile, pl.BlockSpec((tm, ROUTER_COLS), lambda i: (i, 0))],
        out_shape=[jax.ShapeDtypeStruct((t, d), F32), jax.ShapeDtypeStruct((t, d), BF16),
                   jax.ShapeDtypeStruct((t, ROUTER_COLS), F32)],
        compiler_params=_params("parallel"),
        name="post_router",
    )(pre, w, b, x, mod_l, norm_g2.reshape(1, d), *_split_bf16(w_router), b_router)


def _expert_kernel(be_ref, nu_ref, x_ref, wg_ref, wu_ref, wd_ref, o_ref, wgb_ref, wub_ref, wdb_ref):
    i = pl.program_id(0)
    used = i < nu_ref[0]
    new_expert = (i == 0) | (be_ref[i] != be_ref[jnp.maximum(i - 1, 0)])

    @pl.when(used & new_expert)
    def _():
        wgb_ref[...] = wg_ref[...].astype(BF16)
        wub_ref[...] = wu_ref[...].astype(BF16)
        wdb_ref[...] = wd_ref[...].astype(BF16)

    @pl.when(used)
    def _():
        x = x_ref[...]
        hg = _dot(x, wgb_ref[...])
        hu = _dot(x, wub_ref[...])
        act = hg * _sigmoid(hg) * hu
        o_ref[...] = _dot(act.astype(BF16), wdb_ref[...]).astype(BF16)

    @pl.when(i >= nu_ref[0])
    def _():
        o_ref[...] = jnp.zeros_like(o_ref)


def _experts(xs, block_e, n_used, wg, wu, wd, layer):
    p_rows, d = xs.shape
    n_blocks = p_rows // MOE_BLOCK
    grid_spec = pltpu.PrefetchScalarGridSpec(
        num_scalar_prefetch=2,
        grid=(n_blocks,),
        in_specs=[pl.BlockSpec((MOE_BLOCK, d), lambda i, be, nu: (i, 0)),
                  pl.BlockSpec((None, None, d, EXPERT_DIM), lambda i, be, nu: (layer, be[i], 0, 0)),
                  pl.BlockSpec((None, None, d, EXPERT_DIM), lambda i, be, nu: (layer, be[i], 0, 0)),
                  pl.BlockSpec((None, None, EXPERT_DIM, d), lambda i, be, nu: (layer, be[i], 0, 0))],
        out_specs=pl.BlockSpec((MOE_BLOCK, d), lambda i, be, nu: (i, 0)),
        scratch_shapes=[pltpu.VMEM((d, EXPERT_DIM), BF16), pltpu.VMEM((d, EXPERT_DIM), BF16),
                        pltpu.VMEM((EXPERT_DIM, d), BF16)],
    )
    return pl.pallas_call(
        _expert_kernel,
        grid_spec=grid_spec,
        out_shape=jax.ShapeDtypeStruct((p_rows, d), BF16),
        compiler_params=_params("arbitrary"),
        name="moe_experts",
    )(block_e, n_used, xs, wg, wu, wd)


def _combine_kernel(x_ref, y0_ref, y1_ref, gate_ref, mod_ref, fg_ref, o_ref, *, final):
    gate = gate_ref[...]
    moe = gate[:, 0:1] * y0_ref[...].astype(F32) + gate[:, 1:2] * y1_ref[...].astype(F32)
    x = x_ref[...] + mod_ref[5:6, :] * moe
    if final:
        x = x * lax.rsqrt(jnp.mean(x * x, axis=-1, keepdims=True) + NORM_EPS) * fg_ref[...]
    o_ref[...] = x


def _combine(x1, y0, y1, gate, mod_l, final_g, final, seq_len):
    t, d = x1.shape
    tm = TOKEN_TILE
    tiles_per_seq = seq_len // tm
    tile = pl.BlockSpec((tm, d), lambda i: (i, 0))
    return pl.pallas_call(
        functools.partial(_combine_kernel, final=final),
        grid=(t // tm,),
        in_specs=[tile, tile, tile, pl.BlockSpec((tm, MOE_TOP_K), lambda i: (i, 0)),
                  pl.BlockSpec((None, 6, d), lambda i: (i // tiles_per_seq, 0, 0)),
                  _resident((1, d))],
        out_specs=tile,
        out_shape=jax.ShapeDtypeStruct((t, d), F32),
        compiler_params=_params("parallel"),
        name="moe_combine",
    )(x1, y0, y1, gate, mod_l, final_g.reshape(1, d))


def _route(logits):
    t = logits.shape[0]
    g_logits = logits[:, :MOE_GROUPS]
    e_logits = logits[:, MOE_GROUPS:MOE_GROUPS + N_EXPERTS].reshape(t, MOE_GROUPS, EXPERTS_PER_GROUP)
    grp = jnp.argmax(g_logits, axis=-1)
    p_grp = 1.0 / jnp.sum(jnp.exp(g_logits - jnp.max(g_logits, axis=-1, keepdims=True)), axis=-1)
    e_in = jnp.take_along_axis(e_logits, grp[:, None, None], axis=1)[:, 0]
    top_v, top_i = lax.top_k(e_in, MOE_TOP_K)
    gate = p_grp[:, None] * jax.nn.softmax(top_v, axis=-1)
    expert = grp[:, None].astype(jnp.int32) * EXPERTS_PER_GROUP + top_i.astype(jnp.int32)
    return expert, gate


def _rank_kernel(e_ref, rank_ref, cnt_ref, carry_ref):
    @pl.when(pl.program_id(0) == 0)
    def _():
        carry_ref[...] = jnp.zeros_like(carry_ref)

    e = e_ref[...]
    expert_id = lax.broadcasted_iota(jnp.int32, (N_EXPERTS, RANK_TILE), 0)
    onehot = jnp.where(e == expert_id, 1.0, 0.0)
    ri = lax.broadcasted_iota(jnp.int32, (RANK_TILE, RANK_TILE), 0)
    ci = lax.broadcasted_iota(jnp.int32, (RANK_TILE, RANK_TILE), 1)
    prefix = _dot(onehot.astype(BF16), (ri <= ci).astype(BF16))
    carry = carry_ref[...]
    rank = jnp.sum(onehot * (prefix + carry), axis=0, keepdims=True) - 1.0
    rank_ref[...] = rank.astype(jnp.int32)
    carry = carry + jnp.sum(onehot, axis=1, keepdims=True)
    carry_ref[...] = carry
    cnt_ref[...] = jnp.broadcast_to(carry, cnt_ref.shape).astype(jnp.int32)


def _rank(e):
    a = e.shape[0]
    n_tiles = a // RANK_TILE
    tile = pl.BlockSpec((None, 1, RANK_TILE), lambda i: (i, 0, 0))
    rank, counts = pl.pallas_call(
        _rank_kernel,
        grid=(n_tiles,),
        in_specs=[tile],
        out_specs=[tile, pl.BlockSpec((N_EXPERTS, LANES), lambda i: (0, 0))],
        out_shape=[jax.ShapeDtypeStruct((n_tiles, 1, RANK_TILE), jnp.int32),
                   jax.ShapeDtypeStruct((N_EXPERTS, LANES), jnp.int32)],
        scratch_shapes=[pltpu.VMEM((N_EXPERTS, 1), F32)],
        compiler_params=_params("arbitrary"),
        name="moe_rank",
    )(e.reshape(n_tiles, 1, RANK_TILE))
    return rank.reshape(a), counts[:, 0]


def _dispatch_plan(expert):
    t = expert.shape[0]
    a = t * MOE_TOP_K
    e = expert.reshape(a)
    rank, counts = _rank(e)
    padded = (counts + MOE_BLOCK - 1) // MOE_BLOCK * MOE_BLOCK
    end_pad = jnp.cumsum(padded)
    start_pad = end_pad - padded
    dest = (start_pad[e] + rank).astype(jnp.int32)
    n_blocks = (a + N_EXPERTS * (MOE_BLOCK - 1) + MOE_BLOCK - 1) // MOE_BLOCK
    row_tok = jnp.zeros((n_blocks * MOE_BLOCK,), jnp.int32).at[dest].set(
        jnp.arange(a, dtype=jnp.int32) // MOE_TOP_K, unique_indices=True, mode="promise_in_bounds")
    block_e = jnp.searchsorted(end_pad, jnp.arange(n_blocks, dtype=jnp.int32) * MOE_BLOCK,
                               side="right")
    block_e = jnp.minimum(block_e, N_EXPERTS - 1).astype(jnp.int32)
    n_used = (end_pad[-1] // MOE_BLOCK).astype(jnp.int32).reshape(1)
    return dest.reshape(t, MOE_TOP_K), row_tok, block_e, n_used


def _moe(x1, h2, logits, mod_l, wg, wu, wd, layer, final_g, seq_len):
    expert, gate = _route(logits)
    dest, row_tok, block_e, n_used = _dispatch_plan(expert)
    ys = _experts(h2[row_tok], block_e, n_used, wg, wu, wd, layer)
    final = layer == wg.shape[0] - 1
    return _combine(x1, ys[dest[:, 0]], ys[dest[:, 1]], gate, mod_l, final_g, final, seq_len)


def _pad_cols(w, n):
    return jnp.pad(w, ((0, 0), (0, n - w.shape[1])))


def _pad_rows(w, n):
    return jnp.pad(w, ((0, n - w.shape[0]), (0, 0)))


def _rwkv_params(j, mu, wr, wk, wv, w0, w1, w2, a0, a1, a2, v0, v1, v2, g1, g2, kk, ka, rk,
                 lnw, lnb):
    d = D_MODEL
    p = {
        "mu": mu[j],
        "wr": wr[j].astype(BF16), "wk": wk[j].astype(BF16), "wv": wv[j].astype(BF16),
        "wl1": jnp.concatenate([w1[j, 0], w1[j, 1]], axis=1).astype(BF16),
        "wl2": _block_diag2(w2[j, 0], w2[j, 1]).astype(BF16),
        "al1": jnp.concatenate([a1[j, 0], a1[j, 1]], axis=1).astype(BF16),
        "al2": _block_diag2(a2[j, 0], a2[j, 1]).astype(BF16),
        "g1": g1[j].astype(BF16), "g2": g2[j].astype(BF16),
        "w0": w0[j].reshape(1, 2 * d), "a0": a0[j].reshape(1, 2 * d),
        "kk": kk[j].reshape(1, d), "ka": ka[j].reshape(1, d), "rk": rk[j].reshape(1, d),
        "lnw": lnw[j].reshape(1, d), "lnb": lnb[j].reshape(1, d),
    }
    if j > 0:
        p["v0"] = v0[j - 1].reshape(1, d)
        p["v1"] = _pad_cols(v1[j - 1], LANES).astype(BF16)
        p["v2"] = _pad_rows(v2[j - 1], LANES).astype(BF16)
    return p


def kernel(x_prompt, x_sample, c_prompt, c_sample, ada_w, ada_b, norm_g, final_g, rwkv_mu, rwkv_wr, rwkv_wk, rwkv_wv, rwkv_wo, rwkv_w0, rwkv_w1, rwkv_w2, rwkv_a0, rwkv_a1, rwkv_a2, rwkv_v0, rwkv_v1, rwkv_v2, rwkv_g1, rwkv_g2, rwkv_kk, rwkv_ka, rwkv_rk, rwkv_lnw, rwkv_lnb, sgu_win, sgu_bin, sgu_ng, sgu_nb, sgu_ws, sgu_bs, sgu_wout, sgu_bout, diff_wqkv, diff_wo, diff_lq1, diff_lk1, diff_lq2, diff_lk2, diff_subln, rel_bias, moe_wrg, moe_brg, moe_wre, moe_bre, moe_wg, moe_wu, moe_wd):
    d = D_MODEL
    nb_p, seq_len, _ = x_prompt.shape
    assert x_sample.shape[1] == seq_len
    t_p = nb_p * seq_len
    x = jnp.concatenate([x_prompt.reshape(-1, d), x_sample.reshape(-1, d)], axis=0)
    c = jnp.concatenate([c_prompt, c_sample], axis=0)
    mod = _modulation(c, ada_w, ada_b)
    zero_bias = jnp.zeros((1, d), F32)
    v_first = None
    for i in range(DEPTH):
        j = i // N_MIXERS
        mod_l = mod[i]
        if i % N_MIXERS == 0:
            p = _rwkv_params(j, rwkv_mu, rwkv_wr, rwkv_wk, rwkv_wv, rwkv_w0, rwkv_w1, rwkv_w2,
                             rwkv_a0, rwkv_a1, rwkv_a2, rwkv_v0, rwkv_v1, rwkv_v2, rwkv_g1, rwkv_g2,
                             rwkv_kk, rwkv_ka, rwkv_rk, rwkv_lnw, rwkv_lnb)
            r, k, v, g, kk, wlf, wlb, af, ab, bonus = _rwkv_pre(x, mod_l, norm_g[i, 0], p, v_first,
                                                                seq_len)
            if v_first is None:
                v_first = v
            y = _wkv(r, k, v, kk, wlf, wlb, af, ab, p["ka"], seq_len)
            pre = _rwkv_mid(y, v, g, bonus, p)
            w_out, b_out = rwkv_wo[j].astype(BF16), zero_bias
        elif i % N_MIXERS == 1:
            p = {"win": sgu_win[j].astype(BF16), "bin": sgu_bin[j].reshape(1, -1),
                 "ng": sgu_ng[j].reshape(1, -1), "nb": sgu_nb[j].reshape(1, -1),
                 "ws": sgu_ws[j].astype(BF16),
                 "bs": jnp.repeat(jnp.transpose(sgu_bs[j]), SGU_GC, axis=1)}
            pre = _sgu(x, mod_l, norm_g[i, 0], p, seq_len)
            w_out, b_out = sgu_wout[j].astype(BF16), sgu_bout[j].reshape(1, d)
        else:
            lambda_init = 0.8 - 0.6 * math.exp(-0.3 * i)
            lam = (jnp.exp(jnp.sum(diff_lq1[j] * diff_lk1[j])) - jnp.exp(jnp.sum(diff_lq2[j] * diff_lk2[j]))
                   + lambda_init)
            q, k, v = _qkv(x, mod_l, norm_g[i, 0], diff_wqkv[j].astype(BF16), seq_len)
            windows = _bias_windows(rel_bias, seq_len, ATTN_TQ)
            pre = _diff_attention(q, k, v, windows, lam, diff_subln[j], lambda_init, seq_len)
            w_out, b_out = diff_wo[j].astype(BF16), zero_bias
        w_router = _pad_cols(jnp.concatenate([moe_wrg[i], moe_wre[i]], axis=1), ROUTER_COLS)
        b_router = _pad_cols(jnp.concatenate([moe_brg[i], moe_bre[i]])[None, :], ROUTER_COLS)
        x1, h2, logits = _post(pre, w_out, b_out, x, mod_l, norm_g[i, 1], w_router, b_router, seq_len)
        x = _moe(x1, h2, logits, mod_l, moe_wg, moe_wu, moe_wd, i, final_g, seq_len)
    return (x[:t_p].reshape(x_prompt.shape), x[t_p:].reshape(x_sample.shape))
```

```python
import functools
import math

import numpy as np
import jax
import jax.numpy as jnp
from jax import lax
from jax.experimental import pallas as pl
from jax.experimental.pallas import tpu as pltpu

F32 = jnp.float32
BF16 = jnp.bfloat16
HIGHEST = lax.Precision.HIGHEST

D_MODEL = 1024
DEPTH = 4
N_MIXERS = 3
NORM_EPS = 1e-6
LANES = 128
VMEM_LIMIT_BYTES = 56 * 1024 * 1024

RWKV_HEAD = 64
RWKV_HEADS = D_MODEL // RWKV_HEAD
RWKV_PAIRS = D_MODEL // LANES
RWKV_GN_EPS = 64e-5
WKV_CHUNK = 64
WKV_PAIRS_PER_STEP = 2
WKV_CHUNKS_PER_STEP = 4

SGU_CHUNK = 128
SGU_DIM = 2 * D_MODEL
SGU_GROUPS = 8
SGU_GC = SGU_DIM // SGU_GROUPS

DIFF_HEADS = 8
DIFF_HEAD = 64
NUM_BUCKETS = 32
MAX_DISTANCE = 128
ATTN_TQ = 256

MOE_GROUPS = 4
EXPERTS_PER_GROUP = 8
N_EXPERTS = MOE_GROUPS * EXPERTS_PER_GROUP
MOE_TOP_K = 2
EXPERT_DIM = D_MODEL // 2
MOE_BLOCK = 256
ROUTER_COLS = LANES
RANK_TILE = 512

TOKEN_TILE = 256


def _params(*semantics):
    return pltpu.CompilerParams(dimension_semantics=semantics, vmem_limit_bytes=VMEM_LIMIT_BYTES)


def _resident(shape):
    zeros = (0,) * len(shape)
    return pl.BlockSpec(shape, lambda *_: zeros, pipeline_mode=pl.Buffered(1))


def _dot(a, b, precision=None):
    return jnp.dot(a, b, preferred_element_type=F32, precision=precision)


def _dot_nt(a, b, precision=None):
    return lax.dot_general(a, b, (((1,), (1,)), ((), ())), preferred_element_type=F32,
                           precision=precision)


def _dot_tn(a, b, precision=None):
    return lax.dot_general(a, b, (((0,), (0,)), ((), ())), preferred_element_type=F32,
                           precision=precision)


def _split_bf16(x):
    hi = x.astype(BF16)
    return hi, (x - hi.astype(F32)).astype(BF16)


def _head_sums(x, hsum, hexp):
    hi, lo = _split_bf16(x)
    shi, slo = _split_bf16(_dot(hi, hsum) + _dot(lo, hsum))
    return _dot(shi, hexp) + _dot(slo, hexp)


def _sigmoid(x):
    return 1.0 / (1.0 + jnp.exp(-x))


def _norm_mod(x, g, scale, shift, eps=NORM_EPS):
    y = x * lax.rsqrt(jnp.mean(x * x, axis=-1, keepdims=True) + eps)
    return (y * g) * (1.0 + scale) + shift


def _mod_kernel(c_ref, w_ref, b_ref, o_ref):
    c = c_ref[...]
    o_ref[...] = _dot(c * _sigmoid(c), w_ref[...], HIGHEST) + b_ref[...]


def _modulation(c, ada_w, ada_b):
    nb = c.shape[0]
    depth, d, n = ada_w.shape
    out = pl.pallas_call(
        _mod_kernel,
        grid=(depth, n // d),
        in_specs=[pl.BlockSpec((nb, d), lambda l, j: (0, 0)),
                  pl.BlockSpec((None, d, d), lambda l, j: (l, 0, j)),
                  pl.BlockSpec((None, 1, d), lambda l, j: (l, 0, j))],
        out_specs=pl.BlockSpec((None, nb, d), lambda l, j: (l, 0, j)),
        out_shape=jax.ShapeDtypeStruct((depth, nb, n), F32),
        compiler_params=_params("parallel", "parallel"),
        name="adaln_modulation",
    )(c, ada_w, ada_b.reshape(depth, 1, n))
    return out.reshape(depth, nb, n // d, d)


def _rwkv_pre_kernel(*refs, tm, tiles_per_seq, has_vres):
    if has_vres:
        (x_ref, xp_ref, xn_ref, mod_ref, ng_ref, mu_ref, wr_ref, wk_ref, wv_ref, wl1_ref, wl2_ref,
         al1_ref, al2_ref, g1_ref, g2_ref, w0_ref, a0_ref, kscale_ref, ka_ref, rk_ref, hsum_ref,
         hexp_ref, vf_ref, v0_ref, v1_ref, v2_ref,
         r_ref, k_ref, v_ref, g_ref, kk_ref, wlf_ref, wlb_ref, af_ref, ab_ref, bonus_ref) = refs
    else:
        (x_ref, xp_ref, xn_ref, mod_ref, ng_ref, mu_ref, wr_ref, wk_ref, wv_ref, wl1_ref, wl2_ref,
         al1_ref, al2_ref, g1_ref, g2_ref, w0_ref, a0_ref, kscale_ref, ka_ref, rk_ref, hsum_ref,
         hexp_ref,
         r_ref, k_ref, v_ref, g_ref, kk_ref, wlf_ref, wlb_ref, af_ref, ab_ref, bonus_ref) = refs
    d = D_MODEL
    i = pl.program_id(0)
    pos = i % tiles_per_seq
    shift, scale = mod_ref[0:1, :], mod_ref[1:2, :]
    ng = ng_ref[...]
    h = _norm_mod(x_ref[...], ng, scale, shift)
    h_prev = jnp.where(pos == 0, 0.0, _norm_mod(xp_ref[7:8, :], ng, scale, shift))
    h_next = jnp.where(pos == tiles_per_seq - 1, 0.0, _norm_mod(xn_ref[0:1, :], ng, scale, shift))
    row = lax.broadcasted_iota(jnp.int32, (tm, 1), 0)
    h_up = jnp.where(row == 0, h_prev, pltpu.roll(h, 1, 0))
    h_dn = jnp.where(row == tm - 1, h_next, pltpu.roll(h, tm - 1, 0))
    xx = 0.5 * (h_up + h_dn) - h

    def mix(n):
        return (h + xx * mu_ref[n:n + 1, :]).astype(BF16)

    xr, xw, xk, xv, xa, xg = [mix(n) for n in range(6)]
    r = _dot(xr, wr_ref[...])
    r_ref[...] = r.astype(BF16)
    k = _dot(xk, wk_ref[...])
    k_ref[...] = k.astype(BF16)
    kkr = k * kscale_ref[...]
    norm = jnp.sqrt(_head_sums(kkr * kkr, hsum_ref[...], hexp_ref[...]))
    kk_ref[...] = (kkr / jnp.maximum(norm, 1e-12)).astype(BF16)
    v = _dot(xv, wv_ref[...])
    if has_vres:
        lv = _dot(_dot(xv, v1_ref[...]).astype(BF16), v2_ref[...])
        v = v + (vf_ref[...].astype(F32) - v) * _sigmoid(v0_ref[...] + lv)
    v_ref[...] = v.astype(BF16)
    g_ref[...] = _dot(_sigmoid(_dot(xg, g1_ref[...])).astype(BF16), g2_ref[...]).astype(BF16)
    wl = w0_ref[...] + _dot(jnp.tanh(_dot(xw, wl1_ref[...])).astype(BF16), wl2_ref[...])
    w_raw = jnp.minimum(wl, 0.0) - jnp.log1p(jnp.exp(-jnp.abs(wl))) - 0.5
    logw = -jnp.exp(w_raw)
    wlf_ref[...] = logw[:, :d]
    wlb_ref[...] = logw[:, d:]
    a = _sigmoid(a0_ref[...] + _dot(_dot(xa, al1_ref[...]).astype(BF16), al2_ref[...]))
    af_ref[...] = a[:, :d].astype(BF16)
    ab_ref[...] = a[:, d:].astype(BF16)
    k_bonus = k * (1.0 + (0.5 * (a[:, :d] + a[:, d:]) - 1.0) * ka_ref[...])
    hi, lo = _split_bf16(r * k_bonus * rk_ref[...])
    bonus_ref[...] = _dot(hi, hsum_ref[...]) + _dot(lo, hsum_ref[...])


def _head_sum_matrix():
    head_of_lane = np.arange(D_MODEL) // RWKV_HEAD
    return jnp.asarray(head_of_lane[:, None] == np.arange(RWKV_HEADS)[None, :], BF16)


def _block_diag2(m0, m1):
    z = jnp.zeros_like(m0)
    return jnp.concatenate([jnp.concatenate([m0, z], 1), jnp.concatenate([z, m1], 1)], 0)


def _rwkv_pre(x, mod_l, norm_g, p, v_first, seq_len):
    t, d = x.shape
    tm = TOKEN_TILE
    tiles_per_seq = seq_len // tm
    n_tiles = t // tm
    has_vres = v_first is not None
    tile = pl.BlockSpec((tm, d), lambda i: (i, 0))
    rows8 = tm // 8
    last8 = t // 8 - 1
    in_specs = [
        tile,
        pl.BlockSpec((8, d), lambda i: (jnp.maximum(i * rows8 - 1, 0), 0)),
        pl.BlockSpec((8, d), lambda i: (jnp.minimum((i + 1) * rows8, last8), 0)),
        pl.BlockSpec((None, 6, d), lambda i: (i // tiles_per_seq, 0, 0)),
        _resident((1, d)), _resident((6, d)),
        _resident((d, d)), _resident((d, d)), _resident((d, d)),
        _resident((d, LANES)), _resident((LANES, 2 * d)),
        _resident((d, LANES)), _resident((LANES, 2 * d)),
        _resident((d, LANES)), _resident((LANES, d)),
        _resident((1, 2 * d)), _resident((1, 2 * d)),
        _resident((1, d)), _resident((1, d)), _resident((1, d)),
        _resident((d, RWKV_HEADS)), _resident((RWKV_HEADS, d)),
    ]
    hsum = _head_sum_matrix()
    args = [x, x, x, mod_l, norm_g.reshape(1, d), p["mu"],
            p["wr"], p["wk"], p["wv"], p["wl1"], p["wl2"], p["al1"], p["al2"], p["g1"], p["g2"],
            p["w0"], p["a0"], p["kk"], p["ka"], p["rk"], hsum, hsum.T]
    if has_vres:
        in_specs += [tile, _resident((1, d)), _resident((d, LANES)), _resident((LANES, d))]
        args += [v_first, p["v0"], p["v1"], p["v2"]]
    dtypes = [BF16] * 5 + [F32] * 2 + [BF16] * 2
    outs = pl.pallas_call(
        functools.partial(_rwkv_pre_kernel, tm=tm, tiles_per_seq=tiles_per_seq, has_vres=has_vres),
        grid=(n_tiles,),
        in_specs=in_specs,
        out_specs=[tile] * 9 + [pl.BlockSpec((tm, RWKV_HEADS), lambda i: (i, 0))],
        out_shape=[jax.ShapeDtypeStruct((t, d), dt) for dt in dtypes]
        + [jax.ShapeDtypeStruct((t, RWKV_HEADS), F32)],
        compiler_params=_params("parallel"),
        name="rwkv_pre",
    )(*args)
    return outs


def _wkv_kernel(r_ref, k_ref, v_ref, kk_ref, wlf_ref, wlb_ref, af_ref, ab_ref, ka_ref,
                lvl_ref, y_ref, state_ref, g_ref, sp_ref, yp_ref, xr_ref, dec_ref,
                *, seq_len, chunk, pairs, group):
    n_chunks = seq_len // chunk
    n_levels = int(math.log2(chunk))
    two = 2 * chunk
    lane = lax.broadcasted_iota(jnp.int32, (1, LANES), 1)
    head0 = lane < RWKV_HEAD
    ri = lax.broadcasted_iota(jnp.int32, (chunk, chunk), 0)
    ci = lax.broadcasted_iota(jnp.int32, (chunk, chunk), 1)
    si = lax.broadcasted_iota(jnp.int32, (two, two), 0)
    sj = lax.broadcasted_iota(jnp.int32, (two, two), 1)
    same_head = (si < chunk) == (sj < chunk)
    st, su = si & (chunk - 1), sj & (chunk - 1)
    eye = (si == sj).astype(F32)
    chain_defs = [(pair, reverse) for pair in range(pairs) for reverse in (False, True)]

    def rows_of(idx, reverse):
        pos = (n_chunks - 1 - idx) if reverse else idx
        return pl.ds(pl.multiple_of(pos * chunk, chunk), chunk)

    def stack(x):
        return jnp.concatenate([jnp.where(head0, x, 0.0), jnp.where(head0, 0.0, x)], axis=0)

    def a_load(q, g, idx, slot):
        pair, reverse = chain_defs[q]
        rows = rows_of(idx, reverse)
        lanes = slice(pair * LANES, (pair + 1) * LANES)
        a = dict(q=q, g=g, slot=slot, rows=rows, lanes=lanes, reverse=reverse)
        a["lw"] = lw = (wlb_ref if reverse else wlf_ref)[rows, lanes]
        a["a"] = (ab_ref if reverse else af_ref)[rows, lanes].astype(F32)
        cum = ((ri <= ci) if reverse else (ri >= ci)).astype(BF16)
        lw_hi = lw.astype(BF16)
        lw_mid, lw_lo = _split_bf16(lw - lw_hi.astype(F32))
        a["logp"] = _dot(cum, lw_hi) + (_dot(cum, lw_mid) + _dot(cum, lw_lo))
        return a

    def a_scale(a):
        rows, lanes, logp, lw, av = a["rows"], a["lanes"], a["logp"], a["lw"], a["a"]
        k = k_ref[rows, lanes].astype(F32)
        kk = kk_ref[rows, lanes].astype(F32)
        kd = k * (1.0 + (av - 1.0) * ka_ref[:, lanes])
        p_in = jnp.exp(logp)
        p_out = jnp.exp(-logp)
        a["xa"] = stack(-kk * jnp.exp(logp - lw)).astype(BF16)
        a["xr"] = stack(r_ref[rows, lanes].astype(F32) * p_in).astype(BF16)
        a["yb"] = stack(kk * av * p_out).astype(BF16)
        a["yk"] = stack(kd * p_out).astype(BF16)
        a["vs"] = stack(v_ref[rows, lanes])
        a["decay"] = jnp.exp(jnp.sum(lw, axis=0, keepdims=True))

    def a_scores(a):
        if a["reverse"]:
            strict, incl = same_head & (st < su), same_head & (st <= su)
        else:
            strict, incl = same_head & (st > su), same_head & (st >= su)
        sc = _dot_nt(jnp.concatenate([a["xa"], a["xr"]], 0),
                     jnp.concatenate([a["yb"], a["yk"]], 0))
        a["m"] = sc[:two, :two].astype(BF16)
        m_ak = jnp.where(strict, sc[:two, two:], 0.0).astype(BF16)
        n_rk = jnp.where(incl, sc[two:, two:], 0.0).astype(BF16)
        a["n_rb"] = jnp.where(incl, sc[two:, :two], 0.0).astype(BF16)
        a["inv"] = eye + (a["m"] * lvl_ref[int(a["reverse"]), 0]).astype(F32)
        a["zp"] = _dot(m_ak, a["vs"]).astype(BF16)
        a["yp"] = _dot(n_rk, a["vs"])
        a["sp"] = _dot_tn(a["vs"], a["yk"])

    def a_double(a, level):
        t = a["inv"].astype(BF16)
        m_n = a["m"] * lvl_ref[int(a["reverse"]), level]
        a["inv"] = a["inv"] + _dot(_dot(t, m_n).astype(BF16), t)

    def a_solve(a):
        t = a["inv"].astype(BF16)
        a["w"] = _dot(t, a["xa"]).astype(BF16)
        a["u0"] = _dot(t, a["zp"]).astype(BF16)

    def a_fold(a):
        at = (a["slot"], a["g"], a["q"])
        g_ref[at] = _dot_tn(a["w"], a["yb"]).astype(BF16)
        sp_ref[at] = a["sp"] + _dot_tn(a["u0"], a["yb"])
        yp_ref[at] = a["yp"] + _dot(a["n_rb"], a["u0"])
        xr_ref[at] = (a["xr"].astype(F32) + _dot(a["n_rb"], a["w"])).astype(BF16)
        dec_ref[at] = a["decay"]

    a_stage_list = ([a_scale, a_scores]
                    + [functools.partial(a_double, level=level) for level in range(1, n_levels)]
                    + [a_solve, a_fold])

    def b_state(b):
        at = b["at"]
        s0 = state_ref[b["q"]]
        b["s0b"] = s0.astype(BF16)
        state_ref[b["q"]] = (s0 + _dot(b["s0b"], g_ref[at]) + sp_ref[at]) * dec_ref[at]

    def b_out(b):
        at = b["at"]
        pair, reverse = chain_defs[b["q"]]
        ys = _dot_nt(xr_ref[at], b["s0b"]) + yp_ref[at]
        y_ref[rows_of(b["idx"], reverse), pair * LANES:(pair + 1) * LANES] += ys[:chunk] + ys[chunk:]

    b_stage_list = [b_state, b_out]

    y_ref[...] = jnp.zeros_like(y_ref)
    state_ref[...] = jnp.zeros_like(state_ref)
    n_chains = len(chain_defs)
    first = [a_load(q, g, g, 0) for g in range(group) for q in range(n_chains)]
    for stage in a_stage_list:
        for a in first:
            stage(a)

    def run_step(step, prepare_next):
        slot = step & 1
        base = step * group
        b_work = []
        for g in range(group):
            now = [dict(q=q, at=(slot, g, q), idx=base + g) for q in range(n_chains)]
            b_work += [(stage, now) for stage in b_stage_list]
        a_work = []
        if prepare_next:
            ahead = [a_load(q, g, base + group + g, 1 - slot)
                     for g in range(group) for q in range(n_chains)]
            a_work = [(stage, ahead) for stage in a_stage_list]
        per_a = -(-len(b_work) // max(len(a_work), 1))
        while a_work or b_work:
            for stage, items in b_work[:per_a]:
                for b in items:
                    stage(b)
            b_work = b_work[per_a:]
            if a_work:
                stage, items = a_work.pop(0)
                for a in items:
                    stage(a)

    n_steps = n_chunks // group

    def body(step, carry):
        run_step(step, True)
        return carry

    lax.fori_loop(0, n_steps - 1, body, 0)
    run_step(n_steps - 1, False)


def _wkv_level_masks(chunk):
    s = np.arange(2 * chunk)
    head, t = s // chunk, s % chunk
    n_levels = int(math.log2(chunk))
    masks = np.zeros((2, n_levels, 2 * chunk, 2 * chunk), np.float32)
    for d in range(2):
        pos = t if d == 0 else chunk - 1 - t
        for j in range(n_levels):
            blk = pos >> j
            masks[d, j] = ((head[:, None] == head[None, :]) & (blk[:, None] % 2 == 1)
                           & (blk[None, :] == blk[:, None] - 1))
    return masks


def _wkv(r, k, v, kk, wlf, wlb, af, ab, k_a, seq_len):
    t, d = r.shape
    nb = t // seq_len
    pairs = WKV_PAIRS_PER_STEP
    width = pairs * LANES
    n_chains = 2 * pairs
    seq = pl.BlockSpec((seq_len, width), lambda b, p: (b, p))
    par = pl.BlockSpec((1, width), lambda b, p: (0, p))
    levels = jnp.asarray(_wkv_level_masks(WKV_CHUNK), BF16)
    group = WKV_CHUNKS_PER_STEP
    parked_f32 = pltpu.VMEM((2, group, n_chains, LANES, LANES), F32)
    parked_bf16 = pltpu.VMEM((2, group, n_chains, LANES, LANES), BF16)
    return pl.pallas_call(
        functools.partial(_wkv_kernel, seq_len=seq_len, chunk=WKV_CHUNK, pairs=pairs, group=group),
        grid=(nb, d // width),
        in_specs=[seq] * 8 + [par, _resident(levels.shape)],
        out_specs=seq,
        out_shape=jax.ShapeDtypeStruct((t, d), F32),
        scratch_shapes=[pltpu.VMEM((n_chains, LANES, LANES), F32), parked_bf16, parked_f32,
                        parked_f32, parked_bf16, pltpu.VMEM((2, group, n_chains, 1, LANES), F32)],
        compiler_params=_params("parallel", "parallel"),
        name="wkv7_chunked",
    )(r, k, v, kk, wlf, wlb, af, ab, k_a.reshape(1, d), levels)


def _rwkv_mid_kernel(y_ref, v_ref, g_ref, bonus_ref, lnw_ref, lnb_ref, hsum_ref, hexp_ref, o_ref):
    hsum = hsum_ref[...]
    hexp = hexp_ref[...]

    def head_sum(x):
        return _head_sums(x, hsum, hexp)

    y = y_ref[...]
    yc = y - head_sum(y) * (1.0 / RWKV_HEAD)
    var = head_sum(yc * yc) * (1.0 / RWKV_HEAD)
    yn = yc * lax.rsqrt(var + RWKV_GN_EPS) * lnw_ref[...] + lnb_ref[...]
    b_hi, b_lo = _split_bf16(bonus_ref[...])
    bonus = (_dot(b_hi, hexp) + _dot(b_lo, hexp)) * v_ref[...].astype(F32)
    o_ref[...] = ((yn + bonus) * g_ref[...].astype(F32)).astype(BF16)


def _rwkv_mid(y, v, g, bonus, p):
    t, d = y.shape
    tm = TOKEN_TILE
    tile = pl.BlockSpec((tm, d), lambda i: (i, 0))
    vec = _resident((1, d))
    hsum = _head_sum_matrix()
    return pl.pallas_call(
        _rwkv_mid_kernel,
        grid=(t // tm,),
        in_specs=[tile] * 3 + [pl.BlockSpec((tm, RWKV_HEADS), lambda i: (i, 0))] + [vec] * 2
        + [_resident((d, RWKV_HEADS)), _resident((RWKV_HEADS, d))],
        out_specs=tile,
        out_shape=jax.ShapeDtypeStruct((t, d), BF16),
        compiler_params=_params("parallel"),
        name="rwkv_mid",
    )(y, v, g, bonus, p["lnw"], p["lnb"], hsum, hsum.T)


def _sgu_kernel(x_ref, mod_ref, ng_ref, win_ref, bin_ref, lg_ref, lb_ref, ws_ref, bs_ref, o_ref,
                *, tm):
    h = _norm_mod(x_ref[...], ng_ref[...], mod_ref[1:2, :], mod_ref[0:1, :]).astype(BF16)
    z = _dot(h, win_ref[...]) + bin_ref[...]
    z = 0.5 * z * (1.0 + lax.erf(z * (1.0 / math.sqrt(2.0))))
    u = z[:, :SGU_DIM]
    v = z[:, SGU_DIM:]
    vc = v - jnp.mean(v, axis=-1, keepdims=True)
    vn = vc * lax.rsqrt(jnp.mean(vc * vc, axis=-1, keepdims=True) + 1e-5) * lg_ref[...] + lb_ref[...]
    vb = vn.astype(BF16)
    for c in range(tm // SGU_CHUNK):
        rows = slice(c * SGU_CHUNK, (c + 1) * SGU_CHUNK)
        for g in range(SGU_GROUPS):
            cols = slice(g * SGU_GC, (g + 1) * SGU_GC)
            s = _dot(ws_ref[g], vb[rows, cols]) + bs_ref[:, cols]
            o_ref[rows, cols] = (u[rows, cols] * s).astype(BF16)


def _sgu(x, mod_l, norm_g, p, seq_len):
    t, d = x.shape
    tm = TOKEN_TILE
    tiles_per_seq = seq_len // tm
    return pl.pallas_call(
        functools.partial(_sgu_kernel, tm=tm),
        grid=(t // tm,),
        in_specs=[pl.BlockSpec((tm, d), lambda i: (i, 0)),
                  pl.BlockSpec((None, 6, d), lambda i: (i // tiles_per_seq, 0, 0)),
                  _resident((1, d)),
                  _resident((d, 2 * SGU_DIM)), _resident((1, 2 * SGU_DIM)),
                  _resident((1, SGU_DIM)), _resident((1, SGU_DIM)),
                  _resident((SGU_GROUPS, SGU_CHUNK, SGU_CHUNK)), _resident((SGU_CHUNK, SGU_DIM))],
        out_specs=pl.BlockSpec((tm, SGU_DIM), lambda i: (i, 0)),
        out_shape=jax.ShapeDtypeStruct((t, SGU_DIM), BF16),
        compiler_params=_params("parallel"),
        name="sgu",
    )(x, mod_l, norm_g.reshape(1, d), p["win"], p["bin"], p["ng"], p["nb"], p["ws"], p["bs"])


def _qkv_kernel(x_ref, mod_ref, ng_ref, w_ref, q_ref, k_ref, v_ref):
    d = D_MODEL
    h = _norm_mod(x_ref[...], ng_ref[...], mod_ref[1:2, :], mod_ref[0:1, :]).astype(BF16)
    qkv = _dot(h, w_ref[...])
    q_ref[...] = (qkv[:, :d] * (DIFF_HEAD ** -0.5)).astype(BF16)
    k_ref[...] = qkv[:, d:2 * d].astype(BF16)
    v_ref[...] = qkv[:, 2 * d:].astype(BF16)


def _qkv(x, mod_l, norm_g, w_qkv, seq_len):
    t, d = x.shape
    tm = TOKEN_TILE
    tiles_per_seq = seq_len // tm
    tile = pl.BlockSpec((tm, d), lambda i: (i, 0))
    return pl.pallas_call(
        _qkv_kernel,
        grid=(t // tm,),
        in_specs=[tile, pl.BlockSpec((None, 6, d), lambda i: (i // tiles_per_seq, 0, 0)),
                  _resident((1, d)), _resident((d, 3 * d))],
        out_specs=[tile] * 3,
        out_shape=[jax.ShapeDtypeStruct((t, d), BF16)] * 3,
        compiler_params=_params("parallel"),
        name="diff_qkv",
    )(x, mod_l, norm_g.reshape(1, d), w_qkv)


def _attn_kernel(q_ref, k_ref, v_ref, win_ref, lam_ref, sg_ref, o_ref, vext_ref, *, tq, seq_len,
                 out_scale):
    @pl.when(pl.program_id(2) == 0)
    def _():
        col = lax.broadcasted_iota(jnp.int32, (seq_len, LANES), 1)
        vext_ref[:, :LANES] = v_ref[...]
        vext_ref[:, LANES:] = jnp.where(col == 0, 1.0, 0.0).astype(BF16)

    q = q_ref[...]
    lane = lax.broadcasted_iota(jnp.int32, (1, LANES), 1)
    zero = jnp.zeros_like(q)
    qs = jnp.concatenate([jnp.where(lane < DIFF_HEAD, q, zero),
                          jnp.where(lane < DIFF_HEAD, zero, q)], axis=0)
    s = _dot_nt(qs, k_ref[...])
    width = seq_len + tq
    window = jnp.broadcast_to(win_ref[...], (tq, width))
    bias = pltpu.roll(window, width - tq + 1, 1, stride=1, stride_axis=0)[:, :seq_len]

    def unnormalised(x):
        e = jnp.exp(x - jnp.max(x, axis=-1, keepdims=True)).astype(BF16)
        pv = _dot(e, vext_ref[...])
        return pv[:, :LANES], pv[:, LANES:LANES + 1]

    o1, l1 = unnormalised(s[:tq] + bias)
    o2, l2 = unnormalised(s[tq:] + bias)
    o = o1 / l1 - lam_ref[...] * (o2 / l2)
    o = o * lax.rsqrt(jnp.mean(o * o, axis=-1, keepdims=True) + 1e-5) * sg_ref[...]
    o_ref[...] = (o * out_scale).astype(BF16)


def _t5_bucket(rel):
    nb = NUM_BUCKETS // 2
    max_exact = nb // 2
    ret = jnp.where(rel > 0, nb, 0)
    n = jnp.abs(rel)
    nf = jnp.maximum(n, 1).astype(F32)
    large = max_exact + (jnp.log(nf / max_exact) / math.log(MAX_DISTANCE / max_exact)
                         * (nb - max_exact)).astype(jnp.int32)
    large = jnp.minimum(large, nb - 1)
    return ret + jnp.where(n < max_exact, n, large)


def _bias_windows(rel_bias, seq_len, tq):
    nqb = seq_len // tq
    j = jnp.arange(seq_len + tq, dtype=jnp.int32)[None, :]
    q_hi = (jnp.arange(nqb, dtype=jnp.int32)[:, None] + 1) * tq
    bucket = _t5_bucket(j - q_hi + 1)
    return jnp.transpose(rel_bias[bucket], (0, 2, 1))[:, :, None, :].astype(F32)


def _diff_attention(q, k, v, windows, lam, subln_g, lambda_init, seq_len):
    t, d = q.shape
    nb = t // seq_len
    tq = ATTN_TQ
    nqb = seq_len // tq
    kv = pl.BlockSpec((seq_len, LANES), lambda b, h, i: (b, h))
    qo = pl.BlockSpec((tq, LANES), lambda b, h, i: (b * nqb + i, h))
    return pl.pallas_call(
        functools.partial(_attn_kernel, tq=tq, seq_len=seq_len, out_scale=1.0 - lambda_init),
        grid=(nb, DIFF_HEADS, nqb),
        in_specs=[qo, kv, kv,
                  pl.BlockSpec((None, None, 1, seq_len + tq), lambda b, h, i: (i, h, 0, 0)),
                  pl.BlockSpec((1, 1), lambda b, h, i: (0, 0)),
                  pl.BlockSpec((1, LANES), lambda b, h, i: (0, 0))],
        out_specs=qo,
        out_shape=jax.ShapeDtypeStruct((t, d), BF16),
        scratch_shapes=[pltpu.VMEM((seq_len, 2 * LANES), BF16)],
        compiler_params=_params("parallel", "parallel", "arbitrary"),
        name="diff_attention",
    )(q, k, v, windows, lam.reshape(1, 1), subln_g.reshape(1, LANES))


def _post_kernel(pre_ref, w_ref, b_ref, x_ref, mod_ref, ng_ref, wrh_ref, wrl_ref, br_ref,
                 x1_ref, h2_ref, lg_ref):
    out = _dot(pre_ref[...], w_ref[...]) + b_ref[...]
    x1 = x_ref[...] + mod_ref[2:3, :] * out
    x1_ref[...] = x1
    h2 = _norm_mod(x1, ng_ref[...], mod_ref[4:5, :], mod_ref[3:4, :])
    h2_hi, h2_lo = _split_bf16(h2)
    h2_ref[...] = h2_hi
    lg_ref[...] = (_dot(h2_hi, wrh_ref[...]) + (_dot(h2_hi, wrl_ref[...]) + _dot(h2_lo, wrh_ref[...]))
                   + br_ref[...])


def _post(pre, w, b, x, mod_l, norm_g2, w_router, b_router, seq_len):
    t, d = x.shape
    din = pre.shape[1]
    tm = TOKEN_TILE
    tiles_per_seq = seq_len // tm
    tile = pl.BlockSpec((tm, d), lambda i: (i, 0))
    return pl.pallas_call(
        _post_kernel,
        grid=(t // tm,),
        in_specs=[pl.BlockSpec((tm, din), lambda i: (i, 0)), _resident((din, d)), _resident((1, d)),
                  tile, pl.BlockSpec((None, 6, d), lambda i: (i // tiles_per_seq, 0, 0)),
                  _resident((1, d)), _resident((d, ROUTER_COLS)), _resident((d, ROUTER_COLS)),
                  _resident((1, ROUTER_COLS))],
        out_specs=[tile, tile, pl.BlockSpec((tm, ROUTER_COLS), lambda i: (i, 0))],
        out_shape=[jax.ShapeDtypeStruct((t, d), F32), jax.ShapeDtypeStruct((t, d), BF16),
                   jax.ShapeDtypeStruct((t, ROUTER_COLS), F32)],
        compiler_params=_params("parallel"),
        name="post_router",
    )(pre, w, b, x, mod_l, norm_g2.reshape(1, d), *_split_bf16(w_router), b_router)


def _expert_kernel(be_ref, nu_ref, x_ref, wg_ref, wu_ref, wd_ref, o_ref, wgb_ref, wub_ref, wdb_ref):
    i = pl.program_id(0)
    used = i < nu_ref[0]
    new_expert = (i == 0) | (be_ref[i] != be_ref[jnp.maximum(i - 1, 0)])

    @pl.when(used & new_expert)
    def _():
        wgb_ref[...] = wg_ref[...].astype(BF16)
        wub_ref[...] = wu_ref[...].astype(BF16)
        wdb_ref[...] = wd_ref[...].astype(BF16)

    @pl.when(used)
    def _():
        x = x_ref[...]
        hg = _dot(x, wgb_ref[...])
        hu = _dot(x, wub_ref[...])
        act = hg * _sigmoid(hg) * hu
        o_ref[...] = _dot(act.astype(BF16), wdb_ref[...]).astype(BF16)

    @pl.when(i >= nu_ref[0])
    def _():
        o_ref[...] = jnp.zeros_like(o_ref)


def _experts(xs, block_e, n_used, wg, wu, wd, layer):
    p_rows, d = xs.shape
    n_blocks = p_rows // MOE_BLOCK
    grid_spec = pltpu.PrefetchScalarGridSpec(
        num_scalar_prefetch=2,
        grid=(n_blocks,),
        in_specs=[pl.BlockSpec((MOE_BLOCK, d), lambda i, be, nu: (i, 0)),
                  pl.BlockSpec((None, None, d, EXPERT_DIM), lambda i, be, nu: (layer, be[i], 0, 0)),
                  pl.BlockSpec((None, None, d, EXPERT_DIM), lambda i, be, nu: (layer, be[i], 0, 0)),
                  pl.BlockSpec((None, None, EXPERT_DIM, d), lambda i, be, nu: (layer, be[i], 0, 0))],
        out_specs=pl.BlockSpec((MOE_BLOCK, d), lambda i, be, nu: (i, 0)),
        scratch_shapes=[pltpu.VMEM((d, EXPERT_DIM), BF16), pltpu.VMEM((d, EXPERT_DIM), BF16),
                        pltpu.VMEM((EXPERT_DIM, d), BF16)],
    )
    return pl.pallas_call(
        _expert_kernel,
        grid_spec=grid_spec,
        out_shape=jax.ShapeDtypeStruct((p_rows, d), BF16),
        compiler_params=_params("arbitrary"),
        name="moe_experts",
    )(block_e, n_used, xs, wg, wu, wd)


def _combine_kernel(x_ref, y0_ref, y1_ref, gate_ref, mod_ref, fg_ref, o_ref, *, final):
    gate = gate_ref[...]
    moe = gate[:, 0:1] * y0_ref[...].astype(F32) + gate[:, 1:2] * y1_ref[...].astype(F32)
    x = x_ref[...] + mod_ref[5:6, :] * moe
    if final:
        x = x * lax.rsqrt(jnp.mean(x * x, axis=-1, keepdims=True) + NORM_EPS) * fg_ref[...]
    o_ref[...] = x


def _combine(x1, y0, y1, gate, mod_l, final_g, final, seq_len):
    t, d = x1.shape
    tm = TOKEN_TILE
    tiles_per_seq = seq_len // tm
    tile = pl.BlockSpec((tm, d), lambda i: (i, 0))
    return pl.pallas_call(
        functools.partial(_combine_kernel, final=final),
        grid=(t // tm,),
        in_specs=[tile, tile, tile, pl.BlockSpec((tm, MOE_TOP_K), lambda i: (i, 0)),
                  pl.BlockSpec((None, 6, d), lambda i: (i // tiles_per_seq, 0, 0)),
                  _resident((1, d))],
        out_specs=tile,
        out_shape=jax.ShapeDtypeStruct((t, d), F32),
        compiler_params=_params("parallel"),
        name="moe_combine",
    )(x1, y0, y1, gate, mod_l, final_g.reshape(1, d))


def _route(logits):
    t = logits.shape[0]
    g_logits = logits[:, :MOE_GROUPS]
    e_logits = logits[:, MOE_GROUPS:MOE_GROUPS + N_EXPERTS].reshape(t, MOE_GROUPS, EXPERTS_PER_GROUP)
    grp = jnp.argmax(g_logits, axis=-1).astype(jnp.int32)
    p_grp = 1.0 / jnp.sum(jnp.exp(g_logits - jnp.max(g_logits, axis=-1, keepdims=True)), axis=-1)
    in_grp = grp[:, None, None] == jnp.arange(MOE_GROUPS, dtype=jnp.int32)[None, :, None]
    e_in = jnp.sum(jnp.where(in_grp, e_logits, 0.0), axis=1)
    idx = jnp.arange(EXPERTS_PER_GROUP, dtype=jnp.int32)[None, :]
    i1 = jnp.argmax(e_in, axis=-1).astype(jnp.int32)
    v1 = jnp.max(e_in, axis=-1)
    rest = jnp.where(idx == i1[:, None], -jnp.inf, e_in)
    i2 = jnp.argmax(rest, axis=-1).astype(jnp.int32)
    v2 = jnp.max(rest, axis=-1)
    p2 = jnp.exp(v2 - v1)
    gate = p_grp[:, None] * jnp.stack([1.0 / (1.0 + p2), p2 / (1.0 + p2)], axis=-1)
    expert = grp[:, None] * EXPERTS_PER_GROUP + jnp.stack([i1, i2], axis=-1)
    return expert, gate


def _rank_kernel(e_ref, rank_ref, cnt_ref, carry_ref):
    @pl.when(pl.program_id(0) == 0)
    def _():
        carry_ref[...] = jnp.zeros_like(carry_ref)

    e = e_ref[...]
    expert_id = lax.broadcasted_iota(jnp.int32, (N_EXPERTS, RANK_TILE), 0)
    onehot = jnp.where(e == expert_id, 1.0, 0.0)
    ri = lax.broadcasted_iota(jnp.int32, (RANK_TILE, RANK_TILE), 0)
    ci = lax.broadcasted_iota(jnp.int32, (RANK_TILE, RANK_TILE), 1)
    prefix = _dot(onehot.astype(BF16), (ri <= ci).astype(BF16))
    carry = carry_ref[...]
    rank = jnp.sum(onehot * (prefix + carry), axis=0, keepdims=True) - 1.0
    rank_ref[...] = rank.astype(jnp.int32)
    carry = carry + jnp.sum(onehot, axis=1, keepdims=True)
    carry_ref[...] = carry
    cnt_ref[...] = jnp.broadcast_to(carry, cnt_ref.shape).astype(jnp.int32)


def _rank(e):
    a = e.shape[0]
    n_tiles = a // RANK_TILE
    tile = pl.BlockSpec((None, 1, RANK_TILE), lambda i: (i, 0, 0))
    rank, counts = pl.pallas_call(
        _rank_kernel,
        grid=(n_tiles,),
        in_specs=[tile],
        out_specs=[tile, pl.BlockSpec((N_EXPERTS, LANES), lambda i: (0, 0))],
        out_shape=[jax.ShapeDtypeStruct((n_tiles, 1, RANK_TILE), jnp.int32),
                   jax.ShapeDtypeStruct((N_EXPERTS, LANES), jnp.int32)],
        scratch_shapes=[pltpu.VMEM((N_EXPERTS, 1), F32)],
        compiler_params=_params("arbitrary"),
        name="moe_rank",
    )(e.reshape(n_tiles, 1, RANK_TILE))
    return rank.reshape(a), counts[:, 0]


def _dispatch_plan(expert):
    t = expert.shape[0]
    a = t * MOE_TOP_K
    e = expert.reshape(a)
    rank, counts = _rank(e)
    padded = (counts + MOE_BLOCK - 1) // MOE_BLOCK * MOE_BLOCK
    end_pad = jnp.cumsum(padded)
    start_pad = end_pad - padded
    dest = (start_pad[e] + rank).astype(jnp.int32)
    n_blocks = (a + N_EXPERTS * (MOE_BLOCK - 1) + MOE_BLOCK - 1) // MOE_BLOCK
    row_tok = jnp.zeros((n_blocks * MOE_BLOCK,), jnp.int32).at[dest].set(
        jnp.arange(a, dtype=jnp.int32) // MOE_TOP_K, unique_indices=True, mode="promise_in_bounds")
    block_start = jnp.arange(n_blocks, dtype=jnp.int32) * MOE_BLOCK
    block_e = jnp.sum((end_pad[None, :] <= block_start[:, None]).astype(jnp.int32), axis=1)
    block_e = jnp.minimum(block_e, N_EXPERTS - 1).astype(jnp.int32)
    n_used = (end_pad[-1] // MOE_BLOCK).astype(jnp.int32).reshape(1)
    return dest.reshape(t, MOE_TOP_K), row_tok, block_e, n_used


def _moe(x1, h2, logits, mod_l, wg, wu, wd, layer, final_g, seq_len):
    expert, gate = _route(logits)
    dest, row_tok, block_e, n_used = _dispatch_plan(expert)
    ys = _experts(h2[row_tok], block_e, n_used, wg, wu, wd, layer)
    final = layer == wg.shape[0] - 1
    return _combine(x1, ys[dest[:, 0]], ys[dest[:, 1]], gate, mod_l, final_g, final, seq_len)


def _pad_cols(w, n):
    return jnp.pad(w, ((0, 0), (0, n - w.shape[1])))


def _pad_rows(w, n):
    return jnp.pad(w, ((0, n - w.shape[0]), (0, 0)))


def _rwkv_params(j, mu, wr, wk, wv, w0, w1, w2, a0, a1, a2, v0, v1, v2, g1, g2, kk, ka, rk,
                 lnw, lnb):
    d = D_MODEL
    p = {
        "mu": mu[j],
        "wr": wr[j].astype(BF16), "wk": wk[j].astype(BF16), "wv": wv[j].astype(BF16),
        "wl1": jnp.concatenate([w1[j, 0], w1[j, 1]], axis=1).astype(BF16),
        "wl2": _block_diag2(w2[j, 0], w2[j, 1]).astype(BF16),
        "al1": jnp.concatenate([a1[j, 0], a1[j, 1]], axis=1).astype(BF16),
        "al2": _block_diag2(a2[j, 0], a2[j, 1]).astype(BF16),
        "g1": g1[j].astype(BF16), "g2": g2[j].astype(BF16),
        "w0": w0[j].reshape(1, 2 * d), "a0": a0[j].reshape(1, 2 * d),
        "kk": kk[j].reshape(1, d), "ka": ka[j].reshape(1, d), "rk": rk[j].reshape(1, d),
        "lnw": lnw[j].reshape(1, d), "lnb": lnb[j].reshape(1, d),
    }
    if j > 0:
        p["v0"] = v0[j - 1].reshape(1, d)
        p["v1"] = _pad_cols(v1[j - 1], LANES).astype(BF16)
        p["v2"] = _pad_rows(v2[j - 1], LANES).astype(BF16)
    return p


def kernel(x_prompt, x_sample, c_prompt, c_sample, ada_w, ada_b, norm_g, final_g, rwkv_mu, rwkv_wr, rwkv_wk, rwkv_wv, rwkv_wo, rwkv_w0, rwkv_w1, rwkv_w2, rwkv_a0, rwkv_a1, rwkv_a2, rwkv_v0, rwkv_v1, rwkv_v2, rwkv_g1, rwkv_g2, rwkv_kk, rwkv_ka, rwkv_rk, rwkv_lnw, rwkv_lnb, sgu_win, sgu_bin, sgu_ng, sgu_nb, sgu_ws, sgu_bs, sgu_wout, sgu_bout, diff_wqkv, diff_wo, diff_lq1, diff_lk1, diff_lq2, diff_lk2, diff_subln, rel_bias, moe_wrg, moe_brg, moe_wre, moe_bre, moe_wg, moe_wu, moe_wd):
    d = D_MODEL
    nb_p, seq_len, _ = x_prompt.shape
    assert x_sample.shape[1] == seq_len
    t_p = nb_p * seq_len
    x = jnp.concatenate([x_prompt.reshape(-1, d), x_sample.reshape(-1, d)], axis=0)
    c = jnp.concatenate([c_prompt, c_sample], axis=0)
    mod = _modulation(c, ada_w, ada_b)
    zero_bias = jnp.zeros((1, d), F32)
    v_first = None
    for i in range(DEPTH):
        j = i // N_MIXERS
        mod_l = mod[i]
        if i % N_MIXERS == 0:
            p = _rwkv_params(j, rwkv_mu, rwkv_wr, rwkv_wk, rwkv_wv, rwkv_w0, rwkv_w1, rwkv_w2,
                             rwkv_a0, rwkv_a1, rwkv_a2, rwkv_v0, rwkv_v1, rwkv_v2, rwkv_g1, rwkv_g2,
                             rwkv_kk, rwkv_ka, rwkv_rk, rwkv_lnw, rwkv_lnb)
            r, k, v, g, kk, wlf, wlb, af, ab, bonus = _rwkv_pre(x, mod_l, norm_g[i, 0], p, v_first,
                                                                seq_len)
            if v_first is None:
                v_first = v
            y = _wkv(r, k, v, kk, wlf, wlb, af, ab, p["ka"], seq_len)
            pre = _rwkv_mid(y, v, g, bonus, p)
            w_out, b_out = rwkv_wo[j].astype(BF16), zero_bias
        elif i % N_MIXERS == 1:
            p = {"win": sgu_win[j].astype(BF16), "bin": sgu_bin[j].reshape(1, -1),
                 "ng": sgu_ng[j].reshape(1, -1), "nb": sgu_nb[j].reshape(1, -1),
                 "ws": sgu_ws[j].astype(BF16),
                 "bs": jnp.repeat(jnp.transpose(sgu_bs[j]), SGU_GC, axis=1)}
            pre = _sgu(x, mod_l, norm_g[i, 0], p, seq_len)
            w_out, b_out = sgu_wout[j].astype(BF16), sgu_bout[j].reshape(1, d)
        else:
            lambda_init = 0.8 - 0.6 * math.exp(-0.3 * i)
            lam = (jnp.exp(jnp.sum(diff_lq1[j] * diff_lk1[j])) - jnp.exp(jnp.sum(diff_lq2[j] * diff_lk2[j]))
                   + lambda_init)
            q, k, v = _qkv(x, mod_l, norm_g[i, 0], diff_wqkv[j].astype(BF16), seq_len)
            windows = _bias_windows(rel_bias, seq_len, ATTN_TQ)
            pre = _diff_attention(q, k, v, windows, lam, diff_subln[j], lambda_init, seq_len)
            w_out, b_out = diff_wo[j].astype(BF16), zero_bias
        w_router = _pad_cols(jnp.concatenate([moe_wrg[i], moe_wre[i]], axis=1), ROUTER_COLS)
        b_router = _pad_cols(jnp.concatenate([moe_brg[i], moe_bre[i]])[None, :], ROUTER_COLS)
        x1, h2, logits = _post(pre, w_out, b_out, x, mod_l, norm_g[i, 1], w_router, b_router, seq_len)
        x = _moe(x1, h2, logits, mod_l, moe_wg, moe_wu, moe_wd, i, final_g, seq_len)
    return (x[:t_p].reshape(x_prompt.shape), x[t_p:].reshape(x_sample.shape))
```

```python
import functools
import math

import numpy as np
import jax
import jax.numpy as jnp
from jax import lax
from jax.experimental import pallas as pl
from jax.experimental.pallas import tpu as pltpu

F32 = jnp.float32
BF16 = jnp.bfloat16
HIGHEST = lax.Precision.HIGHEST

D_MODEL = 1024
DEPTH = 4
N_MIXERS = 3
NORM_EPS = 1e-6
LANES = 128
VMEM_LIMIT_BYTES = 56 * 1024 * 1024

RWKV_HEAD = 64
RWKV_HEADS = D_MODEL // RWKV_HEAD
RWKV_PAIRS = D_MODEL // LANES
RWKV_GN_EPS = 64e-5
WKV_CHUNK = 64
WKV_PAIRS_PER_STEP = 2
WKV_CHUNKS_PER_STEP = 4

SGU_CHUNK = 128
SGU_DIM = 2 * D_MODEL
SGU_GROUPS = 8
SGU_GC = SGU_DIM // SGU_GROUPS

DIFF_HEADS = 8
DIFF_HEAD = 64
NUM_BUCKETS = 32
MAX_DISTANCE = 128
ATTN_TQ = 256

MOE_GROUPS = 4
EXPERTS_PER_GROUP = 8
N_EXPERTS = MOE_GROUPS * EXPERTS_PER_GROUP
MOE_TOP_K = 2
EXPERT_DIM = D_MODEL // 2
MOE_BLOCK = 256
ROUTER_COLS = LANES
RANK_TILE = 512
MOE_SLICES = 4

TOKEN_TILE = 256


def _params(*semantics):
    return pltpu.CompilerParams(dimension_semantics=semantics, vmem_limit_bytes=VMEM_LIMIT_BYTES)


def _resident(shape):
    zeros = (0,) * len(shape)
    return pl.BlockSpec(shape, lambda *_: zeros, pipeline_mode=pl.Buffered(1))


def _dot(a, b, precision=None):
    return jnp.dot(a, b, preferred_element_type=F32, precision=precision)


def _dot_nt(a, b, precision=None):
    return lax.dot_general(a, b, (((1,), (1,)), ((), ())), preferred_element_type=F32,
                           precision=precision)


def _dot_tn(a, b, precision=None):
    return lax.dot_general(a, b, (((0,), (0,)), ((), ())), preferred_element_type=F32,
                           precision=precision)


def _split_bf16(x):
    hi = x.astype(BF16)
    return hi, (x - hi.astype(F32)).astype(BF16)


def _head_sums(x, hsum, hexp):
    hi, lo = _split_bf16(x)
    shi, slo = _split_bf16(_dot(hi, hsum) + _dot(lo, hsum))
    return _dot(shi, hexp) + _dot(slo, hexp)


def _sigmoid(x):
    return 1.0 / (1.0 + jnp.exp(-x))


def _norm_mod(x, g, scale, shift, eps=NORM_EPS):
    y = x * lax.rsqrt(jnp.mean(x * x, axis=-1, keepdims=True) + eps)
    return (y * g) * (1.0 + scale) + shift


def _mod_kernel(c_ref, w_ref, b_ref, o_ref):
    c = c_ref[...]
    o_ref[...] = _dot(c * _sigmoid(c), w_ref[...], HIGHEST) + b_ref[...]


def _modulation(c, ada_w, ada_b):
    nb = c.shape[0]
    depth, d, n = ada_w.shape
    out = pl.pallas_call(
        _mod_kernel,
        grid=(depth, n // d),
        in_specs=[pl.BlockSpec((nb, d), lambda l, j: (0, 0)),
                  pl.BlockSpec((None, d, d), lambda l, j: (l, 0, j)),
                  pl.BlockSpec((None, 1, d), lambda l, j: (l, 0, j))],
        out_specs=pl.BlockSpec((None, nb, d), lambda l, j: (l, 0, j)),
        out_shape=jax.ShapeDtypeStruct((depth, nb, n), F32),
        compiler_params=_params("parallel", "parallel"),
        name="adaln_modulation",
    )(c, ada_w, ada_b.reshape(depth, 1, n))
    return out.reshape(depth, nb, n // d, d)


def _rwkv_pre_kernel(*refs, tm, tiles_per_seq, has_vres):
    if has_vres:
        (x_ref, xp_ref, xn_ref, mod_ref, ng_ref, mu_ref, wr_ref, wk_ref, wv_ref, wl1_ref, wl2_ref,
         al1_ref, al2_ref, g1_ref, g2_ref, w0_ref, a0_ref, kscale_ref, ka_ref, rk_ref, hsum_ref,
         hexp_ref, vf_ref, v0_ref, v1_ref, v2_ref,
         r_ref, k_ref, v_ref, g_ref, kk_ref, wlf_ref, wlb_ref, af_ref, ab_ref, bonus_ref) = refs
    else:
        (x_ref, xp_ref, xn_ref, mod_ref, ng_ref, mu_ref, wr_ref, wk_ref, wv_ref, wl1_ref, wl2_ref,
         al1_ref, al2_ref, g1_ref, g2_ref, w0_ref, a0_ref, kscale_ref, ka_ref, rk_ref, hsum_ref,
         hexp_ref,
         r_ref, k_ref, v_ref, g_ref, kk_ref, wlf_ref, wlb_ref, af_ref, ab_ref, bonus_ref) = refs
    d = D_MODEL
    i = pl.program_id(0)
    pos = i % tiles_per_seq
    shift, scale = mod_ref[0:1, :], mod_ref[1:2, :]
    ng = ng_ref[...]
    h = _norm_mod(x_ref[...], ng, scale, shift)
    h_prev = jnp.where(pos == 0, 0.0, _norm_mod(xp_ref[7:8, :], ng, scale, shift))
    h_next = jnp.where(pos == tiles_per_seq - 1, 0.0, _norm_mod(xn_ref[0:1, :], ng, scale, shift))
    row = lax.broadcasted_iota(jnp.int32, (tm, 1), 0)
    h_up = jnp.where(row == 0, h_prev, pltpu.roll(h, 1, 0))
    h_dn = jnp.where(row == tm - 1, h_next, pltpu.roll(h, tm - 1, 0))
    xx = 0.5 * (h_up + h_dn) - h

    def mix(n):
        return (h + xx * mu_ref[n:n + 1, :]).astype(BF16)

    xr, xw, xk, xv, xa, xg = [mix(n) for n in range(6)]
    r = _dot(xr, wr_ref[...])
    r_ref[...] = r.astype(BF16)
    k = _dot(xk, wk_ref[...])
    k_ref[...] = k.astype(BF16)
    kkr = k * kscale_ref[...]
    norm = jnp.sqrt(_head_sums(kkr * kkr, hsum_ref[...], hexp_ref[...]))
    kk_ref[...] = (kkr / jnp.maximum(norm, 1e-12)).astype(BF16)
    v = _dot(xv, wv_ref[...])
    if has_vres:
        lv = _dot(_dot(xv, v1_ref[...]).astype(BF16), v2_ref[...])
        v = v + (vf_ref[...].astype(F32) - v) * _sigmoid(v0_ref[...] + lv)
    v_ref[...] = v.astype(BF16)
    g_ref[...] = _dot(_sigmoid(_dot(xg, g1_ref[...])).astype(BF16), g2_ref[...]).astype(BF16)
    wl = w0_ref[...] + _dot(jnp.tanh(_dot(xw, wl1_ref[...])).astype(BF16), wl2_ref[...])
    logw = -math.exp(-0.5) * _sigmoid(wl)
    wlf_ref[...] = logw[:, :d]
    wlb_ref[...] = logw[:, d:]
    a = _sigmoid(a0_ref[...] + _dot(_dot(xa, al1_ref[...]).astype(BF16), al2_ref[...]))
    af_ref[...] = a[:, :d].astype(BF16)
    ab_ref[...] = a[:, d:].astype(BF16)
    k_bonus = k * (1.0 + (0.5 * (a[:, :d] + a[:, d:]) - 1.0) * ka_ref[...])
    hi, lo = _split_bf16(r * k_bonus * rk_ref[...])
    bonus_ref[...] = _dot(hi, hsum_ref[...]) + _dot(lo, hsum_ref[...])


def _head_sum_matrix():
    head_of_lane = np.arange(D_MODEL) // RWKV_HEAD
    return jnp.asarray(head_of_lane[:, None] == np.arange(RWKV_HEADS)[None, :], BF16)


def _block_diag2(m0, m1):
    z = jnp.zeros_like(m0)
    return jnp.concatenate([jnp.concatenate([m0, z], 1), jnp.concatenate([z, m1], 1)], 0)


def _rwkv_pre(x, mod_l, norm_g, p, v_first, seq_len):
    t, d = x.shape
    tm = TOKEN_TILE
    tiles_per_seq = seq_len // tm
    n_tiles = t // tm
    has_vres = v_first is not None
    tile = pl.BlockSpec((tm, d), lambda i: (i, 0))
    rows8 = tm // 8
    last8 = t // 8 - 1
    in_specs = [
        tile,
        pl.BlockSpec((8, d), lambda i: (jnp.maximum(i * rows8 - 1, 0), 0)),
        pl.BlockSpec((8, d), lambda i: (jnp.minimum((i + 1) * rows8, last8), 0)),
        pl.BlockSpec((None, 6, d), lambda i: (i // tiles_per_seq, 0, 0)),
        _resident((1, d)), _resident((6, d)),
        _resident((d, d)), _resident((d, d)), _resident((d, d)),
        _resident((d, LANES)), _resident((LANES, 2 * d)),
        _resident((d, LANES)), _resident((LANES, 2 * d)),
        _resident((d, LANES)), _resident((LANES, d)),
        _resident((1, 2 * d)), _resident((1, 2 * d)),
        _resident((1, d)), _resident((1, d)), _resident((1, d)),
        _resident((d, RWKV_HEADS)), _resident((RWKV_HEADS, d)),
    ]
    hsum = _head_sum_matrix()
    args = [x, x, x, mod_l, norm_g.reshape(1, d), p["mu"],
            p["wr"], p["wk"], p["wv"], p["wl1"], p["wl2"], p["al1"], p["al2"], p["g1"], p["g2"],
            p["w0"], p["a0"], p["kk"], p["ka"], p["rk"], hsum, hsum.T]
    if has_vres:
        in_specs += [tile, _resident((1, d)), _resident((d, LANES)), _resident((LANES, d))]
        args += [v_first, p["v0"], p["v1"], p["v2"]]
    dtypes = [BF16] * 5 + [F32] * 2 + [BF16] * 2
    outs = pl.pallas_call(
        functools.partial(_rwkv_pre_kernel, tm=tm, tiles_per_seq=tiles_per_seq, has_vres=has_vres),
        grid=(n_tiles,),
        in_specs=in_specs,
        out_specs=[tile] * 9 + [pl.BlockSpec((tm, RWKV_HEADS), lambda i: (i, 0))],
        out_shape=[jax.ShapeDtypeStruct((t, d), dt) for dt in dtypes]
        + [jax.ShapeDtypeStruct((t, RWKV_HEADS), F32)],
        compiler_params=_params("parallel"),
        name="rwkv_pre",
    )(*args)
    return outs


def _wkv_kernel(r_ref, k_ref, v_ref, kk_ref, wlf_ref, wlb_ref, af_ref, ab_ref, ka_ref,
                lvl_ref, y_ref, state_ref, g_ref, sp_ref, yp_ref, xr_ref, dec_ref,
                *, seq_len, chunk, pairs, group):
    n_chunks = seq_len // chunk
    n_levels = int(math.log2(chunk))
    two = 2 * chunk
    lane = lax.broadcasted_iota(jnp.int32, (1, LANES), 1)
    head0 = lane < RWKV_HEAD
    ri = lax.broadcasted_iota(jnp.int32, (chunk, chunk), 0)
    ci = lax.broadcasted_iota(jnp.int32, (chunk, chunk), 1)
    si = lax.broadcasted_iota(jnp.int32, (two, two), 0)
    sj = lax.broadcasted_iota(jnp.int32, (two, two), 1)
    same_head = (si < chunk) == (sj < chunk)
    st, su = si & (chunk - 1), sj & (chunk - 1)
    eye = (si == sj).astype(F32)
    chain_defs = [(pair, reverse) for pair in range(pairs) for reverse in (False, True)]

    def rows_of(idx, reverse):
        pos = (n_chunks - 1 - idx) if reverse else idx
        return pl.ds(pl.multiple_of(pos * chunk, chunk), chunk)

    def stack(x):
        return jnp.concatenate([jnp.where(head0, x, 0.0), jnp.where(head0, 0.0, x)], axis=0)

    def a_load(q, g, idx, slot):
        pair, reverse = chain_defs[q]
        rows = rows_of(idx, reverse)
        lanes = slice(pair * LANES, (pair + 1) * LANES)
        a = dict(q=q, g=g, slot=slot, rows=rows, lanes=lanes, reverse=reverse)
        a["lw"] = lw = (wlb_ref if reverse else wlf_ref)[rows, lanes]
        a["a"] = (ab_ref if reverse else af_ref)[rows, lanes].astype(F32)
        cum = ((ri <= ci) if reverse else (ri >= ci)).astype(BF16)
        lw_hi = lw.astype(BF16)
        lw_mid, lw_lo = _split_bf16(lw - lw_hi.astype(F32))
        a["logp"] = _dot(cum, lw_hi) + (_dot(cum, lw_mid) + _dot(cum, lw_lo))
        return a

    def a_scale(a):
        rows, lanes, logp, lw, av = a["rows"], a["lanes"], a["logp"], a["lw"], a["a"]
        k = k_ref[rows, lanes].astype(F32)
        kk = kk_ref[rows, lanes].astype(F32)
        kd = k * (1.0 + (av - 1.0) * ka_ref[:, lanes])
        p_in = jnp.exp(logp)
        p_out = jnp.exp(-logp)
        a["xa"] = stack(-kk * jnp.exp(logp - lw)).astype(BF16)
        a["xr"] = stack(r_ref[rows, lanes].astype(F32) * p_in).astype(BF16)
        a["yb"] = stack(kk * av * p_out).astype(BF16)
        a["yk"] = stack(kd * p_out).astype(BF16)
        a["vs"] = stack(v_ref[rows, lanes])
        a["decay"] = jnp.exp(jnp.sum(lw, axis=0, keepdims=True))

    def a_scores(a):
        if a["reverse"]:
            strict, incl = same_head & (st < su), same_head & (st <= su)
        else:
            strict, incl = same_head & (st > su), same_head & (st >= su)
        sc = _dot_nt(jnp.concatenate([a["xa"], a["xr"]], 0),
                     jnp.concatenate([a["yb"], a["yk"]], 0))
        a["m"] = sc[:two, :two].astype(BF16)
        m_ak = jnp.where(strict, sc[:two, two:], 0.0).astype(BF16)
        n_rk = jnp.where(incl, sc[two:, two:], 0.0).astype(BF16)
        a["n_rb"] = jnp.where(incl, sc[two:, :two], 0.0).astype(BF16)
        a["inv"] = eye + (a["m"] * lvl_ref[int(a["reverse"]), 0]).astype(F32)
        a["zp"] = _dot(m_ak, a["vs"]).astype(BF16)
        a["yp"] = _dot(n_rk, a["vs"])
        a["sp"] = _dot_tn(a["vs"], a["yk"])

    def a_double(a, level):
        t = a["inv"].astype(BF16)
        m_n = a["m"] * lvl_ref[int(a["reverse"]), level]
        a["inv"] = a["inv"] + _dot(_dot(t, m_n).astype(BF16), t)

    def a_solve(a):
        t = a["inv"].astype(BF16)
        a["w"] = _dot(t, a["xa"]).astype(BF16)
        a["u0"] = _dot(t, a["zp"]).astype(BF16)

    def a_fold(a):
        at = (a["slot"], a["g"], a["q"])
        g_ref[at] = _dot_tn(a["w"], a["yb"]).astype(BF16)
        sp_ref[at] = a["sp"] + _dot_tn(a["u0"], a["yb"])
        yp_ref[at] = a["yp"] + _dot(a["n_rb"], a["u0"])
        xr_ref[at] = (a["xr"].astype(F32) + _dot(a["n_rb"], a["w"])).astype(BF16)
        dec_ref[at] = a["decay"]

    a_stage_list = ([a_scale, a_scores]
                    + [functools.partial(a_double, level=level) for level in range(1, n_levels)]
                    + [a_solve, a_fold])

    def b_state(b):
        at = b["at"]
        s0 = state_ref[b["q"]]
        b["s0b"] = s0.astype(BF16)
        state_ref[b["q"]] = (s0 + _dot(b["s0b"], g_ref[at]) + sp_ref[at]) * dec_ref[at]

    def b_out(b):
        at = b["at"]
        pair, reverse = chain_defs[b["q"]]
        ys = _dot_nt(xr_ref[at], b["s0b"]) + yp_ref[at]
        y_ref[rows_of(b["idx"], reverse), pair * LANES:(pair + 1) * LANES] += ys[:chunk] + ys[chunk:]

    b_stage_list = [b_state, b_out]

    y_ref[...] = jnp.zeros_like(y_ref)
    state_ref[...] = jnp.zeros_like(state_ref)
    n_chains = len(chain_defs)
    first = [a_load(q, g, g, 0) for g in range(group) for q in range(n_chains)]
    for stage in a_stage_list:
        for a in first:
            stage(a)

    def run_step(step, prepare_next):
        slot = step & 1
        base = step * group
        b_work = []
        for g in range(group):
            now = [dict(q=q, at=(slot, g, q), idx=base + g) for q in range(n_chains)]
            b_work += [(stage, now) for stage in b_stage_list]
        a_work = []
        if prepare_next:
            ahead = [a_load(q, g, base + group + g, 1 - slot)
                     for g in range(group) for q in range(n_chains)]
            a_work = [(stage, ahead) for stage in a_stage_list]
        per_a = -(-len(b_work) // max(len(a_work), 1))
        while a_work or b_work:
            for stage, items in b_work[:per_a]:
                for b in items:
                    stage(b)
            b_work = b_work[per_a:]
            if a_work:
                stage, items = a_work.pop(0)
                for a in items:
                    stage(a)

    n_steps = n_chunks // group

    def body(step, carry):
        run_step(step, True)
        return carry

    lax.fori_loop(0, n_steps - 1, body, 0)
    run_step(n_steps - 1, False)


def _wkv_level_masks(chunk):
    s = np.arange(2 * chunk)
    head, t = s // chunk, s % chunk
    n_levels = int(math.log2(chunk))
    masks = np.zeros((2, n_levels, 2 * chunk, 2 * chunk), np.float32)
    for d in range(2):
        pos = t if d == 0 else chunk - 1 - t
        for j in range(n_levels):
            blk = pos >> j
            masks[d, j] = ((head[:, None] == head[None, :]) & (blk[:, None] % 2 == 1)
                           & (blk[None, :] == blk[:, None] - 1))
    return masks


def _wkv(r, k, v, kk, wlf, wlb, af, ab, k_a, seq_len):
    t, d = r.shape
    nb = t // seq_len
    pairs = WKV_PAIRS_PER_STEP
    width = pairs * LANES
    n_chains = 2 * pairs
    seq = pl.BlockSpec((seq_len, width), lambda b, p: (b, p))
    par = pl.BlockSpec((1, width), lambda b, p: (0, p))
    levels = jnp.asarray(_wkv_level_masks(WKV_CHUNK), BF16)
    group = WKV_CHUNKS_PER_STEP
    parked_f32 = pltpu.VMEM((2, group, n_chains, LANES, LANES), F32)
    parked_bf16 = pltpu.VMEM((2, group, n_chains, LANES, LANES), BF16)
    return pl.pallas_call(
        functools.partial(_wkv_kernel, seq_len=seq_len, chunk=WKV_CHUNK, pairs=pairs, group=group),
        grid=(nb, d // width),
        in_specs=[seq] * 8 + [par, _resident(levels.shape)],
        out_specs=seq,
        out_shape=jax.ShapeDtypeStruct((t, d), F32),
        scratch_shapes=[pltpu.VMEM((n_chains, LANES, LANES), F32), parked_bf16, parked_f32,
                        parked_f32, parked_bf16, pltpu.VMEM((2, group, n_chains, 1, LANES), F32)],
        compiler_params=_params("parallel", "parallel"),
        name="wkv7_chunked",
    )(r, k, v, kk, wlf, wlb, af, ab, k_a.reshape(1, d), levels)


def _rwkv_mid_kernel(y_ref, v_ref, g_ref, bonus_ref, lnw_ref, lnb_ref, hsum_ref, hexp_ref, o_ref):
    hsum = hsum_ref[...]
    hexp = hexp_ref[...]

    def head_sum(x):
        return _head_sums(x, hsum, hexp)

    y = y_ref[...]
    yc = y - head_sum(y) * (1.0 / RWKV_HEAD)
    var = head_sum(yc * yc) * (1.0 / RWKV_HEAD)
    yn = yc * lax.rsqrt(var + RWKV_GN_EPS) * lnw_ref[...] + lnb_ref[...]
    b_hi, b_lo = _split_bf16(bonus_ref[...])
    bonus = (_dot(b_hi, hexp) + _dot(b_lo, hexp)) * v_ref[...].astype(F32)
    o_ref[...] = ((yn + bonus) * g_ref[...].astype(F32)).astype(BF16)


def _rwkv_mid(y, v, g, bonus, p):
    t, d = y.shape
    tm = TOKEN_TILE
    tile = pl.BlockSpec((tm, d), lambda i: (i, 0))
    vec = _resident((1, d))
    hsum = _head_sum_matrix()
    return pl.pallas_call(
        _rwkv_mid_kernel,
        grid=(t // tm,),
        in_specs=[tile] * 3 + [pl.BlockSpec((tm, RWKV_HEADS), lambda i: (i, 0))] + [vec] * 2
        + [_resident((d, RWKV_HEADS)), _resident((RWKV_HEADS, d))],
        out_specs=tile,
        out_shape=jax.ShapeDtypeStruct((t, d), BF16),
        compiler_params=_params("parallel"),
        name="rwkv_mid",
    )(y, v, g, bonus, p["lnw"], p["lnb"], hsum, hsum.T)


def _sgu_kernel(x_ref, mod_ref, ng_ref, win_ref, bin_ref, lg_ref, lb_ref, ws_ref, bs_ref, o_ref,
                *, tm):
    h = _norm_mod(x_ref[...], ng_ref[...], mod_ref[1:2, :], mod_ref[0:1, :]).astype(BF16)
    z = _dot(h, win_ref[...]) + bin_ref[...]
    z = 0.5 * z * (1.0 + lax.erf(z * (1.0 / math.sqrt(2.0))))
    u = z[:, :SGU_DIM]
    v = z[:, SGU_DIM:]
    vc = v - jnp.mean(v, axis=-1, keepdims=True)
    vn = vc * lax.rsqrt(jnp.mean(vc * vc, axis=-1, keepdims=True) + 1e-5) * lg_ref[...] + lb_ref[...]
    vb = vn.astype(BF16)
    for c in range(tm // SGU_CHUNK):
        rows = slice(c * SGU_CHUNK, (c + 1) * SGU_CHUNK)
        for g in range(SGU_GROUPS):
            cols = slice(g * SGU_GC, (g + 1) * SGU_GC)
            s = _dot(ws_ref[g], vb[rows, cols]) + bs_ref[:, cols]
            o_ref[rows, cols] = (u[rows, cols] * s).astype(BF16)


def _sgu(x, mod_l, norm_g, p, seq_len):
    t, d = x.shape
    tm = TOKEN_TILE
    tiles_per_seq = seq_len // tm
    return pl.pallas_call(
        functools.partial(_sgu_kernel, tm=tm),
        grid=(t // tm,),
        in_specs=[pl.BlockSpec((tm, d), lambda i: (i, 0)),
                  pl.BlockSpec((None, 6, d), lambda i: (i // tiles_per_seq, 0, 0)),
                  _resident((1, d)),
                  _resident((d, 2 * SGU_DIM)), _resident((1, 2 * SGU_DIM)),
                  _resident((1, SGU_DIM)), _resident((1, SGU_DIM)),
                  _resident((SGU_GROUPS, SGU_CHUNK, SGU_CHUNK)), _resident((SGU_CHUNK, SGU_DIM))],
        out_specs=pl.BlockSpec((tm, SGU_DIM), lambda i: (i, 0)),
        out_shape=jax.ShapeDtypeStruct((t, SGU_DIM), BF16),
        compiler_params=_params("parallel"),
        name="sgu",
    )(x, mod_l, norm_g.reshape(1, d), p["win"], p["bin"], p["ng"], p["nb"], p["ws"], p["bs"])


def _qkv_kernel(x_ref, mod_ref, ng_ref, w_ref, q_ref, k_ref, v_ref):
    d = D_MODEL
    h = _norm_mod(x_ref[...], ng_ref[...], mod_ref[1:2, :], mod_ref[0:1, :]).astype(BF16)
    qkv = _dot(h, w_ref[...])
    q_ref[...] = (qkv[:, :d] * (DIFF_HEAD ** -0.5)).astype(BF16)
    k_ref[...] = qkv[:, d:2 * d].astype(BF16)
    v_ref[...] = qkv[:, 2 * d:].astype(BF16)


def _qkv(x, mod_l, norm_g, w_qkv, seq_len):
    t, d = x.shape
    tm = TOKEN_TILE
    tiles_per_seq = seq_len // tm
    tile = pl.BlockSpec((tm, d), lambda i: (i, 0))
    return pl.pallas_call(
        _qkv_kernel,
        grid=(t // tm,),
        in_specs=[tile, pl.BlockSpec((None, 6, d), lambda i: (i // tiles_per_seq, 0, 0)),
                  _resident((1, d)), _resident((d, 3 * d))],
        out_specs=[tile] * 3,
        out_shape=[jax.ShapeDtypeStruct((t, d), BF16)] * 3,
        compiler_params=_params("parallel"),
        name="diff_qkv",
    )(x, mod_l, norm_g.reshape(1, d), w_qkv)


def _attn_kernel(q_ref, k_ref, v_ref, win_ref, lam_ref, sg_ref, o_ref, vext_ref, *, tq, seq_len,
                 out_scale):
    @pl.when(pl.program_id(2) == 0)
    def _():
        col = lax.broadcasted_iota(jnp.int32, (seq_len, LANES), 1)
        vext_ref[:, :LANES] = v_ref[...]
        vext_ref[:, LANES:] = jnp.where(col == 0, 1.0, 0.0).astype(BF16)

    q = q_ref[...]
    lane = lax.broadcasted_iota(jnp.int32, (1, LANES), 1)
    zero = jnp.zeros_like(q)
    qs = jnp.concatenate([jnp.where(lane < DIFF_HEAD, q, zero),
                          jnp.where(lane < DIFF_HEAD, zero, q)], axis=0)
    s = _dot_nt(qs, k_ref[...])
    width = seq_len + tq
    window = jnp.broadcast_to(win_ref[...], (tq, width))
    bias = pltpu.roll(window, width - tq + 1, 1, stride=1, stride_axis=0)[:, :seq_len]

    def unnormalised(x):
        e = jnp.exp(x - jnp.max(x, axis=-1, keepdims=True)).astype(BF16)
        pv = _dot(e, vext_ref[...])
        return pv[:, :LANES], pv[:, LANES:LANES + 1]

    o1, l1 = unnormalised(s[:tq] + bias)
    o2, l2 = unnormalised(s[tq:] + bias)
    o = o1 / l1 - lam_ref[...] * (o2 / l2)
    o = o * lax.rsqrt(jnp.mean(o * o, axis=-1, keepdims=True) + 1e-5) * sg_ref[...]
    o_ref[...] = (o * out_scale).astype(BF16)


def _t5_bucket(rel):
    nb = NUM_BUCKETS // 2
    max_exact = nb // 2
    ret = jnp.where(rel > 0, nb, 0)
    n = jnp.abs(rel)
    nf = jnp.maximum(n, 1).astype(F32)
    large = max_exact + (jnp.log(nf / max_exact) / math.log(MAX_DISTANCE / max_exact)
                         * (nb - max_exact)).astype(jnp.int32)
    large = jnp.minimum(large, nb - 1)
    return ret + jnp.where(n < max_exact, n, large)


def _bias_windows(rel_bias, seq_len, tq):
    nqb = seq_len // tq
    j = jnp.arange(seq_len + tq, dtype=jnp.int32)[None, :]
    q_hi = (jnp.arange(nqb, dtype=jnp.int32)[:, None] + 1) * tq
    bucket = _t5_bucket(j - q_hi + 1)
    return jnp.transpose(rel_bias[bucket], (0, 2, 1))[:, :, None, :].astype(F32)


def _diff_attention(q, k, v, windows, lam, subln_g, lambda_init, seq_len):
    t, d = q.shape
    nb = t // seq_len
    tq = ATTN_TQ
    nqb = seq_len // tq
    kv = pl.BlockSpec((seq_len, LANES), lambda b, h, i: (b, h))
    qo = pl.BlockSpec((tq, LANES), lambda b, h, i: (b * nqb + i, h))
    return pl.pallas_call(
        functools.partial(_attn_kernel, tq=tq, seq_len=seq_len, out_scale=1.0 - lambda_init),
        grid=(nb, DIFF_HEADS, nqb),
        in_specs=[qo, kv, kv,
                  pl.BlockSpec((None, None, 1, seq_len + tq), lambda b, h, i: (i, h, 0, 0)),
                  pl.BlockSpec((1, 1), lambda b, h, i: (0, 0)),
                  pl.BlockSpec((1, LANES), lambda b, h, i: (0, 0))],
        out_specs=qo,
        out_shape=jax.ShapeDtypeStruct((t, d), BF16),
        scratch_shapes=[pltpu.VMEM((seq_len, 2 * LANES), BF16)],
        compiler_params=_params("parallel", "parallel", "arbitrary"),
        name="diff_attention",
    )(q, k, v, windows, lam.reshape(1, 1), subln_g.reshape(1, LANES))


def _post_kernel(pre_ref, w_ref, b_ref, x_ref, mod_ref, ng_ref, wrh_ref, wrl_ref, br_ref,
                 x1_ref, h2_ref, lg_ref):
    out = _dot(pre_ref[...], w_ref[...]) + b_ref[...]
    x1 = x_ref[...] + mod_ref[2:3, :] * out
    x1_ref[...] = x1
    h2 = _norm_mod(x1, ng_ref[...], mod_ref[4:5, :], mod_ref[3:4, :])
    h2_hi, h2_lo = _split_bf16(h2)
    h2_ref[...] = h2_hi
    lg_ref[...] = (_dot(h2_hi, wrh_ref[...]) + (_dot(h2_hi, wrl_ref[...]) + _dot(h2_lo, wrh_ref[...]))
                   + br_ref[...])


def _post(pre, w, b, x, mod_l, norm_g2, w_router, b_router, seq_len):
    t, d = x.shape
    din = pre.shape[1]
    tm = TOKEN_TILE
    tiles_per_seq = seq_len // tm
    tile = pl.BlockSpec((tm, d), lambda i: (i, 0))
    return pl.pallas_call(
        _post_kernel,
        grid=(t // tm,),
        in_specs=[pl.BlockSpec((tm, din), lambda i: (i, 0)), _resident((din, d)), _resident((1, d)),
                  tile, pl.BlockSpec((None, 6, d), lambda i: (i // tiles_per_seq, 0, 0)),
                  _resident((1, d)), _resident((d, ROUTER_COLS)), _resident((d, ROUTER_COLS)),
                  _resident((1, ROUTER_COLS))],
        out_specs=[tile, tile, pl.BlockSpec((tm, ROUTER_COLS), lambda i: (i, 0))],
        out_shape=[jax.ShapeDtypeStruct((t, d), F32), jax.ShapeDtypeStruct((t, d), BF16),
                   jax.ShapeDtypeStruct((t, ROUTER_COLS), F32)],
        compiler_params=_params("parallel"),
        name="post_router",
    )(pre, w, b, x, mod_l, norm_g2.reshape(1, d), *_split_bf16(w_router), b_router)


def _expert_kernel(be_ref, nu_ref, x_ref, wg_ref, wu_ref, wd_ref, *rest, first_block):
    o_ref, wgb_ref, wub_ref, wdb_ref = rest[-4:]
    i = pl.program_id(0)
    blk = first_block + i
    used = blk < nu_ref[0]
    new_expert = (i == 0) | (be_ref[blk] != be_ref[jnp.maximum(blk - 1, 0)])

    @pl.when(used & new_expert)
    def _():
        wgb_ref[...] = wg_ref[...].astype(BF16)
        wub_ref[...] = wu_ref[...].astype(BF16)
        wdb_ref[...] = wd_ref[...].astype(BF16)

    @pl.when(used)
    def _():
        x = x_ref[...]
        hg = _dot(x, wgb_ref[...])
        hu = _dot(x, wub_ref[...])
        act = hg * _sigmoid(hg) * hu
        o_ref[...] = _dot(act.astype(BF16), wdb_ref[...]).astype(BF16)

    @pl.when(jnp.logical_not(used))
    def _():
        o_ref[...] = jnp.zeros_like(o_ref)


def _experts(xs, block_e, n_used, wg, wu, wd, layer, ys_buf, first_block, total_blocks):
    p_rows, d = xs.shape
    n_blocks = p_rows // MOE_BLOCK

    def weight(shape):
        return pl.BlockSpec((None, None) + shape, lambda i, be, nu: (layer, be[first_block + i], 0, 0))

    in_specs = [pl.BlockSpec((MOE_BLOCK, d), lambda i, be, nu: (i, 0)),
                weight((d, EXPERT_DIM)), weight((d, EXPERT_DIM)), weight((EXPERT_DIM, d))]
    args = [block_e, n_used, xs, wg, wu, wd]
    aliases = {}
    if ys_buf is not None:
        in_specs.append(pl.BlockSpec(memory_space=pl.ANY))
        aliases = {len(args): 0}
        args.append(ys_buf)
    grid_spec = pltpu.PrefetchScalarGridSpec(
        num_scalar_prefetch=2,
        grid=(n_blocks,),
        in_specs=in_specs,
        out_specs=pl.BlockSpec((MOE_BLOCK, d), lambda i, be, nu: (first_block + i, 0)),
        scratch_shapes=[pltpu.VMEM((d, EXPERT_DIM), BF16), pltpu.VMEM((d, EXPERT_DIM), BF16),
                        pltpu.VMEM((EXPERT_DIM, d), BF16)],
    )
    return pl.pallas_call(
        functools.partial(_expert_kernel, first_block=first_block),
        grid_spec=grid_spec,
        out_shape=jax.ShapeDtypeStruct((total_blocks * MOE_BLOCK, d), BF16),
        input_output_aliases=aliases,
        compiler_params=_params("arbitrary"),
        name="moe_experts",
    )(*args)


def _combine_kernel(x_ref, y0_ref, y1_ref, gate_ref, mod_ref, fg_ref, o_ref, *, final):
    gate = gate_ref[...]
    moe = gate[:, 0:1] * y0_ref[...].astype(F32) + gate[:, 1:2] * y1_ref[...].astype(F32)
    x = x_ref[...] + mod_ref[5:6, :] * moe
    if final:
        x = x * lax.rsqrt(jnp.mean(x * x, axis=-1, keepdims=True) + NORM_EPS) * fg_ref[...]
    o_ref[...] = x


def _combine(x_buf, y0, y1, gate, mod_l, final_g, final, seq_len, first_tile):
    t, d = x_buf.shape
    tm = TOKEN_TILE
    tiles_per_seq = seq_len // tm
    here = pl.BlockSpec((tm, d), lambda i: (first_tile + i, 0))
    local = pl.BlockSpec((tm, d), lambda i: (i, 0))
    return pl.pallas_call(
        functools.partial(_combine_kernel, final=final),
        grid=(y0.shape[0] // tm,),
        in_specs=[here, local, local, pl.BlockSpec((tm, MOE_TOP_K), lambda i: (first_tile + i, 0)),
                  pl.BlockSpec((None, 6, d), lambda i: ((first_tile + i) // tiles_per_seq, 0, 0)),
                  _resident((1, d))],
        out_specs=here,
        out_shape=jax.ShapeDtypeStruct((t, d), F32),
        input_output_aliases={0: 0},
        compiler_params=_params("parallel"),
        name="moe_combine",
    )(x_buf, y0, y1, gate, mod_l, final_g.reshape(1, d))


def _route(logits):
    t = logits.shape[0]
    g_logits = logits[:, :MOE_GROUPS]
    e_logits = logits[:, MOE_GROUPS:MOE_GROUPS + N_EXPERTS].reshape(t, MOE_GROUPS, EXPERTS_PER_GROUP)
    grp = jnp.argmax(g_logits, axis=-1).astype(jnp.int32)
    p_grp = 1.0 / jnp.sum(jnp.exp(g_logits - jnp.max(g_logits, axis=-1, keepdims=True)), axis=-1)
    in_grp = grp[:, None, None] == jnp.arange(MOE_GROUPS, dtype=jnp.int32)[None, :, None]
    e_in = jnp.sum(jnp.where(in_grp, e_logits, 0.0), axis=1)
    idx = jnp.arange(EXPERTS_PER_GROUP, dtype=jnp.int32)[None, :]
    i1 = jnp.argmax(e_in, axis=-1).astype(jnp.int32)
    v1 = jnp.max(e_in, axis=-1)
    rest = jnp.where(idx == i1[:, None], -jnp.inf, e_in)
    i2 = jnp.argmax(rest, axis=-1).astype(jnp.int32)
    v2 = jnp.max(rest, axis=-1)
    p2 = jnp.exp(v2 - v1)
    gate = p_grp[:, None] * jnp.stack([1.0 / (1.0 + p2), p2 / (1.0 + p2)], axis=-1)
    expert = grp[:, None] * EXPERTS_PER_GROUP + jnp.stack([i1, i2], axis=-1)
    return expert, gate


def _rank_kernel(e_ref, rank_ref, cnt_ref, carry_ref):
    @pl.when(pl.program_id(0) == 0)
    def _():
        carry_ref[...] = jnp.zeros_like(carry_ref)

    e = e_ref[...]
    expert_id = lax.broadcasted_iota(jnp.int32, (N_EXPERTS, RANK_TILE), 0)
    onehot = jnp.where(e == expert_id, 1.0, 0.0)
    ri = lax.broadcasted_iota(jnp.int32, (RANK_TILE, RANK_TILE), 0)
    ci = lax.broadcasted_iota(jnp.int32, (RANK_TILE, RANK_TILE), 1)
    prefix = _dot(onehot.astype(BF16), (ri <= ci).astype(BF16))
    carry = carry_ref[...]
    rank = jnp.sum(onehot * (prefix + carry), axis=0, keepdims=True) - 1.0
    rank_ref[...] = rank.astype(jnp.int32)
    carry = carry + jnp.sum(onehot, axis=1, keepdims=True)
    carry_ref[...] = carry
    cnt_ref[...] = jnp.broadcast_to(carry, cnt_ref.shape).astype(jnp.int32)


def _rank(e):
    a = e.shape[0]
    n_tiles = a // RANK_TILE
    tile = pl.BlockSpec((None, 1, RANK_TILE), lambda i: (i, 0, 0))
    rank, counts = pl.pallas_call(
        _rank_kernel,
        grid=(n_tiles,),
        in_specs=[tile],
        out_specs=[tile, pl.BlockSpec((N_EXPERTS, LANES), lambda i: (0, 0))],
        out_shape=[jax.ShapeDtypeStruct((n_tiles, 1, RANK_TILE), jnp.int32),
                   jax.ShapeDtypeStruct((N_EXPERTS, LANES), jnp.int32)],
        scratch_shapes=[pltpu.VMEM((N_EXPERTS, 1), F32)],
        compiler_params=_params("arbitrary"),
        name="moe_rank",
    )(e.reshape(n_tiles, 1, RANK_TILE))
    return rank.reshape(a), counts[:, 0]


def _dispatch_plan(expert):
    t = expert.shape[0]
    a = t * MOE_TOP_K
    e = expert.reshape(a)
    rank, counts = _rank(e)
    padded = (counts + MOE_BLOCK - 1) // MOE_BLOCK * MOE_BLOCK
    end_pad = jnp.cumsum(padded)
    start_pad = end_pad - padded
    dest = (start_pad[e] + rank).astype(jnp.int32)
    n_blocks = (a + N_EXPERTS * (MOE_BLOCK - 1) + MOE_BLOCK - 1) // MOE_BLOCK
    row_tok = jnp.zeros((n_blocks * MOE_BLOCK,), jnp.int32).at[dest].set(
        jnp.arange(a, dtype=jnp.int32) // MOE_TOP_K, unique_indices=True, mode="promise_in_bounds")
    block_start = jnp.arange(n_blocks, dtype=jnp.int32) * MOE_BLOCK
    block_e = jnp.sum((end_pad[None, :] <= block_start[:, None]).astype(jnp.int32), axis=1)
    block_e = jnp.minimum(block_e, N_EXPERTS - 1).astype(jnp.int32)
    n_used = (end_pad[-1] // MOE_BLOCK).astype(jnp.int32).reshape(1)
    return dest.reshape(t, MOE_TOP_K), row_tok, block_e, n_used


def _split_count(n, want):
    return max(k for k in range(1, want + 1) if n % k == 0)


def _moe(x1, h2, logits, mod_l, wg, wu, wd, layer, final_g, seq_len):
    expert, gate = _route(logits)
    dest, row_tok, block_e, n_used = _dispatch_plan(expert)
    total_blocks = row_tok.shape[0] // MOE_BLOCK
    n_slices = _split_count(total_blocks, MOE_SLICES)
    per = total_blocks // n_slices
    ys = None
    for c in range(n_slices):
        rows = row_tok[c * per * MOE_BLOCK:(c + 1) * per * MOE_BLOCK]
        ys = _experts(h2[rows], block_e, n_used, wg, wu, wd, layer, ys, c * per, total_blocks)
    final = layer == wg.shape[0] - 1
    n_tiles = x1.shape[0] // TOKEN_TILE
    n_slices = _split_count(n_tiles, MOE_SLICES)
    per = n_tiles // n_slices
    x = x1
    for c in range(n_slices):
        d_c = dest[c * per * TOKEN_TILE:(c + 1) * per * TOKEN_TILE]
        x = _combine(x, ys[d_c[:, 0]], ys[d_c[:, 1]], gate, mod_l, final_g, final, seq_len, c * per)
    return x


def _pad_cols(w, n):
    return jnp.pad(w, ((0, 0), (0, n - w.shape[1])))


def _pad_rows(w, n):
    return jnp.pad(w, ((0, n - w.shape[0]), (0, 0)))


def _rwkv_params(j, mu, wr, wk, wv, w0, w1, w2, a0, a1, a2, v0, v1, v2, g1, g2, kk, ka, rk,
                 lnw, lnb):
    d = D_MODEL
    p = {
        "mu": mu[j],
        "wr": wr[j].astype(BF16), "wk": wk[j].astype(BF16), "wv": wv[j].astype(BF16),
        "wl1": jnp.concatenate([w1[j, 0], w1[j, 1]], axis=1).astype(BF16),
        "wl2": _block_diag2(w2[j, 0], w2[j, 1]).astype(BF16),
        "al1": jnp.concatenate([a1[j, 0], a1[j, 1]], axis=1).astype(BF16),
        "al2": _block_diag2(a2[j, 0], a2[j, 1]).astype(BF16),
        "g1": g1[j].astype(BF16), "g2": g2[j].astype(BF16),
        "w0": w0[j].reshape(1, 2 * d), "a0": a0[j].reshape(1, 2 * d),
        "kk": kk[j].reshape(1, d), "ka": ka[j].reshape(1, d), "rk": rk[j].reshape(1, d),
        "lnw": lnw[j].reshape(1, d), "lnb": lnb[j].reshape(1, d),
    }
    if j > 0:
        p["v0"] = v0[j - 1].reshape(1, d)
        p["v1"] = _pad_cols(v1[j - 1], LANES).astype(BF16)
        p["v2"] = _pad_rows(v2[j - 1], LANES).astype(BF16)
    return p


def kernel(x_prompt, x_sample, c_prompt, c_sample, ada_w, ada_b, norm_g, final_g, rwkv_mu, rwkv_wr, rwkv_wk, rwkv_wv, rwkv_wo, rwkv_w0, rwkv_w1, rwkv_w2, rwkv_a0, rwkv_a1, rwkv_a2, rwkv_v0, rwkv_v1, rwkv_v2, rwkv_g1, rwkv_g2, rwkv_kk, rwkv_ka, rwkv_rk, rwkv_lnw, rwkv_lnb, sgu_win, sgu_bin, sgu_ng, sgu_nb, sgu_ws, sgu_bs, sgu_wout, sgu_bout, diff_wqkv, diff_wo, diff_lq1, diff_lk1, diff_lq2, diff_lk2, diff_subln, rel_bias, moe_wrg, moe_brg, moe_wre, moe_bre, moe_wg, moe_wu, moe_wd):
    d = D_MODEL
    nb_p, seq_len, _ = x_prompt.shape
    assert x_sample.shape[1] == seq_len
    t_p = nb_p * seq_len
    x = jnp.concatenate([x_prompt.reshape(-1, d), x_sample.reshape(-1, d)], axis=0)
    c = jnp.concatenate([c_prompt, c_sample], axis=0)
    mod = _modulation(c, ada_w, ada_b)
    zero_bias = jnp.zeros((1, d), F32)
    v_first = None
    for i in range(DEPTH):
        j = i // N_MIXERS
        mod_l = mod[i]
        if i % N_MIXERS == 0:
            p = _rwkv_params(j, rwkv_mu, rwkv_wr, rwkv_wk, rwkv_wv, rwkv_w0, rwkv_w1, rwkv_w2,
                             rwkv_a0, rwkv_a1, rwkv_a2, rwkv_v0, rwkv_v1, rwkv_v2, rwkv_g1, rwkv_g2,
                             rwkv_kk, rwkv_ka, rwkv_rk, rwkv_lnw, rwkv_lnb)
            r, k, v, g, kk, wlf, wlb, af, ab, bonus = _rwkv_pre(x, mod_l, norm_g[i, 0], p, v_first,
                                                                seq_len)
            if v_first is None:
                v_first = v
            y = _wkv(r, k, v, kk, wlf, wlb, af, ab, p["ka"], seq_len)
            pre = _rwkv_mid(y, v, g, bonus, p)
            w_out, b_out = rwkv_wo[j].astype(BF16), zero_bias
        elif i % N_MIXERS == 1:
            p = {"win": sgu_win[j].astype(BF16), "bin": sgu_bin[j].reshape(1, -1),
                 "ng": sgu_ng[j].reshape(1, -1), "nb": sgu_nb[j].reshape(1, -1),
                 "ws": sgu_ws[j].astype(BF16),
                 "bs": jnp.repeat(jnp.transpose(sgu_bs[j]), SGU_GC, axis=1)}
            pre = _sgu(x, mod_l, norm_g[i, 0], p, seq_len)
            w_out, b_out = sgu_wout[j].astype(BF16), sgu_bout[j].reshape(1, d)
        else:
            lambda_init = 0.8 - 0.6 * math.exp(-0.3 * i)
            lam = (jnp.exp(jnp.sum(diff_lq1[j] * diff_lk1[j])) - jnp.exp(jnp.sum(diff_lq2[j] * diff_lk2[j]))
                   + lambda_init)
            q, k, v = _qkv(x, mod_l, norm_g[i, 0], diff_wqkv[j].astype(BF16), seq_len)
            windows = _bias_windows(rel_bias, seq_len, ATTN_TQ)
            pre = _diff_attention(q, k, v, windows, lam, diff_subln[j], lambda_init, seq_len)
            w_out, b_out = diff_wo[j].astype(BF16), zero_bias
        w_router = _pad_cols(jnp.concatenate([moe_wrg[i], moe_wre[i]], axis=1), ROUTER_COLS)
        b_router = _pad_cols(jnp.concatenate([moe_brg[i], moe_bre[i]])[None, :], ROUTER_COLS)
        x1, h2, logits = _post(pre, w_out, b_out, x, mod_l, norm_g[i, 1], w_router, b_router, seq_len)
        x = _moe(x1, h2, logits, mod_l, moe_wg, moe_wu, moe_wd, i, final_g, seq_len)
    return (x[:t_p].reshape(x_prompt.shape), x[t_p:].reshape(x_sample.shape))
```

```python
import functools
import math

import numpy as np
import jax
import jax.numpy as jnp
from jax import lax
from jax.experimental import pallas as pl
from jax.experimental.pallas import tpu as pltpu

F32 = jnp.float32
BF16 = jnp.bfloat16
HIGHEST = lax.Precision.HIGHEST

D_MODEL = 1024
DEPTH = 4
N_MIXERS = 3
NORM_EPS = 1e-6
LANES = 128
VMEM_LIMIT_BYTES = 56 * 1024 * 1024

RWKV_HEAD = 64
RWKV_HEADS = D_MODEL // RWKV_HEAD
RWKV_PAIRS = D_MODEL // LANES
RWKV_GN_EPS = 64e-5
WKV_CHUNK = 64
WKV_PAIRS_PER_STEP = 2
WKV_CHUNKS_PER_STEP = 4

SGU_CHUNK = 128
SGU_DIM = 2 * D_MODEL
SGU_GROUPS = 8
SGU_GC = SGU_DIM // SGU_GROUPS

DIFF_HEADS = 8
DIFF_HEAD = 64
NUM_BUCKETS = 32
MAX_DISTANCE = 128
ATTN_TQ = 256

MOE_GROUPS = 4
EXPERTS_PER_GROUP = 8
N_EXPERTS = MOE_GROUPS * EXPERTS_PER_GROUP
MOE_TOP_K = 2
EXPERT_DIM = D_MODEL // 2
MOE_BLOCK = 256
ROUTER_COLS = LANES
ROUTE_TILE = 2048
RANK_TILE = 512
MOE_SLICES = 4

TOKEN_TILE = 256


def _params(*semantics):
    return pltpu.CompilerParams(dimension_semantics=semantics, vmem_limit_bytes=VMEM_LIMIT_BYTES)


def _resident(shape):
    zeros = (0,) * len(shape)
    return pl.BlockSpec(shape, lambda *_: zeros, pipeline_mode=pl.Buffered(1))


def _dot(a, b, precision=None):
    return jnp.dot(a, b, preferred_element_type=F32, precision=precision)


def _dot_nt(a, b, precision=None):
    return lax.dot_general(a, b, (((1,), (1,)), ((), ())), preferred_element_type=F32,
                           precision=precision)


def _dot_tn(a, b, precision=None):
    return lax.dot_general(a, b, (((0,), (0,)), ((), ())), preferred_element_type=F32,
                           precision=precision)


def _split_bf16(x):
    hi = x.astype(BF16)
    return hi, (x - hi.astype(F32)).astype(BF16)


def _head_sums(x, hsum, hexp):
    hi, lo = _split_bf16(x)
    shi, slo = _split_bf16(_dot(hi, hsum) + _dot(lo, hsum))
    return _dot(shi, hexp) + _dot(slo, hexp)


def _sigmoid(x):
    return 1.0 / (1.0 + jnp.exp(-x))


def _norm_mod(x, g, scale, shift, eps=NORM_EPS):
    y = x * lax.rsqrt(jnp.mean(x * x, axis=-1, keepdims=True) + eps)
    return (y * g) * (1.0 + scale) + shift


def _mod_kernel(c_ref, w_ref, b_ref, o_ref):
    c = c_ref[...]
    o_ref[...] = _dot(c * _sigmoid(c), w_ref[...], HIGHEST) + b_ref[...]


def _modulation(c, ada_w, ada_b):
    nb = c.shape[0]
    depth, d, n = ada_w.shape
    out = pl.pallas_call(
        _mod_kernel,
        grid=(depth, n // d),
        in_specs=[pl.BlockSpec((nb, d), lambda l, j: (0, 0)),
                  pl.BlockSpec((None, d, d), lambda l, j: (l, 0, j)),
                  pl.BlockSpec((None, 1, d), lambda l, j: (l, 0, j))],
        out_specs=pl.BlockSpec((None, nb, d), lambda l, j: (l, 0, j)),
        out_shape=jax.ShapeDtypeStruct((depth, nb, n), F32),
        compiler_params=_params("parallel", "parallel"),
        name="adaln_modulation",
    )(c, ada_w, ada_b.reshape(depth, 1, n))
    return out.reshape(depth, nb, n // d, d)


def _rwkv_pre_kernel(*refs, tm, tiles_per_seq, has_vres):
    if has_vres:
        (x_ref, xp_ref, xn_ref, mod_ref, ng_ref, mu_ref, wr_ref, wk_ref, wv_ref, wl1_ref, wl2_ref,
         al1_ref, al2_ref, g1_ref, g2_ref, w0_ref, a0_ref, kscale_ref, ka_ref, rk_ref, hsum_ref,
         hexp_ref, vf_ref, v0_ref, v1_ref, v2_ref,
         r_ref, k_ref, v_ref, g_ref, kk_ref, wlf_ref, wlb_ref, af_ref, ab_ref, bonus_ref) = refs
    else:
        (x_ref, xp_ref, xn_ref, mod_ref, ng_ref, mu_ref, wr_ref, wk_ref, wv_ref, wl1_ref, wl2_ref,
         al1_ref, al2_ref, g1_ref, g2_ref, w0_ref, a0_ref, kscale_ref, ka_ref, rk_ref, hsum_ref,
         hexp_ref,
         r_ref, k_ref, v_ref, g_ref, kk_ref, wlf_ref, wlb_ref, af_ref, ab_ref, bonus_ref) = refs
    d = D_MODEL
    i = pl.program_id(0)
    pos = i % tiles_per_seq
    shift, scale = mod_ref[0:1, :], mod_ref[1:2, :]
    ng = ng_ref[...]
    h = _norm_mod(x_ref[...], ng, scale, shift)
    h_prev = jnp.where(pos == 0, 0.0, _norm_mod(xp_ref[7:8, :], ng, scale, shift))
    h_next = jnp.where(pos == tiles_per_seq - 1, 0.0, _norm_mod(xn_ref[0:1, :], ng, scale, shift))
    row = lax.broadcasted_iota(jnp.int32, (tm, 1), 0)
    h_up = jnp.where(row == 0, h_prev, pltpu.roll(h, 1, 0))
    h_dn = jnp.where(row == tm - 1, h_next, pltpu.roll(h, tm - 1, 0))
    xx = 0.5 * (h_up + h_dn) - h

    def mix(n):
        return (h + xx * mu_ref[n:n + 1, :]).astype(BF16)

    xr, xw, xk, xv, xa, xg = [mix(n) for n in range(6)]
    r = _dot(xr, wr_ref[...])
    r_ref[...] = r.astype(BF16)
    k = _dot(xk, wk_ref[...])
    k_ref[...] = k.astype(BF16)
    kkr = k * kscale_ref[...]
    norm = jnp.sqrt(_head_sums(kkr * kkr, hsum_ref[...], hexp_ref[...]))
    kk_ref[...] = (kkr / jnp.maximum(norm, 1e-12)).astype(BF16)
    v = _dot(xv, wv_ref[...])
    if has_vres:
        lv = _dot(_dot(xv, v1_ref[...]).astype(BF16), v2_ref[...])
        v = v + (vf_ref[...].astype(F32) - v) * _sigmoid(v0_ref[...] + lv)
    v_ref[...] = v.astype(BF16)
    g_ref[...] = _dot(_sigmoid(_dot(xg, g1_ref[...])).astype(BF16), g2_ref[...]).astype(BF16)
    wl = w0_ref[...] + _dot(jnp.tanh(_dot(xw, wl1_ref[...])).astype(BF16), wl2_ref[...])
    logw = -math.exp(-0.5) * _sigmoid(wl)
    wlf_ref[...] = logw[:, :d]
    wlb_ref[...] = logw[:, d:]
    a = _sigmoid(a0_ref[...] + _dot(_dot(xa, al1_ref[...]).astype(BF16), al2_ref[...]))
    af_ref[...] = a[:, :d].astype(BF16)
    ab_ref[...] = a[:, d:].astype(BF16)
    k_bonus = k * (1.0 + (0.5 * (a[:, :d] + a[:, d:]) - 1.0) * ka_ref[...])
    hi, lo = _split_bf16(r * k_bonus * rk_ref[...])
    bonus_ref[...] = _dot(hi, hsum_ref[...]) + _dot(lo, hsum_ref[...])


def _head_sum_matrix():
    head_of_lane = np.arange(D_MODEL) // RWKV_HEAD
    return jnp.asarray(head_of_lane[:, None] == np.arange(RWKV_HEADS)[None, :], BF16)


def _block_diag2(m0, m1):
    z = jnp.zeros_like(m0)
    return jnp.concatenate([jnp.concatenate([m0, z], 1), jnp.concatenate([z, m1], 1)], 0)


def _rwkv_pre(x, mod_l, norm_g, p, v_first, seq_len):
    t, d = x.shape
    tm = TOKEN_TILE
    tiles_per_seq = seq_len // tm
    n_tiles = t // tm
    has_vres = v_first is not None
    tile = pl.BlockSpec((tm, d), lambda i: (i, 0))
    rows8 = tm // 8
    last8 = t // 8 - 1
    in_specs = [
        tile,
        pl.BlockSpec((8, d), lambda i: (jnp.maximum(i * rows8 - 1, 0), 0)),
        pl.BlockSpec((8, d), lambda i: (jnp.minimum((i + 1) * rows8, last8), 0)),
        pl.BlockSpec((None, 6, d), lambda i: (i // tiles_per_seq, 0, 0)),
        _resident((1, d)), _resident((6, d)),
        _resident((d, d)), _resident((d, d)), _resident((d, d)),
        _resident((d, LANES)), _resident((LANES, 2 * d)),
        _resident((d, LANES)), _resident((LANES, 2 * d)),
        _resident((d, LANES)), _resident((LANES, d)),
        _resident((1, 2 * d)), _resident((1, 2 * d)),
        _resident((1, d)), _resident((1, d)), _resident((1, d)),
        _resident((d, RWKV_HEADS)), _resident((RWKV_HEADS, d)),
    ]
    hsum = _head_sum_matrix()
    args = [x, x, x, mod_l, norm_g.reshape(1, d), p["mu"],
            p["wr"], p["wk"], p["wv"], p["wl1"], p["wl2"], p["al1"], p["al2"], p["g1"], p["g2"],
            p["w0"], p["a0"], p["kk"], p["ka"], p["rk"], hsum, hsum.T]
    if has_vres:
        in_specs += [tile, _resident((1, d)), _resident((d, LANES)), _resident((LANES, d))]
        args += [v_first, p["v0"], p["v1"], p["v2"]]
    dtypes = [BF16] * 5 + [F32] * 2 + [BF16] * 2
    outs = pl.pallas_call(
        functools.partial(_rwkv_pre_kernel, tm=tm, tiles_per_seq=tiles_per_seq, has_vres=has_vres),
        grid=(n_tiles,),
        in_specs=in_specs,
        out_specs=[tile] * 9 + [pl.BlockSpec((tm, RWKV_HEADS), lambda i: (i, 0))],
        out_shape=[jax.ShapeDtypeStruct((t, d), dt) for dt in dtypes]
        + [jax.ShapeDtypeStruct((t, RWKV_HEADS), F32)],
        compiler_params=_params("parallel"),
        name="rwkv_pre",
    )(*args)
    return outs


def _wkv_kernel(r_ref, k_ref, v_ref, kk_ref, wlf_ref, wlb_ref, af_ref, ab_ref, ka_ref,
                lvl_ref, y_ref, state_ref, g_ref, sp_ref, yp_ref, xr_ref, dec_ref,
                *, seq_len, chunk, pairs, group):
    n_chunks = seq_len // chunk
    n_levels = int(math.log2(chunk))
    two = 2 * chunk
    lane = lax.broadcasted_iota(jnp.int32, (1, LANES), 1)
    head0 = lane < RWKV_HEAD
    ri = lax.broadcasted_iota(jnp.int32, (chunk, chunk), 0)
    ci = lax.broadcasted_iota(jnp.int32, (chunk, chunk), 1)
    si = lax.broadcasted_iota(jnp.int32, (two, two), 0)
    sj = lax.broadcasted_iota(jnp.int32, (two, two), 1)
    same_head = (si < chunk) == (sj < chunk)
    st, su = si & (chunk - 1), sj & (chunk - 1)
    eye = (si == sj).astype(F32)
    chain_defs = [(pair, reverse) for pair in range(pairs) for reverse in (False, True)]

    def rows_of(idx, reverse):
        pos = (n_chunks - 1 - idx) if reverse else idx
        return pl.ds(pl.multiple_of(pos * chunk, chunk), chunk)

    def stack(x):
        return jnp.concatenate([jnp.where(head0, x, 0.0), jnp.where(head0, 0.0, x)], axis=0)

    def a_load(q, g, idx, slot):
        pair, reverse = chain_defs[q]
        rows = rows_of(idx, reverse)
        lanes = slice(pair * LANES, (pair + 1) * LANES)
        a = dict(q=q, g=g, slot=slot, rows=rows, lanes=lanes, reverse=reverse)
        a["lw"] = lw = (wlb_ref if reverse else wlf_ref)[rows, lanes]
        a["a"] = (ab_ref if reverse else af_ref)[rows, lanes].astype(F32)
        cum = ((ri <= ci) if reverse else (ri >= ci)).astype(BF16)
        lw_hi = lw.astype(BF16)
        lw_mid, lw_lo = _split_bf16(lw - lw_hi.astype(F32))
        a["logp"] = _dot(cum, lw_hi) + (_dot(cum, lw_mid) + _dot(cum, lw_lo))
        return a

    def a_scale(a):
        rows, lanes, logp, lw, av = a["rows"], a["lanes"], a["logp"], a["lw"], a["a"]
        k = k_ref[rows, lanes].astype(F32)
        kk = kk_ref[rows, lanes].astype(F32)
        kd = k * (1.0 + (av - 1.0) * ka_ref[:, lanes])
        p_in = jnp.exp(logp)
        p_out = jnp.exp(-logp)
        a["xa"] = stack(-kk * jnp.exp(logp - lw)).astype(BF16)
        a["xr"] = stack(r_ref[rows, lanes].astype(F32) * p_in).astype(BF16)
        a["yb"] = stack(kk * av * p_out).astype(BF16)
        a["yk"] = stack(kd * p_out).astype(BF16)
        a["vs"] = stack(v_ref[rows, lanes])
        a["decay"] = jnp.exp(jnp.sum(lw, axis=0, keepdims=True))

    def a_scores(a):
        if a["reverse"]:
            strict, incl = same_head & (st < su), same_head & (st <= su)
        else:
            strict, incl = same_head & (st > su), same_head & (st >= su)
        sc = _dot_nt(jnp.concatenate([a["xa"], a["xr"]], 0),
                     jnp.concatenate([a["yb"], a["yk"]], 0))
        a["m"] = sc[:two, :two].astype(BF16)
        m_ak = jnp.where(strict, sc[:two, two:], 0.0).astype(BF16)
        n_rk = jnp.where(incl, sc[two:, two:], 0.0).astype(BF16)
        a["n_rb"] = jnp.where(incl, sc[two:, :two], 0.0).astype(BF16)
        a["inv"] = eye + (a["m"] * lvl_ref[int(a["reverse"]), 0]).astype(F32)
        a["zp"] = _dot(m_ak, a["vs"]).astype(BF16)
        a["yp"] = _dot(n_rk, a["vs"])
        a["sp"] = _dot_tn(a["vs"], a["yk"])

    def a_double(a, level):
        t = a["inv"].astype(BF16)
        m_n = a["m"] * lvl_ref[int(a["reverse"]), level]
        a["inv"] = a["inv"] + _dot(_dot(t, m_n).astype(BF16), t)

    def a_solve(a):
        t = a["inv"].astype(BF16)
        a["w"] = _dot(t, a["xa"]).astype(BF16)
        a["u0"] = _dot(t, a["zp"]).astype(BF16)

    def a_fold(a):
        at = (a["slot"], a["g"], a["q"])
        g_ref[at] = _dot_tn(a["w"], a["yb"]).astype(BF16)
        sp_ref[at] = a["sp"] + _dot_tn(a["u0"], a["yb"])
        yp_ref[at] = a["yp"] + _dot(a["n_rb"], a["u0"])
        xr_ref[at] = (a["xr"].astype(F32) + _dot(a["n_rb"], a["w"])).astype(BF16)
        dec_ref[at] = a["decay"]

    a_stage_list = ([a_scale, a_scores]
                    + [functools.partial(a_double, level=level) for level in range(1, n_levels)]
                    + [a_solve, a_fold])

    def b_state(b):
        at = b["at"]
        s0 = state_ref[b["q"]]
        b["s0b"] = s0.astype(BF16)
        state_ref[b["q"]] = (s0 + _dot(b["s0b"], g_ref[at]) + sp_ref[at]) * dec_ref[at]

    def b_out(b):
        at = b["at"]
        pair, reverse = chain_defs[b["q"]]
        ys = _dot_nt(xr_ref[at], b["s0b"]) + yp_ref[at]
        y_ref[rows_of(b["idx"], reverse), pair * LANES:(pair + 1) * LANES] += ys[:chunk] + ys[chunk:]

    b_stage_list = [b_state, b_out]

    y_ref[...] = jnp.zeros_like(y_ref)
    state_ref[...] = jnp.zeros_like(state_ref)
    n_chains = len(chain_defs)
    first = [a_load(q, g, g, 0) for g in range(group) for q in range(n_chains)]
    for stage in a_stage_list:
        for a in first:
            stage(a)

    def run_step(step, prepare_next):
        slot = step & 1
        base = step * group
        b_work = []
        for g in range(group):
            now = [dict(q=q, at=(slot, g, q), idx=base + g) for q in range(n_chains)]
            b_work += [(stage, now) for stage in b_stage_list]
        a_work = []
        if prepare_next:
            ahead = [a_load(q, g, base + group + g, 1 - slot)
                     for g in range(group) for q in range(n_chains)]
            a_work = [(stage, ahead) for stage in a_stage_list]
        per_a = -(-len(b_work) // max(len(a_work), 1))
        while a_work or b_work:
            for stage, items in b_work[:per_a]:
                for b in items:
                    stage(b)
            b_work = b_work[per_a:]
            if a_work:
                stage, items = a_work.pop(0)
                for a in items:
                    stage(a)

    n_steps = n_chunks // group

    def body(step, carry):
        run_step(step, True)
        return carry

    lax.fori_loop(0, n_steps - 1, body, 0)
    run_step(n_steps - 1, False)


def _wkv_level_masks(chunk):
    s = np.arange(2 * chunk)
    head, t = s // chunk, s % chunk
    n_levels = int(math.log2(chunk))
    masks = np.zeros((2, n_levels, 2 * chunk, 2 * chunk), np.float32)
    for d in range(2):
        pos = t if d == 0 else chunk - 1 - t
        for j in range(n_levels):
            blk = pos >> j
            masks[d, j] = ((head[:, None] == head[None, :]) & (blk[:, None] % 2 == 1)
                           & (blk[None, :] == blk[:, None] - 1))
    return masks


def _wkv(r, k, v, kk, wlf, wlb, af, ab, k_a, seq_len):
    t, d = r.shape
    nb = t // seq_len
    pairs = WKV_PAIRS_PER_STEP
    width = pairs * LANES
    n_chains = 2 * pairs
    seq = pl.BlockSpec((seq_len, width), lambda b, p: (b, p))
    par = pl.BlockSpec((1, width), lambda b, p: (0, p))
    levels = jnp.asarray(_wkv_level_masks(WKV_CHUNK), BF16)
    group = WKV_CHUNKS_PER_STEP
    parked_f32 = pltpu.VMEM((2, group, n_chains, LANES, LANES), F32)
    parked_bf16 = pltpu.VMEM((2, group, n_chains, LANES, LANES), BF16)
    return pl.pallas_call(
        functools.partial(_wkv_kernel, seq_len=seq_len, chunk=WKV_CHUNK, pairs=pairs, group=group),
        grid=(nb, d // width),
        in_specs=[seq] * 8 + [par, _resident(levels.shape)],
        out_specs=seq,
        out_shape=jax.ShapeDtypeStruct((t, d), F32),
        scratch_shapes=[pltpu.VMEM((n_chains, LANES, LANES), F32), parked_bf16, parked_f32,
                        parked_f32, parked_bf16, pltpu.VMEM((2, group, n_chains, 1, LANES), F32)],
        compiler_params=_params("parallel", "parallel"),
        name="wkv7_chunked",
    )(r, k, v, kk, wlf, wlb, af, ab, k_a.reshape(1, d), levels)


def _rwkv_mid_kernel(y_ref, v_ref, g_ref, bonus_ref, lnw_ref, lnb_ref, hsum_ref, hexp_ref, o_ref):
    hsum = hsum_ref[...]
    hexp = hexp_ref[...]

    def head_sum(x):
        return _head_sums(x, hsum, hexp)

    y = y_ref[...]
    yc = y - head_sum(y) * (1.0 / RWKV_HEAD)
    var = head_sum(yc * yc) * (1.0 / RWKV_HEAD)
    yn = yc * lax.rsqrt(var + RWKV_GN_EPS) * lnw_ref[...] + lnb_ref[...]
    b_hi, b_lo = _split_bf16(bonus_ref[...])
    bonus = (_dot(b_hi, hexp) + _dot(b_lo, hexp)) * v_ref[...].astype(F32)
    o_ref[...] = ((yn + bonus) * g_ref[...].astype(F32)).astype(BF16)


def _rwkv_mid(y, v, g, bonus, p):
    t, d = y.shape
    tm = TOKEN_TILE
    tile = pl.BlockSpec((tm, d), lambda i: (i, 0))
    vec = _resident((1, d))
    hsum = _head_sum_matrix()
    return pl.pallas_call(
        _rwkv_mid_kernel,
        grid=(t // tm,),
        in_specs=[tile] * 3 + [pl.BlockSpec((tm, RWKV_HEADS), lambda i: (i, 0))] + [vec] * 2
        + [_resident((d, RWKV_HEADS)), _resident((RWKV_HEADS, d))],
        out_specs=tile,
        out_shape=jax.ShapeDtypeStruct((t, d), BF16),
        compiler_params=_params("parallel"),
        name="rwkv_mid",
    )(y, v, g, bonus, p["lnw"], p["lnb"], hsum, hsum.T)


def _sgu_kernel(x_ref, mod_ref, ng_ref, win_ref, bin_ref, lg_ref, lb_ref, ws_ref, bs_ref, o_ref,
                *, tm):
    h = _norm_mod(x_ref[...], ng_ref[...], mod_ref[1:2, :], mod_ref[0:1, :]).astype(BF16)
    z = _dot(h, win_ref[...]) + bin_ref[...]
    z = 0.5 * z * (1.0 + lax.erf(z * (1.0 / math.sqrt(2.0))))
    u = z[:, :SGU_DIM]
    v = z[:, SGU_DIM:]
    vc = v - jnp.mean(v, axis=-1, keepdims=True)
    vn = vc * lax.rsqrt(jnp.mean(vc * vc, axis=-1, keepdims=True) + 1e-5) * lg_ref[...] + lb_ref[...]
    vb = vn.astype(BF16)
    for c in range(tm // SGU_CHUNK):
        rows = slice(c * SGU_CHUNK, (c + 1) * SGU_CHUNK)
        for g in range(SGU_GROUPS):
            cols = slice(g * SGU_GC, (g + 1) * SGU_GC)
            s = _dot(ws_ref[g], vb[rows, cols]) + bs_ref[:, cols]
            o_ref[rows, cols] = (u[rows, cols] * s).astype(BF16)


def _sgu(x, mod_l, norm_g, p, seq_len):
    t, d = x.shape
    tm = TOKEN_TILE
    tiles_per_seq = seq_len // tm
    return pl.pallas_call(
        functools.partial(_sgu_kernel, tm=tm),
        grid=(t // tm,),
        in_specs=[pl.BlockSpec((tm, d), lambda i: (i, 0)),
                  pl.BlockSpec((None, 6, d), lambda i: (i // tiles_per_seq, 0, 0)),
                  _resident((1, d)),
                  _resident((d, 2 * SGU_DIM)), _resident((1, 2 * SGU_DIM)),
                  _resident((1, SGU_DIM)), _resident((1, SGU_DIM)),
                  _resident((SGU_GROUPS, SGU_CHUNK, SGU_CHUNK)), _resident((SGU_CHUNK, SGU_DIM))],
        out_specs=pl.BlockSpec((tm, SGU_DIM), lambda i: (i, 0)),
        out_shape=jax.ShapeDtypeStruct((t, SGU_DIM), BF16),
        compiler_params=_params("parallel"),
        name="sgu",
    )(x, mod_l, norm_g.reshape(1, d), p["win"], p["bin"], p["ng"], p["nb"], p["ws"], p["bs"])


def _qkv_kernel(x_ref, mod_ref, ng_ref, w_ref, q_ref, k_ref, v_ref):
    d = D_MODEL
    h = _norm_mod(x_ref[...], ng_ref[...], mod_ref[1:2, :], mod_ref[0:1, :]).astype(BF16)
    qkv = _dot(h, w_ref[...])
    q_ref[...] = (qkv[:, :d] * (DIFF_HEAD ** -0.5)).astype(BF16)
    k_ref[...] = qkv[:, d:2 * d].astype(BF16)
    v_ref[...] = qkv[:, 2 * d:].astype(BF16)


def _qkv(x, mod_l, norm_g, w_qkv, seq_len):
    t, d = x.shape
    tm = TOKEN_TILE
    tiles_per_seq = seq_len // tm
    tile = pl.BlockSpec((tm, d), lambda i: (i, 0))
    return pl.pallas_call(
        _qkv_kernel,
        grid=(t // tm,),
        in_specs=[tile, pl.BlockSpec((None, 6, d), lambda i: (i // tiles_per_seq, 0, 0)),
                  _resident((1, d)), _resident((d, 3 * d))],
        out_specs=[tile] * 3,
        out_shape=[jax.ShapeDtypeStruct((t, d), BF16)] * 3,
        compiler_params=_params("parallel"),
        name="diff_qkv",
    )(x, mod_l, norm_g.reshape(1, d), w_qkv)


def _attn_kernel(q_ref, k_ref, v_ref, win_ref, lam_ref, sg_ref, o_ref, vext_ref, *, tq, seq_len,
                 out_scale):
    @pl.when(pl.program_id(2) == 0)
    def _():
        col = lax.broadcasted_iota(jnp.int32, (seq_len, LANES), 1)
        vext_ref[:, :LANES] = v_ref[...]
        vext_ref[:, LANES:] = jnp.where(col == 0, 1.0, 0.0).astype(BF16)

    q = q_ref[...]
    lane = lax.broadcasted_iota(jnp.int32, (1, LANES), 1)
    zero = jnp.zeros_like(q)
    qs = jnp.concatenate([jnp.where(lane < DIFF_HEAD, q, zero),
                          jnp.where(lane < DIFF_HEAD, zero, q)], axis=0)
    s = _dot_nt(qs, k_ref[...])
    width = seq_len + tq
    window = jnp.broadcast_to(win_ref[...], (tq, width))
    bias = pltpu.roll(window, width - tq + 1, 1, stride=1, stride_axis=0)[:, :seq_len]

    def unnormalised(x):
        e = jnp.exp(x - jnp.max(x, axis=-1, keepdims=True)).astype(BF16)
        pv = _dot(e, vext_ref[...])
        return pv[:, :LANES], pv[:, LANES:LANES + 1]

    o1, l1 = unnormalised(s[:tq] + bias)
    o2, l2 = unnormalised(s[tq:] + bias)
    o = o1 / l1 - lam_ref[...] * (o2 / l2)
    o = o * lax.rsqrt(jnp.mean(o * o, axis=-1, keepdims=True) + 1e-5) * sg_ref[...]
    o_ref[...] = (o * out_scale).astype(BF16)


def _t5_bucket(rel):
    nb = NUM_BUCKETS // 2
    max_exact = nb // 2
    ret = jnp.where(rel > 0, nb, 0)
    n = jnp.abs(rel)
    nf = jnp.maximum(n, 1).astype(F32)
    large = max_exact + (jnp.log(nf / max_exact) / math.log(MAX_DISTANCE / max_exact)
                         * (nb - max_exact)).astype(jnp.int32)
    large = jnp.minimum(large, nb - 1)
    return ret + jnp.where(n < max_exact, n, large)


def _bias_windows(rel_bias, seq_len, tq):
    nqb = seq_len // tq
    j = jnp.arange(seq_len + tq, dtype=jnp.int32)[None, :]
    q_hi = (jnp.arange(nqb, dtype=jnp.int32)[:, None] + 1) * tq
    bucket = _t5_bucket(j - q_hi + 1)
    return jnp.transpose(rel_bias[bucket], (0, 2, 1))[:, :, None, :].astype(F32)


def _diff_attention(q, k, v, windows, lam, subln_g, lambda_init, seq_len):
    t, d = q.shape
    nb = t // seq_len
    tq = ATTN_TQ
    nqb = seq_len // tq
    kv = pl.BlockSpec((seq_len, LANES), lambda b, h, i: (b, h))
    qo = pl.BlockSpec((tq, LANES), lambda b, h, i: (b * nqb + i, h))
    return pl.pallas_call(
        functools.partial(_attn_kernel, tq=tq, seq_len=seq_len, out_scale=1.0 - lambda_init),
        grid=(nb, DIFF_HEADS, nqb),
        in_specs=[qo, kv, kv,
                  pl.BlockSpec((None, None, 1, seq_len + tq), lambda b, h, i: (i, h, 0, 0)),
                  pl.BlockSpec((1, 1), lambda b, h, i: (0, 0)),
                  pl.BlockSpec((1, LANES), lambda b, h, i: (0, 0))],
        out_specs=qo,
        out_shape=jax.ShapeDtypeStruct((t, d), BF16),
        scratch_shapes=[pltpu.VMEM((seq_len, 2 * LANES), BF16)],
        compiler_params=_params("parallel", "parallel", "arbitrary"),
        name="diff_attention",
    )(q, k, v, windows, lam.reshape(1, 1), subln_g.reshape(1, LANES))


def _post_kernel(pre_ref, w_ref, b_ref, x_ref, mod_ref, ng_ref, wrh_ref, wrl_ref, br_ref,
                 x1_ref, h2_ref, lg_ref):
    out = _dot(pre_ref[...], w_ref[...]) + b_ref[...]
    x1 = x_ref[...] + mod_ref[2:3, :] * out
    x1_ref[...] = x1
    h2 = _norm_mod(x1, ng_ref[...], mod_ref[4:5, :], mod_ref[3:4, :])
    h2_hi, h2_lo = _split_bf16(h2)
    h2_ref[...] = h2_hi
    lg_ref[...] = (_dot(h2_hi, wrh_ref[...]) + (_dot(h2_hi, wrl_ref[...]) + _dot(h2_lo, wrh_ref[...]))
                   + br_ref[...])


def _route(lg):
    lane = lax.broadcasted_iota(jnp.int32, lg.shape, 1)
    first_e, end_e = MOE_GROUPS, MOE_GROUPS + N_EXPERTS
    neg = -jnp.inf

    def max_and_first(x):
        m = jnp.max(x, axis=-1, keepdims=True)
        return m, jnp.min(jnp.where(x == m, lane, LANES), axis=-1, keepdims=True)

    is_group = lane < first_e
    g_max, grp = max_and_first(jnp.where(is_group, lg, neg))
    p_grp = 1.0 / jnp.sum(jnp.where(is_group, jnp.exp(lg - g_max), 0.0), axis=-1, keepdims=True)
    lane_grp = (lane - first_e) >> int(math.log2(EXPERTS_PER_GROUP))
    in_grp = (lane >= first_e) & (lane < end_e) & (lane_grp == grp)
    cand = jnp.where(in_grp, lg, neg)
    v1, i1 = max_and_first(cand)
    v2, i2 = max_and_first(jnp.where(lane == i1, neg, cand))
    p2 = jnp.exp(v2 - v1)
    gate1 = p_grp / (1.0 + p2)
    gate2 = gate1 * p2
    e1 = (i1 - first_e).astype(F32)
    e2 = (i2 - first_e).astype(F32)
    return jnp.where(lane == 0, e1, jnp.where(lane == 1, e2, jnp.where(lane == 2, gate1,
                                                                    jnp.where(lane == 3, gate2, 0.0))))


def _route_kernel(lg_ref, route_ref, route_t_ref):
    route = _route(lg_ref[...])
    route_ref[...] = route
    route_t_ref[...] = route.T[:8, :]


def _routing(logits):
    t = logits.shape[0]
    tr = max(k for k in range(LANES, ROUTE_TILE + 1, LANES) if t % k == 0)
    return pl.pallas_call(
        _route_kernel,
        grid=(t // tr,),
        in_specs=[pl.BlockSpec((tr, ROUTER_COLS), lambda i: (i, 0))],
        out_specs=[pl.BlockSpec((tr, ROUTER_COLS), lambda i: (i, 0)),
                   pl.BlockSpec((None, 8, tr), lambda i: (i, 0, 0))],
        out_shape=[jax.ShapeDtypeStruct((t, ROUTER_COLS), F32),
                   jax.ShapeDtypeStruct((t // tr, 8, tr), F32)],
        compiler_params=_params("parallel"),
        name="moe_route",
    )(logits)


def _post(pre, w, b, x, mod_l, norm_g2, w_router, b_router, seq_len):
    t, d = x.shape
    din = pre.shape[1]
    tm = TOKEN_TILE
    tiles_per_seq = seq_len // tm
    tile = pl.BlockSpec((tm, d), lambda i: (i, 0))
    return pl.pallas_call(
        _post_kernel,
        grid=(t // tm,),
        in_specs=[pl.BlockSpec((tm, din), lambda i: (i, 0)), _resident((din, d)), _resident((1, d)),
                  tile, pl.BlockSpec((None, 6, d), lambda i: (i // tiles_per_seq, 0, 0)),
                  _resident((1, d)), _resident((d, ROUTER_COLS)), _resident((d, ROUTER_COLS)),
                  _resident((1, ROUTER_COLS))],
        out_specs=[tile, tile, pl.BlockSpec((tm, ROUTER_COLS), lambda i: (i, 0))],
        out_shape=[jax.ShapeDtypeStruct((t, d), F32), jax.ShapeDtypeStruct((t, d), BF16),
                   jax.ShapeDtypeStruct((t, ROUTER_COLS), F32)],
        compiler_params=_params("parallel"),
        name="post_router",
    )(pre, w, b, x, mod_l, norm_g2.reshape(1, d), *_split_bf16(w_router), b_router)


def _expert_kernel(be_ref, nu_ref, x_ref, wg_ref, wu_ref, wd_ref, *rest, first_block):
    o_ref, wgb_ref, wub_ref, wdb_ref = rest[-4:]
    i = pl.program_id(0)
    blk = first_block + i
    used = blk < nu_ref[0]
    new_expert = (i == 0) | (be_ref[blk] != be_ref[jnp.maximum(blk - 1, 0)])

    @pl.when(used & new_expert)
    def _():
        wgb_ref[...] = wg_ref[...].astype(BF16)
        wub_ref[...] = wu_ref[...].astype(BF16)
        wdb_ref[...] = wd_ref[...].astype(BF16)

    @pl.when(used)
    def _():
        x = x_ref[...]
        hg = _dot(x, wgb_ref[...])
        hu = _dot(x, wub_ref[...])
        act = hg * _sigmoid(hg) * hu
        o_ref[...] = _dot(act.astype(BF16), wdb_ref[...]).astype(BF16)

    @pl.when(jnp.logical_not(used))
    def _():
        o_ref[...] = jnp.zeros_like(o_ref)


def _experts(xs, block_e, n_used, wg, wu, wd, layer, ys_buf, first_block, total_blocks):
    p_rows, d = xs.shape
    n_blocks = p_rows // MOE_BLOCK

    def weight(shape):
        return pl.BlockSpec((None, None) + shape, lambda i, be, nu: (layer, be[first_block + i], 0, 0))

    in_specs = [pl.BlockSpec((MOE_BLOCK, d), lambda i, be, nu: (i, 0)),
                weight((d, EXPERT_DIM)), weight((d, EXPERT_DIM)), weight((EXPERT_DIM, d))]
    args = [block_e, n_used, xs, wg, wu, wd]
    aliases = {}
    if ys_buf is not None:
        in_specs.append(pl.BlockSpec(memory_space=pl.ANY))
        aliases = {len(args): 0}
        args.append(ys_buf)
    grid_spec = pltpu.PrefetchScalarGridSpec(
        num_scalar_prefetch=2,
        grid=(n_blocks,),
        in_specs=in_specs,
        out_specs=pl.BlockSpec((MOE_BLOCK, d), lambda i, be, nu: (first_block + i, 0)),
        scratch_shapes=[pltpu.VMEM((d, EXPERT_DIM), BF16), pltpu.VMEM((d, EXPERT_DIM), BF16),
                        pltpu.VMEM((EXPERT_DIM, d), BF16)],
    )
    return pl.pallas_call(
        functools.partial(_expert_kernel, first_block=first_block),
        grid_spec=grid_spec,
        out_shape=jax.ShapeDtypeStruct((total_blocks * MOE_BLOCK, d), BF16),
        input_output_aliases=aliases,
        compiler_params=_params("arbitrary"),
        name="moe_experts",
    )(*args)


def _combine_kernel(x_ref, y0_ref, y1_ref, route_ref, mod_ref, fg_ref, o_ref, *, final):
    route = route_ref[...]
    moe = route[:, 2:3] * y0_ref[...].astype(F32) + route[:, 3:4] * y1_ref[...].astype(F32)
    x = x_ref[...] + mod_ref[5:6, :] * moe
    if final:
        x = x * lax.rsqrt(jnp.mean(x * x, axis=-1, keepdims=True) + NORM_EPS) * fg_ref[...]
    o_ref[...] = x


def _combine(x_buf, y0, y1, route, mod_l, final_g, final, seq_len, first_tile):
    t, d = x_buf.shape
    tm = TOKEN_TILE
    tiles_per_seq = seq_len // tm
    here = pl.BlockSpec((tm, d), lambda i: (first_tile + i, 0))
    local = pl.BlockSpec((tm, d), lambda i: (i, 0))
    return pl.pallas_call(
        functools.partial(_combine_kernel, final=final),
        grid=(y0.shape[0] // tm,),
        in_specs=[here, local, local, pl.BlockSpec((tm, ROUTER_COLS), lambda i: (first_tile + i, 0)),
                  pl.BlockSpec((None, 6, d), lambda i: ((first_tile + i) // tiles_per_seq, 0, 0)),
                  _resident((1, d))],
        out_specs=here,
        out_shape=jax.ShapeDtypeStruct((t, d), F32),
        input_output_aliases={0: 0},
        compiler_params=_params("parallel"),
        name="moe_combine",
    )(x_buf, y0, y1, route, mod_l, final_g.reshape(1, d))


def _rank_kernel(e_ref, rank_ref, cnt_ref, carry_ref):
    @pl.when(pl.program_id(0) == 0)
    def _():
        carry_ref[...] = jnp.zeros_like(carry_ref)

    e = e_ref[...]
    expert_id = lax.broadcasted_iota(jnp.int32, (N_EXPERTS, RANK_TILE), 0)
    onehot = jnp.where(e == expert_id, 1.0, 0.0)
    ri = lax.broadcasted_iota(jnp.int32, (RANK_TILE, RANK_TILE), 0)
    ci = lax.broadcasted_iota(jnp.int32, (RANK_TILE, RANK_TILE), 1)
    prefix = _dot(onehot.astype(BF16), (ri <= ci).astype(BF16))
    carry = carry_ref[...]
    rank = jnp.sum(onehot * (prefix + carry), axis=0, keepdims=True) - 1.0
    rank_ref[...] = rank.astype(jnp.int32)
    carry = carry + jnp.sum(onehot, axis=1, keepdims=True)
    carry_ref[...] = carry
    cnt_ref[...] = jnp.broadcast_to(carry, cnt_ref.shape).astype(jnp.int32)


def _rank(e):
    a = e.shape[0]
    n_tiles = a // RANK_TILE
    tile = pl.BlockSpec((None, 1, RANK_TILE), lambda i: (i, 0, 0))
    rank, counts = pl.pallas_call(
        _rank_kernel,
        grid=(n_tiles,),
        in_specs=[tile],
        out_specs=[tile, pl.BlockSpec((N_EXPERTS, LANES), lambda i: (0, 0))],
        out_shape=[jax.ShapeDtypeStruct((n_tiles, 1, RANK_TILE), jnp.int32),
                   jax.ShapeDtypeStruct((N_EXPERTS, LANES), jnp.int32)],
        scratch_shapes=[pltpu.VMEM((N_EXPERTS, 1), F32)],
        compiler_params=_params("arbitrary"),
        name="moe_rank",
    )(e.reshape(n_tiles, 1, RANK_TILE))
    return rank.reshape(a), counts[:, 0]


def _dispatch_plan(e, t):
    a = t * MOE_TOP_K
    rank, counts = _rank(e)
    padded = (counts + MOE_BLOCK - 1) // MOE_BLOCK * MOE_BLOCK
    end_pad = jnp.cumsum(padded)
    start_pad = end_pad - padded
    dest = (start_pad[e] + rank).astype(jnp.int32)
    n_blocks = (a + N_EXPERTS * (MOE_BLOCK - 1) + MOE_BLOCK - 1) // MOE_BLOCK
    token = jnp.tile(jnp.arange(t, dtype=jnp.int32), MOE_TOP_K)
    row_tok = jnp.zeros((n_blocks * MOE_BLOCK,), jnp.int32).at[dest].set(
        token, unique_indices=True, mode="promise_in_bounds")
    block_start = jnp.arange(n_blocks, dtype=jnp.int32) * MOE_BLOCK
    block_e = jnp.sum((end_pad[None, :] <= block_start[:, None]).astype(jnp.int32), axis=1)
    block_e = jnp.minimum(block_e, N_EXPERTS - 1).astype(jnp.int32)
    n_used = (end_pad[-1] // MOE_BLOCK).astype(jnp.int32).reshape(1)
    return dest.reshape(MOE_TOP_K, t), row_tok, block_e, n_used


def _split_count(n, want):
    return max(k for k in range(1, want + 1) if n % k == 0)


def _moe(x1, h2, logits, mod_l, wg, wu, wd, layer, final_g, seq_len):
    t = x1.shape[0]
    route, route_t = _routing(logits)
    e = jnp.concatenate([route_t[:, 0, :].reshape(t), route_t[:, 1, :].reshape(t)]).astype(jnp.int32)
    dest, row_tok, block_e, n_used = _dispatch_plan(e, t)
    total_blocks = row_tok.shape[0] // MOE_BLOCK
    n_slices = _split_count(total_blocks, MOE_SLICES)
    per = total_blocks // n_slices
    ys = None
    for c in range(n_slices):
        rows = row_tok[c * per * MOE_BLOCK:(c + 1) * per * MOE_BLOCK]
        ys = _experts(h2[rows], block_e, n_used, wg, wu, wd, layer, ys, c * per, total_blocks)
    final = layer == wg.shape[0] - 1
    n_tiles = x1.shape[0] // TOKEN_TILE
    n_slices = _split_count(n_tiles, MOE_SLICES)
    per = n_tiles // n_slices
    x = x1
    for c in range(n_slices):
        d_c = dest[:, c * per * TOKEN_TILE:(c + 1) * per * TOKEN_TILE]
        x = _combine(x, ys[d_c[0]], ys[d_c[1]], route, mod_l, final_g, final, seq_len, c * per)
    return x


def _pad_cols(w, n):
    return jnp.pad(w, ((0, 0), (0, n - w.shape[1])))


def _pad_rows(w, n):
    return jnp.pad(w, ((0, n - w.shape[0]), (0, 0)))


def _rwkv_params(j, mu, wr, wk, wv, w0, w1, w2, a0, a1, a2, v0, v1, v2, g1, g2, kk, ka, rk,
                 lnw, lnb):
    d = D_MODEL
    p = {
        "mu": mu[j],
        "wr": wr[j].astype(BF16), "wk": wk[j].astype(BF16), "wv": wv[j].astype(BF16),
        "wl1": jnp.concatenate([w1[j, 0], w1[j, 1]], axis=1).astype(BF16),
        "wl2": _block_diag2(w2[j, 0], w2[j, 1]).astype(BF16),
        "al1": jnp.concatenate([a1[j, 0], a1[j, 1]], axis=1).astype(BF16),
        "al2": _block_diag2(a2[j, 0], a2[j, 1]).astype(BF16),
        "g1": g1[j].astype(BF16), "g2": g2[j].astype(BF16),
        "w0": w0[j].reshape(1, 2 * d), "a0": a0[j].reshape(1, 2 * d),
        "kk": kk[j].reshape(1, d), "ka": ka[j].reshape(1, d), "rk": rk[j].reshape(1, d),
        "lnw": lnw[j].reshape(1, d), "lnb": lnb[j].reshape(1, d),
    }
    if j > 0:
        p["v0"] = v0[j - 1].reshape(1, d)
        p["v1"] = _pad_cols(v1[j - 1], LANES).astype(BF16)
        p["v2"] = _pad_rows(v2[j - 1], LANES).astype(BF16)
    return p


def kernel(x_prompt, x_sample, c_prompt, c_sample, ada_w, ada_b, norm_g, final_g, rwkv_mu, rwkv_wr, rwkv_wk, rwkv_wv, rwkv_wo, rwkv_w0, rwkv_w1, rwkv_w2, rwkv_a0, rwkv_a1, rwkv_a2, rwkv_v0, rwkv_v1, rwkv_v2, rwkv_g1, rwkv_g2, rwkv_kk, rwkv_ka, rwkv_rk, rwkv_lnw, rwkv_lnb, sgu_win, sgu_bin, sgu_ng, sgu_nb, sgu_ws, sgu_bs, sgu_wout, sgu_bout, diff_wqkv, diff_wo, diff_lq1, diff_lk1, diff_lq2, diff_lk2, diff_subln, rel_bias, moe_wrg, moe_brg, moe_wre, moe_bre, moe_wg, moe_wu, moe_wd):
    d = D_MODEL
    nb_p, seq_len, _ = x_prompt.shape
    assert x_sample.shape[1] == seq_len
    t_p = nb_p * seq_len
    x = jnp.concatenate([x_prompt.reshape(-1, d), x_sample.reshape(-1, d)], axis=0)
    c = jnp.concatenate([c_prompt, c_sample], axis=0)
    mod = _modulation(c, ada_w, ada_b)
    zero_bias = jnp.zeros((1, d), F32)
    v_first = None
    for i in range(DEPTH):
        j = i // N_MIXERS
        mod_l = mod[i]
        if i % N_MIXERS == 0:
            p = _rwkv_params(j, rwkv_mu, rwkv_wr, rwkv_wk, rwkv_wv, rwkv_w0, rwkv_w1, rwkv_w2,
                             rwkv_a0, rwkv_a1, rwkv_a2, rwkv_v0, rwkv_v1, rwkv_v2, rwkv_g1, rwkv_g2,
                             rwkv_kk, rwkv_ka, rwkv_rk, rwkv_lnw, rwkv_lnb)
            r, k, v, g, kk, wlf, wlb, af, ab, bonus = _rwkv_pre(x, mod_l, norm_g[i, 0], p, v_first,
                                                                seq_len)
            if v_first is None:
                v_first = v
            y = _wkv(r, k, v, kk, wlf, wlb, af, ab, p["ka"], seq_len)
            pre = _rwkv_mid(y, v, g, bonus, p)
            w_out, b_out = rwkv_wo[j].astype(BF16), zero_bias
        elif i % N_MIXERS == 1:
            p = {"win": sgu_win[j].astype(BF16), "bin": sgu_bin[j].reshape(1, -1),
                 "ng": sgu_ng[j].reshape(1, -1), "nb": sgu_nb[j].reshape(1, -1),
                 "ws": sgu_ws[j].astype(BF16),
                 "bs": jnp.repeat(jnp.transpose(sgu_bs[j]), SGU_GC, axis=1)}
            pre = _sgu(x, mod_l, norm_g[i, 0], p, seq_len)
            w_out, b_out = sgu_wout[j].astype(BF16), sgu_bout[j].reshape(1, d)
        else:
            lambda_init = 0.8 - 0.6 * math.exp(-0.3 * i)
            lam = (jnp.exp(jnp.sum(diff_lq1[j] * diff_lk1[j])) - jnp.exp(jnp.sum(diff_lq2[j] * diff_lk2[j]))
                   + lambda_init)
            q, k, v = _qkv(x, mod_l, norm_g[i, 0], diff_wqkv[j].astype(BF16), seq_len)
            windows = _bias_windows(rel_bias, seq_len, ATTN_TQ)
            pre = _diff_attention(q, k, v, windows, lam, diff_subln[j], lambda_init, seq_len)
            w_out, b_out = diff_wo[j].astype(BF16), zero_bias
        w_router = _pad_cols(jnp.concatenate([moe_wrg[i], moe_wre[i]], axis=1), ROUTER_COLS)
        b_router = _pad_cols(jnp.concatenate([moe_brg[i], moe_bre[i]])[None, :], ROUTER_COLS)
        x1, h2, logits = _post(pre, w_out, b_out, x, mod_l, norm_g[i, 1], w_router, b_router, seq_len)
        x = _moe(x1, h2, logits, mod_l, moe_wg, moe_wu, moe_wd, i, final_g, seq_len)
    return (x[:t_p].reshape(x_prompt.shape), x[t_p:].reshape(x_sample.shape))
```

```python
import functools
import math

import numpy as np
import jax
import jax.numpy as jnp
from jax import lax
from jax.experimental import pallas as pl
from jax.experimental.pallas import tpu as pltpu

F32 = jnp.float32
BF16 = jnp.bfloat16
HIGHEST = lax.Precision.HIGHEST

D_MODEL = 1024
DEPTH = 4
N_MIXERS = 3
NORM_EPS = 1e-6
LANES = 128
VMEM_LIMIT_BYTES = 56 * 1024 * 1024

RWKV_HEAD = 64
RWKV_HEADS = D_MODEL // RWKV_HEAD
RWKV_PAIRS = D_MODEL // LANES
RWKV_GN_EPS = 64e-5
WKV_CHUNK = 64
WKV_PAIRS_PER_STEP = 2
WKV_CHUNKS_PER_STEP = 4

SGU_CHUNK = 128
SGU_DIM = 2 * D_MODEL
SGU_GROUPS = 8
SGU_GC = SGU_DIM // SGU_GROUPS

DIFF_HEADS = 8
DIFF_HEAD = 64
NUM_BUCKETS = 32
MAX_DISTANCE = 128
ATTN_TQ = 512

MOE_GROUPS = 4
EXPERTS_PER_GROUP = 8
N_EXPERTS = MOE_GROUPS * EXPERTS_PER_GROUP
MOE_TOP_K = 2
EXPERT_DIM = D_MODEL // 2
MOE_BLOCK = 256
ROUTER_COLS = LANES
ROUTE_TILE = 2048
RANK_TILE = 512
MOE_SLICES = 4

TOKEN_TILE = 256
WIDE_TILE = 512


def _params(*semantics):
    return pltpu.CompilerParams(dimension_semantics=semantics, vmem_limit_bytes=VMEM_LIMIT_BYTES)


def _resident(shape):
    zeros = (0,) * len(shape)
    return pl.BlockSpec(shape, lambda *_: zeros, pipeline_mode=pl.Buffered(1))


def _dot(a, b, precision=None):
    return jnp.dot(a, b, preferred_element_type=F32, precision=precision)


def _dot_nt(a, b, precision=None):
    return lax.dot_general(a, b, (((1,), (1,)), ((), ())), preferred_element_type=F32,
                           precision=precision)


def _dot_tn(a, b, precision=None):
    return lax.dot_general(a, b, (((0,), (0,)), ((), ())), preferred_element_type=F32,
                           precision=precision)


def _split_bf16(x):
    hi = x.astype(BF16)
    return hi, (x - hi.astype(F32)).astype(BF16)


def _head_sums(x, hsum, hexp):
    hi, lo = _split_bf16(x)
    shi, slo = _split_bf16(_dot(hi, hsum) + _dot(lo, hsum))
    return _dot(shi, hexp) + _dot(slo, hexp)


def _sigmoid(x):
    return 1.0 / (1.0 + jnp.exp(-x))


def _norm_mod(x, g, scale, shift, eps=NORM_EPS):
    y = x * lax.rsqrt(jnp.mean(x * x, axis=-1, keepdims=True) + eps)
    return (y * g) * (1.0 + scale) + shift


def _mod_kernel(c_ref, w_ref, b_ref, o_ref):
    c = c_ref[...]
    o_ref[...] = _dot(c * _sigmoid(c), w_ref[...], HIGHEST) + b_ref[...]


def _modulation(c, ada_w, ada_b):
    nb = c.shape[0]
    depth, d, n = ada_w.shape
    out = pl.pallas_call(
        _mod_kernel,
        grid=(depth, n // d),
        in_specs=[pl.BlockSpec((nb, d), lambda l, j: (0, 0)),
                  pl.BlockSpec((None, d, d), lambda l, j: (l, 0, j)),
                  pl.BlockSpec((None, 1, d), lambda l, j: (l, 0, j))],
        out_specs=pl.BlockSpec((None, nb, d), lambda l, j: (l, 0, j)),
        out_shape=jax.ShapeDtypeStruct((depth, nb, n), F32),
        compiler_params=_params("parallel", "parallel"),
        name="adaln_modulation",
    )(c, ada_w, ada_b.reshape(depth, 1, n))
    return out.reshape(depth, nb, n // d, d)


def _rwkv_pre_kernel(*refs, tm, tiles_per_seq, has_vres):
    if has_vres:
        (x_ref, xp_ref, xn_ref, mod_ref, ng_ref, mu_ref, wr_ref, wk_ref, wv_ref, wl1_ref, wl2_ref,
         al1_ref, al2_ref, g1_ref, g2_ref, w0_ref, a0_ref, kscale_ref, ka_ref, rk_ref, hsum_ref,
         hexp_ref, vf_ref, v0_ref, v1_ref, v2_ref,
         r_ref, k_ref, v_ref, g_ref, kk_ref, wlf_ref, wlb_ref, af_ref, ab_ref, bonus_ref) = refs
    else:
        (x_ref, xp_ref, xn_ref, mod_ref, ng_ref, mu_ref, wr_ref, wk_ref, wv_ref, wl1_ref, wl2_ref,
         al1_ref, al2_ref, g1_ref, g2_ref, w0_ref, a0_ref, kscale_ref, ka_ref, rk_ref, hsum_ref,
         hexp_ref,
         r_ref, k_ref, v_ref, g_ref, kk_ref, wlf_ref, wlb_ref, af_ref, ab_ref, bonus_ref) = refs
    d = D_MODEL
    i = pl.program_id(0)
    pos = i % tiles_per_seq
    shift, scale = mod_ref[0:1, :], mod_ref[1:2, :]
    ng = ng_ref[...]
    h = _norm_mod(x_ref[...], ng, scale, shift)
    h_prev = jnp.where(pos == 0, 0.0, _norm_mod(xp_ref[7:8, :], ng, scale, shift))
    h_next = jnp.where(pos == tiles_per_seq - 1, 0.0, _norm_mod(xn_ref[0:1, :], ng, scale, shift))
    row = lax.broadcasted_iota(jnp.int32, (tm, 1), 0)
    h_up = jnp.where(row == 0, h_prev, pltpu.roll(h, 1, 0))
    h_dn = jnp.where(row == tm - 1, h_next, pltpu.roll(h, tm - 1, 0))
    xx = 0.5 * (h_up + h_dn) - h

    def mix(n):
        return (h + xx * mu_ref[n:n + 1, :]).astype(BF16)

    xr, xw, xk, xv, xa, xg = [mix(n) for n in range(6)]
    r = _dot(xr, wr_ref[...])
    r_ref[...] = r.astype(BF16)
    k = _dot(xk, wk_ref[...])
    k_ref[...] = k.astype(BF16)
    kkr = k * kscale_ref[...]
    norm = jnp.sqrt(_head_sums(kkr * kkr, hsum_ref[...], hexp_ref[...]))
    kk_ref[...] = (kkr / jnp.maximum(norm, 1e-12)).astype(BF16)
    v = _dot(xv, wv_ref[...])
    if has_vres:
        lv = _dot(_dot(xv, v1_ref[...]).astype(BF16), v2_ref[...])
        v = v + (vf_ref[...].astype(F32) - v) * _sigmoid(v0_ref[...] + lv)
    v_ref[...] = v.astype(BF16)
    g_ref[...] = _dot(_sigmoid(_dot(xg, g1_ref[...])).astype(BF16), g2_ref[...]).astype(BF16)
    wl = w0_ref[...] + _dot(jnp.tanh(_dot(xw, wl1_ref[...])).astype(BF16), wl2_ref[...])
    logw = -math.exp(-0.5) * _sigmoid(wl)
    wlf_ref[...] = logw[:, :d]
    wlb_ref[...] = logw[:, d:]
    a = _sigmoid(a0_ref[...] + _dot(_dot(xa, al1_ref[...]).astype(BF16), al2_ref[...]))
    af_ref[...] = a[:, :d].astype(BF16)
    ab_ref[...] = a[:, d:].astype(BF16)
    k_bonus = k * (1.0 + (0.5 * (a[:, :d] + a[:, d:]) - 1.0) * ka_ref[...])
    hi, lo = _split_bf16(r * k_bonus * rk_ref[...])
    bonus_ref[...] = _dot(hi, hsum_ref[...]) + _dot(lo, hsum_ref[...])


def _head_sum_matrix():
    head_of_lane = np.arange(D_MODEL) // RWKV_HEAD
    return jnp.asarray(head_of_lane[:, None] == np.arange(RWKV_HEADS)[None, :], BF16)


def _block_diag2(m0, m1):
    z = jnp.zeros_like(m0)
    return jnp.concatenate([jnp.concatenate([m0, z], 1), jnp.concatenate([z, m1], 1)], 0)


def _rwkv_pre(x, mod_l, norm_g, p, v_first, seq_len):
    t, d = x.shape
    tm = TOKEN_TILE
    tiles_per_seq = seq_len // tm
    n_tiles = t // tm
    has_vres = v_first is not None
    tile = pl.BlockSpec((tm, d), lambda i: (i, 0))
    rows8 = tm // 8
    last8 = t // 8 - 1
    in_specs = [
        tile,
        pl.BlockSpec((8, d), lambda i: (jnp.maximum(i * rows8 - 1, 0), 0)),
        pl.BlockSpec((8, d), lambda i: (jnp.minimum((i + 1) * rows8, last8), 0)),
        pl.BlockSpec((None, 6, d), lambda i: (i // tiles_per_seq, 0, 0)),
        _resident((1, d)), _resident((6, d)),
        _resident((d, d)), _resident((d, d)), _resident((d, d)),
        _resident((d, LANES)), _resident((LANES, 2 * d)),
        _resident((d, LANES)), _resident((LANES, 2 * d)),
        _resident((d, LANES)), _resident((LANES, d)),
        _resident((1, 2 * d)), _resident((1, 2 * d)),
        _resident((1, d)), _resident((1, d)), _resident((1, d)),
        _resident((d, RWKV_HEADS)), _resident((RWKV_HEADS, d)),
    ]
    hsum = _head_sum_matrix()
    args = [x, x, x, mod_l, norm_g.reshape(1, d), p["mu"],
            p["wr"], p["wk"], p["wv"], p["wl1"], p["wl2"], p["al1"], p["al2"], p["g1"], p["g2"],
            p["w0"], p["a0"], p["kk"], p["ka"], p["rk"], hsum, hsum.T]
    if has_vres:
        in_specs += [tile, _resident((1, d)), _resident((d, LANES)), _resident((LANES, d))]
        args += [v_first, p["v0"], p["v1"], p["v2"]]
    dtypes = [BF16] * 5 + [F32] * 2 + [BF16] * 2
    outs = pl.pallas_call(
        functools.partial(_rwkv_pre_kernel, tm=tm, tiles_per_seq=tiles_per_seq, has_vres=has_vres),
        grid=(n_tiles,),
        in_specs=in_specs,
        out_specs=[tile] * 9 + [pl.BlockSpec((tm, RWKV_HEADS), lambda i: (i, 0))],
        out_shape=[jax.ShapeDtypeStruct((t, d), dt) for dt in dtypes]
        + [jax.ShapeDtypeStruct((t, RWKV_HEADS), F32)],
        compiler_params=_params("parallel"),
        name="rwkv_pre",
    )(*args)
    return outs


def _wkv_kernel(r_ref, k_ref, v_ref, kk_ref, wlf_ref, wlb_ref, af_ref, ab_ref, ka_ref,
                lvl_ref, y_ref, state_ref, g_ref, sp_ref, yp_ref, xr_ref, dec_ref,
                *, seq_len, chunk, pairs, group):
    n_chunks = seq_len // chunk
    n_levels = int(math.log2(chunk))
    two = 2 * chunk
    lane = lax.broadcasted_iota(jnp.int32, (1, LANES), 1)
    head0 = lane < RWKV_HEAD
    ri = lax.broadcasted_iota(jnp.int32, (chunk, chunk), 0)
    ci = lax.broadcasted_iota(jnp.int32, (chunk, chunk), 1)
    si = lax.broadcasted_iota(jnp.int32, (two, two), 0)
    sj = lax.broadcasted_iota(jnp.int32, (two, two), 1)
    same_head = (si < chunk) == (sj < chunk)
    st, su = si & (chunk - 1), sj & (chunk - 1)
    eye = (si == sj).astype(F32)
    chain_defs = [(pair, reverse) for pair in range(pairs) for reverse in (False, True)]

    def rows_of(idx, reverse):
        pos = (n_chunks - 1 - idx) if reverse else idx
        return pl.ds(pl.multiple_of(pos * chunk, chunk), chunk)

    def stack(x):
        return jnp.concatenate([jnp.where(head0, x, 0.0), jnp.where(head0, 0.0, x)], axis=0)

    def a_load(q, g, idx, slot):
        pair, reverse = chain_defs[q]
        rows = rows_of(idx, reverse)
        lanes = slice(pair * LANES, (pair + 1) * LANES)
        a = dict(q=q, g=g, slot=slot, rows=rows, lanes=lanes, reverse=reverse)
        a["lw"] = lw = (wlb_ref if reverse else wlf_ref)[rows, lanes]
        a["a"] = (ab_ref if reverse else af_ref)[rows, lanes].astype(F32)
        cum = ((ri <= ci) if reverse else (ri >= ci)).astype(BF16)
        lw_hi = lw.astype(BF16)
        lw_mid, lw_lo = _split_bf16(lw - lw_hi.astype(F32))
        a["logp"] = _dot(cum, lw_hi) + (_dot(cum, lw_mid) + _dot(cum, lw_lo))
        return a

    def a_scale(a):
        rows, lanes, logp, lw, av = a["rows"], a["lanes"], a["logp"], a["lw"], a["a"]
        k = k_ref[rows, lanes].astype(F32)
        kk = kk_ref[rows, lanes].astype(F32)
        kd = k * (1.0 + (av - 1.0) * ka_ref[:, lanes])
        p_in = jnp.exp(logp)
        p_out = jnp.exp(-logp)
        a["xa"] = stack(-kk * jnp.exp(logp - lw)).astype(BF16)
        a["xr"] = stack(r_ref[rows, lanes].astype(F32) * p_in).astype(BF16)
        a["yb"] = stack(kk * av * p_out).astype(BF16)
        a["yk"] = stack(kd * p_out).astype(BF16)
        a["vs"] = stack(v_ref[rows, lanes])
        a["decay"] = jnp.exp(jnp.sum(lw, axis=0, keepdims=True))

    def a_scores(a):
        if a["reverse"]:
            strict, incl = same_head & (st < su), same_head & (st <= su)
        else:
            strict, incl = same_head & (st > su), same_head & (st >= su)
        sc = _dot_nt(jnp.concatenate([a["xa"], a["xr"]], 0),
                     jnp.concatenate([a["yb"], a["yk"]], 0))
        a["m"] = sc[:two, :two].astype(BF16)
        m_ak = jnp.where(strict, sc[:two, two:], 0.0).astype(BF16)
        n_rk = jnp.where(incl, sc[two:, two:], 0.0).astype(BF16)
        a["n_rb"] = jnp.where(incl, sc[two:, :two], 0.0).astype(BF16)
        a["inv"] = eye + (a["m"] * lvl_ref[int(a["reverse"]), 0]).astype(F32)
        a["zp"] = _dot(m_ak, a["vs"]).astype(BF16)
        a["yp"] = _dot(n_rk, a["vs"])
        a["sp"] = _dot_tn(a["vs"], a["yk"])

    def a_double(a, level):
        t = a["inv"].astype(BF16)
        m_n = a["m"] * lvl_ref[int(a["reverse"]), level]
        a["inv"] = a["inv"] + _dot(_dot(t, m_n).astype(BF16), t)

    def a_solve(a):
        t = a["inv"].astype(BF16)
        a["w"] = _dot(t, a["xa"]).astype(BF16)
        a["u0"] = _dot(t, a["zp"]).astype(BF16)

    def a_fold(a):
        at = (a["slot"], a["g"], a["q"])
        g_ref[at] = _dot_tn(a["w"], a["yb"]).astype(BF16)
        sp_ref[at] = a["sp"] + _dot_tn(a["u0"], a["yb"])
        yp_ref[at] = a["yp"] + _dot(a["n_rb"], a["u0"])
        xr_ref[at] = (a["xr"].astype(F32) + _dot(a["n_rb"], a["w"])).astype(BF16)
        dec_ref[at] = a["decay"]

    a_stage_list = ([a_scale, a_scores]
                    + [functools.partial(a_double, level=level) for level in range(1, n_levels)]
                    + [a_solve, a_fold])

    def b_state(b):
        at = b["at"]
        s0 = state_ref[b["q"]]
        b["s0b"] = s0.astype(BF16)
        state_ref[b["q"]] = (s0 + _dot(b["s0b"], g_ref[at]) + sp_ref[at]) * dec_ref[at]

    def b_out(b):
        at = b["at"]
        pair, reverse = chain_defs[b["q"]]
        ys = _dot_nt(xr_ref[at], b["s0b"]) + yp_ref[at]
        y_ref[rows_of(b["idx"], reverse), pair * LANES:(pair + 1) * LANES] += ys[:chunk] + ys[chunk:]

    b_stage_list = [b_state, b_out]

    y_ref[...] = jnp.zeros_like(y_ref)
    state_ref[...] = jnp.zeros_like(state_ref)
    n_chains = len(chain_defs)
    first = [a_load(q, g, g, 0) for g in range(group) for q in range(n_chains)]
    for stage in a_stage_list:
        for a in first:
            stage(a)

    def run_step(step, prepare_next):
        slot = step & 1
        base = step * group
        b_work = []
        for g in range(group):
            now = [dict(q=q, at=(slot, g, q), idx=base + g) for q in range(n_chains)]
            b_work += [(stage, now) for stage in b_stage_list]
        a_work = []
        if prepare_next:
            ahead = [a_load(q, g, base + group + g, 1 - slot)
                     for g in range(group) for q in range(n_chains)]
            a_work = [(stage, ahead) for stage in a_stage_list]
        per_a = -(-len(b_work) // max(len(a_work), 1))
        while a_work or b_work:
            for stage, items in b_work[:per_a]:
                for b in items:
                    stage(b)
            b_work = b_work[per_a:]
            if a_work:
                stage, items = a_work.pop(0)
                for a in items:
                    stage(a)

    n_steps = n_chunks // group

    def body(step, carry):
        run_step(step, True)
        return carry

    lax.fori_loop(0, n_steps - 1, body, 0)
    run_step(n_steps - 1, False)


def _wkv_level_masks(chunk):
    s = np.arange(2 * chunk)
    head, t = s // chunk, s % chunk
    n_levels = int(math.log2(chunk))
    masks = np.zeros((2, n_levels, 2 * chunk, 2 * chunk), np.float32)
    for d in range(2):
        pos = t if d == 0 else chunk - 1 - t
        for j in range(n_levels):
            blk = pos >> j
            masks[d, j] = ((head[:, None] == head[None, :]) & (blk[:, None] % 2 == 1)
                           & (blk[None, :] == blk[:, None] - 1))
    return masks


def _wkv(r, k, v, kk, wlf, wlb, af, ab, k_a, seq_len):
    t, d = r.shape
    nb = t // seq_len
    pairs = WKV_PAIRS_PER_STEP
    width = pairs * LANES
    n_chains = 2 * pairs
    seq = pl.BlockSpec((seq_len, width), lambda b, p: (b, p))
    par = pl.BlockSpec((1, width), lambda b, p: (0, p))
    levels = jnp.asarray(_wkv_level_masks(WKV_CHUNK), BF16)
    group = WKV_CHUNKS_PER_STEP
    parked_f32 = pltpu.VMEM((2, group, n_chains, LANES, LANES), F32)
    parked_bf16 = pltpu.VMEM((2, group, n_chains, LANES, LANES), BF16)
    return pl.pallas_call(
        functools.partial(_wkv_kernel, seq_len=seq_len, chunk=WKV_CHUNK, pairs=pairs, group=group),
        grid=(nb, d // width),
        in_specs=[seq] * 8 + [par, _resident(levels.shape)],
        out_specs=seq,
        out_shape=jax.ShapeDtypeStruct((t, d), F32),
        scratch_shapes=[pltpu.VMEM((n_chains, LANES, LANES), F32), parked_bf16, parked_f32,
                        parked_f32, parked_bf16, pltpu.VMEM((2, group, n_chains, 1, LANES), F32)],
        compiler_params=_params("parallel", "parallel"),
        name="wkv7_chunked",
    )(r, k, v, kk, wlf, wlb, af, ab, k_a.reshape(1, d), levels)


def _rwkv_mid_kernel(y_ref, v_ref, g_ref, bonus_ref, lnw_ref, lnb_ref, hsum_ref, hexp_ref, o_ref):
    hsum = hsum_ref[...]
    hexp = hexp_ref[...]

    def head_sum(x):
        return _head_sums(x, hsum, hexp)

    y = y_ref[...]
    yc = y - head_sum(y) * (1.0 / RWKV_HEAD)
    var = head_sum(yc * yc) * (1.0 / RWKV_HEAD)
    yn = yc * lax.rsqrt(var + RWKV_GN_EPS) * lnw_ref[...] + lnb_ref[...]
    b_hi, b_lo = _split_bf16(bonus_ref[...])
    bonus = (_dot(b_hi, hexp) + _dot(b_lo, hexp)) * v_ref[...].astype(F32)
    o_ref[...] = ((yn + bonus) * g_ref[...].astype(F32)).astype(BF16)


def _rwkv_mid(y, v, g, bonus, p):
    t, d = y.shape
    tm = TOKEN_TILE
    tile = pl.BlockSpec((tm, d), lambda i: (i, 0))
    vec = _resident((1, d))
    hsum = _head_sum_matrix()
    return pl.pallas_call(
        _rwkv_mid_kernel,
        grid=(t // tm,),
        in_specs=[tile] * 3 + [pl.BlockSpec((tm, RWKV_HEADS), lambda i: (i, 0))] + [vec] * 2
        + [_resident((d, RWKV_HEADS)), _resident((RWKV_HEADS, d))],
        out_specs=tile,
        out_shape=jax.ShapeDtypeStruct((t, d), BF16),
        compiler_params=_params("parallel"),
        name="rwkv_mid",
    )(y, v, g, bonus, p["lnw"], p["lnb"], hsum, hsum.T)


def _sgu_kernel(x_ref, mod_ref, ng_ref, win_ref, bin_ref, lg_ref, lb_ref, ws_ref, bs_ref, o_ref,
                *, tm):
    h = _norm_mod(x_ref[...], ng_ref[...], mod_ref[1:2, :], mod_ref[0:1, :]).astype(BF16)
    z = _dot(h, win_ref[...]) + bin_ref[...]
    z = 0.5 * z * (1.0 + lax.erf(z * (1.0 / math.sqrt(2.0))))
    u = z[:, :SGU_DIM]
    v = z[:, SGU_DIM:]
    vc = v - jnp.mean(v, axis=-1, keepdims=True)
    vn = vc * lax.rsqrt(jnp.mean(vc * vc, axis=-1, keepdims=True) + 1e-5) * lg_ref[...] + lb_ref[...]
    vb = vn.astype(BF16)
    for c in range(tm // SGU_CHUNK):
        rows = slice(c * SGU_CHUNK, (c + 1) * SGU_CHUNK)
        for g in range(SGU_GROUPS):
            cols = slice(g * SGU_GC, (g + 1) * SGU_GC)
            s = _dot(ws_ref[g], vb[rows, cols]) + bs_ref[:, cols]
            o_ref[rows, cols] = (u[rows, cols] * s).astype(BF16)


def _sgu(x, mod_l, norm_g, p, seq_len):
    t, d = x.shape
    tm = TOKEN_TILE
    tiles_per_seq = seq_len // tm
    return pl.pallas_call(
        functools.partial(_sgu_kernel, tm=tm),
        grid=(t // tm,),
        in_specs=[pl.BlockSpec((tm, d), lambda i: (i, 0)),
                  pl.BlockSpec((None, 6, d), lambda i: (i // tiles_per_seq, 0, 0)),
                  _resident((1, d)),
                  _resident((d, 2 * SGU_DIM)), _resident((1, 2 * SGU_DIM)),
                  _resident((1, SGU_DIM)), _resident((1, SGU_DIM)),
                  _resident((SGU_GROUPS, SGU_CHUNK, SGU_CHUNK)), _resident((SGU_CHUNK, SGU_DIM))],
        out_specs=pl.BlockSpec((tm, SGU_DIM), lambda i: (i, 0)),
        out_shape=jax.ShapeDtypeStruct((t, SGU_DIM), BF16),
        compiler_params=_params("parallel"),
        name="sgu",
    )(x, mod_l, norm_g.reshape(1, d), p["win"], p["bin"], p["ng"], p["nb"], p["ws"], p["bs"])


def _qkv_kernel(x_ref, mod_ref, ng_ref, w_ref, q_ref, k_ref, v_ref):
    d = D_MODEL
    h = _norm_mod(x_ref[...], ng_ref[...], mod_ref[1:2, :], mod_ref[0:1, :]).astype(BF16)
    qkv = _dot(h, w_ref[...])
    q_ref[...] = (qkv[:, :d] * (DIFF_HEAD ** -0.5)).astype(BF16)
    k_ref[...] = qkv[:, d:2 * d].astype(BF16)
    v_ref[...] = qkv[:, 2 * d:].astype(BF16)


def _qkv(x, mod_l, norm_g, w_qkv, seq_len):
    t, d = x.shape
    tm = TOKEN_TILE
    tiles_per_seq = seq_len // tm
    tile = pl.BlockSpec((tm, d), lambda i: (i, 0))
    return pl.pallas_call(
        _qkv_kernel,
        grid=(t // tm,),
        in_specs=[tile, pl.BlockSpec((None, 6, d), lambda i: (i // tiles_per_seq, 0, 0)),
                  _resident((1, d)), _resident((d, 3 * d))],
        out_specs=[tile] * 3,
        out_shape=[jax.ShapeDtypeStruct((t, d), BF16)] * 3,
        compiler_params=_params("parallel"),
        name="diff_qkv",
    )(x, mod_l, norm_g.reshape(1, d), w_qkv)


def _attn_kernel(q_ref, k_ref, v_ref, win_ref, lam_ref, sg_ref, o_ref, vext_ref, *, tq, seq_len,
                 out_scale):
    @pl.when(pl.program_id(2) == 0)
    def _():
        col = lax.broadcasted_iota(jnp.int32, (seq_len, LANES), 1)
        vext_ref[:, :LANES] = v_ref[...]
        vext_ref[:, LANES:] = jnp.where(col == 0, 1.0, 0.0).astype(BF16)

    q = q_ref[...]
    lane = lax.broadcasted_iota(jnp.int32, (1, LANES), 1)
    zero = jnp.zeros_like(q)
    qs = jnp.concatenate([jnp.where(lane < DIFF_HEAD, q, zero),
                          jnp.where(lane < DIFF_HEAD, zero, q)], axis=0)
    s = _dot_nt(qs, k_ref[...])
    width = seq_len + tq
    window = jnp.broadcast_to(win_ref[...], (tq, width))
    bias = pltpu.roll(window, width - tq + 1, 1, stride=1, stride_axis=0)[:, :seq_len]

    def unnormalised(x):
        e = jnp.exp(x - jnp.max(x, axis=-1, keepdims=True)).astype(BF16)
        pv = _dot(e, vext_ref[...])
        return pv[:, :LANES], pv[:, LANES:LANES + 1]

    o1, l1 = unnormalised(s[:tq] + bias)
    o2, l2 = unnormalised(s[tq:] + bias)
    o = o1 / l1 - lam_ref[...] * (o2 / l2)
    o = o * lax.rsqrt(jnp.mean(o * o, axis=-1, keepdims=True) + 1e-5) * sg_ref[...]
    o_ref[...] = (o * out_scale).astype(BF16)


def _t5_bucket(rel):
    nb = NUM_BUCKETS // 2
    max_exact = nb // 2
    ret = jnp.where(rel > 0, nb, 0)
    n = jnp.abs(rel)
    nf = jnp.maximum(n, 1).astype(F32)
    large = max_exact + (jnp.log(nf / max_exact) / math.log(MAX_DISTANCE / max_exact)
                         * (nb - max_exact)).astype(jnp.int32)
    large = jnp.minimum(large, nb - 1)
    return ret + jnp.where(n < max_exact, n, large)


def _bias_windows(rel_bias, seq_len, tq):
    nqb = seq_len // tq
    j = jnp.arange(seq_len + tq, dtype=jnp.int32)[None, :]
    q_hi = (jnp.arange(nqb, dtype=jnp.int32)[:, None] + 1) * tq
    bucket = _t5_bucket(j - q_hi + 1)
    return jnp.transpose(rel_bias[bucket], (0, 2, 1))[:, :, None, :].astype(F32)


def _diff_attention(q, k, v, windows, lam, subln_g, lambda_init, seq_len):
    t, d = q.shape
    nb = t // seq_len
    tq = min(ATTN_TQ, seq_len)
    nqb = seq_len // tq
    kv = pl.BlockSpec((seq_len, LANES), lambda b, h, i: (b, h))
    qo = pl.BlockSpec((tq, LANES), lambda b, h, i: (b * nqb + i, h))
    return pl.pallas_call(
        functools.partial(_attn_kernel, tq=tq, seq_len=seq_len, out_scale=1.0 - lambda_init),
        grid=(nb, DIFF_HEADS, nqb),
        in_specs=[qo, kv, kv,
                  pl.BlockSpec((None, None, 1, seq_len + tq), lambda b, h, i: (i, h, 0, 0)),
                  pl.BlockSpec((1, 1), lambda b, h, i: (0, 0)),
                  pl.BlockSpec((1, LANES), lambda b, h, i: (0, 0))],
        out_specs=qo,
        out_shape=jax.ShapeDtypeStruct((t, d), BF16),
        scratch_shapes=[pltpu.VMEM((seq_len, 2 * LANES), BF16)],
        compiler_params=_params("parallel", "parallel", "arbitrary"),
        name="diff_attention",
    )(q, k, v, windows, lam.reshape(1, 1), subln_g.reshape(1, LANES))


def _post_kernel(pre_ref, w_ref, b_ref, x_ref, mod_ref, ng_ref, wrh_ref, wrl_ref, br_ref,
                 x1_ref, h2_ref, lg_ref):
    out = _dot(pre_ref[...], w_ref[...]) + b_ref[...]
    x1 = x_ref[...] + mod_ref[2:3, :] * out
    x1_ref[...] = x1
    h2 = _norm_mod(x1, ng_ref[...], mod_ref[4:5, :], mod_ref[3:4, :])
    h2_hi, h2_lo = _split_bf16(h2)
    h2_ref[...] = h2_hi
    lg_ref[...] = (_dot(h2_hi, wrh_ref[...]) + (_dot(h2_hi, wrl_ref[...]) + _dot(h2_lo, wrh_ref[...]))
                   + br_ref[...])


def _route(lg):
    lane = lax.broadcasted_iota(jnp.int32, lg.shape, 1)
    first_e, end_e = MOE_GROUPS, MOE_GROUPS + N_EXPERTS
    neg = -jnp.inf

    def max_and_first(x):
        m = jnp.max(x, axis=-1, keepdims=True)
        return m, jnp.min(jnp.where(x == m, lane, LANES), axis=-1, keepdims=True)

    is_group = lane < first_e
    g_max, grp = max_and_first(jnp.where(is_group, lg, neg))
    p_grp = 1.0 / jnp.sum(jnp.where(is_group, jnp.exp(lg - g_max), 0.0), axis=-1, keepdims=True)
    lane_grp = (lane - first_e) >> int(math.log2(EXPERTS_PER_GROUP))
    in_grp = (lane >= first_e) & (lane < end_e) & (lane_grp == grp)
    cand = jnp.where(in_grp, lg, neg)
    v1, i1 = max_and_first(cand)
    v2, i2 = max_and_first(jnp.where(lane == i1, neg, cand))
    p2 = jnp.exp(v2 - v1)
    gate1 = p_grp / (1.0 + p2)
    gate2 = gate1 * p2
    e1 = (i1 - first_e).astype(F32)
    e2 = (i2 - first_e).astype(F32)
    return jnp.where(lane == 0, e1, jnp.where(lane == 1, e2, jnp.where(lane == 2, gate1,
                                                                    jnp.where(lane == 3, gate2, 0.0))))


def _route_kernel(lg_ref, route_ref, route_t_ref):
    route = _route(lg_ref[...])
    route_ref[...] = route
    route_t_ref[...] = route.T[:8, :]


def _routing(logits):
    t = logits.shape[0]
    tr = max(k for k in range(LANES, ROUTE_TILE + 1, LANES) if t % k == 0)
    return pl.pallas_call(
        _route_kernel,
        grid=(t // tr,),
        in_specs=[pl.BlockSpec((tr, ROUTER_COLS), lambda i: (i, 0))],
        out_specs=[pl.BlockSpec((tr, ROUTER_COLS), lambda i: (i, 0)),
                   pl.BlockSpec((None, 8, tr), lambda i: (i, 0, 0))],
        out_shape=[jax.ShapeDtypeStruct((t, ROUTER_COLS), F32),
                   jax.ShapeDtypeStruct((t // tr, 8, tr), F32)],
        compiler_params=_params("parallel"),
        name="moe_route",
    )(logits)


def _post(pre, w, b, x, mod_l, norm_g2, w_router, b_router, seq_len):
    t, d = x.shape
    din = pre.shape[1]
    tm = min(WIDE_TILE, seq_len)
    tiles_per_seq = seq_len // tm
    tile = pl.BlockSpec((tm, d), lambda i: (i, 0))
    return pl.pallas_call(
        _post_kernel,
        grid=(t // tm,),
        in_specs=[pl.BlockSpec((tm, din), lambda i: (i, 0)), _resident((din, d)), _resident((1, d)),
                  tile, pl.BlockSpec((None, 6, d), lambda i: (i // tiles_per_seq, 0, 0)),
                  _resident((1, d)), _resident((d, ROUTER_COLS)), _resident((d, ROUTER_COLS)),
                  _resident((1, ROUTER_COLS))],
        out_specs=[tile, tile, pl.BlockSpec((tm, ROUTER_COLS), lambda i: (i, 0))],
        out_shape=[jax.ShapeDtypeStruct((t, d), F32), jax.ShapeDtypeStruct((t, d), BF16),
                   jax.ShapeDtypeStruct((t, ROUTER_COLS), F32)],
        compiler_params=_params("parallel"),
        name="post_router",
    )(pre, w, b, x, mod_l, norm_g2.reshape(1, d), *_split_bf16(w_router), b_router)


def _expert_kernel(be_ref, nu_ref, x_ref, wg_ref, wu_ref, wd_ref, *rest, first_block):
    o_ref, wgb_ref, wub_ref, wdb_ref = rest[-4:]
    i = pl.program_id(0)
    blk = first_block + i
    used = blk < nu_ref[0]
    new_expert = (i == 0) | (be_ref[blk] != be_ref[jnp.maximum(blk - 1, 0)])

    @pl.when(used & new_expert)
    def _():
        wgb_ref[...] = wg_ref[...].astype(BF16)
        wub_ref[...] = wu_ref[...].astype(BF16)
        wdb_ref[...] = wd_ref[...].astype(BF16)

    @pl.when(used)
    def _():
        x = x_ref[...]
        hg = _dot(x, wgb_ref[...])
        hu = _dot(x, wub_ref[...])
        act = hg * _sigmoid(hg) * hu
        o_ref[...] = _dot(act.astype(BF16), wdb_ref[...]).astype(BF16)

    @pl.when(jnp.logical_not(used))
    def _():
        o_ref[...] = jnp.zeros_like(o_ref)


def _experts(xs, block_e, n_used, wg, wu, wd, layer, ys_buf, first_block, total_blocks):
    p_rows, d = xs.shape
    n_blocks = p_rows // MOE_BLOCK

    def weight(shape):
        return pl.BlockSpec((None, None) + shape, lambda i, be, nu: (layer, be[first_block + i], 0, 0))

    in_specs = [pl.BlockSpec((MOE_BLOCK, d), lambda i, be, nu: (i, 0)),
                weight((d, EXPERT_DIM)), weight((d, EXPERT_DIM)), weight((EXPERT_DIM, d))]
    args = [block_e, n_used, xs, wg, wu, wd]
    aliases = {}
    if ys_buf is not None:
        in_specs.append(pl.BlockSpec(memory_space=pl.ANY))
        aliases = {len(args): 0}
        args.append(ys_buf)
    grid_spec = pltpu.PrefetchScalarGridSpec(
        num_scalar_prefetch=2,
        grid=(n_blocks,),
        in_specs=in_specs,
        out_specs=pl.BlockSpec((MOE_BLOCK, d), lambda i, be, nu: (first_block + i, 0)),
        scratch_shapes=[pltpu.VMEM((d, EXPERT_DIM), BF16), pltpu.VMEM((d, EXPERT_DIM), BF16),
                        pltpu.VMEM((EXPERT_DIM, d), BF16)],
    )
    return pl.pallas_call(
        functools.partial(_expert_kernel, first_block=first_block),
        grid_spec=grid_spec,
        out_shape=jax.ShapeDtypeStruct((total_blocks * MOE_BLOCK, d), BF16),
        input_output_aliases=aliases,
        compiler_params=_params("arbitrary"),
        name="moe_experts",
    )(*args)


def _combine_kernel(x_ref, y0_ref, y1_ref, route_ref, mod_ref, fg_ref, o_ref, *, final):
    route = route_ref[...]
    moe = route[:, 2:3] * y0_ref[...].astype(F32) + route[:, 3:4] * y1_ref[...].astype(F32)
    x = x_ref[...] + mod_ref[5:6, :] * moe
    if final:
        x = x * lax.rsqrt(jnp.mean(x * x, axis=-1, keepdims=True) + NORM_EPS) * fg_ref[...]
    o_ref[...] = x


def _combine(x_buf, y0, y1, route, mod_l, final_g, final, seq_len, first_tile):
    t, d = x_buf.shape
    tm = min(WIDE_TILE, seq_len)
    tiles_per_seq = seq_len // tm
    here = pl.BlockSpec((tm, d), lambda i: (first_tile + i, 0))
    local = pl.BlockSpec((tm, d), lambda i: (i, 0))
    return pl.pallas_call(
        functools.partial(_combine_kernel, final=final),
        grid=(y0.shape[0] // tm,),
        in_specs=[here, local, local, pl.BlockSpec((tm, ROUTER_COLS), lambda i: (first_tile + i, 0)),
                  pl.BlockSpec((None, 6, d), lambda i: ((first_tile + i) // tiles_per_seq, 0, 0)),
                  _resident((1, d))],
        out_specs=here,
        out_shape=jax.ShapeDtypeStruct((t, d), F32),
        input_output_aliases={0: 0},
        compiler_params=_params("parallel"),
        name="moe_combine",
    )(x_buf, y0, y1, route, mod_l, final_g.reshape(1, d))


def _rank_kernel(e_ref, rank_ref, cnt_ref, carry_ref):
    @pl.when(pl.program_id(0) == 0)
    def _():
        carry_ref[...] = jnp.zeros_like(carry_ref)

    e = e_ref[...]
    expert_id = lax.broadcasted_iota(jnp.int32, (N_EXPERTS, RANK_TILE), 0)
    onehot = jnp.where(e == expert_id, 1.0, 0.0)
    ri = lax.broadcasted_iota(jnp.int32, (RANK_TILE, RANK_TILE), 0)
    ci = lax.broadcasted_iota(jnp.int32, (RANK_TILE, RANK_TILE), 1)
    prefix = _dot(onehot.astype(BF16), (ri <= ci).astype(BF16))
    carry = carry_ref[...]
    rank = jnp.sum(onehot * (prefix + carry), axis=0, keepdims=True) - 1.0
    rank_ref[...] = rank.astype(jnp.int32)
    carry = carry + jnp.sum(onehot, axis=1, keepdims=True)
    carry_ref[...] = carry
    cnt_ref[...] = jnp.broadcast_to(carry, cnt_ref.shape).astype(jnp.int32)


def _rank(e):
    a = e.shape[0]
    n_tiles = a // RANK_TILE
    tile = pl.BlockSpec((None, 1, RANK_TILE), lambda i: (i, 0, 0))
    rank, counts = pl.pallas_call(
        _rank_kernel,
        grid=(n_tiles,),
        in_specs=[tile],
        out_specs=[tile, pl.BlockSpec((N_EXPERTS, LANES), lambda i: (0, 0))],
        out_shape=[jax.ShapeDtypeStruct((n_tiles, 1, RANK_TILE), jnp.int32),
                   jax.ShapeDtypeStruct((N_EXPERTS, LANES), jnp.int32)],
        scratch_shapes=[pltpu.VMEM((N_EXPERTS, 1), F32)],
        compiler_params=_params("arbitrary"),
        name="moe_rank",
    )(e.reshape(n_tiles, 1, RANK_TILE))
    return rank.reshape(a), counts[:, 0]


def _dispatch_plan(e, t):
    a = t * MOE_TOP_K
    rank, counts = _rank(e)
    padded = (counts + MOE_BLOCK - 1) // MOE_BLOCK * MOE_BLOCK
    end_pad = jnp.cumsum(padded)
    start_pad = end_pad - padded
    dest = (start_pad[e] + rank).astype(jnp.int32)
    n_blocks = (a + N_EXPERTS * (MOE_BLOCK - 1) + MOE_BLOCK - 1) // MOE_BLOCK
    token = jnp.tile(jnp.arange(t, dtype=jnp.int32), MOE_TOP_K)
    row_tok = jnp.zeros((n_blocks * MOE_BLOCK,), jnp.int32).at[dest].set(
        token, unique_indices=True, mode="promise_in_bounds")
    block_start = jnp.arange(n_blocks, dtype=jnp.int32) * MOE_BLOCK
    block_e = jnp.sum((end_pad[None, :] <= block_start[:, None]).astype(jnp.int32), axis=1)
    block_e = jnp.minimum(block_e, N_EXPERTS - 1).astype(jnp.int32)
    n_used = (end_pad[-1] // MOE_BLOCK).astype(jnp.int32).reshape(1)
    return dest.reshape(MOE_TOP_K, t), row_tok, block_e, n_used


def _split_count(n, want):
    return max(k for k in range(1, want + 1) if n % k == 0)


def _moe(x1, h2, logits, mod_l, wg, wu, wd, layer, final_g, seq_len):
    t = x1.shape[0]
    route, route_t = _routing(logits)
    e = jnp.concatenate([route_t[:, 0, :].reshape(t), route_t[:, 1, :].reshape(t)]).astype(jnp.int32)
    dest, row_tok, block_e, n_used = _dispatch_plan(e, t)
    total_blocks = row_tok.shape[0] // MOE_BLOCK
    n_slices = _split_count(total_blocks, MOE_SLICES)
    per = total_blocks // n_slices
    ys = None
    for c in range(n_slices):
        rows = row_tok[c * per * MOE_BLOCK:(c + 1) * per * MOE_BLOCK]
        ys = _experts(h2[rows], block_e, n_used, wg, wu, wd, layer, ys, c * per, total_blocks)
    final = layer == wg.shape[0] - 1
    tm = min(WIDE_TILE, seq_len)
    n_tiles = x1.shape[0] // tm
    n_slices = _split_count(n_tiles, MOE_SLICES)
    per = n_tiles // n_slices
    x = x1
    for c in range(n_slices):
        d_c = dest[:, c * per * tm:(c + 1) * per * tm]
        x = _combine(x, ys[d_c[0]], ys[d_c[1]], route, mod_l, final_g, final, seq_len, c * per)
    return x


def _pad_cols(w, n):
    return jnp.pad(w, ((0, 0), (0, n - w.shape[1])))


def _pad_rows(w, n):
    return jnp.pad(w, ((0, n - w.shape[0]), (0, 0)))


def _rwkv_params(j, mu, wr, wk, wv, w0, w1, w2, a0, a1, a2, v0, v1, v2, g1, g2, kk, ka, rk,
                 lnw, lnb):
    d = D_MODEL
    p = {
        "mu": mu[j],
        "wr": wr[j].astype(BF16), "wk": wk[j].astype(BF16), "wv": wv[j].astype(BF16),
        "wl1": jnp.concatenate([w1[j, 0], w1[j, 1]], axis=1).astype(BF16),
        "wl2": _block_diag2(w2[j, 0], w2[j, 1]).astype(BF16),
        "al1": jnp.concatenate([a1[j, 0], a1[j, 1]], axis=1).astype(BF16),
        "al2": _block_diag2(a2[j, 0], a2[j, 1]).astype(BF16),
        "g1": g1[j].astype(BF16), "g2": g2[j].astype(BF16),
        "w0": w0[j].reshape(1, 2 * d), "a0": a0[j].reshape(1, 2 * d),
        "kk": kk[j].reshape(1, d), "ka": ka[j].reshape(1, d), "rk": rk[j].reshape(1, d),
        "lnw": lnw[j].reshape(1, d), "lnb": lnb[j].reshape(1, d),
    }
    if j > 0:
        p["v0"] = v0[j - 1].reshape(1, d)
        p["v1"] = _pad_cols(v1[j - 1], LANES).astype(BF16)
        p["v2"] = _pad_rows(v2[j - 1], LANES).astype(BF16)
    return p


def kernel(x_prompt, x_sample, c_prompt, c_sample, ada_w, ada_b, norm_g, final_g, rwkv_mu, rwkv_wr, rwkv_wk, rwkv_wv, rwkv_wo, rwkv_w0, rwkv_w1, rwkv_w2, rwkv_a0, rwkv_a1, rwkv_a2, rwkv_v0, rwkv_v1, rwkv_v2, rwkv_g1, rwkv_g2, rwkv_kk, rwkv_ka, rwkv_rk, rwkv_lnw, rwkv_lnb, sgu_win, sgu_bin, sgu_ng, sgu_nb, sgu_ws, sgu_bs, sgu_wout, sgu_bout, diff_wqkv, diff_wo, diff_lq1, diff_lk1, diff_lq2, diff_lk2, diff_subln, rel_bias, moe_wrg, moe_brg, moe_wre, moe_bre, moe_wg, moe_wu, moe_wd):
    d = D_MODEL
    nb_p, seq_len, _ = x_prompt.shape
    assert x_sample.shape[1] == seq_len
    t_p = nb_p * seq_len
    x = jnp.concatenate([x_prompt.reshape(-1, d), x_sample.reshape(-1, d)], axis=0)
    c = jnp.concatenate([c_prompt, c_sample], axis=0)
    mod = _modulation(c, ada_w, ada_b)
    zero_bias = jnp.zeros((1, d), F32)
    v_first = None
    for i in range(DEPTH):
        j = i // N_MIXERS
        mod_l = mod[i]
        if i % N_MIXERS == 0:
            p = _rwkv_params(j, rwkv_mu, rwkv_wr, rwkv_wk, rwkv_wv, rwkv_w0, rwkv_w1, rwkv_w2,
                             rwkv_a0, rwkv_a1, rwkv_a2, rwkv_v0, rwkv_v1, rwkv_v2, rwkv_g1, rwkv_g2,
                             rwkv_kk, rwkv_ka, rwkv_rk, rwkv_lnw, rwkv_lnb)
            r, k, v, g, kk, wlf, wlb, af, ab, bonus = _rwkv_pre(x, mod_l, norm_g[i, 0], p, v_first,
                                                                seq_len)
            if v_first is None:
                v_first = v
            y = _wkv(r, k, v, kk, wlf, wlb, af, ab, p["ka"], seq_len)
            pre = _rwkv_mid(y, v, g, bonus, p)
            w_out, b_out = rwkv_wo[j].astype(BF16), zero_bias
        elif i % N_MIXERS == 1:
            p = {"win": sgu_win[j].astype(BF16), "bin": sgu_bin[j].reshape(1, -1),
                 "ng": sgu_ng[j].reshape(1, -1), "nb": sgu_nb[j].reshape(1, -1),
                 "ws": sgu_ws[j].astype(BF16),
                 "bs": jnp.repeat(jnp.transpose(sgu_bs[j]), SGU_GC, axis=1)}
            pre = _sgu(x, mod_l, norm_g[i, 0], p, seq_len)
            w_out, b_out = sgu_wout[j].astype(BF16), sgu_bout[j].reshape(1, d)
        else:
            lambda_init = 0.8 - 0.6 * math.exp(-0.3 * i)
            lam = (jnp.exp(jnp.sum(diff_lq1[j] * diff_lk1[j])) - jnp.exp(jnp.sum(diff_lq2[j] * diff_lk2[j]))
                   + lambda_init)
            q, k, v = _qkv(x, mod_l, norm_g[i, 0], diff_wqkv[j].astype(BF16), seq_len)
            windows = _bias_windows(rel_bias, seq_len, min(ATTN_TQ, seq_len))
            pre = _diff_attention(q, k, v, windows, lam, diff_subln[j], lambda_init, seq_len)
            w_out, b_out = diff_wo[j].astype(BF16), zero_bias
        w_router = _pad_cols(jnp.concatenate([moe_wrg[i], moe_wre[i]], axis=1), ROUTER_COLS)
        b_router = _pad_cols(jnp.concatenate([moe_brg[i], moe_bre[i]])[None, :], ROUTER_COLS)
        x1, h2, logits = _post(pre, w_out, b_out, x, mod_l, norm_g[i, 1], w_router, b_router, seq_len)
        x = _moe(x1, h2, logits, mod_l, moe_wg, moe_wu, moe_wd, i, final_g, seq_len)
    return (x[:t_p].reshape(x_prompt.shape), x[t_p:].reshape(x_sample.shape))
```

```python
import functools
import math

import numpy as np
import jax
import jax.numpy as jnp
from jax import lax
from jax.experimental import pallas as pl
from jax.experimental.pallas import tpu as pltpu

F32 = jnp.float32
BF16 = jnp.bfloat16
HIGHEST = lax.Precision.HIGHEST

D_MODEL = 1024
DEPTH = 4
N_MIXERS = 3
NORM_EPS = 1e-6
LANES = 128
VMEM_LIMIT_BYTES = 56 * 1024 * 1024

RWKV_HEAD = 64
RWKV_HEADS = D_MODEL // RWKV_HEAD
RWKV_PAIRS = D_MODEL // LANES
RWKV_GN_EPS = 64e-5
WKV_CHUNK = 64
WKV_PAIRS_PER_STEP = 2
WKV_CHUNKS_PER_STEP = 4

SGU_CHUNK = 128
SGU_DIM = 2 * D_MODEL
SGU_GROUPS = 8
SGU_GC = SGU_DIM // SGU_GROUPS

DIFF_HEADS = 8
DIFF_HEAD = 64
NUM_BUCKETS = 32
MAX_DISTANCE = 128
ATTN_TQ = 512

MOE_GROUPS = 4
EXPERTS_PER_GROUP = 8
N_EXPERTS = MOE_GROUPS * EXPERTS_PER_GROUP
MOE_TOP_K = 2
EXPERT_DIM = D_MODEL // 2
MOE_BLOCK = 256
ROUTER_COLS = LANES
ROUTE_TILE = 2048
RANK_TILE = 512
RANK_ROWS = 8
MOE_SLICES = 4

TOKEN_TILE = 256
WIDE_TILE = 512


def _params(*semantics):
    return pltpu.CompilerParams(dimension_semantics=semantics, vmem_limit_bytes=VMEM_LIMIT_BYTES)


def _resident(shape):
    zeros = (0,) * len(shape)
    return pl.BlockSpec(shape, lambda *_: zeros, pipeline_mode=pl.Buffered(1))


def _dot(a, b, precision=None):
    return jnp.dot(a, b, preferred_element_type=F32, precision=precision)


def _dot_nt(a, b, precision=None):
    return lax.dot_general(a, b, (((1,), (1,)), ((), ())), preferred_element_type=F32,
                           precision=precision)


def _dot_tn(a, b, precision=None):
    return lax.dot_general(a, b, (((0,), (0,)), ((), ())), preferred_element_type=F32,
                           precision=precision)


def _split_bf16(x):
    hi = x.astype(BF16)
    return hi, (x - hi.astype(F32)).astype(BF16)


def _head_sums(x, hsum, hexp):
    hi, lo = _split_bf16(x)
    shi, slo = _split_bf16(_dot(hi, hsum) + _dot(lo, hsum))
    return _dot(shi, hexp) + _dot(slo, hexp)


def _sigmoid(x):
    return 1.0 / (1.0 + jnp.exp(-x))


def _norm_mod(x, g, scale, shift, eps=NORM_EPS):
    y = x * lax.rsqrt(jnp.mean(x * x, axis=-1, keepdims=True) + eps)
    return (y * g) * (1.0 + scale) + shift


def _mod_kernel(c_ref, w_ref, b_ref, o_ref):
    c = c_ref[...]
    o_ref[...] = _dot(c * _sigmoid(c), w_ref[...], HIGHEST) + b_ref[...]


def _modulation(c, ada_w, ada_b):
    nb = c.shape[0]
    depth, d, n = ada_w.shape
    out = pl.pallas_call(
        _mod_kernel,
        grid=(depth, n // d),
        in_specs=[pl.BlockSpec((nb, d), lambda l, j: (0, 0)),
                  pl.BlockSpec((None, d, d), lambda l, j: (l, 0, j)),
                  pl.BlockSpec((None, 1, d), lambda l, j: (l, 0, j))],
        out_specs=pl.BlockSpec((None, nb, d), lambda l, j: (l, 0, j)),
        out_shape=jax.ShapeDtypeStruct((depth, nb, n), F32),
        compiler_params=_params("parallel", "parallel"),
        name="adaln_modulation",
    )(c, ada_w, ada_b.reshape(depth, 1, n))
    return out.reshape(depth, nb, n // d, d)


def _rwkv_pre_kernel(*refs, tm, tiles_per_seq, has_vres):
    if has_vres:
        (x_ref, xp_ref, xn_ref, mod_ref, ng_ref, mu_ref, wr_ref, wk_ref, wv_ref, wl1_ref, wl2_ref,
         al1_ref, al2_ref, g1_ref, g2_ref, w0_ref, a0_ref, kscale_ref, ka_ref, rk_ref, hsum_ref,
         hexp_ref, vf_ref, v0_ref, v1_ref, v2_ref,
         r_ref, k_ref, v_ref, g_ref, kk_ref, wlf_ref, wlb_ref, af_ref, ab_ref, bonus_ref) = refs
    else:
        (x_ref, xp_ref, xn_ref, mod_ref, ng_ref, mu_ref, wr_ref, wk_ref, wv_ref, wl1_ref, wl2_ref,
         al1_ref, al2_ref, g1_ref, g2_ref, w0_ref, a0_ref, kscale_ref, ka_ref, rk_ref, hsum_ref,
         hexp_ref,
         r_ref, k_ref, v_ref, g_ref, kk_ref, wlf_ref, wlb_ref, af_ref, ab_ref, bonus_ref) = refs
    d = D_MODEL
    i = pl.program_id(0)
    pos = i % tiles_per_seq
    shift, scale = mod_ref[0:1, :], mod_ref[1:2, :]
    ng = ng_ref[...]
    h = _norm_mod(x_ref[...], ng, scale, shift)
    h_prev = jnp.where(pos == 0, 0.0, _norm_mod(xp_ref[7:8, :], ng, scale, shift))
    h_next = jnp.where(pos == tiles_per_seq - 1, 0.0, _norm_mod(xn_ref[0:1, :], ng, scale, shift))
    row = lax.broadcasted_iota(jnp.int32, (tm, 1), 0)
    h_up = jnp.where(row == 0, h_prev, pltpu.roll(h, 1, 0))
    h_dn = jnp.where(row == tm - 1, h_next, pltpu.roll(h, tm - 1, 0))
    xx = 0.5 * (h_up + h_dn) - h

    def mix(n):
        return (h + xx * mu_ref[n:n + 1, :]).astype(BF16)

    xr, xw, xk, xv, xa, xg = [mix(n) for n in range(6)]
    r = _dot(xr, wr_ref[...])
    r_ref[...] = r.astype(BF16)
    k = _dot(xk, wk_ref[...])
    k_ref[...] = k.astype(BF16)
    kkr = k * kscale_ref[...]
    norm = jnp.sqrt(_head_sums(kkr * kkr, hsum_ref[...], hexp_ref[...]))
    kk_ref[...] = (kkr / jnp.maximum(norm, 1e-12)).astype(BF16)
    v = _dot(xv, wv_ref[...])
    if has_vres:
        lv = _dot(_dot(xv, v1_ref[...]).astype(BF16), v2_ref[...])
        v = v + (vf_ref[...].astype(F32) - v) * _sigmoid(v0_ref[...] + lv)
    v_ref[...] = v.astype(BF16)
    g_ref[...] = _dot(_sigmoid(_dot(xg, g1_ref[...])).astype(BF16), g2_ref[...]).astype(BF16)
    wl = w0_ref[...] + _dot(jnp.tanh(_dot(xw, wl1_ref[...])).astype(BF16), wl2_ref[...])
    logw = -math.exp(-0.5) * _sigmoid(wl)
    wlf_ref[...] = logw[:, :d]
    wlb_ref[...] = logw[:, d:]
    a = _sigmoid(a0_ref[...] + _dot(_dot(xa, al1_ref[...]).astype(BF16), al2_ref[...]))
    af_ref[...] = a[:, :d].astype(BF16)
    ab_ref[...] = a[:, d:].astype(BF16)
    k_bonus = k * (1.0 + (0.5 * (a[:, :d] + a[:, d:]) - 1.0) * ka_ref[...])
    hi, lo = _split_bf16(r * k_bonus * rk_ref[...])
    bonus_ref[...] = _dot(hi, hsum_ref[...]) + _dot(lo, hsum_ref[...])


def _head_sum_matrix():
    head_of_lane = np.arange(D_MODEL) // RWKV_HEAD
    return jnp.asarray(head_of_lane[:, None] == np.arange(RWKV_HEADS)[None, :], BF16)


def _block_diag2(m0, m1):
    z = jnp.zeros_like(m0)
    return jnp.concatenate([jnp.concatenate([m0, z], 1), jnp.concatenate([z, m1], 1)], 0)


def _rwkv_pre(x, mod_l, norm_g, p, v_first, seq_len):
    t, d = x.shape
    tm = TOKEN_TILE
    tiles_per_seq = seq_len // tm
    n_tiles = t // tm
    has_vres = v_first is not None
    tile = pl.BlockSpec((tm, d), lambda i: (i, 0))
    rows8 = tm // 8
    last8 = t // 8 - 1
    in_specs = [
        tile,
        pl.BlockSpec((8, d), lambda i: (jnp.maximum(i * rows8 - 1, 0), 0)),
        pl.BlockSpec((8, d), lambda i: (jnp.minimum((i + 1) * rows8, last8), 0)),
        pl.BlockSpec((None, 6, d), lambda i: (i // tiles_per_seq, 0, 0)),
        _resident((1, d)), _resident((6, d)),
        _resident((d, d)), _resident((d, d)), _resident((d, d)),
        _resident((d, LANES)), _resident((LANES, 2 * d)),
        _resident((d, LANES)), _resident((LANES, 2 * d)),
        _resident((d, LANES)), _resident((LANES, d)),
        _resident((1, 2 * d)), _resident((1, 2 * d)),
        _resident((1, d)), _resident((1, d)), _resident((1, d)),
        _resident((d, RWKV_HEADS)), _resident((RWKV_HEADS, d)),
    ]
    hsum = _head_sum_matrix()
    args = [x, x, x, mod_l, norm_g.reshape(1, d), p["mu"],
            p["wr"], p["wk"], p["wv"], p["wl1"], p["wl2"], p["al1"], p["al2"], p["g1"], p["g2"],
            p["w0"], p["a0"], p["kk"], p["ka"], p["rk"], hsum, hsum.T]
    if has_vres:
        in_specs += [tile, _resident((1, d)), _resident((d, LANES)), _resident((LANES, d))]
        args += [v_first, p["v0"], p["v1"], p["v2"]]
    dtypes = [BF16] * 5 + [F32] * 2 + [BF16] * 2
    outs = pl.pallas_call(
        functools.partial(_rwkv_pre_kernel, tm=tm, tiles_per_seq=tiles_per_seq, has_vres=has_vres),
        grid=(n_tiles,),
        in_specs=in_specs,
        out_specs=[tile] * 9 + [pl.BlockSpec((tm, RWKV_HEADS), lambda i: (i, 0))],
        out_shape=[jax.ShapeDtypeStruct((t, d), dt) for dt in dtypes]
        + [jax.ShapeDtypeStruct((t, RWKV_HEADS), F32)],
        compiler_params=_params("parallel"),
        name="rwkv_pre",
    )(*args)
    return outs


def _wkv_kernel(r_ref, k_ref, v_ref, kk_ref, wlf_ref, wlb_ref, af_ref, ab_ref, ka_ref,
                lvl_ref, y_ref, state_ref, g_ref, sp_ref, yp_ref, xr_ref, dec_ref,
                *, seq_len, chunk, pairs, group):
    n_chunks = seq_len // chunk
    n_levels = int(math.log2(chunk))
    two = 2 * chunk
    lane = lax.broadcasted_iota(jnp.int32, (1, LANES), 1)
    head0 = lane < RWKV_HEAD
    ri = lax.broadcasted_iota(jnp.int32, (chunk, chunk), 0)
    ci = lax.broadcasted_iota(jnp.int32, (chunk, chunk), 1)
    si = lax.broadcasted_iota(jnp.int32, (two, two), 0)
    sj = lax.broadcasted_iota(jnp.int32, (two, two), 1)
    same_head = (si < chunk) == (sj < chunk)
    st, su = si & (chunk - 1), sj & (chunk - 1)
    eye = (si == sj).astype(F32)
    chain_defs = [(pair, reverse) for pair in range(pairs) for reverse in (False, True)]

    def rows_of(idx, reverse):
        pos = (n_chunks - 1 - idx) if reverse else idx
        return pl.ds(pl.multiple_of(pos * chunk, chunk), chunk)

    def stack(x):
        return jnp.concatenate([jnp.where(head0, x, 0.0), jnp.where(head0, 0.0, x)], axis=0)

    def a_load(q, g, idx, slot):
        pair, reverse = chain_defs[q]
        rows = rows_of(idx, reverse)
        lanes = slice(pair * LANES, (pair + 1) * LANES)
        a = dict(q=q, g=g, slot=slot, rows=rows, lanes=lanes, reverse=reverse)
        a["lw"] = lw = (wlb_ref if reverse else wlf_ref)[rows, lanes]
        a["a"] = (ab_ref if reverse else af_ref)[rows, lanes].astype(F32)
        cum = ((ri <= ci) if reverse else (ri >= ci)).astype(BF16)
        lw_hi = lw.astype(BF16)
        lw_mid, lw_lo = _split_bf16(lw - lw_hi.astype(F32))
        a["logp"] = _dot(cum, lw_hi) + (_dot(cum, lw_mid) + _dot(cum, lw_lo))
        return a

    def a_scale(a):
        rows, lanes, logp, lw, av = a["rows"], a["lanes"], a["logp"], a["lw"], a["a"]
        k = k_ref[rows, lanes].astype(F32)
        kk = kk_ref[rows, lanes].astype(F32)
        kd = k * (1.0 + (av - 1.0) * ka_ref[:, lanes])
        p_in = jnp.exp(logp)
        p_out = jnp.exp(-logp)
        a["xa"] = stack(-kk * jnp.exp(logp - lw)).astype(BF16)
        a["xr"] = stack(r_ref[rows, lanes].astype(F32) * p_in).astype(BF16)
        a["yb"] = stack(kk * av * p_out).astype(BF16)
        a["yk"] = stack(kd * p_out).astype(BF16)
        a["vs"] = stack(v_ref[rows, lanes])
        a["decay"] = jnp.exp(jnp.sum(lw, axis=0, keepdims=True))

    def a_scores(a):
        if a["reverse"]:
            strict, incl = same_head & (st < su), same_head & (st <= su)
        else:
            strict, incl = same_head & (st > su), same_head & (st >= su)
        sc = _dot_nt(jnp.concatenate([a["xa"], a["xr"]], 0),
                     jnp.concatenate([a["yb"], a["yk"]], 0))
        a["m"] = sc[:two, :two].astype(BF16)
        m_ak = jnp.where(strict, sc[:two, two:], 0.0).astype(BF16)
        n_rk = jnp.where(incl, sc[two:, two:], 0.0).astype(BF16)
        a["n_rb"] = jnp.where(incl, sc[two:, :two], 0.0).astype(BF16)
        a["inv"] = eye + (a["m"] * lvl_ref[int(a["reverse"]), 0]).astype(F32)
        a["zp"] = _dot(m_ak, a["vs"]).astype(BF16)
        a["yp"] = _dot(n_rk, a["vs"])
        a["sp"] = _dot_tn(a["vs"], a["yk"])

    def a_double(a, level):
        t = a["inv"].astype(BF16)
        m_n = a["m"] * lvl_ref[int(a["reverse"]), level]
        a["inv"] = a["inv"] + _dot(_dot(t, m_n).astype(BF16), t)

    def a_solve(a):
        t = a["inv"].astype(BF16)
        a["w"] = _dot(t, a["xa"]).astype(BF16)
        a["u0"] = _dot(t, a["zp"]).astype(BF16)

    def a_fold(a):
        at = (a["slot"], a["g"], a["q"])
        g_ref[at] = _dot_tn(a["w"], a["yb"]).astype(BF16)
        sp_ref[at] = a["sp"] + _dot_tn(a["u0"], a["yb"])
        yp_ref[at] = a["yp"] + _dot(a["n_rb"], a["u0"])
        xr_ref[at] = (a["xr"].astype(F32) + _dot(a["n_rb"], a["w"])).astype(BF16)
        dec_ref[at] = a["decay"]

    a_stage_list = ([a_scale, a_scores]
                    + [functools.partial(a_double, level=level) for level in range(1, n_levels)]
                    + [a_solve, a_fold])

    def b_state(b):
        at = b["at"]
        s0 = state_ref[b["q"]]
        b["s0b"] = s0.astype(BF16)
        state_ref[b["q"]] = (s0 + _dot(b["s0b"], g_ref[at]) + sp_ref[at]) * dec_ref[at]

    def b_out(b):
        at = b["at"]
        pair, reverse = chain_defs[b["q"]]
        ys = _dot_nt(xr_ref[at], b["s0b"]) + yp_ref[at]
        y_ref[rows_of(b["idx"], reverse), pair * LANES:(pair + 1) * LANES] += ys[:chunk] + ys[chunk:]

    b_stage_list = [b_state, b_out]

    y_ref[...] = jnp.zeros_like(y_ref)
    state_ref[...] = jnp.zeros_like(state_ref)
    n_chains = len(chain_defs)
    first = [a_load(q, g, g, 0) for g in range(group) for q in range(n_chains)]
    for stage in a_stage_list:
        for a in first:
            stage(a)

    def run_step(step, prepare_next):
        slot = step & 1
        base = step * group
        b_work = []
        for g in range(group):
            now = [dict(q=q, at=(slot, g, q), idx=base + g) for q in range(n_chains)]
            b_work += [(stage, now) for stage in b_stage_list]
        a_work = []
        if prepare_next:
            ahead = [a_load(q, g, base + group + g, 1 - slot)
                     for g in range(group) for q in range(n_chains)]
            a_work = [(stage, ahead) for stage in a_stage_list]
        per_a = -(-len(b_work) // max(len(a_work), 1))
        while a_work or b_work:
            for stage, items in b_work[:per_a]:
                for b in items:
                    stage(b)
            b_work = b_work[per_a:]
            if a_work:
                stage, items = a_work.pop(0)
                for a in items:
                    stage(a)

    n_steps = n_chunks // group

    def body(step, carry):
        run_step(step, True)
        return carry

    lax.fori_loop(0, n_steps - 1, body, 0)
    run_step(n_steps - 1, False)


def _wkv_level_masks(chunk):
    s = np.arange(2 * chunk)
    head, t = s // chunk, s % chunk
    n_levels = int(math.log2(chunk))
    masks = np.zeros((2, n_levels, 2 * chunk, 2 * chunk), np.float32)
    for d in range(2):
        pos = t if d == 0 else chunk - 1 - t
        for j in range(n_levels):
            blk = pos >> j
            masks[d, j] = ((head[:, None] == head[None, :]) & (blk[:, None] % 2 == 1)
                           & (blk[None, :] == blk[:, None] - 1))
    return masks


def _wkv(r, k, v, kk, wlf, wlb, af, ab, k_a, seq_len):
    t, d = r.shape
    nb = t // seq_len
    pairs = WKV_PAIRS_PER_STEP
    width = pairs * LANES
    n_chains = 2 * pairs
    seq = pl.BlockSpec((seq_len, width), lambda b, p: (b, p))
    par = pl.BlockSpec((1, width), lambda b, p: (0, p))
    levels = jnp.asarray(_wkv_level_masks(WKV_CHUNK), BF16)
    group = WKV_CHUNKS_PER_STEP
    parked_f32 = pltpu.VMEM((2, group, n_chains, LANES, LANES), F32)
    parked_bf16 = pltpu.VMEM((2, group, n_chains, LANES, LANES), BF16)
    return pl.pallas_call(
        functools.partial(_wkv_kernel, seq_len=seq_len, chunk=WKV_CHUNK, pairs=pairs, group=group),
        grid=(nb, d // width),
        in_specs=[seq] * 8 + [par, _resident(levels.shape)],
        out_specs=seq,
        out_shape=jax.ShapeDtypeStruct((t, d), F32),
        scratch_shapes=[pltpu.VMEM((n_chains, LANES, LANES), F32), parked_bf16, parked_f32,
                        parked_f32, parked_bf16, pltpu.VMEM((2, group, n_chains, 1, LANES), F32)],
        compiler_params=_params("parallel", "parallel"),
        name="wkv7_chunked",
    )(r, k, v, kk, wlf, wlb, af, ab, k_a.reshape(1, d), levels)


def _rwkv_mid_kernel(y_ref, v_ref, g_ref, bonus_ref, lnw_ref, lnb_ref, hsum_ref, hexp_ref, o_ref):
    hsum = hsum_ref[...]
    hexp = hexp_ref[...]

    def head_sum(x):
        return _head_sums(x, hsum, hexp)

    y = y_ref[...]
    yc = y - head_sum(y) * (1.0 / RWKV_HEAD)
    var = head_sum(yc * yc) * (1.0 / RWKV_HEAD)
    yn = yc * lax.rsqrt(var + RWKV_GN_EPS) * lnw_ref[...] + lnb_ref[...]
    b_hi, b_lo = _split_bf16(bonus_ref[...])
    bonus = (_dot(b_hi, hexp) + _dot(b_lo, hexp)) * v_ref[...].astype(F32)
    o_ref[...] = ((yn + bonus) * g_ref[...].astype(F32)).astype(BF16)


def _rwkv_mid(y, v, g, bonus, p):
    t, d = y.shape
    tm = TOKEN_TILE
    tile = pl.BlockSpec((tm, d), lambda i: (i, 0))
    vec = _resident((1, d))
    hsum = _head_sum_matrix()
    return pl.pallas_call(
        _rwkv_mid_kernel,
        grid=(t // tm,),
        in_specs=[tile] * 3 + [pl.BlockSpec((tm, RWKV_HEADS), lambda i: (i, 0))] + [vec] * 2
        + [_resident((d, RWKV_HEADS)), _resident((RWKV_HEADS, d))],
        out_specs=tile,
        out_shape=jax.ShapeDtypeStruct((t, d), BF16),
        compiler_params=_params("parallel"),
        name="rwkv_mid",
    )(y, v, g, bonus, p["lnw"], p["lnb"], hsum, hsum.T)


def _sgu_kernel(x_ref, mod_ref, ng_ref, win_ref, bin_ref, lg_ref, lb_ref, ws_ref, bs_ref, o_ref,
                *, tm):
    h = _norm_mod(x_ref[...], ng_ref[...], mod_ref[1:2, :], mod_ref[0:1, :]).astype(BF16)
    z = _dot(h, win_ref[...]) + bin_ref[...]
    z = 0.5 * z * (1.0 + lax.erf(z * (1.0 / math.sqrt(2.0))))
    u = z[:, :SGU_DIM]
    v = z[:, SGU_DIM:]
    vc = v - jnp.mean(v, axis=-1, keepdims=True)
    vn = vc * lax.rsqrt(jnp.mean(vc * vc, axis=-1, keepdims=True) + 1e-5) * lg_ref[...] + lb_ref[...]
    vb = vn.astype(BF16)
    for c in range(tm // SGU_CHUNK):
        rows = slice(c * SGU_CHUNK, (c + 1) * SGU_CHUNK)
        for g in range(SGU_GROUPS):
            cols = slice(g * SGU_GC, (g + 1) * SGU_GC)
            s = _dot(ws_ref[g], vb[rows, cols]) + bs_ref[:, cols]
            o_ref[rows, cols] = (u[rows, cols] * s).astype(BF16)


def _sgu(x, mod_l, norm_g, p, seq_len):
    t, d = x.shape
    tm = TOKEN_TILE
    tiles_per_seq = seq_len // tm
    return pl.pallas_call(
        functools.partial(_sgu_kernel, tm=tm),
        grid=(t // tm,),
        in_specs=[pl.BlockSpec((tm, d), lambda i: (i, 0)),
                  pl.BlockSpec((None, 6, d), lambda i: (i // tiles_per_seq, 0, 0)),
                  _resident((1, d)),
                  _resident((d, 2 * SGU_DIM)), _resident((1, 2 * SGU_DIM)),
                  _resident((1, SGU_DIM)), _resident((1, SGU_DIM)),
                  _resident((SGU_GROUPS, SGU_CHUNK, SGU_CHUNK)), _resident((SGU_CHUNK, SGU_DIM))],
        out_specs=pl.BlockSpec((tm, SGU_DIM), lambda i: (i, 0)),
        out_shape=jax.ShapeDtypeStruct((t, SGU_DIM), BF16),
        compiler_params=_params("parallel"),
        name="sgu",
    )(x, mod_l, norm_g.reshape(1, d), p["win"], p["bin"], p["ng"], p["nb"], p["ws"], p["bs"])


def _qkv_kernel(x_ref, mod_ref, ng_ref, w_ref, q_ref, k_ref, v_ref):
    d = D_MODEL
    h = _norm_mod(x_ref[...], ng_ref[...], mod_ref[1:2, :], mod_ref[0:1, :]).astype(BF16)
    qkv = _dot(h, w_ref[...])
    q_ref[...] = (qkv[:, :d] * (DIFF_HEAD ** -0.5)).astype(BF16)
    k_ref[...] = qkv[:, d:2 * d].astype(BF16)
    v_ref[...] = qkv[:, 2 * d:].astype(BF16)


def _qkv(x, mod_l, norm_g, w_qkv, seq_len):
    t, d = x.shape
    tm = TOKEN_TILE
    tiles_per_seq = seq_len // tm
    tile = pl.BlockSpec((tm, d), lambda i: (i, 0))
    return pl.pallas_call(
        _qkv_kernel,
        grid=(t // tm,),
        in_specs=[tile, pl.BlockSpec((None, 6, d), lambda i: (i // tiles_per_seq, 0, 0)),
                  _resident((1, d)), _resident((d, 3 * d))],
        out_specs=[tile] * 3,
        out_shape=[jax.ShapeDtypeStruct((t, d), BF16)] * 3,
        compiler_params=_params("parallel"),
        name="diff_qkv",
    )(x, mod_l, norm_g.reshape(1, d), w_qkv)


def _attn_kernel(q_ref, k_ref, v_ref, win_ref, lam_ref, sg_ref, o_ref, vext_ref, *, tq, seq_len,
                 out_scale):
    @pl.when(pl.program_id(2) == 0)
    def _():
        col = lax.broadcasted_iota(jnp.int32, (seq_len, LANES), 1)
        vext_ref[:, :LANES] = v_ref[...]
        vext_ref[:, LANES:] = jnp.where(col == 0, 1.0, 0.0).astype(BF16)

    q = q_ref[...]
    lane = lax.broadcasted_iota(jnp.int32, (1, LANES), 1)
    zero = jnp.zeros_like(q)
    qs = jnp.concatenate([jnp.where(lane < DIFF_HEAD, q, zero),
                          jnp.where(lane < DIFF_HEAD, zero, q)], axis=0)
    s = _dot_nt(qs, k_ref[...])
    width = seq_len + tq
    window = jnp.broadcast_to(win_ref[...], (tq, width))
    bias = pltpu.roll(window, width - tq + 1, 1, stride=1, stride_axis=0)[:, :seq_len]

    def unnormalised(x):
        e = jnp.exp(x - jnp.max(x, axis=-1, keepdims=True)).astype(BF16)
        pv = _dot(e, vext_ref[...])
        return pv[:, :LANES], pv[:, LANES:LANES + 1]

    o1, l1 = unnormalised(s[:tq] + bias)
    o2, l2 = unnormalised(s[tq:] + bias)
    o = o1 / l1 - lam_ref[...] * (o2 / l2)
    o = o * lax.rsqrt(jnp.mean(o * o, axis=-1, keepdims=True) + 1e-5) * sg_ref[...]
    o_ref[...] = (o * out_scale).astype(BF16)


def _t5_bucket(rel):
    nb = NUM_BUCKETS // 2
    max_exact = nb // 2
    ret = jnp.where(rel > 0, nb, 0)
    n = jnp.abs(rel)
    nf = jnp.maximum(n, 1).astype(F32)
    large = max_exact + (jnp.log(nf / max_exact) / math.log(MAX_DISTANCE / max_exact)
                         * (nb - max_exact)).astype(jnp.int32)
    large = jnp.minimum(large, nb - 1)
    return ret + jnp.where(n < max_exact, n, large)


def _bias_windows(rel_bias, seq_len, tq):
    nqb = seq_len // tq
    j = jnp.arange(seq_len + tq, dtype=jnp.int32)[None, :]
    q_hi = (jnp.arange(nqb, dtype=jnp.int32)[:, None] + 1) * tq
    bucket = _t5_bucket(j - q_hi + 1)
    return jnp.transpose(rel_bias[bucket], (0, 2, 1))[:, :, None, :].astype(F32)


def _diff_attention(q, k, v, windows, lam, subln_g, lambda_init, seq_len):
    t, d = q.shape
    nb = t // seq_len
    tq = min(ATTN_TQ, seq_len)
    nqb = seq_len // tq
    kv = pl.BlockSpec((seq_len, LANES), lambda b, h, i: (b, h))
    qo = pl.BlockSpec((tq, LANES), lambda b, h, i: (b * nqb + i, h))
    return pl.pallas_call(
        functools.partial(_attn_kernel, tq=tq, seq_len=seq_len, out_scale=1.0 - lambda_init),
        grid=(nb, DIFF_HEADS, nqb),
        in_specs=[qo, kv, kv,
                  pl.BlockSpec((None, None, 1, seq_len + tq), lambda b, h, i: (i, h, 0, 0)),
                  pl.BlockSpec((1, 1), lambda b, h, i: (0, 0)),
                  pl.BlockSpec((1, LANES), lambda b, h, i: (0, 0))],
        out_specs=qo,
        out_shape=jax.ShapeDtypeStruct((t, d), BF16),
        scratch_shapes=[pltpu.VMEM((seq_len, 2 * LANES), BF16)],
        compiler_params=_params("parallel", "parallel", "arbitrary"),
        name="diff_attention",
    )(q, k, v, windows, lam.reshape(1, 1), subln_g.reshape(1, LANES))


def _post_kernel(pre_ref, w_ref, b_ref, x_ref, mod_ref, ng_ref, wrh_ref, wrl_ref, br_ref,
                 x1_ref, h2_ref, lg_ref):
    out = _dot(pre_ref[...], w_ref[...]) + b_ref[...]
    x1 = x_ref[...] + mod_ref[2:3, :] * out
    x1_ref[...] = x1
    h2 = _norm_mod(x1, ng_ref[...], mod_ref[4:5, :], mod_ref[3:4, :])
    h2_hi, h2_lo = _split_bf16(h2)
    h2_ref[...] = h2_hi
    lg_ref[...] = (_dot(h2_hi, wrh_ref[...]) + (_dot(h2_hi, wrl_ref[...]) + _dot(h2_lo, wrh_ref[...]))
                   + br_ref[...])


def _route(lg):
    lane = lax.broadcasted_iota(jnp.int32, lg.shape, 1)
    first_e, end_e = MOE_GROUPS, MOE_GROUPS + N_EXPERTS
    neg = -jnp.inf

    def max_and_first(x):
        m = jnp.max(x, axis=-1, keepdims=True)
        return m, jnp.min(jnp.where(x == m, lane, LANES), axis=-1, keepdims=True)

    is_group = lane < first_e
    g_max, grp = max_and_first(jnp.where(is_group, lg, neg))
    p_grp = 1.0 / jnp.sum(jnp.where(is_group, jnp.exp(lg - g_max), 0.0), axis=-1, keepdims=True)
    lane_grp = (lane - first_e) >> int(math.log2(EXPERTS_PER_GROUP))
    in_grp = (lane >= first_e) & (lane < end_e) & (lane_grp == grp)
    cand = jnp.where(in_grp, lg, neg)
    v1, i1 = max_and_first(cand)
    v2, i2 = max_and_first(jnp.where(lane == i1, neg, cand))
    p2 = jnp.exp(v2 - v1)
    gate1 = p_grp / (1.0 + p2)
    gate2 = gate1 * p2
    e1 = (i1 - first_e).astype(F32)
    e2 = (i2 - first_e).astype(F32)
    return jnp.where(lane == 0, e1, jnp.where(lane == 1, e2, jnp.where(lane == 2, gate1,
                                                                    jnp.where(lane == 3, gate2, 0.0))))


def _route_kernel(lg_ref, route_ref, route_t_ref):
    route = _route(lg_ref[...])
    route_ref[...] = route
    route_t_ref[...] = route.T[:8, :]


def _routing(logits):
    t = logits.shape[0]
    tr = max(k for k in range(LANES, ROUTE_TILE + 1, LANES) if t % k == 0)
    return pl.pallas_call(
        _route_kernel,
        grid=(t // tr,),
        in_specs=[pl.BlockSpec((tr, ROUTER_COLS), lambda i: (i, 0))],
        out_specs=[pl.BlockSpec((tr, ROUTER_COLS), lambda i: (i, 0)),
                   pl.BlockSpec((None, 8, tr), lambda i: (i, 0, 0))],
        out_shape=[jax.ShapeDtypeStruct((t, ROUTER_COLS), F32),
                   jax.ShapeDtypeStruct((t // tr, 8, tr), F32)],
        compiler_params=_params("parallel"),
        name="moe_route",
    )(logits)


def _post(pre, w, b, x, mod_l, norm_g2, w_router, b_router, seq_len):
    t, d = x.shape
    din = pre.shape[1]
    tm = min(WIDE_TILE, seq_len)
    tiles_per_seq = seq_len // tm
    tile = pl.BlockSpec((tm, d), lambda i: (i, 0))
    return pl.pallas_call(
        _post_kernel,
        grid=(t // tm,),
        in_specs=[pl.BlockSpec((tm, din), lambda i: (i, 0)), _resident((din, d)), _resident((1, d)),
                  tile, pl.BlockSpec((None, 6, d), lambda i: (i // tiles_per_seq, 0, 0)),
                  _resident((1, d)), _resident((d, ROUTER_COLS)), _resident((d, ROUTER_COLS)),
                  _resident((1, ROUTER_COLS))],
        out_specs=[tile, tile, pl.BlockSpec((tm, ROUTER_COLS), lambda i: (i, 0))],
        out_shape=[jax.ShapeDtypeStruct((t, d), F32), jax.ShapeDtypeStruct((t, d), BF16),
                   jax.ShapeDtypeStruct((t, ROUTER_COLS), F32)],
        compiler_params=_params("parallel"),
        name="post_router",
    )(pre, w, b, x, mod_l, norm_g2.reshape(1, d), *_split_bf16(w_router), b_router)


def _expert_kernel(be_ref, nu_ref, x_ref, wg_ref, wu_ref, wd_ref, *rest, first_block):
    o_ref, wgb_ref, wub_ref, wdb_ref = rest[-4:]
    i = pl.program_id(0)
    blk = first_block + i
    used = blk < nu_ref[0]
    new_expert = (i == 0) | (be_ref[blk] != be_ref[jnp.maximum(blk - 1, 0)])

    @pl.when(used & new_expert)
    def _():
        wgb_ref[...] = wg_ref[...].astype(BF16)
        wub_ref[...] = wu_ref[...].astype(BF16)
        wdb_ref[...] = wd_ref[...].astype(BF16)

    @pl.when(used)
    def _():
        x = x_ref[...]
        hg = _dot(x, wgb_ref[...])
        hu = _dot(x, wub_ref[...])
        act = hg * _sigmoid(hg) * hu
        o_ref[...] = _dot(act.astype(BF16), wdb_ref[...]).astype(BF16)

    @pl.when(jnp.logical_not(used))
    def _():
        o_ref[...] = jnp.zeros_like(o_ref)


def _experts(xs, block_e, n_used, wg, wu, wd, layer, ys_buf, first_block, total_blocks):
    p_rows, d = xs.shape
    n_blocks = p_rows // MOE_BLOCK

    def weight(shape):
        return pl.BlockSpec((None, None) + shape, lambda i, be, nu: (layer, be[first_block + i], 0, 0))

    in_specs = [pl.BlockSpec((MOE_BLOCK, d), lambda i, be, nu: (i, 0)),
                weight((d, EXPERT_DIM)), weight((d, EXPERT_DIM)), weight((EXPERT_DIM, d))]
    args = [block_e, n_used, xs, wg, wu, wd]
    aliases = {}
    if ys_buf is not None:
        in_specs.append(pl.BlockSpec(memory_space=pl.ANY))
        aliases = {len(args): 0}
        args.append(ys_buf)
    grid_spec = pltpu.PrefetchScalarGridSpec(
        num_scalar_prefetch=2,
        grid=(n_blocks,),
        in_specs=in_specs,
        out_specs=pl.BlockSpec((MOE_BLOCK, d), lambda i, be, nu: (first_block + i, 0)),
        scratch_shapes=[pltpu.VMEM((d, EXPERT_DIM), BF16), pltpu.VMEM((d, EXPERT_DIM), BF16),
                        pltpu.VMEM((EXPERT_DIM, d), BF16)],
    )
    return pl.pallas_call(
        functools.partial(_expert_kernel, first_block=first_block),
        grid_spec=grid_spec,
        out_shape=jax.ShapeDtypeStruct((total_blocks * MOE_BLOCK, d), BF16),
        input_output_aliases=aliases,
        compiler_params=_params("arbitrary"),
        name="moe_experts",
    )(*args)


def _combine_kernel(x_ref, y0_ref, y1_ref, route_ref, mod_ref, fg_ref, o_ref, *, final):
    route = route_ref[...]
    moe = route[:, 2:3] * y0_ref[...].astype(F32) + route[:, 3:4] * y1_ref[...].astype(F32)
    x = x_ref[...] + mod_ref[5:6, :] * moe
    if final:
        x = x * lax.rsqrt(jnp.mean(x * x, axis=-1, keepdims=True) + NORM_EPS) * fg_ref[...]
    o_ref[...] = x


def _combine(x_buf, y0, y1, route, mod_l, final_g, final, seq_len, first_tile):
    t, d = x_buf.shape
    tm = min(WIDE_TILE, seq_len)
    tiles_per_seq = seq_len // tm
    here = pl.BlockSpec((tm, d), lambda i: (first_tile + i, 0))
    local = pl.BlockSpec((tm, d), lambda i: (i, 0))
    return pl.pallas_call(
        functools.partial(_combine_kernel, final=final),
        grid=(y0.shape[0] // tm,),
        in_specs=[here, local, local, pl.BlockSpec((tm, ROUTER_COLS), lambda i: (first_tile + i, 0)),
                  pl.BlockSpec((None, 6, d), lambda i: ((first_tile + i) // tiles_per_seq, 0, 0)),
                  _resident((1, d))],
        out_specs=here,
        out_shape=jax.ShapeDtypeStruct((t, d), F32),
        input_output_aliases={0: 0},
        compiler_params=_params("parallel"),
        name="moe_combine",
    )(x_buf, y0, y1, route, mod_l, final_g.reshape(1, d))


def _rank_kernel(e_ref, rank_ref, cnt_ref, carry_ref):
    @pl.when(pl.program_id(0) == 0)
    def _():
        carry_ref[...] = jnp.zeros_like(carry_ref)

    expert_id = lax.broadcasted_iota(jnp.int32, (N_EXPERTS, RANK_TILE), 0)
    ri = lax.broadcasted_iota(jnp.int32, (RANK_TILE, RANK_TILE), 0)
    ci = lax.broadcasted_iota(jnp.int32, (RANK_TILE, RANK_TILE), 1)
    upper = (ri <= ci).astype(BF16)
    carry = carry_ref[...]
    for r in range(e_ref.shape[0]):
        e = e_ref[r:r + 1, :]
        onehot = jnp.where(e == expert_id, 1.0, 0.0)
        prefix = _dot(onehot.astype(BF16), upper)
        rank = jnp.sum(onehot * (prefix + carry), axis=0, keepdims=True) - 1.0
        rank_ref[r:r + 1, :] = rank.astype(jnp.int32)
        carry = carry + jnp.sum(onehot, axis=1, keepdims=True)
    carry_ref[...] = carry
    cnt_ref[...] = jnp.broadcast_to(carry, cnt_ref.shape).astype(jnp.int32)


def _rank(e):
    a = e.shape[0]
    n_tiles = a // RANK_TILE
    rows = _split_count(n_tiles, RANK_ROWS)
    tile = pl.BlockSpec((None, rows, RANK_TILE), lambda i: (i, 0, 0))
    rank, counts = pl.pallas_call(
        _rank_kernel,
        grid=(n_tiles // rows,),
        in_specs=[tile],
        out_specs=[tile, pl.BlockSpec((N_EXPERTS, LANES), lambda i: (0, 0))],
        out_shape=[jax.ShapeDtypeStruct((n_tiles // rows, rows, RANK_TILE), jnp.int32),
                   jax.ShapeDtypeStruct((N_EXPERTS, LANES), jnp.int32)],
        scratch_shapes=[pltpu.VMEM((N_EXPERTS, 1), F32)],
        compiler_params=_params("arbitrary"),
        name="moe_rank",
    )(e.reshape(n_tiles // rows, rows, RANK_TILE))
    return rank.reshape(a), counts[:, 0]


def _dispatch_plan(e, t):
    a = t * MOE_TOP_K
    rank, counts = _rank(e)
    padded = (counts + MOE_BLOCK - 1) // MOE_BLOCK * MOE_BLOCK
    end_pad = jnp.cumsum(padded)
    start_pad = end_pad - padded
    dest = (start_pad[e] + rank).astype(jnp.int32)
    n_blocks = (a + N_EXPERTS * (MOE_BLOCK - 1) + MOE_BLOCK - 1) // MOE_BLOCK
    token = jnp.tile(jnp.arange(t, dtype=jnp.int32), MOE_TOP_K)
    filler = jnp.arange(n_blocks * MOE_BLOCK, dtype=jnp.int32) % t
    row_tok = filler + jnp.zeros_like(filler).at[dest].add(token - dest % t, unique_indices=True,
                                                           mode="promise_in_bounds")
    block_start = jnp.arange(n_blocks, dtype=jnp.int32) * MOE_BLOCK
    block_e = jnp.sum((end_pad[None, :] <= block_start[:, None]).astype(jnp.int32), axis=1)
    block_e = jnp.minimum(block_e, N_EXPERTS - 1).astype(jnp.int32)
    n_used = (end_pad[-1] // MOE_BLOCK).astype(jnp.int32).reshape(1)
    return dest.reshape(MOE_TOP_K, t), row_tok, block_e, n_used


def _split_count(n, want):
    return max(k for k in range(1, want + 1) if n % k == 0)


def _moe(x1, h2, logits, mod_l, wg, wu, wd, layer, final_g, seq_len):
    t = x1.shape[0]
    route, route_t = _routing(logits)
    e = jnp.concatenate([route_t[:, 0, :].reshape(t), route_t[:, 1, :].reshape(t)]).astype(jnp.int32)
    dest, row_tok, block_e, n_used = _dispatch_plan(e, t)
    total_blocks = row_tok.shape[0] // MOE_BLOCK
    n_slices = _split_count(total_blocks, MOE_SLICES)
    per = total_blocks // n_slices
    ys = None
    for c in range(n_slices):
        rows = row_tok[c * per * MOE_BLOCK:(c + 1) * per * MOE_BLOCK]
        ys = _experts(h2[rows], block_e, n_used, wg, wu, wd, layer, ys, c * per, total_blocks)
    final = layer == wg.shape[0] - 1
    tm = min(WIDE_TILE, seq_len)
    n_tiles = x1.shape[0] // tm
    n_slices = _split_count(n_tiles, MOE_SLICES)
    per = n_tiles // n_slices
    x = x1
    for c in range(n_slices):
        d_c = dest[:, c * per * tm:(c + 1) * per * tm]
        x = _combine(x, ys[d_c[0]], ys[d_c[1]], route, mod_l, final_g, final, seq_len, c * per)
    return x


def _pad_cols(w, n):
    return jnp.pad(w, ((0, 0), (0, n - w.shape[1])))


def _pad_rows(w, n):
    return jnp.pad(w, ((0, n - w.shape[0]), (0, 0)))


def _rwkv_params(j, mu, wr, wk, wv, w0, w1, w2, a0, a1, a2, v0, v1, v2, g1, g2, kk, ka, rk,
                 lnw, lnb):
    d = D_MODEL
    p = {
        "mu": mu[j],
        "wr": wr[j].astype(BF16), "wk": wk[j].astype(BF16), "wv": wv[j].astype(BF16),
        "wl1": jnp.concatenate([w1[j, 0], w1[j, 1]], axis=1).astype(BF16),
        "wl2": _block_diag2(w2[j, 0], w2[j, 1]).astype(BF16),
        "al1": jnp.concatenate([a1[j, 0], a1[j, 1]], axis=1).astype(BF16),
        "al2": _block_diag2(a2[j, 0], a2[j, 1]).astype(BF16),
        "g1": g1[j].astype(BF16), "g2": g2[j].astype(BF16),
        "w0": w0[j].reshape(1, 2 * d), "a0": a0[j].reshape(1, 2 * d),
        "kk": kk[j].reshape(1, d), "ka": ka[j].reshape(1, d), "rk": rk[j].reshape(1, d),
        "lnw": lnw[j].reshape(1, d), "lnb": lnb[j].reshape(1, d),
    }
    if j > 0:
        p["v0"] = v0[j - 1].reshape(1, d)
        p["v1"] = _pad_cols(v1[j - 1], LANES).astype(BF16)
        p["v2"] = _pad_rows(v2[j - 1], LANES).astype(BF16)
    return p


def kernel(x_prompt, x_sample, c_prompt, c_sample, ada_w, ada_b, norm_g, final_g, rwkv_mu, rwkv_wr, rwkv_wk, rwkv_wv, rwkv_wo, rwkv_w0, rwkv_w1, rwkv_w2, rwkv_a0, rwkv_a1, rwkv_a2, rwkv_v0, rwkv_v1, rwkv_v2, rwkv_g1, rwkv_g2, rwkv_kk, rwkv_ka, rwkv_rk, rwkv_lnw, rwkv_lnb, sgu_win, sgu_bin, sgu_ng, sgu_nb, sgu_ws, sgu_bs, sgu_wout, sgu_bout, diff_wqkv, diff_wo, diff_lq1, diff_lk1, diff_lq2, diff_lk2, diff_subln, rel_bias, moe_wrg, moe_brg, moe_wre, moe_bre, moe_wg, moe_wu, moe_wd):
    d = D_MODEL
    nb_p, seq_len, _ = x_prompt.shape
    assert x_sample.shape[1] == seq_len
    t_p = nb_p * seq_len
    x = jnp.concatenate([x_prompt.reshape(-1, d), x_sample.reshape(-1, d)], axis=0)
    c = jnp.concatenate([c_prompt, c_sample], axis=0)
    mod = _modulation(c, ada_w, ada_b)
    zero_bias = jnp.zeros((1, d), F32)
    v_first = None
    for i in range(DEPTH):
        j = i // N_MIXERS
        mod_l = mod[i]
        if i % N_MIXERS == 0:
            p = _rwkv_params(j, rwkv_mu, rwkv_wr, rwkv_wk, rwkv_wv, rwkv_w0, rwkv_w1, rwkv_w2,
                             rwkv_a0, rwkv_a1, rwkv_a2, rwkv_v0, rwkv_v1, rwkv_v2, rwkv_g1, rwkv_g2,
                             rwkv_kk, rwkv_ka, rwkv_rk, rwkv_lnw, rwkv_lnb)
            r, k, v, g, kk, wlf, wlb, af, ab, bonus = _rwkv_pre(x, mod_l, norm_g[i, 0], p, v_first,
                                                                seq_len)
            if v_first is None:
                v_first = v
            y = _wkv(r, k, v, kk, wlf, wlb, af, ab, p["ka"], seq_len)
            pre = _rwkv_mid(y, v, g, bonus, p)
            w_out, b_out = rwkv_wo[j].astype(BF16), zero_bias
        elif i % N_MIXERS == 1:
            p = {"win": sgu_win[j].astype(BF16), "bin": sgu_bin[j].reshape(1, -1),
                 "ng": sgu_ng[j].reshape(1, -1), "nb": sgu_nb[j].reshape(1, -1),
                 "ws": sgu_ws[j].astype(BF16),
                 "bs": jnp.repeat(jnp.transpose(sgu_bs[j]), SGU_GC, axis=1)}
            pre = _sgu(x, mod_l, norm_g[i, 0], p, seq_len)
            w_out, b_out = sgu_wout[j].astype(BF16), sgu_bout[j].reshape(1, d)
        else:
            lambda_init = 0.8 - 0.6 * math.exp(-0.3 * i)
            lam = (jnp.exp(jnp.sum(diff_lq1[j] * diff_lk1[j])) - jnp.exp(jnp.sum(diff_lq2[j] * diff_lk2[j]))
                   + lambda_init)
            q, k, v = _qkv(x, mod_l, norm_g[i, 0], diff_wqkv[j].astype(BF16), seq_len)
            windows = _bias_windows(rel_bias, seq_len, min(ATTN_TQ, seq_len))
            pre = _diff_attention(q, k, v, windows, lam, diff_subln[j], lambda_init, seq_len)
            w_out, b_out = diff_wo[j].astype(BF16), zero_bias
        w_router = _pad_cols(jnp.concatenate([moe_wrg[i], moe_wre[i]], axis=1), ROUTER_COLS)
        b_router = _pad_cols(jnp.concatenate([moe_brg[i], moe_bre[i]])[None, :], ROUTER_COLS)
        x1, h2, logits = _post(pre, w_out, b_out, x, mod_l, norm_g[i, 1], w_router, b_router, seq_len)
        x = _moe(x1, h2, logits, mod_l, moe_wg, moe_wu, moe_wd, i, final_g, seq_len)
    return (x[:t_p].reshape(x_prompt.shape), x[t_p:].reshape(x_sample.shape))
```

```python
import functools
import math

import numpy as np
import jax
import jax.numpy as jnp
from jax import lax
from jax.experimental import pallas as pl
from jax.experimental.pallas import tpu as pltpu

F32 = jnp.float32
BF16 = jnp.bfloat16
HIGHEST = lax.Precision.HIGHEST

D_MODEL = 1024
DEPTH = 4
N_MIXERS = 3
NORM_EPS = 1e-6
LANES = 128
VMEM_LIMIT_BYTES = 56 * 1024 * 1024

RWKV_HEAD = 64
RWKV_HEADS = D_MODEL // RWKV_HEAD
RWKV_PAIRS = D_MODEL // LANES
RWKV_GN_EPS = 64e-5
WKV_CHUNK = 64
WKV_PAIRS_PER_STEP = 2
WKV_CHUNKS_PER_STEP = 4

SGU_CHUNK = 128
SGU_DIM = 2 * D_MODEL
SGU_GROUPS = 8
SGU_GC = SGU_DIM // SGU_GROUPS

DIFF_HEADS = 8
DIFF_HEAD = 64
NUM_BUCKETS = 32
MAX_DISTANCE = 128
ATTN_TQ = 512

MOE_GROUPS = 4
EXPERTS_PER_GROUP = 8
N_EXPERTS = MOE_GROUPS * EXPERTS_PER_GROUP
MOE_TOP_K = 2
EXPERT_DIM = D_MODEL // 2
MOE_BLOCK = 512
ROUTER_COLS = LANES
ROUTE_TILE = 2048
RANK_TILE = 512
RANK_ROWS = 8
MOE_SLICES = 4

TOKEN_TILE = 256
WIDE_TILE = 512


def _params(*semantics):
    return pltpu.CompilerParams(dimension_semantics=semantics, vmem_limit_bytes=VMEM_LIMIT_BYTES)


def _resident(shape):
    zeros = (0,) * len(shape)
    return pl.BlockSpec(shape, lambda *_: zeros, pipeline_mode=pl.Buffered(1))


def _dot(a, b, precision=None):
    return jnp.dot(a, b, preferred_element_type=F32, precision=precision)


def _dot_nt(a, b, precision=None):
    return lax.dot_general(a, b, (((1,), (1,)), ((), ())), preferred_element_type=F32,
                           precision=precision)


def _dot_tn(a, b, precision=None):
    return lax.dot_general(a, b, (((0,), (0,)), ((), ())), preferred_element_type=F32,
                           precision=precision)


def _split_bf16(x):
    hi = x.astype(BF16)
    return hi, (x - hi.astype(F32)).astype(BF16)


def _head_sums(x, hsum, hexp):
    hi, lo = _split_bf16(x)
    shi, slo = _split_bf16(_dot(hi, hsum) + _dot(lo, hsum))
    return _dot(shi, hexp) + _dot(slo, hexp)


def _sigmoid(x):
    return 1.0 / (1.0 + jnp.exp(-x))


def _norm_mod(x, g, scale, shift, eps=NORM_EPS):
    y = x * lax.rsqrt(jnp.mean(x * x, axis=-1, keepdims=True) + eps)
    return (y * g) * (1.0 + scale) + shift


def _mod_kernel(c_ref, w_ref, b_ref, o_ref):
    c = c_ref[...]
    o_ref[...] = _dot(c * _sigmoid(c), w_ref[...], HIGHEST) + b_ref[...]


def _modulation(c, ada_w, ada_b):
    nb = c.shape[0]
    depth, d, n = ada_w.shape
    out = pl.pallas_call(
        _mod_kernel,
        grid=(depth, n // d),
        in_specs=[pl.BlockSpec((nb, d), lambda l, j: (0, 0)),
                  pl.BlockSpec((None, d, d), lambda l, j: (l, 0, j)),
                  pl.BlockSpec((None, 1, d), lambda l, j: (l, 0, j))],
        out_specs=pl.BlockSpec((None, nb, d), lambda l, j: (l, 0, j)),
        out_shape=jax.ShapeDtypeStruct((depth, nb, n), F32),
        compiler_params=_params("parallel", "parallel"),
        name="adaln_modulation",
    )(c, ada_w, ada_b.reshape(depth, 1, n))
    return out.reshape(depth, nb, n // d, d)


def _rwkv_pre_kernel(*refs, tm, tiles_per_seq, has_vres):
    if has_vres:
        (x_ref, xp_ref, xn_ref, mod_ref, ng_ref, mu_ref, wr_ref, wk_ref, wv_ref, wl1_ref, wl2_ref,
         al1_ref, al2_ref, g1_ref, g2_ref, w0_ref, a0_ref, kscale_ref, ka_ref, rk_ref, hsum_ref,
         hexp_ref, vf_ref, v0_ref, v1_ref, v2_ref,
         r_ref, k_ref, v_ref, g_ref, kk_ref, wlf_ref, wlb_ref, af_ref, ab_ref, bonus_ref) = refs
    else:
        (x_ref, xp_ref, xn_ref, mod_ref, ng_ref, mu_ref, wr_ref, wk_ref, wv_ref, wl1_ref, wl2_ref,
         al1_ref, al2_ref, g1_ref, g2_ref, w0_ref, a0_ref, kscale_ref, ka_ref, rk_ref, hsum_ref,
         hexp_ref,
         r_ref, k_ref, v_ref, g_ref, kk_ref, wlf_ref, wlb_ref, af_ref, ab_ref, bonus_ref) = refs
    d = D_MODEL
    i = pl.program_id(0)
    pos = i % tiles_per_seq
    shift, scale = mod_ref[0:1, :], mod_ref[1:2, :]
    ng = ng_ref[...]
    h = _norm_mod(x_ref[...], ng, scale, shift)
    h_prev = jnp.where(pos == 0, 0.0, _norm_mod(xp_ref[7:8, :], ng, scale, shift))
    h_next = jnp.where(pos == tiles_per_seq - 1, 0.0, _norm_mod(xn_ref[0:1, :], ng, scale, shift))
    row = lax.broadcasted_iota(jnp.int32, (tm, 1), 0)
    h_up = jnp.where(row == 0, h_prev, pltpu.roll(h, 1, 0))
    h_dn = jnp.where(row == tm - 1, h_next, pltpu.roll(h, tm - 1, 0))
    xx = 0.5 * (h_up + h_dn) - h

    def mix(n):
        return (h + xx * mu_ref[n:n + 1, :]).astype(BF16)

    xr, xw, xk, xv, xa, xg = [mix(n) for n in range(6)]
    r = _dot(xr, wr_ref[...])
    r_ref[...] = r.astype(BF16)
    k = _dot(xk, wk_ref[...])
    k_ref[...] = k.astype(BF16)
    kkr = k * kscale_ref[...]
    norm = jnp.sqrt(_head_sums(kkr * kkr, hsum_ref[...], hexp_ref[...]))
    kk_ref[...] = (kkr / jnp.maximum(norm, 1e-12)).astype(BF16)
    v = _dot(xv, wv_ref[...])
    if has_vres:
        lv = _dot(_dot(xv, v1_ref[...]).astype(BF16), v2_ref[...])
        v = v + (vf_ref[...].astype(F32) - v) * _sigmoid(v0_ref[...] + lv)
    v_ref[...] = v.astype(BF16)
    g_ref[...] = _dot(_sigmoid(_dot(xg, g1_ref[...])).astype(BF16), g2_ref[...]).astype(BF16)
    wl = w0_ref[...] + _dot(jnp.tanh(_dot(xw, wl1_ref[...])).astype(BF16), wl2_ref[...])
    logw = -math.exp(-0.5) * _sigmoid(wl)
    wlf_ref[...] = logw[:, :d]
    wlb_ref[...] = logw[:, d:]
    a = _sigmoid(a0_ref[...] + _dot(_dot(xa, al1_ref[...]).astype(BF16), al2_ref[...]))
    af_ref[...] = a[:, :d].astype(BF16)
    ab_ref[...] = a[:, d:].astype(BF16)
    k_bonus = k * (1.0 + (0.5 * (a[:, :d] + a[:, d:]) - 1.0) * ka_ref[...])
    hi, lo = _split_bf16(r * k_bonus * rk_ref[...])
    bonus_ref[...] = _dot(hi, hsum_ref[...]) + _dot(lo, hsum_ref[...])


def _head_sum_matrix():
    head_of_lane = np.arange(D_MODEL) // RWKV_HEAD
    return jnp.asarray(head_of_lane[:, None] == np.arange(RWKV_HEADS)[None, :], BF16)


def _block_diag2(m0, m1):
    z = jnp.zeros_like(m0)
    return jnp.concatenate([jnp.concatenate([m0, z], 1), jnp.concatenate([z, m1], 1)], 0)


def _rwkv_pre(x, mod_l, norm_g, p, v_first, seq_len):
    t, d = x.shape
    tm = min(WIDE_TILE, seq_len)
    tiles_per_seq = seq_len // tm
    n_tiles = t // tm
    has_vres = v_first is not None
    tile = pl.BlockSpec((tm, d), lambda i: (i, 0))
    rows8 = tm // 8
    last8 = t // 8 - 1
    in_specs = [
        tile,
        pl.BlockSpec((8, d), lambda i: (jnp.maximum(i * rows8 - 1, 0), 0)),
        pl.BlockSpec((8, d), lambda i: (jnp.minimum((i + 1) * rows8, last8), 0)),
        pl.BlockSpec((None, 6, d), lambda i: (i // tiles_per_seq, 0, 0)),
        _resident((1, d)), _resident((6, d)),
        _resident((d, d)), _resident((d, d)), _resident((d, d)),
        _resident((d, LANES)), _resident((LANES, 2 * d)),
        _resident((d, LANES)), _resident((LANES, 2 * d)),
        _resident((d, LANES)), _resident((LANES, d)),
        _resident((1, 2 * d)), _resident((1, 2 * d)),
        _resident((1, d)), _resident((1, d)), _resident((1, d)),
        _resident((d, RWKV_HEADS)), _resident((RWKV_HEADS, d)),
    ]
    hsum = _head_sum_matrix()
    args = [x, x, x, mod_l, norm_g.reshape(1, d), p["mu"],
            p["wr"], p["wk"], p["wv"], p["wl1"], p["wl2"], p["al1"], p["al2"], p["g1"], p["g2"],
            p["w0"], p["a0"], p["kk"], p["ka"], p["rk"], hsum, hsum.T]
    if has_vres:
        in_specs += [tile, _resident((1, d)), _resident((d, LANES)), _resident((LANES, d))]
        args += [v_first, p["v0"], p["v1"], p["v2"]]
    dtypes = [BF16] * 5 + [F32] * 2 + [BF16] * 2
    outs = pl.pallas_call(
        functools.partial(_rwkv_pre_kernel, tm=tm, tiles_per_seq=tiles_per_seq, has_vres=has_vres),
        grid=(n_tiles,),
        in_specs=in_specs,
        out_specs=[tile] * 9 + [pl.BlockSpec((tm, RWKV_HEADS), lambda i: (i, 0))],
        out_shape=[jax.ShapeDtypeStruct((t, d), dt) for dt in dtypes]
        + [jax.ShapeDtypeStruct((t, RWKV_HEADS), F32)],
        compiler_params=_params("parallel"),
        name="rwkv_pre",
    )(*args)
    return outs


def _wkv_kernel(r_ref, k_ref, v_ref, kk_ref, wlf_ref, wlb_ref, af_ref, ab_ref, ka_ref,
                lvl_ref, y_ref, state_ref, g_ref, sp_ref, yp_ref, xr_ref, dec_ref,
                *, seq_len, chunk, pairs, group):
    n_chunks = seq_len // chunk
    n_levels = int(math.log2(chunk))
    two = 2 * chunk
    lane = lax.broadcasted_iota(jnp.int32, (1, LANES), 1)
    head0 = lane < RWKV_HEAD
    ri = lax.broadcasted_iota(jnp.int32, (chunk, chunk), 0)
    ci = lax.broadcasted_iota(jnp.int32, (chunk, chunk), 1)
    si = lax.broadcasted_iota(jnp.int32, (two, two), 0)
    sj = lax.broadcasted_iota(jnp.int32, (two, two), 1)
    same_head = (si < chunk) == (sj < chunk)
    st, su = si & (chunk - 1), sj & (chunk - 1)
    eye = (si == sj).astype(F32)
    chain_defs = [(pair, reverse) for pair in range(pairs) for reverse in (False, True)]

    def rows_of(idx, reverse):
        pos = (n_chunks - 1 - idx) if reverse else idx
        return pl.ds(pl.multiple_of(pos * chunk, chunk), chunk)

    def stack(x):
        return jnp.concatenate([jnp.where(head0, x, 0.0), jnp.where(head0, 0.0, x)], axis=0)

    def a_load(q, g, idx, slot):
        pair, reverse = chain_defs[q]
        rows = rows_of(idx, reverse)
        lanes = slice(pair * LANES, (pair + 1) * LANES)
        a = dict(q=q, g=g, slot=slot, rows=rows, lanes=lanes, reverse=reverse)
        a["lw"] = lw = (wlb_ref if reverse else wlf_ref)[rows, lanes]
        a["a"] = (ab_ref if reverse else af_ref)[rows, lanes].astype(F32)
        cum = ((ri <= ci) if reverse else (ri >= ci)).astype(BF16)
        lw_hi = lw.astype(BF16)
        lw_mid, lw_lo = _split_bf16(lw - lw_hi.astype(F32))
        a["logp"] = _dot(cum, lw_hi) + (_dot(cum, lw_mid) + _dot(cum, lw_lo))
        return a

    def a_scale(a):
        rows, lanes, logp, lw, av = a["rows"], a["lanes"], a["logp"], a["lw"], a["a"]
        k = k_ref[rows, lanes].astype(F32)
        kk = kk_ref[rows, lanes].astype(F32)
        kd = k * (1.0 + (av - 1.0) * ka_ref[:, lanes])
        p_in = jnp.exp(logp)
        p_out = jnp.exp(-logp)
        a["xa"] = stack(-kk * jnp.exp(logp - lw)).astype(BF16)
        a["xr"] = stack(r_ref[rows, lanes].astype(F32) * p_in).astype(BF16)
        a["yb"] = stack(kk * av * p_out).astype(BF16)
        a["yk"] = stack(kd * p_out).astype(BF16)
        a["vs"] = stack(v_ref[rows, lanes])
        a["decay"] = jnp.exp(jnp.sum(lw, axis=0, keepdims=True))

    def a_scores(a):
        if a["reverse"]:
            strict, incl = same_head & (st < su), same_head & (st <= su)
        else:
            strict, incl = same_head & (st > su), same_head & (st >= su)
        sc = _dot_nt(jnp.concatenate([a["xa"], a["xr"]], 0),
                     jnp.concatenate([a["yb"], a["yk"]], 0))
        a["m"] = sc[:two, :two].astype(BF16)
        m_ak = jnp.where(strict, sc[:two, two:], 0.0).astype(BF16)
        n_rk = jnp.where(incl, sc[two:, two:], 0.0).astype(BF16)
        a["n_rb"] = jnp.where(incl, sc[two:, :two], 0.0).astype(BF16)
        a["inv"] = eye + (a["m"] * lvl_ref[int(a["reverse"]), 0]).astype(F32)
        a["zp"] = _dot(m_ak, a["vs"]).astype(BF16)
        a["yp"] = _dot(n_rk, a["vs"])
        a["sp"] = _dot_tn(a["vs"], a["yk"])

    def a_double(a, level):
        t = a["inv"].astype(BF16)
        m_n = a["m"] * lvl_ref[int(a["reverse"]), level]
        a["inv"] = a["inv"] + _dot(_dot(t, m_n).astype(BF16), t)

    def a_solve(a):
        t = a["inv"].astype(BF16)
        a["w"] = _dot(t, a["xa"]).astype(BF16)
        a["u0"] = _dot(t, a["zp"]).astype(BF16)

    def a_fold(a):
        at = (a["slot"], a["g"], a["q"])
        g_ref[at] = _dot_tn(a["w"], a["yb"]).astype(BF16)
        sp_ref[at] = a["sp"] + _dot_tn(a["u0"], a["yb"])
        yp_ref[at] = a["yp"] + _dot(a["n_rb"], a["u0"])
        xr_ref[at] = (a["xr"].astype(F32) + _dot(a["n_rb"], a["w"])).astype(BF16)
        dec_ref[at] = a["decay"]

    a_stage_list = ([a_scale, a_scores]
                    + [functools.partial(a_double, level=level) for level in range(1, n_levels)]
                    + [a_solve, a_fold])

    def b_state(b):
        at = b["at"]
        s0 = state_ref[b["q"]]
        b["s0b"] = s0.astype(BF16)
        state_ref[b["q"]] = (s0 + _dot(b["s0b"], g_ref[at]) + sp_ref[at]) * dec_ref[at]

    def b_out(b):
        at = b["at"]
        pair, reverse = chain_defs[b["q"]]
        ys = _dot_nt(xr_ref[at], b["s0b"]) + yp_ref[at]
        y_ref[rows_of(b["idx"], reverse), pair * LANES:(pair + 1) * LANES] += ys[:chunk] + ys[chunk:]

    b_stage_list = [b_state, b_out]

    y_ref[...] = jnp.zeros_like(y_ref)
    state_ref[...] = jnp.zeros_like(state_ref)
    n_chains = len(chain_defs)
    first = [a_load(q, g, g, 0) for g in range(group) for q in range(n_chains)]
    for stage in a_stage_list:
        for a in first:
            stage(a)

    def run_step(step, prepare_next):
        slot = step & 1
        base = step * group
        b_work = []
        for g in range(group):
            now = [dict(q=q, at=(slot, g, q), idx=base + g) for q in range(n_chains)]
            b_work += [(stage, now) for stage in b_stage_list]
        a_work = []
        if prepare_next:
            ahead = [a_load(q, g, base + group + g, 1 - slot)
                     for g in range(group) for q in range(n_chains)]
            a_work = [(stage, ahead) for stage in a_stage_list]
        per_a = -(-len(b_work) // max(len(a_work), 1))
        while a_work or b_work:
            for stage, items in b_work[:per_a]:
                for b in items:
                    stage(b)
            b_work = b_work[per_a:]
            if a_work:
                stage, items = a_work.pop(0)
                for a in items:
                    stage(a)

    n_steps = n_chunks // group

    def body(step, carry):
        run_step(step, True)
        return carry

    lax.fori_loop(0, n_steps - 1, body, 0)
    run_step(n_steps - 1, False)


def _wkv_level_masks(chunk):
    s = np.arange(2 * chunk)
    head, t = s // chunk, s % chunk
    n_levels = int(math.log2(chunk))
    masks = np.zeros((2, n_levels, 2 * chunk, 2 * chunk), np.float32)
    for d in range(2):
        pos = t if d == 0 else chunk - 1 - t
        for j in range(n_levels):
            blk = pos >> j
            masks[d, j] = ((head[:, None] == head[None, :]) & (blk[:, None] % 2 == 1)
                           & (blk[None, :] == blk[:, None] - 1))
    return masks


def _wkv(r, k, v, kk, wlf, wlb, af, ab, k_a, seq_len):
    t, d = r.shape
    nb = t // seq_len
    pairs = WKV_PAIRS_PER_STEP
    width = pairs * LANES
    n_chains = 2 * pairs
    seq = pl.BlockSpec((seq_len, width), lambda b, p: (b, p))
    par = pl.BlockSpec((1, width), lambda b, p: (0, p))
    levels = jnp.asarray(_wkv_level_masks(WKV_CHUNK), BF16)
    group = WKV_CHUNKS_PER_STEP
    parked_f32 = pltpu.VMEM((2, group, n_chains, LANES, LANES), F32)
    parked_bf16 = pltpu.VMEM((2, group, n_chains, LANES, LANES), BF16)
    return pl.pallas_call(
        functools.partial(_wkv_kernel, seq_len=seq_len, chunk=WKV_CHUNK, pairs=pairs, group=group),
        grid=(nb, d // width),
        in_specs=[seq] * 8 + [par, _resident(levels.shape)],
        out_specs=seq,
        out_shape=jax.ShapeDtypeStruct((t, d), F32),
        scratch_shapes=[pltpu.VMEM((n_chains, LANES, LANES), F32), parked_bf16, parked_f32,
                        parked_f32, parked_bf16, pltpu.VMEM((2, group, n_chains, 1, LANES), F32)],
        compiler_params=_params("parallel", "parallel"),
        name="wkv7_chunked",
    )(r, k, v, kk, wlf, wlb, af, ab, k_a.reshape(1, d), levels)


def _rwkv_mid_kernel(y_ref, v_ref, g_ref, bonus_ref, lnw_ref, lnb_ref, hsum_ref, hexp_ref, o_ref):
    hsum = hsum_ref[...]
    hexp = hexp_ref[...]

    def head_sum(x):
        return _head_sums(x, hsum, hexp)

    y = y_ref[...]
    yc = y - head_sum(y) * (1.0 / RWKV_HEAD)
    var = head_sum(yc * yc) * (1.0 / RWKV_HEAD)
    yn = yc * lax.rsqrt(var + RWKV_GN_EPS) * lnw_ref[...] + lnb_ref[...]
    b_hi, b_lo = _split_bf16(bonus_ref[...])
    bonus = (_dot(b_hi, hexp) + _dot(b_lo, hexp)) * v_ref[...].astype(F32)
    o_ref[...] = ((yn + bonus) * g_ref[...].astype(F32)).astype(BF16)


def _rwkv_mid(y, v, g, bonus, p):
    t, d = y.shape
    tm = TOKEN_TILE
    tile = pl.BlockSpec((tm, d), lambda i: (i, 0))
    vec = _resident((1, d))
    hsum = _head_sum_matrix()
    return pl.pallas_call(
        _rwkv_mid_kernel,
        grid=(t // tm,),
        in_specs=[tile] * 3 + [pl.BlockSpec((tm, RWKV_HEADS), lambda i: (i, 0))] + [vec] * 2
        + [_resident((d, RWKV_HEADS)), _resident((RWKV_HEADS, d))],
        out_specs=tile,
        out_shape=jax.ShapeDtypeStruct((t, d), BF16),
        compiler_params=_params("parallel"),
        name="rwkv_mid",
    )(y, v, g, bonus, p["lnw"], p["lnb"], hsum, hsum.T)


def _sgu_kernel(x_ref, mod_ref, ng_ref, win_ref, bin_ref, lg_ref, lb_ref, ws_ref, bs_ref, o_ref,
                *, tm):
    h = _norm_mod(x_ref[...], ng_ref[...], mod_ref[1:2, :], mod_ref[0:1, :]).astype(BF16)
    z = _dot(h, win_ref[...]) + bin_ref[...]
    z = 0.5 * z * (1.0 + lax.erf(z * (1.0 / math.sqrt(2.0))))
    u = z[:, :SGU_DIM]
    v = z[:, SGU_DIM:]
    vc = v - jnp.mean(v, axis=-1, keepdims=True)
    vn = vc * lax.rsqrt(jnp.mean(vc * vc, axis=-1, keepdims=True) + 1e-5) * lg_ref[...] + lb_ref[...]
    vb = vn.astype(BF16)
    for c in range(tm // SGU_CHUNK):
        rows = slice(c * SGU_CHUNK, (c + 1) * SGU_CHUNK)
        for g in range(SGU_GROUPS):
            cols = slice(g * SGU_GC, (g + 1) * SGU_GC)
            s = _dot(ws_ref[g], vb[rows, cols]) + bs_ref[:, cols]
            o_ref[rows, cols] = (u[rows, cols] * s).astype(BF16)


def _sgu(x, mod_l, norm_g, p, seq_len):
    t, d = x.shape
    tm = min(WIDE_TILE, seq_len)
    tiles_per_seq = seq_len // tm
    return pl.pallas_call(
        functools.partial(_sgu_kernel, tm=tm),
        grid=(t // tm,),
        in_specs=[pl.BlockSpec((tm, d), lambda i: (i, 0)),
                  pl.BlockSpec((None, 6, d), lambda i: (i // tiles_per_seq, 0, 0)),
                  _resident((1, d)),
                  _resident((d, 2 * SGU_DIM)), _resident((1, 2 * SGU_DIM)),
                  _resident((1, SGU_DIM)), _resident((1, SGU_DIM)),
                  _resident((SGU_GROUPS, SGU_CHUNK, SGU_CHUNK)), _resident((SGU_CHUNK, SGU_DIM))],
        out_specs=pl.BlockSpec((tm, SGU_DIM), lambda i: (i, 0)),
        out_shape=jax.ShapeDtypeStruct((t, SGU_DIM), BF16),
        compiler_params=_params("parallel"),
        name="sgu",
    )(x, mod_l, norm_g.reshape(1, d), p["win"], p["bin"], p["ng"], p["nb"], p["ws"], p["bs"])


def _qkv_kernel(x_ref, mod_ref, ng_ref, w_ref, q_ref, k_ref, v_ref):
    d = D_MODEL
    h = _norm_mod(x_ref[...], ng_ref[...], mod_ref[1:2, :], mod_ref[0:1, :]).astype(BF16)
    qkv = _dot(h, w_ref[...])
    q_ref[...] = (qkv[:, :d] * (DIFF_HEAD ** -0.5)).astype(BF16)
    k_ref[...] = qkv[:, d:2 * d].astype(BF16)
    v_ref[...] = qkv[:, 2 * d:].astype(BF16)


def _qkv(x, mod_l, norm_g, w_qkv, seq_len):
    t, d = x.shape
    tm = TOKEN_TILE
    tiles_per_seq = seq_len // tm
    tile = pl.BlockSpec((tm, d), lambda i: (i, 0))
    return pl.pallas_call(
        _qkv_kernel,
        grid=(t // tm,),
        in_specs=[tile, pl.BlockSpec((None, 6, d), lambda i: (i // tiles_per_seq, 0, 0)),
                  _resident((1, d)), _resident((d, 3 * d))],
        out_specs=[tile] * 3,
        out_shape=[jax.ShapeDtypeStruct((t, d), BF16)] * 3,
        compiler_params=_params("parallel"),
        name="diff_qkv",
    )(x, mod_l, norm_g.reshape(1, d), w_qkv)


def _attn_kernel(q_ref, k_ref, v_ref, win_ref, lam_ref, sg_ref, o_ref, vext_ref, *, tq, seq_len,
                 out_scale):
    @pl.when(pl.program_id(2) == 0)
    def _():
        col = lax.broadcasted_iota(jnp.int32, (seq_len, LANES), 1)
        vext_ref[:, :LANES] = v_ref[...]
        vext_ref[:, LANES:] = jnp.where(col == 0, 1.0, 0.0).astype(BF16)

    q = q_ref[...]
    lane = lax.broadcasted_iota(jnp.int32, (1, LANES), 1)
    zero = jnp.zeros_like(q)
    qs = jnp.concatenate([jnp.where(lane < DIFF_HEAD, q, zero),
                          jnp.where(lane < DIFF_HEAD, zero, q)], axis=0)
    s = _dot_nt(qs, k_ref[...])
    width = seq_len + tq
    window = jnp.broadcast_to(win_ref[...], (tq, width))
    bias = pltpu.roll(window, width - tq + 1, 1, stride=1, stride_axis=0)[:, :seq_len]

    def unnormalised(x):
        e = jnp.exp(x - jnp.max(x, axis=-1, keepdims=True)).astype(BF16)
        pv = _dot(e, vext_ref[...])
        return pv[:, :LANES], pv[:, LANES:LANES + 1]

    o1, l1 = unnormalised(s[:tq] + bias)
    o2, l2 = unnormalised(s[tq:] + bias)
    o = o1 / l1 - lam_ref[...] * (o2 / l2)
    o = o * lax.rsqrt(jnp.mean(o * o, axis=-1, keepdims=True) + 1e-5) * sg_ref[...]
    o_ref[...] = (o * out_scale).astype(BF16)


def _t5_bucket(rel):
    nb = NUM_BUCKETS // 2
    max_exact = nb // 2
    ret = jnp.where(rel > 0, nb, 0)
    n = jnp.abs(rel)
    nf = jnp.maximum(n, 1).astype(F32)
    large = max_exact + (jnp.log(nf / max_exact) / math.log(MAX_DISTANCE / max_exact)
                         * (nb - max_exact)).astype(jnp.int32)
    large = jnp.minimum(large, nb - 1)
    return ret + jnp.where(n < max_exact, n, large)


def _bias_windows(rel_bias, seq_len, tq):
    nqb = seq_len // tq
    j = jnp.arange(seq_len + tq, dtype=jnp.int32)[None, :]
    q_hi = (jnp.arange(nqb, dtype=jnp.int32)[:, None] + 1) * tq
    bucket = _t5_bucket(j - q_hi + 1)
    return jnp.transpose(rel_bias[bucket], (0, 2, 1))[:, :, None, :].astype(F32)


def _diff_attention(q, k, v, windows, lam, subln_g, lambda_init, seq_len):
    t, d = q.shape
    nb = t // seq_len
    tq = min(ATTN_TQ, seq_len)
    nqb = seq_len // tq
    kv = pl.BlockSpec((seq_len, LANES), lambda b, h, i: (b, h))
    qo = pl.BlockSpec((tq, LANES), lambda b, h, i: (b * nqb + i, h))
    return pl.pallas_call(
        functools.partial(_attn_kernel, tq=tq, seq_len=seq_len, out_scale=1.0 - lambda_init),
        grid=(nb, DIFF_HEADS, nqb),
        in_specs=[qo, kv, kv,
                  pl.BlockSpec((None, None, 1, seq_len + tq), lambda b, h, i: (i, h, 0, 0)),
                  pl.BlockSpec((1, 1), lambda b, h, i: (0, 0)),
                  pl.BlockSpec((1, LANES), lambda b, h, i: (0, 0))],
        out_specs=qo,
        out_shape=jax.ShapeDtypeStruct((t, d), BF16),
        scratch_shapes=[pltpu.VMEM((seq_len, 2 * LANES), BF16)],
        compiler_params=_params("parallel", "parallel", "arbitrary"),
        name="diff_attention",
    )(q, k, v, windows, lam.reshape(1, 1), subln_g.reshape(1, LANES))


def _post_kernel(pre_ref, w_ref, b_ref, x_ref, mod_ref, ng_ref, wrh_ref, wrl_ref, br_ref,
                 x1_ref, h2_ref, lg_ref):
    out = _dot(pre_ref[...], w_ref[...]) + b_ref[...]
    x1 = x_ref[...] + mod_ref[2:3, :] * out
    x1_ref[...] = x1
    h2 = _norm_mod(x1, ng_ref[...], mod_ref[4:5, :], mod_ref[3:4, :])
    h2_hi, h2_lo = _split_bf16(h2)
    h2_ref[...] = h2_hi
    lg_ref[...] = (_dot(h2_hi, wrh_ref[...]) + (_dot(h2_hi, wrl_ref[...]) + _dot(h2_lo, wrh_ref[...]))
                   + br_ref[...])


def _route(lg):
    lane = lax.broadcasted_iota(jnp.int32, lg.shape, 1)
    first_e, end_e = MOE_GROUPS, MOE_GROUPS + N_EXPERTS
    neg = -jnp.inf

    def max_and_first(x):
        m = jnp.max(x, axis=-1, keepdims=True)
        return m, jnp.min(jnp.where(x == m, lane, LANES), axis=-1, keepdims=True)

    is_group = lane < first_e
    g_max, grp = max_and_first(jnp.where(is_group, lg, neg))
    p_grp = 1.0 / jnp.sum(jnp.where(is_group, jnp.exp(lg - g_max), 0.0), axis=-1, keepdims=True)
    lane_grp = (lane - first_e) >> int(math.log2(EXPERTS_PER_GROUP))
    in_grp = (lane >= first_e) & (lane < end_e) & (lane_grp == grp)
    cand = jnp.where(in_grp, lg, neg)
    v1, i1 = max_and_first(cand)
    v2, i2 = max_and_first(jnp.where(lane == i1, neg, cand))
    p2 = jnp.exp(v2 - v1)
    gate1 = p_grp / (1.0 + p2)
    gate2 = gate1 * p2
    e1 = (i1 - first_e).astype(F32)
    e2 = (i2 - first_e).astype(F32)
    return jnp.where(lane == 0, e1, jnp.where(lane == 1, e2, jnp.where(lane == 2, gate1,
                                                                    jnp.where(lane == 3, gate2, 0.0))))


def _route_kernel(lg_ref, route_ref, route_t_ref):
    route = _route(lg_ref[...])
    route_ref[...] = route
    route_t_ref[...] = route.T[:8, :]


def _routing(logits):
    t = logits.shape[0]
    tr = max(k for k in range(LANES, ROUTE_TILE + 1, LANES) if t % k == 0)
    return pl.pallas_call(
        _route_kernel,
        grid=(t // tr,),
        in_specs=[pl.BlockSpec((tr, ROUTER_COLS), lambda i: (i, 0))],
        out_specs=[pl.BlockSpec((tr, ROUTER_COLS), lambda i: (i, 0)),
                   pl.BlockSpec((None, 8, tr), lambda i: (i, 0, 0))],
        out_shape=[jax.ShapeDtypeStruct((t, ROUTER_COLS), F32),
                   jax.ShapeDtypeStruct((t // tr, 8, tr), F32)],
        compiler_params=_params("parallel"),
        name="moe_route",
    )(logits)


def _post(pre, w, b, x, mod_l, norm_g2, w_router, b_router, seq_len):
    t, d = x.shape
    din = pre.shape[1]
    tm = min(WIDE_TILE, seq_len)
    tiles_per_seq = seq_len // tm
    tile = pl.BlockSpec((tm, d), lambda i: (i, 0))
    return pl.pallas_call(
        _post_kernel,
        grid=(t // tm,),
        in_specs=[pl.BlockSpec((tm, din), lambda i: (i, 0)), _resident((din, d)), _resident((1, d)),
                  tile, pl.BlockSpec((None, 6, d), lambda i: (i // tiles_per_seq, 0, 0)),
                  _resident((1, d)), _resident((d, ROUTER_COLS)), _resident((d, ROUTER_COLS)),
                  _resident((1, ROUTER_COLS))],
        out_specs=[tile, tile, pl.BlockSpec((tm, ROUTER_COLS), lambda i: (i, 0))],
        out_shape=[jax.ShapeDtypeStruct((t, d), F32), jax.ShapeDtypeStruct((t, d), BF16),
                   jax.ShapeDtypeStruct((t, ROUTER_COLS), F32)],
        compiler_params=_params("parallel"),
        name="post_router",
    )(pre, w, b, x, mod_l, norm_g2.reshape(1, d), *_split_bf16(w_router), b_router)


def _expert_kernel(be_ref, nu_ref, x_ref, wg_ref, wu_ref, wd_ref, *rest, first_block):
    o_ref, wgb_ref, wub_ref, wdb_ref = rest[-4:]
    i = pl.program_id(0)
    blk = first_block + i
    used = blk < nu_ref[0]
    new_expert = (i == 0) | (be_ref[blk] != be_ref[jnp.maximum(blk - 1, 0)])

    @pl.when(used & new_expert)
    def _():
        wgb_ref[...] = wg_ref[...].astype(BF16)
        wub_ref[...] = wu_ref[...].astype(BF16)
        wdb_ref[...] = wd_ref[...].astype(BF16)

    @pl.when(used)
    def _():
        x = x_ref[...]
        hg = _dot(x, wgb_ref[...])
        hu = _dot(x, wub_ref[...])
        act = hg * _sigmoid(hg) * hu
        o_ref[...] = _dot(act.astype(BF16), wdb_ref[...]).astype(BF16)

    @pl.when(jnp.logical_not(used))
    def _():
        o_ref[...] = jnp.zeros_like(o_ref)


def _experts(xs, block_e, n_used, wg, wu, wd, layer, ys_buf, first_block, total_blocks):
    p_rows, d = xs.shape
    n_blocks = p_rows // MOE_BLOCK

    def weight(shape):
        return pl.BlockSpec((None, None) + shape, lambda i, be, nu: (layer, be[first_block + i], 0, 0))

    in_specs = [pl.BlockSpec((MOE_BLOCK, d), lambda i, be, nu: (i, 0)),
                weight((d, EXPERT_DIM)), weight((d, EXPERT_DIM)), weight((EXPERT_DIM, d))]
    args = [block_e, n_used, xs, wg, wu, wd]
    aliases = {}
    if ys_buf is not None:
        in_specs.append(pl.BlockSpec(memory_space=pl.ANY))
        aliases = {len(args): 0}
        args.append(ys_buf)
    grid_spec = pltpu.PrefetchScalarGridSpec(
        num_scalar_prefetch=2,
        grid=(n_blocks,),
        in_specs=in_specs,
        out_specs=pl.BlockSpec((MOE_BLOCK, d), lambda i, be, nu: (first_block + i, 0)),
        scratch_shapes=[pltpu.VMEM((d, EXPERT_DIM), BF16), pltpu.VMEM((d, EXPERT_DIM), BF16),
                        pltpu.VMEM((EXPERT_DIM, d), BF16)],
    )
    return pl.pallas_call(
        functools.partial(_expert_kernel, first_block=first_block),
        grid_spec=grid_spec,
        out_shape=jax.ShapeDtypeStruct((total_blocks * MOE_BLOCK, d), BF16),
        input_output_aliases=aliases,
        compiler_params=_params("arbitrary"),
        name="moe_experts",
    )(*args)


def _combine_kernel(x_ref, y0_ref, y1_ref, route_ref, mod_ref, fg_ref, o_ref, *, final):
    route = route_ref[...]
    moe = route[:, 2:3] * y0_ref[...].astype(F32) + route[:, 3:4] * y1_ref[...].astype(F32)
    x = x_ref[...] + mod_ref[5:6, :] * moe
    if final:
        x = x * lax.rsqrt(jnp.mean(x * x, axis=-1, keepdims=True) + NORM_EPS) * fg_ref[...]
    o_ref[...] = x


def _combine(x_buf, y0, y1, route, mod_l, final_g, final, seq_len, first_tile):
    t, d = x_buf.shape
    tm = min(WIDE_TILE, seq_len)
    tiles_per_seq = seq_len // tm
    here = pl.BlockSpec((tm, d), lambda i: (first_tile + i, 0))
    local = pl.BlockSpec((tm, d), lambda i: (i, 0))
    return pl.pallas_call(
        functools.partial(_combine_kernel, final=final),
        grid=(y0.shape[0] // tm,),
        in_specs=[here, local, local, pl.BlockSpec((tm, ROUTER_COLS), lambda i: (first_tile + i, 0)),
                  pl.BlockSpec((None, 6, d), lambda i: ((first_tile + i) // tiles_per_seq, 0, 0)),
                  _resident((1, d))],
        out_specs=here,
        out_shape=jax.ShapeDtypeStruct((t, d), F32),
        input_output_aliases={0: 0},
        compiler_params=_params("parallel"),
        name="moe_combine",
    )(x_buf, y0, y1, route, mod_l, final_g.reshape(1, d))


def _rank_kernel(e_ref, rank_ref, cnt_ref, carry_ref):
    @pl.when(pl.program_id(0) == 0)
    def _():
        carry_ref[...] = jnp.zeros_like(carry_ref)

    expert_id = lax.broadcasted_iota(jnp.int32, (N_EXPERTS, RANK_TILE), 0)
    ri = lax.broadcasted_iota(jnp.int32, (RANK_TILE, RANK_TILE), 0)
    ci = lax.broadcasted_iota(jnp.int32, (RANK_TILE, RANK_TILE), 1)
    upper = (ri <= ci).astype(BF16)
    carry = carry_ref[...]
    for r in range(e_ref.shape[0]):
        e = e_ref[r:r + 1, :]
        onehot = jnp.where(e == expert_id, 1.0, 0.0)
        prefix = _dot(onehot.astype(BF16), upper)
        rank = jnp.sum(onehot * (prefix + carry), axis=0, keepdims=True) - 1.0
        rank_ref[r:r + 1, :] = rank.astype(jnp.int32)
        carry = carry + jnp.sum(onehot, axis=1, keepdims=True)
    carry_ref[...] = carry
    cnt_ref[...] = jnp.broadcast_to(carry, cnt_ref.shape).astype(jnp.int32)


def _rank(e):
    a = e.shape[0]
    n_tiles = a // RANK_TILE
    rows = _split_count(n_tiles, RANK_ROWS)
    tile = pl.BlockSpec((None, rows, RANK_TILE), lambda i: (i, 0, 0))
    rank, counts = pl.pallas_call(
        _rank_kernel,
        grid=(n_tiles // rows,),
        in_specs=[tile],
        out_specs=[tile, pl.BlockSpec((N_EXPERTS, LANES), lambda i: (0, 0))],
        out_shape=[jax.ShapeDtypeStruct((n_tiles // rows, rows, RANK_TILE), jnp.int32),
                   jax.ShapeDtypeStruct((N_EXPERTS, LANES), jnp.int32)],
        scratch_shapes=[pltpu.VMEM((N_EXPERTS, 1), F32)],
        compiler_params=_params("arbitrary"),
        name="moe_rank",
    )(e.reshape(n_tiles // rows, rows, RANK_TILE))
    return rank.reshape(a), counts[:, 0]


def _dispatch_plan(e, t):
    a = t * MOE_TOP_K
    rank, counts = _rank(e)
    padded = (counts + MOE_BLOCK - 1) // MOE_BLOCK * MOE_BLOCK
    end_pad = jnp.cumsum(padded)
    start_pad = end_pad - padded
    dest = (start_pad[e] + rank).astype(jnp.int32)
    n_blocks = (a + N_EXPERTS * (MOE_BLOCK - 1) + MOE_BLOCK - 1) // MOE_BLOCK
    token = jnp.tile(jnp.arange(t, dtype=jnp.int32), MOE_TOP_K)
    filler = jnp.arange(n_blocks * MOE_BLOCK, dtype=jnp.int32) % t
    row_tok = filler + jnp.zeros_like(filler).at[dest].add(token - dest % t, unique_indices=True,
                                                           mode="promise_in_bounds")
    block_start = jnp.arange(n_blocks, dtype=jnp.int32) * MOE_BLOCK
    block_e = jnp.sum((end_pad[None, :] <= block_start[:, None]).astype(jnp.int32), axis=1)
    block_e = jnp.minimum(block_e, N_EXPERTS - 1).astype(jnp.int32)
    n_used = (end_pad[-1] // MOE_BLOCK).astype(jnp.int32).reshape(1)
    return dest.reshape(MOE_TOP_K, t), row_tok, block_e, n_used


def _split_count(n, want):
    return max(k for k in range(1, want + 1) if n % k == 0)


def _moe(x1, h2, logits, mod_l, wg, wu, wd, layer, final_g, seq_len):
    t = x1.shape[0]
    route, route_t = _routing(logits)
    e = jnp.concatenate([route_t[:, 0, :].reshape(t), route_t[:, 1, :].reshape(t)]).astype(jnp.int32)
    dest, row_tok, block_e, n_used = _dispatch_plan(e, t)
    total_blocks = row_tok.shape[0] // MOE_BLOCK
    n_slices = _split_count(total_blocks, MOE_SLICES)
    per = total_blocks // n_slices
    ys = None
    for c in range(n_slices):
        rows = row_tok[c * per * MOE_BLOCK:(c + 1) * per * MOE_BLOCK]
        ys = _experts(h2[rows], block_e, n_used, wg, wu, wd, layer, ys, c * per, total_blocks)
    final = layer == wg.shape[0] - 1
    tm = min(WIDE_TILE, seq_len)
    n_tiles = x1.shape[0] // tm
    n_slices = _split_count(n_tiles, MOE_SLICES)
    per = n_tiles // n_slices
    x = x1
    for c in range(n_slices):
        d_c = dest[:, c * per * tm:(c + 1) * per * tm]
        x = _combine(x, ys[d_c[0]], ys[d_c[1]], route, mod_l, final_g, final, seq_len, c * per)
    return x


def _pad_cols(w, n):
    return jnp.pad(w, ((0, 0), (0, n - w.shape[1])))


def _pad_rows(w, n):
    return jnp.pad(w, ((0, n - w.shape[0]), (0, 0)))


def _rwkv_params(j, mu, wr, wk, wv, w0, w1, w2, a0, a1, a2, v0, v1, v2, g1, g2, kk, ka, rk,
                 lnw, lnb):
    d = D_MODEL
    p = {
        "mu": mu[j],
        "wr": wr[j].astype(BF16), "wk": wk[j].astype(BF16), "wv": wv[j].astype(BF16),
        "wl1": jnp.concatenate([w1[j, 0], w1[j, 1]], axis=1).astype(BF16),
        "wl2": _block_diag2(w2[j, 0], w2[j, 1]).astype(BF16),
        "al1": jnp.concatenate([a1[j, 0], a1[j, 1]], axis=1).astype(BF16),
        "al2": _block_diag2(a2[j, 0], a2[j, 1]).astype(BF16),
        "g1": g1[j].astype(BF16), "g2": g2[j].astype(BF16),
        "w0": w0[j].reshape(1, 2 * d), "a0": a0[j].reshape(1, 2 * d),
        "kk": kk[j].reshape(1, d), "ka": ka[j].reshape(1, d), "rk": rk[j].reshape(1, d),
        "lnw": lnw[j].reshape(1, d), "lnb": lnb[j].reshape(1, d),
    }
    if j > 0:
        p["v0"] = v0[j - 1].reshape(1, d)
        p["v1"] = _pad_cols(v1[j - 1], LANES).astype(BF16)
        p["v2"] = _pad_rows(v2[j - 1], LANES).astype(BF16)
    return p


def kernel(x_prompt, x_sample, c_prompt, c_sample, ada_w, ada_b, norm_g, final_g, rwkv_mu, rwkv_wr, rwkv_wk, rwkv_wv, rwkv_wo, rwkv_w0, rwkv_w1, rwkv_w2, rwkv_a0, rwkv_a1, rwkv_a2, rwkv_v0, rwkv_v1, rwkv_v2, rwkv_g1, rwkv_g2, rwkv_kk, rwkv_ka, rwkv_rk, rwkv_lnw, rwkv_lnb, sgu_win, sgu_bin, sgu_ng, sgu_nb, sgu_ws, sgu_bs, sgu_wout, sgu_bout, diff_wqkv, diff_wo, diff_lq1, diff_lk1, diff_lq2, diff_lk2, diff_subln, rel_bias, moe_wrg, moe_brg, moe_wre, moe_bre, moe_wg, moe_wu, moe_wd):
    d = D_MODEL
    nb_p, seq_len, _ = x_prompt.shape
    assert x_sample.shape[1] == seq_len
    t_p = nb_p * seq_len
    x = jnp.concatenate([x_prompt.reshape(-1, d), x_sample.reshape(-1, d)], axis=0)
    c = jnp.concatenate([c_prompt, c_sample], axis=0)
    mod = _modulation(c, ada_w, ada_b)
    zero_bias = jnp.zeros((1, d), F32)
    v_first = None
    for i in range(DEPTH):
        j = i // N_MIXERS
        mod_l = mod[i]
        if i % N_MIXERS == 0:
            p = _rwkv_params(j, rwkv_mu, rwkv_wr, rwkv_wk, rwkv_wv, rwkv_w0, rwkv_w1, rwkv_w2,
                             rwkv_a0, rwkv_a1, rwkv_a2, rwkv_v0, rwkv_v1, rwkv_v2, rwkv_g1, rwkv_g2,
                             rwkv_kk, rwkv_ka, rwkv_rk, rwkv_lnw, rwkv_lnb)
            r, k, v, g, kk, wlf, wlb, af, ab, bonus = _rwkv_pre(x, mod_l, norm_g[i, 0], p, v_first,
                                                                seq_len)
            if v_first is None:
                v_first = v
            y = _wkv(r, k, v, kk, wlf, wlb, af, ab, p["ka"], seq_len)
            pre = _rwkv_mid(y, v, g, bonus, p)
            w_out, b_out = rwkv_wo[j].astype(BF16), zero_bias
        elif i % N_MIXERS == 1:
            p = {"win": sgu_win[j].astype(BF16), "bin": sgu_bin[j].reshape(1, -1),
                 "ng": sgu_ng[j].reshape(1, -1), "nb": sgu_nb[j].reshape(1, -1),
                 "ws": sgu_ws[j].astype(BF16),
                 "bs": jnp.repeat(jnp.transpose(sgu_bs[j]), SGU_GC, axis=1)}
            pre = _sgu(x, mod_l, norm_g[i, 0], p, seq_len)
            w_out, b_out = sgu_wout[j].astype(BF16), sgu_bout[j].reshape(1, d)
        else:
            lambda_init = 0.8 - 0.6 * math.exp(-0.3 * i)
            lam = (jnp.exp(jnp.sum(diff_lq1[j] * diff_lk1[j])) - jnp.exp(jnp.sum(diff_lq2[j] * diff_lk2[j]))
                   + lambda_init)
            q, k, v = _qkv(x, mod_l, norm_g[i, 0], diff_wqkv[j].astype(BF16), seq_len)
            windows = _bias_windows(rel_bias, seq_len, min(ATTN_TQ, seq_len))
            pre = _diff_attention(q, k, v, windows, lam, diff_subln[j], lambda_init, seq_len)
            w_out, b_out = diff_wo[j].astype(BF16), zero_bias
        w_router = _pad_cols(jnp.concatenate([moe_wrg[i], moe_wre[i]], axis=1), ROUTER_COLS)
        b_router = _pad_cols(jnp.concatenate([moe_brg[i], moe_bre[i]])[None, :], ROUTER_COLS)
        x1, h2, logits = _post(pre, w_out, b_out, x, mod_l, norm_g[i, 1], w_router, b_router, seq_len)
        x = _moe(x1, h2, logits, mod_l, moe_wg, moe_wu, moe_wd, i, final_g, seq_len)
    return (x[:t_p].reshape(x_prompt.shape), x[t_p:].reshape(x_sample.shape))
```

```python
import functools
import math

import numpy as np
import jax
import jax.numpy as jnp
from jax import lax
from jax.experimental import pallas as pl
from jax.experimental.pallas import tpu as pltpu

F32 = jnp.float32
BF16 = jnp.bfloat16
HIGHEST = lax.Precision.HIGHEST

D_MODEL = 1024
DEPTH = 4
N_MIXERS = 3
NORM_EPS = 1e-6
LANES = 128
VMEM_LIMIT_BYTES = 56 * 1024 * 1024

RWKV_HEAD = 64
RWKV_HEADS = D_MODEL // RWKV_HEAD
RWKV_PAIRS = D_MODEL // LANES
RWKV_GN_EPS = 64e-5
WKV_CHUNK = 64
WKV_PAIRS_PER_STEP = 2
WKV_CHUNKS_PER_STEP = 4

SGU_CHUNK = 128
SGU_DIM = 2 * D_MODEL
SGU_GROUPS = 8
SGU_GC = SGU_DIM // SGU_GROUPS

DIFF_HEADS = 8
DIFF_HEAD = 64
NUM_BUCKETS = 32
MAX_DISTANCE = 128
ATTN_TQ = 512

MOE_GROUPS = 4
EXPERTS_PER_GROUP = 8
N_EXPERTS = MOE_GROUPS * EXPERTS_PER_GROUP
MOE_TOP_K = 2
EXPERT_DIM = D_MODEL // 2
MOE_BLOCK = 512
ROUTER_COLS = LANES
ROUTE_TILE = 2048
RANK_TILE = 512
RANK_ROWS = 8
MOE_SLICES = 4

TOKEN_TILE = 256
WIDE_TILE = 512


def _params(*semantics):
    return pltpu.CompilerParams(dimension_semantics=semantics, vmem_limit_bytes=VMEM_LIMIT_BYTES)


def _resident(shape):
    zeros = (0,) * len(shape)
    return pl.BlockSpec(shape, lambda *_: zeros, pipeline_mode=pl.Buffered(1))


def _dot(a, b, precision=None):
    return jnp.dot(a, b, preferred_element_type=F32, precision=precision)


def _dot_nt(a, b, precision=None):
    return lax.dot_general(a, b, (((1,), (1,)), ((), ())), preferred_element_type=F32,
                           precision=precision)


def _dot_tn(a, b, precision=None):
    return lax.dot_general(a, b, (((0,), (0,)), ((), ())), preferred_element_type=F32,
                           precision=precision)


def _split_bf16(x):
    hi = x.astype(BF16)
    return hi, (x - hi.astype(F32)).astype(BF16)


def _head_sums(x, hsum, hexp):
    hi, lo = _split_bf16(x)
    shi, slo = _split_bf16(_dot(hi, hsum) + _dot(lo, hsum))
    return _dot(shi, hexp) + _dot(slo, hexp)


def _sigmoid(x):
    return 1.0 / (1.0 + jnp.exp(-x))


def _norm_mod(x, g, scale, shift, eps=NORM_EPS):
    y = x * lax.rsqrt(jnp.mean(x * x, axis=-1, keepdims=True) + eps)
    return (y * g) * (1.0 + scale) + shift


def _mod_kernel(c_ref, w_ref, b_ref, o_ref):
    c = c_ref[...]
    o_ref[...] = _dot(c * _sigmoid(c), w_ref[...], HIGHEST) + b_ref[...]


def _modulation(c, ada_w, ada_b):
    nb = c.shape[0]
    depth, d, n = ada_w.shape
    out = pl.pallas_call(
        _mod_kernel,
        grid=(depth, n // d),
        in_specs=[pl.BlockSpec((nb, d), lambda l, j: (0, 0)),
                  pl.BlockSpec((None, d, d), lambda l, j: (l, 0, j)),
                  pl.BlockSpec((None, 1, d), lambda l, j: (l, 0, j))],
        out_specs=pl.BlockSpec((None, nb, d), lambda l, j: (l, 0, j)),
        out_shape=jax.ShapeDtypeStruct((depth, nb, n), F32),
        compiler_params=_params("parallel", "parallel"),
        name="adaln_modulation",
    )(c, ada_w, ada_b.reshape(depth, 1, n))
    return out.reshape(depth, nb, n // d, d)


def _rwkv_pre_kernel(*refs, tm, tiles_per_seq, has_vres):
    if has_vres:
        (x_ref, xp_ref, xn_ref, mod_ref, ng_ref, mu_ref, wr_ref, wk_ref, wv_ref, wl1_ref, wl2_ref,
         al1_ref, al2_ref, g1_ref, g2_ref, w0_ref, a0_ref, kscale_ref, ka_ref, rk_ref, hsum_ref,
         hexp_ref, vf_ref, v0_ref, v1_ref, v2_ref,
         r_ref, k_ref, v_ref, g_ref, kk_ref, wlf_ref, wlb_ref, af_ref, ab_ref, bonus_ref) = refs
    else:
        (x_ref, xp_ref, xn_ref, mod_ref, ng_ref, mu_ref, wr_ref, wk_ref, wv_ref, wl1_ref, wl2_ref,
         al1_ref, al2_ref, g1_ref, g2_ref, w0_ref, a0_ref, kscale_ref, ka_ref, rk_ref, hsum_ref,
         hexp_ref,
         r_ref, k_ref, v_ref, g_ref, kk_ref, wlf_ref, wlb_ref, af_ref, ab_ref, bonus_ref) = refs
    d = D_MODEL
    i = pl.program_id(0)
    pos = i % tiles_per_seq
    shift, scale = mod_ref[0:1, :], mod_ref[1:2, :]
    ng = ng_ref[...]
    h = _norm_mod(x_ref[...], ng, scale, shift)
    h_prev = jnp.where(pos == 0, 0.0, _norm_mod(xp_ref[7:8, :], ng, scale, shift))
    h_next = jnp.where(pos == tiles_per_seq - 1, 0.0, _norm_mod(xn_ref[0:1, :], ng, scale, shift))
    row = lax.broadcasted_iota(jnp.int32, (tm, 1), 0)
    h_up = jnp.where(row == 0, h_prev, pltpu.roll(h, 1, 0))
    h_dn = jnp.where(row == tm - 1, h_next, pltpu.roll(h, tm - 1, 0))
    xx = 0.5 * (h_up + h_dn) - h

    def mix(n):
        return (h + xx * mu_ref[n:n + 1, :]).astype(BF16)

    xr, xw, xk, xv, xa, xg = [mix(n) for n in range(6)]
    r = _dot(xr, wr_ref[...])
    r_ref[...] = r.astype(BF16)
    k = _dot(xk, wk_ref[...])
    k_ref[...] = k.astype(BF16)
    kkr = k * kscale_ref[...]
    norm = jnp.sqrt(_head_sums(kkr * kkr, hsum_ref[...], hexp_ref[...]))
    kk_ref[...] = (kkr / jnp.maximum(norm, 1e-12)).astype(BF16)
    v = _dot(xv, wv_ref[...])
    if has_vres:
        lv = _dot(_dot(xv, v1_ref[...]).astype(BF16), v2_ref[...])
        v = v + (vf_ref[...].astype(F32) - v) * _sigmoid(v0_ref[...] + lv)
    v_ref[...] = v.astype(BF16)
    g_ref[...] = _dot(_sigmoid(_dot(xg, g1_ref[...])).astype(BF16), g2_ref[...]).astype(BF16)
    wl = w0_ref[...] + _dot(jnp.tanh(_dot(xw, wl1_ref[...])).astype(BF16), wl2_ref[...])
    logw = -math.exp(-0.5) * _sigmoid(wl)
    wlf_ref[...] = logw[:, :d]
    wlb_ref[...] = logw[:, d:]
    a = _sigmoid(a0_ref[...] + _dot(_dot(xa, al1_ref[...]).astype(BF16), al2_ref[...]))
    af_ref[...] = a[:, :d].astype(BF16)
    ab_ref[...] = a[:, d:].astype(BF16)
    k_bonus = k * (1.0 + (0.5 * (a[:, :d] + a[:, d:]) - 1.0) * ka_ref[...])
    hi, lo = _split_bf16(r * k_bonus * rk_ref[...])
    bonus_ref[...] = _dot(hi, hsum_ref[...]) + _dot(lo, hsum_ref[...])


def _head_sum_matrix():
    head_of_lane = np.arange(D_MODEL) // RWKV_HEAD
    return jnp.asarray(head_of_lane[:, None] == np.arange(RWKV_HEADS)[None, :], BF16)


def _block_diag2(m0, m1):
    z = jnp.zeros_like(m0)
    return jnp.concatenate([jnp.concatenate([m0, z], 1), jnp.concatenate([z, m1], 1)], 0)


def _rwkv_pre(x, mod_l, norm_g, p, v_first, seq_len):
    t, d = x.shape
    tm = min(WIDE_TILE, seq_len)
    tiles_per_seq = seq_len // tm
    n_tiles = t // tm
    has_vres = v_first is not None
    tile = pl.BlockSpec((tm, d), lambda i: (i, 0))
    rows8 = tm // 8
    last8 = t // 8 - 1
    in_specs = [
        tile,
        pl.BlockSpec((8, d), lambda i: (jnp.maximum(i * rows8 - 1, 0), 0)),
        pl.BlockSpec((8, d), lambda i: (jnp.minimum((i + 1) * rows8, last8), 0)),
        pl.BlockSpec((None, 6, d), lambda i: (i // tiles_per_seq, 0, 0)),
        _resident((1, d)), _resident((6, d)),
        _resident((d, d)), _resident((d, d)), _resident((d, d)),
        _resident((d, LANES)), _resident((LANES, 2 * d)),
        _resident((d, LANES)), _resident((LANES, 2 * d)),
        _resident((d, LANES)), _resident((LANES, d)),
        _resident((1, 2 * d)), _resident((1, 2 * d)),
        _resident((1, d)), _resident((1, d)), _resident((1, d)),
        _resident((d, RWKV_HEADS)), _resident((RWKV_HEADS, d)),
    ]
    hsum = _head_sum_matrix()
    args = [x, x, x, mod_l, norm_g.reshape(1, d), p["mu"],
            p["wr"], p["wk"], p["wv"], p["wl1"], p["wl2"], p["al1"], p["al2"], p["g1"], p["g2"],
            p["w0"], p["a0"], p["kk"], p["ka"], p["rk"], hsum, hsum.T]
    if has_vres:
        in_specs += [tile, _resident((1, d)), _resident((d, LANES)), _resident((LANES, d))]
        args += [v_first, p["v0"], p["v1"], p["v2"]]
    dtypes = [BF16] * 5 + [F32] * 2 + [BF16] * 2
    outs = pl.pallas_call(
        functools.partial(_rwkv_pre_kernel, tm=tm, tiles_per_seq=tiles_per_seq, has_vres=has_vres),
        grid=(n_tiles,),
        in_specs=in_specs,
        out_specs=[tile] * 9 + [pl.BlockSpec((tm, RWKV_HEADS), lambda i: (i, 0))],
        out_shape=[jax.ShapeDtypeStruct((t, d), dt) for dt in dtypes]
        + [jax.ShapeDtypeStruct((t, RWKV_HEADS), F32)],
        compiler_params=_params("parallel"),
        name="rwkv_pre",
    )(*args)
    return outs


def _wkv_kernel(r_ref, k_ref, v_ref, kk_ref, wlf_ref, wlb_ref, af_ref, ab_ref, ka_ref,
                lvl_ref, y_ref, state_ref, g_ref, sp_ref, yp_ref, xr_ref, dec_ref,
                *, seq_len, chunk, pairs, group):
    n_chunks = seq_len // chunk
    n_levels = int(math.log2(chunk))
    two = 2 * chunk
    lane = lax.broadcasted_iota(jnp.int32, (1, LANES), 1)
    head0 = lane < RWKV_HEAD
    trow = lax.broadcasted_iota(jnp.int32, (chunk, 1), 0)
    si = lax.broadcasted_iota(jnp.int32, (two, two), 0)
    sj = lax.broadcasted_iota(jnp.int32, (two, two), 1)
    same_head = (si < chunk) == (sj < chunk)
    st, su = si & (chunk - 1), sj & (chunk - 1)
    eye = (si == sj).astype(F32)
    chain_defs = [(pair, reverse) for pair in range(pairs) for reverse in (False, True)]

    def rows_of(idx, reverse):
        pos = (n_chunks - 1 - idx) if reverse else idx
        return pl.ds(pl.multiple_of(pos * chunk, chunk), chunk)

    def stack(x):
        return jnp.concatenate([jnp.where(head0, x, 0.0), jnp.where(head0, 0.0, x)], axis=0)

    def a_load(q, g, idx, slot):
        pair, reverse = chain_defs[q]
        rows = rows_of(idx, reverse)
        lanes = slice(pair * LANES, (pair + 1) * LANES)
        a = dict(q=q, g=g, slot=slot, rows=rows, lanes=lanes, reverse=reverse)
        a["lw"] = lw = (wlb_ref if reverse else wlf_ref)[rows, lanes]
        a["a"] = (ab_ref if reverse else af_ref)[rows, lanes].astype(F32)
        logp = lw
        step = 1
        while step < chunk:
            if reverse:
                logp = logp + jnp.where(trow < chunk - step, pltpu.roll(logp, chunk - step, 0), 0.0)
            else:
                logp = logp + jnp.where(trow >= step, pltpu.roll(logp, step, 0), 0.0)
            step *= 2
        a["logp"] = logp
        return a

    def a_scale(a):
        rows, lanes, logp, lw, av = a["rows"], a["lanes"], a["logp"], a["lw"], a["a"]
        k = k_ref[rows, lanes].astype(F32)
        kk = kk_ref[rows, lanes].astype(F32)
        kd = k * (1.0 + (av - 1.0) * ka_ref[:, lanes])
        p_in = jnp.exp(logp)
        p_out = jnp.exp(-logp)
        a["xa"] = stack(-kk * jnp.exp(logp - lw)).astype(BF16)
        a["xr"] = stack(r_ref[rows, lanes].astype(F32) * p_in).astype(BF16)
        a["yb"] = stack(kk * av * p_out).astype(BF16)
        a["yk"] = stack(kd * p_out).astype(BF16)
        a["vs"] = stack(v_ref[rows, lanes])
        a["decay"] = jnp.exp(jnp.sum(lw, axis=0, keepdims=True))

    def a_scores(a):
        if a["reverse"]:
            strict, incl = same_head & (st < su), same_head & (st <= su)
        else:
            strict, incl = same_head & (st > su), same_head & (st >= su)
        sc = _dot_nt(jnp.concatenate([a["xa"], a["xr"]], 0),
                     jnp.concatenate([a["yb"], a["yk"]], 0))
        a["m"] = sc[:two, :two].astype(BF16)
        m_ak = jnp.where(strict, sc[:two, two:], 0.0).astype(BF16)
        n_rk = jnp.where(incl, sc[two:, two:], 0.0).astype(BF16)
        a["n_rb"] = jnp.where(incl, sc[two:, :two], 0.0).astype(BF16)
        a["inv"] = eye + (a["m"] * lvl_ref[int(a["reverse"]), 0]).astype(F32)
        a["zp"] = _dot(m_ak, a["vs"]).astype(BF16)
        a["yp"] = _dot(n_rk, a["vs"])
        a["sp"] = _dot_tn(a["vs"], a["yk"])

    def a_double(a, level):
        t = a["inv"].astype(BF16)
        m_n = a["m"] * lvl_ref[int(a["reverse"]), level]
        a["inv"] = a["inv"] + _dot(_dot(t, m_n).astype(BF16), t)

    def a_solve(a):
        t = a["inv"].astype(BF16)
        a["w"] = _dot(t, a["xa"]).astype(BF16)
        a["u0"] = _dot(t, a["zp"]).astype(BF16)

    def a_fold(a):
        at = (a["slot"], a["g"], a["q"])
        g_ref[at] = _dot_tn(a["w"], a["yb"]).astype(BF16)
        sp_ref[at] = a["sp"] + _dot_tn(a["u0"], a["yb"])
        yp_ref[at] = a["yp"] + _dot(a["n_rb"], a["u0"])
        xr_ref[at] = (a["xr"].astype(F32) + _dot(a["n_rb"], a["w"])).astype(BF16)
        dec_ref[at] = a["decay"]

    a_stage_list = ([a_scale, a_scores]
                    + [functools.partial(a_double, level=level) for level in range(1, n_levels)]
                    + [a_solve, a_fold])

    def b_state(b):
        at = b["at"]
        s0 = state_ref[b["q"]]
        b["s0b"] = s0.astype(BF16)
        state_ref[b["q"]] = (s0 + _dot(b["s0b"], g_ref[at]) + sp_ref[at]) * dec_ref[at]

    def b_out(b):
        at = b["at"]
        pair, reverse = chain_defs[b["q"]]
        ys = _dot_nt(xr_ref[at], b["s0b"]) + yp_ref[at]
        y_ref[rows_of(b["idx"], reverse), pair * LANES:(pair + 1) * LANES] += ys[:chunk] + ys[chunk:]

    b_stage_list = [b_state, b_out]

    y_ref[...] = jnp.zeros_like(y_ref)
    state_ref[...] = jnp.zeros_like(state_ref)
    n_chains = len(chain_defs)
    first = [a_load(q, g, g, 0) for g in range(group) for q in range(n_chains)]
    for stage in a_stage_list:
        for a in first:
            stage(a)

    def run_step(step, prepare_next):
        slot = step & 1
        base = step * group
        b_work = []
        for g in range(group):
            now = [dict(q=q, at=(slot, g, q), idx=base + g) for q in range(n_chains)]
            b_work += [(stage, now) for stage in b_stage_list]
        a_work = []
        if prepare_next:
            ahead = [a_load(q, g, base + group + g, 1 - slot)
                     for g in range(group) for q in range(n_chains)]
            a_work = [(stage, ahead) for stage in a_stage_list]
        per_a = -(-len(b_work) // max(len(a_work), 1))
        while a_work or b_work:
            for stage, items in b_work[:per_a]:
                for b in items:
                    stage(b)
            b_work = b_work[per_a:]
            if a_work:
                stage, items = a_work.pop(0)
                for a in items:
                    stage(a)

    n_steps = n_chunks // group

    def body(step, carry):
        run_step(step, True)
        return carry

    lax.fori_loop(0, n_steps - 1, body, 0)
    run_step(n_steps - 1, False)


def _wkv_level_masks(chunk):
    s = np.arange(2 * chunk)
    head, t = s // chunk, s % chunk
    n_levels = int(math.log2(chunk))
    masks = np.zeros((2, n_levels, 2 * chunk, 2 * chunk), np.float32)
    for d in range(2):
        pos = t if d == 0 else chunk - 1 - t
        for j in range(n_levels):
            blk = pos >> j
            masks[d, j] = ((head[:, None] == head[None, :]) & (blk[:, None] % 2 == 1)
                           & (blk[None, :] == blk[:, None] - 1))
    return masks


def _wkv(r, k, v, kk, wlf, wlb, af, ab, k_a, seq_len):
    t, d = r.shape
    nb = t // seq_len
    pairs = WKV_PAIRS_PER_STEP
    width = pairs * LANES
    n_chains = 2 * pairs
    seq = pl.BlockSpec((seq_len, width), lambda b, p: (b, p))
    par = pl.BlockSpec((1, width), lambda b, p: (0, p))
    levels = jnp.asarray(_wkv_level_masks(WKV_CHUNK), BF16)
    group = WKV_CHUNKS_PER_STEP
    parked_f32 = pltpu.VMEM((2, group, n_chains, LANES, LANES), F32)
    parked_bf16 = pltpu.VMEM((2, group, n_chains, LANES, LANES), BF16)
    return pl.pallas_call(
        functools.partial(_wkv_kernel, seq_len=seq_len, chunk=WKV_CHUNK, pairs=pairs, group=group),
        grid=(nb, d // width),
        in_specs=[seq] * 8 + [par, _resident(levels.shape)],
        out_specs=seq,
        out_shape=jax.ShapeDtypeStruct((t, d), F32),
        scratch_shapes=[pltpu.VMEM((n_chains, LANES, LANES), F32), parked_bf16, parked_f32,
                        parked_f32, parked_bf16, pltpu.VMEM((2, group, n_chains, 1, LANES), F32)],
        compiler_params=_params("parallel", "parallel"),
        name="wkv7_chunked",
    )(r, k, v, kk, wlf, wlb, af, ab, k_a.reshape(1, d), levels)


def _rwkv_mid_kernel(y_ref, v_ref, g_ref, bonus_ref, lnw_ref, lnb_ref, hsum_ref, hexp_ref, o_ref):
    hsum = hsum_ref[...]
    hexp = hexp_ref[...]

    def head_sum(x):
        return _head_sums(x, hsum, hexp)

    y = y_ref[...]
    yc = y - head_sum(y) * (1.0 / RWKV_HEAD)
    var = head_sum(yc * yc) * (1.0 / RWKV_HEAD)
    yn = yc * lax.rsqrt(var + RWKV_GN_EPS) * lnw_ref[...] + lnb_ref[...]
    b_hi, b_lo = _split_bf16(bonus_ref[...])
    bonus = (_dot(b_hi, hexp) + _dot(b_lo, hexp)) * v_ref[...].astype(F32)
    o_ref[...] = ((yn + bonus) * g_ref[...].astype(F32)).astype(BF16)


def _rwkv_mid(y, v, g, bonus, p):
    t, d = y.shape
    tm = TOKEN_TILE
    tile = pl.BlockSpec((tm, d), lambda i: (i, 0))
    vec = _resident((1, d))
    hsum = _head_sum_matrix()
    return pl.pallas_call(
        _rwkv_mid_kernel,
        grid=(t // tm,),
        in_specs=[tile] * 3 + [pl.BlockSpec((tm, RWKV_HEADS), lambda i: (i, 0))] + [vec] * 2
        + [_resident((d, RWKV_HEADS)), _resident((RWKV_HEADS, d))],
        out_specs=tile,
        out_shape=jax.ShapeDtypeStruct((t, d), BF16),
        compiler_params=_params("parallel"),
        name="rwkv_mid",
    )(y, v, g, bonus, p["lnw"], p["lnb"], hsum, hsum.T)


def _sgu_kernel(x_ref, mod_ref, ng_ref, win_ref, bin_ref, lg_ref, lb_ref, ws_ref, bs_ref, o_ref,
                *, tm):
    h = _norm_mod(x_ref[...], ng_ref[...], mod_ref[1:2, :], mod_ref[0:1, :]).astype(BF16)
    z = _dot(h, win_ref[...]) + bin_ref[...]
    z = 0.5 * z * (1.0 + lax.erf(z * (1.0 / math.sqrt(2.0))))
    u = z[:, :SGU_DIM]
    v = z[:, SGU_DIM:]
    vc = v - jnp.mean(v, axis=-1, keepdims=True)
    vn = vc * lax.rsqrt(jnp.mean(vc * vc, axis=-1, keepdims=True) + 1e-5) * lg_ref[...] + lb_ref[...]
    vb = vn.astype(BF16)
    for c in range(tm // SGU_CHUNK):
        rows = slice(c * SGU_CHUNK, (c + 1) * SGU_CHUNK)
        for g in range(SGU_GROUPS):
            cols = slice(g * SGU_GC, (g + 1) * SGU_GC)
            s = _dot(ws_ref[g], vb[rows, cols]) + bs_ref[:, cols]
            o_ref[rows, cols] = (u[rows, cols] * s).astype(BF16)


def _sgu(x, mod_l, norm_g, p, seq_len):
    t, d = x.shape
    tm = min(WIDE_TILE, seq_len)
    tiles_per_seq = seq_len // tm
    return pl.pallas_call(
        functools.partial(_sgu_kernel, tm=tm),
        grid=(t // tm,),
        in_specs=[pl.BlockSpec((tm, d), lambda i: (i, 0)),
                  pl.BlockSpec((None, 6, d), lambda i: (i // tiles_per_seq, 0, 0)),
                  _resident((1, d)),
                  _resident((d, 2 * SGU_DIM)), _resident((1, 2 * SGU_DIM)),
                  _resident((1, SGU_DIM)), _resident((1, SGU_DIM)),
                  _resident((SGU_GROUPS, SGU_CHUNK, SGU_CHUNK)), _resident((SGU_CHUNK, SGU_DIM))],
        out_specs=pl.BlockSpec((tm, SGU_DIM), lambda i: (i, 0)),
        out_shape=jax.ShapeDtypeStruct((t, SGU_DIM), BF16),
        compiler_params=_params("parallel"),
        name="sgu",
    )(x, mod_l, norm_g.reshape(1, d), p["win"], p["bin"], p["ng"], p["nb"], p["ws"], p["bs"])


def _qkv_kernel(x_ref, mod_ref, ng_ref, w_ref, q_ref, k_ref, v_ref):
    d = D_MODEL
    h = _norm_mod(x_ref[...], ng_ref[...], mod_ref[1:2, :], mod_ref[0:1, :]).astype(BF16)
    qkv = _dot(h, w_ref[...])
    q_ref[...] = (qkv[:, :d] * (DIFF_HEAD ** -0.5)).astype(BF16)
    k_ref[...] = qkv[:, d:2 * d].astype(BF16)
    v_ref[...] = qkv[:, 2 * d:].astype(BF16)


def _qkv(x, mod_l, norm_g, w_qkv, seq_len):
    t, d = x.shape
    tm = TOKEN_TILE
    tiles_per_seq = seq_len // tm
    tile = pl.BlockSpec((tm, d), lambda i: (i, 0))
    return pl.pallas_call(
        _qkv_kernel,
        grid=(t // tm,),
        in_specs=[tile, pl.BlockSpec((None, 6, d), lambda i: (i // tiles_per_seq, 0, 0)),
                  _resident((1, d)), _resident((d, 3 * d))],
        out_specs=[tile] * 3,
        out_shape=[jax.ShapeDtypeStruct((t, d), BF16)] * 3,
        compiler_params=_params("parallel"),
        name="diff_qkv",
    )(x, mod_l, norm_g.reshape(1, d), w_qkv)


def _attn_kernel(q_ref, k_ref, v_ref, win_ref, lam_ref, sg_ref, o_ref, vext_ref, *, tq, seq_len,
                 out_scale):
    @pl.when(pl.program_id(2) == 0)
    def _():
        col = lax.broadcasted_iota(jnp.int32, (seq_len, LANES), 1)
        vext_ref[:, :LANES] = v_ref[...]
        vext_ref[:, LANES:] = jnp.where(col == 0, 1.0, 0.0).astype(BF16)

    q = q_ref[...]
    lane = lax.broadcasted_iota(jnp.int32, (1, LANES), 1)
    zero = jnp.zeros_like(q)
    qs = jnp.concatenate([jnp.where(lane < DIFF_HEAD, q, zero),
                          jnp.where(lane < DIFF_HEAD, zero, q)], axis=0)
    s = _dot_nt(qs, k_ref[...])
    width = seq_len + tq
    window = jnp.broadcast_to(win_ref[...], (tq, width))
    bias = pltpu.roll(window, width - tq + 1, 1, stride=1, stride_axis=0)[:, :seq_len]

    def unnormalised(x):
        e = jnp.exp(x - jnp.max(x, axis=-1, keepdims=True)).astype(BF16)
        pv = _dot(e, vext_ref[...])
        return pv[:, :LANES], pv[:, LANES:LANES + 1]

    o1, l1 = unnormalised(s[:tq] + bias)
    o2, l2 = unnormalised(s[tq:] + bias)
    o = o1 / l1 - lam_ref[...] * (o2 / l2)
    o = o * lax.rsqrt(jnp.mean(o * o, axis=-1, keepdims=True) + 1e-5) * sg_ref[...]
    o_ref[...] = (o * out_scale).astype(BF16)


def _t5_bucket(rel):
    nb = NUM_BUCKETS // 2
    max_exact = nb // 2
    ret = jnp.where(rel > 0, nb, 0)
    n = jnp.abs(rel)
    nf = jnp.maximum(n, 1).astype(F32)
    large = max_exact + (jnp.log(nf / max_exact) / math.log(MAX_DISTANCE / max_exact)
                         * (nb - max_exact)).astype(jnp.int32)
    large = jnp.minimum(large, nb - 1)
    return ret + jnp.where(n < max_exact, n, large)


def _bias_windows(rel_bias, seq_len, tq):
    nqb = seq_len // tq
    j = jnp.arange(seq_len + tq, dtype=jnp.int32)[None, :]
    q_hi = (jnp.arange(nqb, dtype=jnp.int32)[:, None] + 1) * tq
    bucket = _t5_bucket(j - q_hi + 1)
    return jnp.transpose(rel_bias[bucket], (0, 2, 1))[:, :, None, :].astype(F32)


def _diff_attention(q, k, v, windows, lam, subln_g, lambda_init, seq_len):
    t, d = q.shape
    nb = t // seq_len
    tq = min(ATTN_TQ, seq_len)
    nqb = seq_len // tq
    kv = pl.BlockSpec((seq_len, LANES), lambda b, h, i: (b, h))
    qo = pl.BlockSpec((tq, LANES), lambda b, h, i: (b * nqb + i, h))
    return pl.pallas_call(
        functools.partial(_attn_kernel, tq=tq, seq_len=seq_len, out_scale=1.0 - lambda_init),
        grid=(nb, DIFF_HEADS, nqb),
        in_specs=[qo, kv, kv,
                  pl.BlockSpec((None, None, 1, seq_len + tq), lambda b, h, i: (i, h, 0, 0)),
                  pl.BlockSpec((1, 1), lambda b, h, i: (0, 0)),
                  pl.BlockSpec((1, LANES), lambda b, h, i: (0, 0))],
        out_specs=qo,
        out_shape=jax.ShapeDtypeStruct((t, d), BF16),
        scratch_shapes=[pltpu.VMEM((seq_len, 2 * LANES), BF16)],
        compiler_params=_params("parallel", "parallel", "arbitrary"),
        name="diff_attention",
    )(q, k, v, windows, lam.reshape(1, 1), subln_g.reshape(1, LANES))


def _post_kernel(pre_ref, w_ref, b_ref, x_ref, mod_ref, ng_ref, wrh_ref, wrl_ref, br_ref,
                 x1_ref, h2_ref, lg_ref):
    out = _dot(pre_ref[...], w_ref[...]) + b_ref[...]
    x1 = x_ref[...] + mod_ref[2:3, :] * out
    x1_ref[...] = x1
    h2 = _norm_mod(x1, ng_ref[...], mod_ref[4:5, :], mod_ref[3:4, :])
    h2_hi, h2_lo = _split_bf16(h2)
    h2_ref[...] = h2_hi
    lg_ref[...] = (_dot(h2_hi, wrh_ref[...]) + (_dot(h2_hi, wrl_ref[...]) + _dot(h2_lo, wrh_ref[...]))
                   + br_ref[...])


def _route(lg):
    lane = lax.broadcasted_iota(jnp.int32, lg.shape, 1)
    first_e, end_e = MOE_GROUPS, MOE_GROUPS + N_EXPERTS
    neg = -jnp.inf

    def max_and_first(x):
        m = jnp.max(x, axis=-1, keepdims=True)
        return m, jnp.min(jnp.where(x == m, lane, LANES), axis=-1, keepdims=True)

    is_group = lane < first_e
    g_max, grp = max_and_first(jnp.where(is_group, lg, neg))
    p_grp = 1.0 / jnp.sum(jnp.where(is_group, jnp.exp(lg - g_max), 0.0), axis=-1, keepdims=True)
    lane_grp = (lane - first_e) >> int(math.log2(EXPERTS_PER_GROUP))
    in_grp = (lane >= first_e) & (lane < end_e) & (lane_grp == grp)
    cand = jnp.where(in_grp, lg, neg)
    v1, i1 = max_and_first(cand)
    v2, i2 = max_and_first(jnp.where(lane == i1, neg, cand))
    p2 = jnp.exp(v2 - v1)
    gate1 = p_grp / (1.0 + p2)
    gate2 = gate1 * p2
    e1 = (i1 - first_e).astype(F32)
    e2 = (i2 - first_e).astype(F32)
    return jnp.where(lane == 0, e1, jnp.where(lane == 1, e2, jnp.where(lane == 2, gate1,
                                                                    jnp.where(lane == 3, gate2, 0.0))))


def _route_kernel(lg_ref, route_ref, route_t_ref):
    route = _route(lg_ref[...])
    route_ref[...] = route
    route_t_ref[...] = route.T[:8, :]


def _routing(logits):
    t = logits.shape[0]
    tr = max(k for k in range(LANES, ROUTE_TILE + 1, LANES) if t % k == 0)
    return pl.pallas_call(
        _route_kernel,
        grid=(t // tr,),
        in_specs=[pl.BlockSpec((tr, ROUTER_COLS), lambda i: (i, 0))],
        out_specs=[pl.BlockSpec((tr, ROUTER_COLS), lambda i: (i, 0)),
                   pl.BlockSpec((None, 8, tr), lambda i: (i, 0, 0))],
        out_shape=[jax.ShapeDtypeStruct((t, ROUTER_COLS), F32),
                   jax.ShapeDtypeStruct((t // tr, 8, tr), F32)],
        compiler_params=_params("parallel"),
        name="moe_route",
    )(logits)


def _post(pre, w, b, x, mod_l, norm_g2, w_router, b_router, seq_len):
    t, d = x.shape
    din = pre.shape[1]
    tm = min(WIDE_TILE, seq_len)
    tiles_per_seq = seq_len // tm
    tile = pl.BlockSpec((tm, d), lambda i: (i, 0))
    return pl.pallas_call(
        _post_kernel,
        grid=(t // tm,),
        in_specs=[pl.BlockSpec((tm, din), lambda i: (i, 0)), _resident((din, d)), _resident((1, d)),
                  tile, pl.BlockSpec((None, 6, d), lambda i: (i // tiles_per_seq, 0, 0)),
                  _resident((1, d)), _resident((d, ROUTER_COLS)), _resident((d, ROUTER_COLS)),
                  _resident((1, ROUTER_COLS))],
        out_specs=[tile, tile, pl.BlockSpec((tm, ROUTER_COLS), lambda i: (i, 0))],
        out_shape=[jax.ShapeDtypeStruct((t, d), F32), jax.ShapeDtypeStruct((t, d), BF16),
                   jax.ShapeDtypeStruct((t, ROUTER_COLS), F32)],
        compiler_params=_params("parallel"),
        name="post_router",
    )(pre, w, b, x, mod_l, norm_g2.reshape(1, d), *_split_bf16(w_router), b_router)


def _expert_kernel(be_ref, nu_ref, x_ref, wg_ref, wu_ref, wd_ref, *rest, first_block):
    o_ref, wgb_ref, wub_ref, wdb_ref = rest[-4:]
    i = pl.program_id(0)
    blk = first_block + i
    used = blk < nu_ref[0]
    new_expert = (i == 0) | (be_ref[blk] != be_ref[jnp.maximum(blk - 1, 0)])

    @pl.when(used & new_expert)
    def _():
        wgb_ref[...] = wg_ref[...].astype(BF16)
        wub_ref[...] = wu_ref[...].astype(BF16)
        wdb_ref[...] = wd_ref[...].astype(BF16)

    @pl.when(used)
    def _():
        x = x_ref[...]
        hg = _dot(x, wgb_ref[...])
        hu = _dot(x, wub_ref[...])
        act = hg * _sigmoid(hg) * hu
        o_ref[...] = _dot(act.astype(BF16), wdb_ref[...]).astype(BF16)

    @pl.when(jnp.logical_not(used))
    def _():
        o_ref[...] = jnp.zeros_like(o_ref)


def _experts(xs, block_e, n_used, wg, wu, wd, layer, ys_buf, first_block, total_blocks):
    p_rows, d = xs.shape
    n_blocks = p_rows // MOE_BLOCK

    def weight(shape):
        return pl.BlockSpec((None, None) + shape, lambda i, be, nu: (layer, be[first_block + i], 0, 0))

    in_specs = [pl.BlockSpec((MOE_BLOCK, d), lambda i, be, nu: (i, 0)),
                weight((d, EXPERT_DIM)), weight((d, EXPERT_DIM)), weight((EXPERT_DIM, d))]
    args = [block_e, n_used, xs, wg, wu, wd]
    aliases = {}
    if ys_buf is not None:
        in_specs.append(pl.BlockSpec(memory_space=pl.ANY))
        aliases = {len(args): 0}
        args.append(ys_buf)
    grid_spec = pltpu.PrefetchScalarGridSpec(
        num_scalar_prefetch=2,
        grid=(n_blocks,),
        in_specs=in_specs,
        out_specs=pl.BlockSpec((MOE_BLOCK, d), lambda i, be, nu: (first_block + i, 0)),
        scratch_shapes=[pltpu.VMEM((d, EXPERT_DIM), BF16), pltpu.VMEM((d, EXPERT_DIM), BF16),
                        pltpu.VMEM((EXPERT_DIM, d), BF16)],
    )
    return pl.pallas_call(
        functools.partial(_expert_kernel, first_block=first_block),
        grid_spec=grid_spec,
        out_shape=jax.ShapeDtypeStruct((total_blocks * MOE_BLOCK, d), BF16),
        input_output_aliases=aliases,
        compiler_params=_params("arbitrary"),
        name="moe_experts",
    )(*args)


def _combine_kernel(x_ref, y0_ref, y1_ref, route_ref, mod_ref, fg_ref, o_ref, *, final):
    route = route_ref[...]
    moe = route[:, 2:3] * y0_ref[...].astype(F32) + route[:, 3:4] * y1_ref[...].astype(F32)
    x = x_ref[...] + mod_ref[5:6, :] * moe
    if final:
        x = x * lax.rsqrt(jnp.mean(x * x, axis=-1, keepdims=True) + NORM_EPS) * fg_ref[...]
    o_ref[...] = x


def _combine(x_buf, y0, y1, route, mod_l, final_g, final, seq_len, first_tile):
    t, d = x_buf.shape
    tm = min(WIDE_TILE, seq_len)
    tiles_per_seq = seq_len // tm
    here = pl.BlockSpec((tm, d), lambda i: (first_tile + i, 0))
    local = pl.BlockSpec((tm, d), lambda i: (i, 0))
    return pl.pallas_call(
        functools.partial(_combine_kernel, final=final),
        grid=(y0.shape[0] // tm,),
        in_specs=[here, local, local, pl.BlockSpec((tm, ROUTER_COLS), lambda i: (first_tile + i, 0)),
                  pl.BlockSpec((None, 6, d), lambda i: ((first_tile + i) // tiles_per_seq, 0, 0)),
                  _resident((1, d))],
        out_specs=here,
        out_shape=jax.ShapeDtypeStruct((t, d), F32),
        input_output_aliases={0: 0},
        compiler_params=_params("parallel"),
        name="moe_combine",
    )(x_buf, y0, y1, route, mod_l, final_g.reshape(1, d))


def _rank_kernel(e_ref, rank_ref, cnt_ref, carry_ref):
    @pl.when(pl.program_id(0) == 0)
    def _():
        carry_ref[...] = jnp.zeros_like(carry_ref)

    expert_id = lax.broadcasted_iota(jnp.int32, (N_EXPERTS, RANK_TILE), 0)
    ri = lax.broadcasted_iota(jnp.int32, (RANK_TILE, RANK_TILE), 0)
    ci = lax.broadcasted_iota(jnp.int32, (RANK_TILE, RANK_TILE), 1)
    upper = (ri <= ci).astype(BF16)
    carry = carry_ref[...]
    for r in range(e_ref.shape[0]):
        e = e_ref[r:r + 1, :]
        onehot = jnp.where(e == expert_id, 1.0, 0.0)
        prefix = _dot(onehot.astype(BF16), upper)
        rank = jnp.sum(onehot * (prefix + carry), axis=0, keepdims=True) - 1.0
        rank_ref[r:r + 1, :] = rank.astype(jnp.int32)
        carry = carry + jnp.sum(onehot, axis=1, keepdims=True)
    carry_ref[...] = carry
    cnt_ref[...] = jnp.broadcast_to(carry, cnt_ref.shape).astype(jnp.int32)


def _rank(e):
    a = e.shape[0]
    n_tiles = a // RANK_TILE
    rows = _split_count(n_tiles, RANK_ROWS)
    tile = pl.BlockSpec((None, rows, RANK_TILE), lambda i: (i, 0, 0))
    rank, counts = pl.pallas_call(
        _rank_kernel,
        grid=(n_tiles // rows,),
        in_specs=[tile],
        out_specs=[tile, pl.BlockSpec((N_EXPERTS, LANES), lambda i: (0, 0))],
        out_shape=[jax.ShapeDtypeStruct((n_tiles // rows, rows, RANK_TILE), jnp.int32),
                   jax.ShapeDtypeStruct((N_EXPERTS, LANES), jnp.int32)],
        scratch_shapes=[pltpu.VMEM((N_EXPERTS, 1), F32)],
        compiler_params=_params("arbitrary"),
        name="moe_rank",
    )(e.reshape(n_tiles // rows, rows, RANK_TILE))
    return rank.reshape(a), counts[:, 0]


def _dispatch_plan(e, t):
    a = t * MOE_TOP_K
    rank, counts = _rank(e)
    padded = (counts + MOE_BLOCK - 1) // MOE_BLOCK * MOE_BLOCK
    end_pad = jnp.cumsum(padded)
    start_pad = end_pad - padded
    dest = (start_pad[e] + rank).astype(jnp.int32)
    n_blocks = (a + N_EXPERTS * (MOE_BLOCK - 1) + MOE_BLOCK - 1) // MOE_BLOCK
    token = jnp.tile(jnp.arange(t, dtype=jnp.int32), MOE_TOP_K)
    filler = jnp.arange(n_blocks * MOE_BLOCK, dtype=jnp.int32) % t
    row_tok = filler + jnp.zeros_like(filler).at[dest].add(token - dest % t, unique_indices=True,
                                                           mode="promise_in_bounds")
    block_start = jnp.arange(n_blocks, dtype=jnp.int32) * MOE_BLOCK
    block_e = jnp.sum((end_pad[None, :] <= block_start[:, None]).astype(jnp.int32), axis=1)
    block_e = jnp.minimum(block_e, N_EXPERTS - 1).astype(jnp.int32)
    n_used = (end_pad[-1] // MOE_BLOCK).astype(jnp.int32).reshape(1)
    return dest.reshape(MOE_TOP_K, t), row_tok, block_e, n_used


def _split_count(n, want):
    return max(k for k in range(1, want + 1) if n % k == 0)


def _moe(x1, h2, logits, mod_l, wg, wu, wd, layer, final_g, seq_len):
    t = x1.shape[0]
    route, route_t = _routing(logits)
    e = jnp.concatenate([route_t[:, 0, :].reshape(t), route_t[:, 1, :].reshape(t)]).astype(jnp.int32)
    dest, row_tok, block_e, n_used = _dispatch_plan(e, t)
    total_blocks = row_tok.shape[0] // MOE_BLOCK
    n_slices = _split_count(total_blocks, MOE_SLICES)
    per = total_blocks // n_slices
    ys = None
    for c in range(n_slices):
        rows = row_tok[c * per * MOE_BLOCK:(c + 1) * per * MOE_BLOCK]
        ys = _experts(h2[rows], block_e, n_used, wg, wu, wd, layer, ys, c * per, total_blocks)
    final = layer == wg.shape[0] - 1
    tm = min(WIDE_TILE, seq_len)
    n_tiles = x1.shape[0] // tm
    n_slices = _split_count(n_tiles, MOE_SLICES)
    per = n_tiles // n_slices
    x = x1
    for c in range(n_slices):
        d_c = dest[:, c * per * tm:(c + 1) * per * tm]
        x = _combine(x, ys[d_c[0]], ys[d_c[1]], route, mod_l, final_g, final, seq_len, c * per)
    return x


def _pad_cols(w, n):
    return jnp.pad(w, ((0, 0), (0, n - w.shape[1])))


def _pad_rows(w, n):
    return jnp.pad(w, ((0, n - w.shape[0]), (0, 0)))


def _rwkv_params(j, mu, wr, wk, wv, w0, w1, w2, a0, a1, a2, v0, v1, v2, g1, g2, kk, ka, rk,
                 lnw, lnb):
    d = D_MODEL
    p = {
        "mu": mu[j],
        "wr": wr[j].astype(BF16), "wk": wk[j].astype(BF16), "wv": wv[j].astype(BF16),
        "wl1": jnp.concatenate([w1[j, 0], w1[j, 1]], axis=1).astype(BF16),
        "wl2": _block_diag2(w2[j, 0], w2[j, 1]).astype(BF16),
        "al1": jnp.concatenate([a1[j, 0], a1[j, 1]], axis=1).astype(BF16),
        "al2": _block_diag2(a2[j, 0], a2[j, 1]).astype(BF16),
        "g1": g1[j].astype(BF16), "g2": g2[j].astype(BF16),
        "w0": w0[j].reshape(1, 2 * d), "a0": a0[j].reshape(1, 2 * d),
        "kk": kk[j].reshape(1, d), "ka": ka[j].reshape(1, d), "rk": rk[j].reshape(1, d),
        "lnw": lnw[j].reshape(1, d), "lnb": lnb[j].reshape(1, d),
    }
    if j > 0:
        p["v0"] = v0[j - 1].reshape(1, d)
        p["v1"] = _pad_cols(v1[j - 1], LANES).astype(BF16)
        p["v2"] = _pad_rows(v2[j - 1], LANES).astype(BF16)
    return p


def kernel(x_prompt, x_sample, c_prompt, c_sample, ada_w, ada_b, norm_g, final_g, rwkv_mu, rwkv_wr, rwkv_wk, rwkv_wv, rwkv_wo, rwkv_w0, rwkv_w1, rwkv_w2, rwkv_a0, rwkv_a1, rwkv_a2, rwkv_v0, rwkv_v1, rwkv_v2, rwkv_g1, rwkv_g2, rwkv_kk, rwkv_ka, rwkv_rk, rwkv_lnw, rwkv_lnb, sgu_win, sgu_bin, sgu_ng, sgu_nb, sgu_ws, sgu_bs, sgu_wout, sgu_bout, diff_wqkv, diff_wo, diff_lq1, diff_lk1, diff_lq2, diff_lk2, diff_subln, rel_bias, moe_wrg, moe_brg, moe_wre, moe_bre, moe_wg, moe_wu, moe_wd):
    d = D_MODEL
    nb_p, seq_len, _ = x_prompt.shape
    assert x_sample.shape[1] == seq_len
    t_p = nb_p * seq_len
    x = jnp.concatenate([x_prompt.reshape(-1, d), x_sample.reshape(-1, d)], axis=0)
    c = jnp.concatenate([c_prompt, c_sample], axis=0)
    mod = _modulation(c, ada_w, ada_b)
    zero_bias = jnp.zeros((1, d), F32)
    v_first = None
    for i in range(DEPTH):
        j = i // N_MIXERS
        mod_l = mod[i]
        if i % N_MIXERS == 0:
            p = _rwkv_params(j, rwkv_mu, rwkv_wr, rwkv_wk, rwkv_wv, rwkv_w0, rwkv_w1, rwkv_w2,
                             rwkv_a0, rwkv_a1, rwkv_a2, rwkv_v0, rwkv_v1, rwkv_v2, rwkv_g1, rwkv_g2,
                             rwkv_kk, rwkv_ka, rwkv_rk, rwkv_lnw, rwkv_lnb)
            r, k, v, g, kk, wlf, wlb, af, ab, bonus = _rwkv_pre(x, mod_l, norm_g[i, 0], p, v_first,
                                                                seq_len)
            if v_first is None:
                v_first = v
            y = _wkv(r, k, v, kk, wlf, wlb, af, ab, p["ka"], seq_len)
            pre = _rwkv_mid(y, v, g, bonus, p)
            w_out, b_out = rwkv_wo[j].astype(BF16), zero_bias
        elif i % N_MIXERS == 1:
            p = {"win": sgu_win[j].astype(BF16), "bin": sgu_bin[j].reshape(1, -1),
                 "ng": sgu_ng[j].reshape(1, -1), "nb": sgu_nb[j].reshape(1, -1),
                 "ws": sgu_ws[j].astype(BF16),
                 "bs": jnp.repeat(jnp.transpose(sgu_bs[j]), SGU_GC, axis=1)}
            pre = _sgu(x, mod_l, norm_g[i, 0], p, seq_len)
            w_out, b_out = sgu_wout[j].astype(BF16), sgu_bout[j].reshape(1, d)
        else:
            lambda_init = 0.8 - 0.6 * math.exp(-0.3 * i)
            lam = (jnp.exp(jnp.sum(diff_lq1[j] * diff_lk1[j])) - jnp.exp(jnp.sum(diff_lq2[j] * diff_lk2[j]))
                   + lambda_init)
            q, k, v = _qkv(x, mod_l, norm_g[i, 0], diff_wqkv[j].astype(BF16), seq_len)
            windows = _bias_windows(rel_bias, seq_len, min(ATTN_TQ, seq_len))
            pre = _diff_attention(q, k, v, windows, lam, diff_subln[j], lambda_init, seq_len)
            w_out, b_out = diff_wo[j].astype(BF16), zero_bias
        w_router = _pad_cols(jnp.concatenate([moe_wrg[i], moe_wre[i]], axis=1), ROUTER_COLS)
        b_router = _pad_cols(jnp.concatenate([moe_brg[i], moe_bre[i]])[None, :], ROUTER_COLS)
        x1, h2, logits = _post(pre, w_out, b_out, x, mod_l, norm_g[i, 1], w_router, b_router, seq_len)
        x = _moe(x1, h2, logits, mod_l, moe_wg, moe_wu, moe_wd, i, final_g, seq_len)
    return (x[:t_p].reshape(x_prompt.shape), x[t_p:].reshape(x_sample.shape))
```

```python
import functools
import math

import numpy as np
import jax
import jax.numpy as jnp
from jax import lax
from jax.experimental import pallas as pl
from jax.experimental.pallas import tpu as pltpu

F32 = jnp.float32
BF16 = jnp.bfloat16
HIGHEST = lax.Precision.HIGHEST

D_MODEL = 1024
DEPTH = 4
N_MIXERS = 3
NORM_EPS = 1e-6
LANES = 128
VMEM_LIMIT_BYTES = 56 * 1024 * 1024

RWKV_HEAD = 64
RWKV_HEADS = D_MODEL // RWKV_HEAD
RWKV_PAIRS = D_MODEL // LANES
RWKV_GN_EPS = 64e-5
WKV_CHUNK = 64
WKV_PAIRS_PER_STEP = 2
WKV_CHUNKS_PER_STEP = 4

SGU_CHUNK = 128
SGU_DIM = 2 * D_MODEL
SGU_GROUPS = 8
SGU_GC = SGU_DIM // SGU_GROUPS

DIFF_HEADS = 8
DIFF_HEAD = 64
NUM_BUCKETS = 32
MAX_DISTANCE = 128
ATTN_TQ = 512

MOE_GROUPS = 4
EXPERTS_PER_GROUP = 8
N_EXPERTS = MOE_GROUPS * EXPERTS_PER_GROUP
MOE_TOP_K = 2
EXPERT_DIM = D_MODEL // 2
MOE_BLOCK = 512
ROUTER_COLS = LANES
ROUTE_TILE = 2048
RANK_TILE = 512
RANK_ROWS = 8
MOE_SLICES = 4

TOKEN_TILE = 256
WIDE_TILE = 512


def _params(*semantics):
    return pltpu.CompilerParams(dimension_semantics=semantics, vmem_limit_bytes=VMEM_LIMIT_BYTES)


def _resident(shape):
    zeros = (0,) * len(shape)
    return pl.BlockSpec(shape, lambda *_: zeros, pipeline_mode=pl.Buffered(1))


def _dot(a, b, precision=None):
    return jnp.dot(a, b, preferred_element_type=F32, precision=precision)


def _dot_nt(a, b, precision=None):
    return lax.dot_general(a, b, (((1,), (1,)), ((), ())), preferred_element_type=F32,
                           precision=precision)


def _dot_tn(a, b, precision=None):
    return lax.dot_general(a, b, (((0,), (0,)), ((), ())), preferred_element_type=F32,
                           precision=precision)


def _split_bf16(x):
    hi = x.astype(BF16)
    return hi, (x - hi.astype(F32)).astype(BF16)


def _head_sums(x, hsum, hexp):
    hi, lo = _split_bf16(x)
    shi, slo = _split_bf16(_dot(hi, hsum) + _dot(lo, hsum))
    return _dot(shi, hexp) + _dot(slo, hexp)


def _sigmoid(x):
    return 1.0 / (1.0 + jnp.exp(-x))


def _norm_mod(x, g, scale, shift, eps=NORM_EPS):
    y = x * lax.rsqrt(jnp.mean(x * x, axis=-1, keepdims=True) + eps)
    return (y * g) * (1.0 + scale) + shift


def _mod_kernel(c_ref, w_ref, b_ref, o_ref):
    c = c_ref[...]
    o_ref[...] = _dot(c * _sigmoid(c), w_ref[...], HIGHEST) + b_ref[...]


def _modulation(c, ada_w, ada_b):
    nb = c.shape[0]
    depth, d, n = ada_w.shape
    out = pl.pallas_call(
        _mod_kernel,
        grid=(depth, n // d),
        in_specs=[pl.BlockSpec((nb, d), lambda l, j: (0, 0)),
                  pl.BlockSpec((None, d, d), lambda l, j: (l, 0, j)),
                  pl.BlockSpec((None, 1, d), lambda l, j: (l, 0, j))],
        out_specs=pl.BlockSpec((None, nb, d), lambda l, j: (l, 0, j)),
        out_shape=jax.ShapeDtypeStruct((depth, nb, n), F32),
        compiler_params=_params("parallel", "parallel"),
        name="adaln_modulation",
    )(c, ada_w, ada_b.reshape(depth, 1, n))
    return out.reshape(depth, nb, n // d, d)


def _rwkv_pre_kernel(*refs, tm, tiles_per_seq, has_vres):
    if has_vres:
        (x_ref, xp_ref, xn_ref, mod_ref, ng_ref, mu_ref, wr_ref, wk_ref, wv_ref, wl1_ref, wl2_ref,
         al1_ref, al2_ref, g1_ref, g2_ref, w0_ref, a0_ref, kscale_ref, ka_ref, rk_ref, hsum_ref,
         hexp_ref, vf_ref, v0_ref, v1_ref, v2_ref,
         r_ref, k_ref, v_ref, g_ref, kk_ref, wlf_ref, wlb_ref, af_ref, ab_ref, bonus_ref) = refs
    else:
        (x_ref, xp_ref, xn_ref, mod_ref, ng_ref, mu_ref, wr_ref, wk_ref, wv_ref, wl1_ref, wl2_ref,
         al1_ref, al2_ref, g1_ref, g2_ref, w0_ref, a0_ref, kscale_ref, ka_ref, rk_ref, hsum_ref,
         hexp_ref,
         r_ref, k_ref, v_ref, g_ref, kk_ref, wlf_ref, wlb_ref, af_ref, ab_ref, bonus_ref) = refs
    d = D_MODEL
    i = pl.program_id(0)
    pos = i % tiles_per_seq
    shift, scale = mod_ref[0:1, :], mod_ref[1:2, :]
    ng = ng_ref[...]
    h = _norm_mod(x_ref[...], ng, scale, shift)
    h_prev = jnp.where(pos == 0, 0.0, _norm_mod(xp_ref[7:8, :], ng, scale, shift))
    h_next = jnp.where(pos == tiles_per_seq - 1, 0.0, _norm_mod(xn_ref[0:1, :], ng, scale, shift))
    row = lax.broadcasted_iota(jnp.int32, (tm, 1), 0)
    h_up = jnp.where(row == 0, h_prev, pltpu.roll(h, 1, 0))
    h_dn = jnp.where(row == tm - 1, h_next, pltpu.roll(h, tm - 1, 0))
    xx = 0.5 * (h_up + h_dn) - h

    def mix(n):
        return (h + xx * mu_ref[n:n + 1, :]).astype(BF16)

    xr, xw, xk, xv, xa, xg = [mix(n) for n in range(6)]
    r = _dot(xr, wr_ref[...])
    r_ref[...] = r.astype(BF16)
    k = _dot(xk, wk_ref[...])
    k_ref[...] = k.astype(BF16)
    kkr = k * kscale_ref[...]
    norm = jnp.sqrt(_head_sums(kkr * kkr, hsum_ref[...], hexp_ref[...]))
    kk_ref[...] = (kkr / jnp.maximum(norm, 1e-12)).astype(BF16)
    v = _dot(xv, wv_ref[...])
    if has_vres:
        lv = _dot(_dot(xv, v1_ref[...]).astype(BF16), v2_ref[...])
        v = v + (vf_ref[...].astype(F32) - v) * _sigmoid(v0_ref[...] + lv)
    v_ref[...] = v.astype(BF16)
    g_ref[...] = _dot(_sigmoid(_dot(xg, g1_ref[...])).astype(BF16), g2_ref[...]).astype(BF16)
    wl = w0_ref[...] + _dot(jnp.tanh(_dot(xw, wl1_ref[...])).astype(BF16), wl2_ref[...])
    logw = -math.exp(-0.5) * _sigmoid(wl)
    wlf_ref[...] = logw[:, :d]
    wlb_ref[...] = logw[:, d:]
    a = _sigmoid(a0_ref[...] + _dot(_dot(xa, al1_ref[...]).astype(BF16), al2_ref[...]))
    af_ref[...] = a[:, :d].astype(BF16)
    ab_ref[...] = a[:, d:].astype(BF16)
    k_bonus = k * (1.0 + (0.5 * (a[:, :d] + a[:, d:]) - 1.0) * ka_ref[...])
    hi, lo = _split_bf16(r * k_bonus * rk_ref[...])
    bonus_ref[...] = _dot(hi, hsum_ref[...]) + _dot(lo, hsum_ref[...])


def _head_sum_matrix():
    head_of_lane = np.arange(D_MODEL) // RWKV_HEAD
    return jnp.asarray(head_of_lane[:, None] == np.arange(RWKV_HEADS)[None, :], BF16)


def _block_diag2(m0, m1):
    z = jnp.zeros_like(m0)
    return jnp.concatenate([jnp.concatenate([m0, z], 1), jnp.concatenate([z, m1], 1)], 0)


def _rwkv_pre(x, mod_l, norm_g, p, v_first, seq_len):
    t, d = x.shape
    tm = min(WIDE_TILE, seq_len)
    tiles_per_seq = seq_len // tm
    n_tiles = t // tm
    has_vres = v_first is not None
    tile = pl.BlockSpec((tm, d), lambda i: (i, 0))
    rows8 = tm // 8
    last8 = t // 8 - 1
    in_specs = [
        tile,
        pl.BlockSpec((8, d), lambda i: (jnp.maximum(i * rows8 - 1, 0), 0)),
        pl.BlockSpec((8, d), lambda i: (jnp.minimum((i + 1) * rows8, last8), 0)),
        pl.BlockSpec((None, 6, d), lambda i: (i // tiles_per_seq, 0, 0)),
        _resident((1, d)), _resident((6, d)),
        _resident((d, d)), _resident((d, d)), _resident((d, d)),
        _resident((d, LANES)), _resident((LANES, 2 * d)),
        _resident((d, LANES)), _resident((LANES, 2 * d)),
        _resident((d, LANES)), _resident((LANES, d)),
        _resident((1, 2 * d)), _resident((1, 2 * d)),
        _resident((1, d)), _resident((1, d)), _resident((1, d)),
        _resident((d, RWKV_HEADS)), _resident((RWKV_HEADS, d)),
    ]
    hsum = _head_sum_matrix()
    args = [x, x, x, mod_l, norm_g.reshape(1, d), p["mu"],
            p["wr"], p["wk"], p["wv"], p["wl1"], p["wl2"], p["al1"], p["al2"], p["g1"], p["g2"],
            p["w0"], p["a0"], p["kk"], p["ka"], p["rk"], hsum, hsum.T]
    if has_vres:
        in_specs += [tile, _resident((1, d)), _resident((d, LANES)), _resident((LANES, d))]
        args += [v_first, p["v0"], p["v1"], p["v2"]]
    dtypes = [BF16] * 5 + [F32] * 2 + [BF16] * 2
    outs = pl.pallas_call(
        functools.partial(_rwkv_pre_kernel, tm=tm, tiles_per_seq=tiles_per_seq, has_vres=has_vres),
        grid=(n_tiles,),
        in_specs=in_specs,
        out_specs=[tile] * 9 + [pl.BlockSpec((tm, RWKV_HEADS), lambda i: (i, 0))],
        out_shape=[jax.ShapeDtypeStruct((t, d), dt) for dt in dtypes]
        + [jax.ShapeDtypeStruct((t, RWKV_HEADS), F32)],
        compiler_params=_params("parallel"),
        name="rwkv_pre",
    )(*args)
    return outs


def _wkv_kernel(r_ref, k_ref, v_ref, kk_ref, wlf_ref, wlb_ref, af_ref, ab_ref, ka_ref,
                lvl_ref, y_ref, state_ref, g_ref, sp_ref, yp_ref, xr_ref, dec_ref,
                *, seq_len, chunk, pairs, group):
    n_chunks = seq_len // chunk
    n_levels = int(math.log2(chunk))
    two = 2 * chunk
    lane = lax.broadcasted_iota(jnp.int32, (1, LANES), 1)
    head0 = lane < RWKV_HEAD
    trow = lax.broadcasted_iota(jnp.int32, (chunk, 1), 0)
    si = lax.broadcasted_iota(jnp.int32, (two, two), 0)
    sj = lax.broadcasted_iota(jnp.int32, (two, two), 1)
    same_head = (si < chunk) == (sj < chunk)
    st, su = si & (chunk - 1), sj & (chunk - 1)
    eye = (si == sj).astype(F32)
    chain_defs = [(pair, reverse) for pair in range(pairs) for reverse in (False, True)]

    def rows_of(idx, reverse):
        pos = (n_chunks - 1 - idx) if reverse else idx
        return pl.ds(pl.multiple_of(pos * chunk, chunk), chunk)

    def stack(x):
        return jnp.concatenate([jnp.where(head0, x, 0.0), jnp.where(head0, 0.0, x)], axis=0)

    def a_load(q, g, idx, slot):
        pair, reverse = chain_defs[q]
        rows = rows_of(idx, reverse)
        lanes = slice(pair * LANES, (pair + 1) * LANES)
        a = dict(q=q, g=g, slot=slot, rows=rows, lanes=lanes, reverse=reverse)
        a["lw"] = lw = (wlb_ref if reverse else wlf_ref)[rows, lanes]
        a["a"] = (ab_ref if reverse else af_ref)[rows, lanes].astype(F32)
        logp = lw
        step = 1
        while step < chunk:
            if reverse:
                logp = logp + jnp.where(trow < chunk - step, pltpu.roll(logp, chunk - step, 0), 0.0)
            else:
                logp = logp + jnp.where(trow >= step, pltpu.roll(logp, step, 0), 0.0)
            step *= 2
        a["logp"] = logp
        return a

    def a_scale(a):
        rows, lanes, logp, lw, av = a["rows"], a["lanes"], a["logp"], a["lw"], a["a"]
        k = k_ref[rows, lanes].astype(F32)
        kk = kk_ref[rows, lanes].astype(F32)
        kd = k * (1.0 + (av - 1.0) * ka_ref[:, lanes])
        p_in = jnp.exp(logp)
        p_out = jnp.exp(-logp)
        a["xa"] = stack(-kk * jnp.exp(logp - lw)).astype(BF16)
        a["xr"] = stack(r_ref[rows, lanes].astype(F32) * p_in).astype(BF16)
        a["yb"] = stack(kk * av * p_out).astype(BF16)
        a["yk"] = stack(kd * p_out).astype(BF16)
        a["vs"] = stack(v_ref[rows, lanes])
        a["decay"] = jnp.exp(jnp.sum(lw, axis=0, keepdims=True))

    def a_scores(a):
        if a["reverse"]:
            strict, incl = same_head & (st < su), same_head & (st <= su)
        else:
            strict, incl = same_head & (st > su), same_head & (st >= su)
        sc = _dot_nt(jnp.concatenate([a["xa"], a["xr"]], 0),
                     jnp.concatenate([a["yb"], a["yk"]], 0))
        a["m"] = sc[:two, :two].astype(BF16)
        m_ak = jnp.where(strict, sc[:two, two:], 0.0).astype(BF16)
        n_rk = jnp.where(incl, sc[two:, two:], 0.0).astype(BF16)
        a["n_rb"] = jnp.where(incl, sc[two:, :two], 0.0).astype(BF16)
        a["inv"] = eye + (a["m"] * lvl_ref[int(a["reverse"]), 0]).astype(F32)
        a["zp"] = _dot(m_ak, a["vs"]).astype(BF16)
        a["yp"] = _dot(n_rk, a["vs"])
        a["sp"] = _dot_tn(a["vs"], a["yk"])

    def a_double(a, level):
        t = a["inv"].astype(BF16)
        m_n = a["m"] * lvl_ref[int(a["reverse"]), level]
        a["inv"] = a["inv"] + _dot(_dot(t, m_n).astype(BF16), t)

    def a_solve(a):
        t = a["inv"].astype(BF16)
        a["w"] = _dot(t, a["xa"]).astype(BF16)
        a["u0"] = _dot(t, a["zp"]).astype(BF16)

    def a_fold(a):
        at = (a["slot"], a["g"], a["q"])
        g_ref[at] = _dot_tn(a["w"], a["yb"]).astype(BF16)
        sp_ref[at] = a["sp"] + _dot_tn(a["u0"], a["yb"])
        yp_ref[at] = a["yp"] + _dot(a["n_rb"], a["u0"])
        xr_ref[at] = (a["xr"].astype(F32) + _dot(a["n_rb"], a["w"])).astype(BF16)
        dec_ref[at] = a["decay"]

    a_stage_list = ([a_scale, a_scores]
                    + [functools.partial(a_double, level=level) for level in range(1, n_levels)]
                    + [a_solve, a_fold])

    def b_state(b):
        at = b["at"]
        s0 = state_ref[b["q"]]
        b["s0b"] = s0.astype(BF16)
        state_ref[b["q"]] = (s0 + _dot(b["s0b"], g_ref[at]) + sp_ref[at]) * dec_ref[at]

    def b_out(b):
        at = b["at"]
        pair, reverse = chain_defs[b["q"]]
        ys = _dot_nt(xr_ref[at], b["s0b"]) + yp_ref[at]
        y_ref[rows_of(b["idx"], reverse), pair * LANES:(pair + 1) * LANES] += ys[:chunk] + ys[chunk:]

    b_stage_list = [b_state, b_out]

    y_ref[...] = jnp.zeros_like(y_ref)
    state_ref[...] = jnp.zeros_like(state_ref)
    n_chains = len(chain_defs)
    first = [a_load(q, g, g, 0) for g in range(group) for q in range(n_chains)]
    for stage in a_stage_list:
        for a in first:
            stage(a)

    def run_step(step, prepare_next):
        slot = step & 1
        base = step * group
        b_work = []
        for g in range(group):
            now = [dict(q=q, at=(slot, g, q), idx=base + g) for q in range(n_chains)]
            b_work += [(stage, now) for stage in b_stage_list]
        a_work = []
        if prepare_next:
            ahead = [a_load(q, g, base + group + g, 1 - slot)
                     for g in range(group) for q in range(n_chains)]
            a_work = [(stage, ahead) for stage in a_stage_list]
        per_a = -(-len(b_work) // max(len(a_work), 1))
        while a_work or b_work:
            for stage, items in b_work[:per_a]:
                for b in items:
                    stage(b)
            b_work = b_work[per_a:]
            if a_work:
                stage, items = a_work.pop(0)
                for a in items:
                    stage(a)

    n_steps = n_chunks // group

    def body(step, carry):
        run_step(step, True)
        return carry

    lax.fori_loop(0, n_steps - 1, body, 0)
    run_step(n_steps - 1, False)


def _wkv_level_masks(chunk):
    s = np.arange(2 * chunk)
    head, t = s // chunk, s % chunk
    n_levels = int(math.log2(chunk))
    masks = np.zeros((2, n_levels, 2 * chunk, 2 * chunk), np.float32)
    for d in range(2):
        pos = t if d == 0 else chunk - 1 - t
        for j in range(n_levels):
            blk = pos >> j
            masks[d, j] = ((head[:, None] == head[None, :]) & (blk[:, None] % 2 == 1)
                           & (blk[None, :] == blk[:, None] - 1))
    return masks


def _wkv(r, k, v, kk, wlf, wlb, af, ab, k_a, seq_len):
    t, d = r.shape
    nb = t // seq_len
    pairs = WKV_PAIRS_PER_STEP
    width = pairs * LANES
    n_chains = 2 * pairs
    seq = pl.BlockSpec((seq_len, width), lambda b, p: (b, p))
    par = pl.BlockSpec((1, width), lambda b, p: (0, p))
    levels = jnp.asarray(_wkv_level_masks(WKV_CHUNK), BF16)
    group = WKV_CHUNKS_PER_STEP
    parked_f32 = pltpu.VMEM((2, group, n_chains, LANES, LANES), F32)
    parked_bf16 = pltpu.VMEM((2, group, n_chains, LANES, LANES), BF16)
    return pl.pallas_call(
        functools.partial(_wkv_kernel, seq_len=seq_len, chunk=WKV_CHUNK, pairs=pairs, group=group),
        grid=(nb, d // width),
        in_specs=[seq] * 8 + [par, _resident(levels.shape)],
        out_specs=seq,
        out_shape=jax.ShapeDtypeStruct((t, d), F32),
        scratch_shapes=[pltpu.VMEM((n_chains, LANES, LANES), F32), parked_bf16, parked_f32,
                        parked_f32, parked_bf16, pltpu.VMEM((2, group, n_chains, 1, LANES), F32)],
        compiler_params=_params("parallel", "parallel"),
        name="wkv7_chunked",
    )(r, k, v, kk, wlf, wlb, af, ab, k_a.reshape(1, d), levels)


def _rwkv_mid_kernel(y_ref, v_ref, g_ref, bonus_ref, lnw_ref, lnb_ref, hsum_ref, hexp_ref, o_ref):
    hsum = hsum_ref[...]
    hexp = hexp_ref[...]

    def head_sum(x):
        return _head_sums(x, hsum, hexp)

    y = y_ref[...]
    yc = y - head_sum(y) * (1.0 / RWKV_HEAD)
    var = head_sum(yc * yc) * (1.0 / RWKV_HEAD)
    yn = yc * lax.rsqrt(var + RWKV_GN_EPS) * lnw_ref[...] + lnb_ref[...]
    b_hi, b_lo = _split_bf16(bonus_ref[...])
    bonus = (_dot(b_hi, hexp) + _dot(b_lo, hexp)) * v_ref[...].astype(F32)
    o_ref[...] = ((yn + bonus) * g_ref[...].astype(F32)).astype(BF16)


def _rwkv_mid(y, v, g, bonus, p):
    t, d = y.shape
    tm = TOKEN_TILE
    tile = pl.BlockSpec((tm, d), lambda i: (i, 0))
    vec = _resident((1, d))
    hsum = _head_sum_matrix()
    return pl.pallas_call(
        _rwkv_mid_kernel,
        grid=(t // tm,),
        in_specs=[tile] * 3 + [pl.BlockSpec((tm, RWKV_HEADS), lambda i: (i, 0))] + [vec] * 2
        + [_resident((d, RWKV_HEADS)), _resident((RWKV_HEADS, d))],
        out_specs=tile,
        out_shape=jax.ShapeDtypeStruct((t, d), BF16),
        compiler_params=_params("parallel"),
        name="rwkv_mid",
    )(y, v, g, bonus, p["lnw"], p["lnb"], hsum, hsum.T)


def _sgu_kernel(x_ref, mod_ref, ng_ref, win_ref, bin_ref, lg_ref, lb_ref, ws_ref, bs_ref, o_ref,
                *, tm):
    h = _norm_mod(x_ref[...], ng_ref[...], mod_ref[1:2, :], mod_ref[0:1, :]).astype(BF16)
    z = _dot(h, win_ref[...]) + bin_ref[...]
    z = 0.5 * z * (1.0 + lax.erf(z * (1.0 / math.sqrt(2.0))))
    u = z[:, :SGU_DIM]
    v = z[:, SGU_DIM:]
    vc = v - jnp.mean(v, axis=-1, keepdims=True)
    vn = vc * lax.rsqrt(jnp.mean(vc * vc, axis=-1, keepdims=True) + 1e-5) * lg_ref[...] + lb_ref[...]
    vb = vn.astype(BF16)
    for c in range(tm // SGU_CHUNK):
        rows = slice(c * SGU_CHUNK, (c + 1) * SGU_CHUNK)
        for g in range(SGU_GROUPS):
            cols = slice(g * SGU_GC, (g + 1) * SGU_GC)
            s = _dot(ws_ref[g], vb[rows, cols]) + bs_ref[:, cols]
            o_ref[rows, cols] = (u[rows, cols] * s).astype(BF16)


def _sgu(x, mod_l, norm_g, p, seq_len):
    t, d = x.shape
    tm = min(WIDE_TILE, seq_len)
    tiles_per_seq = seq_len // tm
    return pl.pallas_call(
        functools.partial(_sgu_kernel, tm=tm),
        grid=(t // tm,),
        in_specs=[pl.BlockSpec((tm, d), lambda i: (i, 0)),
                  pl.BlockSpec((None, 6, d), lambda i: (i // tiles_per_seq, 0, 0)),
                  _resident((1, d)),
                  _resident((d, 2 * SGU_DIM)), _resident((1, 2 * SGU_DIM)),
                  _resident((1, SGU_DIM)), _resident((1, SGU_DIM)),
                  _resident((SGU_GROUPS, SGU_CHUNK, SGU_CHUNK)), _resident((SGU_CHUNK, SGU_DIM))],
        out_specs=pl.BlockSpec((tm, SGU_DIM), lambda i: (i, 0)),
        out_shape=jax.ShapeDtypeStruct((t, SGU_DIM), BF16),
        compiler_params=_params("parallel"),
        name="sgu",
    )(x, mod_l, norm_g.reshape(1, d), p["win"], p["bin"], p["ng"], p["nb"], p["ws"], p["bs"])


def _qkv_kernel(x_ref, mod_ref, ng_ref, w_ref, q_ref, k_ref, v_ref):
    d = D_MODEL
    h = _norm_mod(x_ref[...], ng_ref[...], mod_ref[1:2, :], mod_ref[0:1, :]).astype(BF16)
    qkv = _dot(h, w_ref[...])
    q_ref[...] = (qkv[:, :d] * (DIFF_HEAD ** -0.5)).astype(BF16)
    k_ref[...] = qkv[:, d:2 * d].astype(BF16)
    v_ref[...] = qkv[:, 2 * d:].astype(BF16)


def _qkv(x, mod_l, norm_g, w_qkv, seq_len):
    t, d = x.shape
    tm = TOKEN_TILE
    tiles_per_seq = seq_len // tm
    tile = pl.BlockSpec((tm, d), lambda i: (i, 0))
    return pl.pallas_call(
        _qkv_kernel,
        grid=(t // tm,),
        in_specs=[tile, pl.BlockSpec((None, 6, d), lambda i: (i // tiles_per_seq, 0, 0)),
                  _resident((1, d)), _resident((d, 3 * d))],
        out_specs=[tile] * 3,
        out_shape=[jax.ShapeDtypeStruct((t, d), BF16)] * 3,
        compiler_params=_params("parallel"),
        name="diff_qkv",
    )(x, mod_l, norm_g.reshape(1, d), w_qkv)


def _attn_kernel(q_ref, k_ref, v_ref, win_ref, lam_ref, sg_ref, o_ref, vext_ref, *, tq, seq_len,
                 out_scale):
    @pl.when(pl.program_id(2) == 0)
    def _():
        col = lax.broadcasted_iota(jnp.int32, (seq_len, LANES), 1)
        vext_ref[:, :LANES] = v_ref[...]
        vext_ref[:, LANES:] = jnp.where(col == 0, 1.0, 0.0).astype(BF16)

    q = q_ref[...]
    lane = lax.broadcasted_iota(jnp.int32, (1, LANES), 1)
    zero = jnp.zeros_like(q)
    qs = jnp.concatenate([jnp.where(lane < DIFF_HEAD, q, zero),
                          jnp.where(lane < DIFF_HEAD, zero, q)], axis=0)
    s = _dot_nt(qs, k_ref[...])
    width = seq_len + tq
    window = jnp.broadcast_to(win_ref[...], (tq, width))
    bias = pltpu.roll(window, width - tq + 1, 1, stride=1, stride_axis=0)[:, :seq_len]

    def unnormalised(x):
        e = jnp.exp(x - jnp.max(x, axis=-1, keepdims=True)).astype(BF16)
        pv = _dot(e, vext_ref[...])
        return pv[:, :LANES], pv[:, LANES:LANES + 1]

    o1, l1 = unnormalised(s[:tq] + bias)
    o2, l2 = unnormalised(s[tq:] + bias)
    o = o1 / l1 - lam_ref[...] * (o2 / l2)
    o = o * lax.rsqrt(jnp.mean(o * o, axis=-1, keepdims=True) + 1e-5) * sg_ref[...]
    o_ref[...] = (o * out_scale).astype(BF16)


def _t5_bucket(rel):
    nb = NUM_BUCKETS // 2
    max_exact = nb // 2
    ret = jnp.where(rel > 0, nb, 0)
    n = jnp.abs(rel)
    nf = jnp.maximum(n, 1).astype(F32)
    large = max_exact + (jnp.log(nf / max_exact) / math.log(MAX_DISTANCE / max_exact)
                         * (nb - max_exact)).astype(jnp.int32)
    large = jnp.minimum(large, nb - 1)
    return ret + jnp.where(n < max_exact, n, large)


def _bias_windows(rel_bias, seq_len, tq):
    nqb = seq_len // tq
    j = jnp.arange(seq_len + tq, dtype=jnp.int32)[None, :]
    q_hi = (jnp.arange(nqb, dtype=jnp.int32)[:, None] + 1) * tq
    bucket = _t5_bucket(j - q_hi + 1)
    return jnp.transpose(rel_bias[bucket], (0, 2, 1))[:, :, None, :].astype(F32)


def _diff_attention(q, k, v, windows, lam, subln_g, lambda_init, seq_len):
    t, d = q.shape
    nb = t // seq_len
    tq = min(ATTN_TQ, seq_len)
    nqb = seq_len // tq
    kv = pl.BlockSpec((seq_len, LANES), lambda b, h, i: (b, h))
    qo = pl.BlockSpec((tq, LANES), lambda b, h, i: (b * nqb + i, h))
    return pl.pallas_call(
        functools.partial(_attn_kernel, tq=tq, seq_len=seq_len, out_scale=1.0 - lambda_init),
        grid=(nb, DIFF_HEADS, nqb),
        in_specs=[qo, kv, kv,
                  pl.BlockSpec((None, None, 1, seq_len + tq), lambda b, h, i: (i, h, 0, 0)),
                  pl.BlockSpec((1, 1), lambda b, h, i: (0, 0)),
                  pl.BlockSpec((1, LANES), lambda b, h, i: (0, 0))],
        out_specs=qo,
        out_shape=jax.ShapeDtypeStruct((t, d), BF16),
        scratch_shapes=[pltpu.VMEM((seq_len, 2 * LANES), BF16)],
        compiler_params=_params("parallel", "parallel", "arbitrary"),
        name="diff_attention",
    )(q, k, v, windows, lam.reshape(1, 1), subln_g.reshape(1, LANES))


def _post_kernel(pre_ref, w_ref, b_ref, x_ref, mod_ref, ng_ref, wr_ref, br_ref,
                 x1_ref, h2_ref, lg_ref):
    out = _dot(pre_ref[...], w_ref[...]) + b_ref[...]
    x1 = x_ref[...] + mod_ref[2:3, :] * out
    x1_ref[...] = x1
    h2 = _norm_mod(x1, ng_ref[...], mod_ref[4:5, :], mod_ref[3:4, :])
    h2_hi, h2_lo = _split_bf16(h2)
    h2_ref[...] = h2_hi
    wide = _dot(h2_hi, wr_ref[...])
    lg_ref[...] = (wide[:, :ROUTER_COLS] + (wide[:, ROUTER_COLS:] + _dot(h2_lo, wr_ref[:, :ROUTER_COLS]))
                   + br_ref[...])


def _route(lg):
    lane = lax.broadcasted_iota(jnp.int32, lg.shape, 1)
    first_e, end_e = MOE_GROUPS, MOE_GROUPS + N_EXPERTS
    neg = -jnp.inf

    def max_and_first(x):
        m = jnp.max(x, axis=-1, keepdims=True)
        return m, jnp.min(jnp.where(x == m, lane, LANES), axis=-1, keepdims=True)

    is_group = lane < first_e
    g_max, grp = max_and_first(jnp.where(is_group, lg, neg))
    p_grp = 1.0 / jnp.sum(jnp.where(is_group, jnp.exp(lg - g_max), 0.0), axis=-1, keepdims=True)
    lane_grp = (lane - first_e) >> int(math.log2(EXPERTS_PER_GROUP))
    in_grp = (lane >= first_e) & (lane < end_e) & (lane_grp == grp)
    cand = jnp.where(in_grp, lg, neg)
    v1, i1 = max_and_first(cand)
    v2, i2 = max_and_first(jnp.where(lane == i1, neg, cand))
    p2 = jnp.exp(v2 - v1)
    gate1 = p_grp / (1.0 + p2)
    gate2 = gate1 * p2
    e1 = (i1 - first_e).astype(F32)
    e2 = (i2 - first_e).astype(F32)
    return jnp.where(lane == 0, e1, jnp.where(lane == 1, e2, jnp.where(lane == 2, gate1,
                                                                    jnp.where(lane == 3, gate2, 0.0))))


def _route_kernel(lg_ref, route_ref, route_t_ref):
    route = _route(lg_ref[...])
    route_ref[...] = route
    route_t_ref[...] = route.T[:8, :]


def _routing(logits):
    t = logits.shape[0]
    tr = max(k for k in range(LANES, ROUTE_TILE + 1, LANES) if t % k == 0)
    return pl.pallas_call(
        _route_kernel,
        grid=(t // tr,),
        in_specs=[pl.BlockSpec((tr, ROUTER_COLS), lambda i: (i, 0))],
        out_specs=[pl.BlockSpec((tr, ROUTER_COLS), lambda i: (i, 0)),
                   pl.BlockSpec((None, 8, tr), lambda i: (i, 0, 0))],
        out_shape=[jax.ShapeDtypeStruct((t, ROUTER_COLS), F32),
                   jax.ShapeDtypeStruct((t // tr, 8, tr), F32)],
        compiler_params=_params("parallel"),
        name="moe_route",
    )(logits)


def _post(pre, w, b, x, mod_l, norm_g2, w_router, b_router, seq_len):
    t, d = x.shape
    din = pre.shape[1]
    tm = min(WIDE_TILE, seq_len)
    tiles_per_seq = seq_len // tm
    tile = pl.BlockSpec((tm, d), lambda i: (i, 0))
    return pl.pallas_call(
        _post_kernel,
        grid=(t // tm,),
        in_specs=[pl.BlockSpec((tm, din), lambda i: (i, 0)), _resident((din, d)), _resident((1, d)),
                  tile, pl.BlockSpec((None, 6, d), lambda i: (i // tiles_per_seq, 0, 0)),
                  _resident((1, d)), _resident((d, 2 * ROUTER_COLS)),
                  _resident((1, ROUTER_COLS))],
        out_specs=[tile, tile, pl.BlockSpec((tm, ROUTER_COLS), lambda i: (i, 0))],
        out_shape=[jax.ShapeDtypeStruct((t, d), F32), jax.ShapeDtypeStruct((t, d), BF16),
                   jax.ShapeDtypeStruct((t, ROUTER_COLS), F32)],
        compiler_params=_params("parallel"),
        name="post_router",
    )(pre, w, b, x, mod_l, norm_g2.reshape(1, d), jnp.concatenate(_split_bf16(w_router), axis=1), b_router)


def _expert_kernel(be_ref, nu_ref, x_ref, wg_ref, wu_ref, wd_ref, *rest, first_block):
    o_ref, wgb_ref, wub_ref, wdb_ref = rest[-4:]
    i = pl.program_id(0)
    blk = first_block + i
    used = blk < nu_ref[0]
    new_expert = (i == 0) | (be_ref[blk] != be_ref[jnp.maximum(blk - 1, 0)])

    @pl.when(used & new_expert)
    def _():
        wgb_ref[...] = wg_ref[...].astype(BF16)
        wub_ref[...] = wu_ref[...].astype(BF16)
        wdb_ref[...] = wd_ref[...].astype(BF16)

    @pl.when(used)
    def _():
        x = x_ref[...]
        hg = _dot(x, wgb_ref[...])
        hu = _dot(x, wub_ref[...])
        act = hg * _sigmoid(hg) * hu
        o_ref[...] = _dot(act.astype(BF16), wdb_ref[...]).astype(BF16)

    @pl.when(jnp.logical_not(used))
    def _():
        o_ref[...] = jnp.zeros_like(o_ref)


def _experts(xs, block_e, n_used, wg, wu, wd, layer, ys_buf, first_block, total_blocks):
    p_rows, d = xs.shape
    n_blocks = p_rows // MOE_BLOCK

    def weight(shape):
        return pl.BlockSpec((None, None) + shape, lambda i, be, nu: (layer, be[first_block + i], 0, 0))

    in_specs = [pl.BlockSpec((MOE_BLOCK, d), lambda i, be, nu: (i, 0)),
                weight((d, EXPERT_DIM)), weight((d, EXPERT_DIM)), weight((EXPERT_DIM, d))]
    args = [block_e, n_used, xs, wg, wu, wd]
    aliases = {}
    if ys_buf is not None:
        in_specs.append(pl.BlockSpec(memory_space=pl.ANY))
        aliases = {len(args): 0}
        args.append(ys_buf)
    grid_spec = pltpu.PrefetchScalarGridSpec(
        num_scalar_prefetch=2,
        grid=(n_blocks,),
        in_specs=in_specs,
        out_specs=pl.BlockSpec((MOE_BLOCK, d), lambda i, be, nu: (first_block + i, 0)),
        scratch_shapes=[pltpu.VMEM((d, EXPERT_DIM), BF16), pltpu.VMEM((d, EXPERT_DIM), BF16),
                        pltpu.VMEM((EXPERT_DIM, d), BF16)],
    )
    return pl.pallas_call(
        functools.partial(_expert_kernel, first_block=first_block),
        grid_spec=grid_spec,
        out_shape=jax.ShapeDtypeStruct((total_blocks * MOE_BLOCK, d), BF16),
        input_output_aliases=aliases,
        compiler_params=_params("arbitrary"),
        name="moe_experts",
    )(*args)


def _combine_kernel(x_ref, y0_ref, y1_ref, route_ref, mod_ref, fg_ref, o_ref, *, final):
    route = route_ref[...]
    moe = route[:, 2:3] * y0_ref[...].astype(F32) + route[:, 3:4] * y1_ref[...].astype(F32)
    x = x_ref[...] + mod_ref[5:6, :] * moe
    if final:
        x = x * lax.rsqrt(jnp.mean(x * x, axis=-1, keepdims=True) + NORM_EPS) * fg_ref[...]
    o_ref[...] = x


def _combine(x_buf, y0, y1, route, mod_l, final_g, final, seq_len, first_tile):
    t, d = x_buf.shape
    tm = min(WIDE_TILE, seq_len)
    tiles_per_seq = seq_len // tm
    here = pl.BlockSpec((tm, d), lambda i: (first_tile + i, 0))
    local = pl.BlockSpec((tm, d), lambda i: (i, 0))
    return pl.pallas_call(
        functools.partial(_combine_kernel, final=final),
        grid=(y0.shape[0] // tm,),
        in_specs=[here, local, local, pl.BlockSpec((tm, ROUTER_COLS), lambda i: (first_tile + i, 0)),
                  pl.BlockSpec((None, 6, d), lambda i: ((first_tile + i) // tiles_per_seq, 0, 0)),
                  _resident((1, d))],
        out_specs=here,
        out_shape=jax.ShapeDtypeStruct((t, d), F32),
        input_output_aliases={0: 0},
        compiler_params=_params("parallel"),
        name="moe_combine",
    )(x_buf, y0, y1, route, mod_l, final_g.reshape(1, d))


def _rank_kernel(e_ref, rank_ref, cnt_ref, carry_ref):
    @pl.when(pl.program_id(0) == 0)
    def _():
        carry_ref[...] = jnp.zeros_like(carry_ref)

    expert_id = lax.broadcasted_iota(jnp.int32, (N_EXPERTS, RANK_TILE), 0)
    ri = lax.broadcasted_iota(jnp.int32, (RANK_TILE, RANK_TILE), 0)
    ci = lax.broadcasted_iota(jnp.int32, (RANK_TILE, RANK_TILE), 1)
    upper = (ri <= ci).astype(BF16)
    carry = carry_ref[...]
    for r in range(e_ref.shape[0]):
        e = e_ref[r:r + 1, :]
        onehot = jnp.where(e == expert_id, 1.0, 0.0)
        prefix = _dot(onehot.astype(BF16), upper)
        rank = jnp.sum(onehot * (prefix + carry), axis=0, keepdims=True) - 1.0
        rank_ref[r:r + 1, :] = rank.astype(jnp.int32)
        carry = carry + jnp.sum(onehot, axis=1, keepdims=True)
    carry_ref[...] = carry
    cnt_ref[...] = jnp.broadcast_to(carry, cnt_ref.shape).astype(jnp.int32)


def _rank(e):
    a = e.shape[0]
    n_tiles = a // RANK_TILE
    rows = _split_count(n_tiles, RANK_ROWS)
    tile = pl.BlockSpec((None, rows, RANK_TILE), lambda i: (i, 0, 0))
    rank, counts = pl.pallas_call(
        _rank_kernel,
        grid=(n_tiles // rows,),
        in_specs=[tile],
        out_specs=[tile, pl.BlockSpec((N_EXPERTS, LANES), lambda i: (0, 0))],
        out_shape=[jax.ShapeDtypeStruct((n_tiles // rows, rows, RANK_TILE), jnp.int32),
                   jax.ShapeDtypeStruct((N_EXPERTS, LANES), jnp.int32)],
        scratch_shapes=[pltpu.VMEM((N_EXPERTS, 1), F32)],
        compiler_params=_params("arbitrary"),
        name="moe_rank",
    )(e.reshape(n_tiles // rows, rows, RANK_TILE))
    return rank.reshape(a), counts[:, 0]


def _dispatch_plan(e, t):
    a = t * MOE_TOP_K
    rank, counts = _rank(e)
    padded = (counts + MOE_BLOCK - 1) // MOE_BLOCK * MOE_BLOCK
    end_pad = jnp.cumsum(padded)
    start_pad = end_pad - padded
    dest = (start_pad[e] + rank).astype(jnp.int32)
    n_blocks = (a + N_EXPERTS * (MOE_BLOCK - 1) + MOE_BLOCK - 1) // MOE_BLOCK
    token = jnp.tile(jnp.arange(t, dtype=jnp.int32), MOE_TOP_K)
    filler = jnp.arange(n_blocks * MOE_BLOCK, dtype=jnp.int32) % t
    row_tok = filler + jnp.zeros_like(filler).at[dest].add(token - dest % t, unique_indices=True,
                                                           mode="promise_in_bounds")
    block_start = jnp.arange(n_blocks, dtype=jnp.int32) * MOE_BLOCK
    block_e = jnp.sum((end_pad[None, :] <= block_start[:, None]).astype(jnp.int32), axis=1)
    block_e = jnp.minimum(block_e, N_EXPERTS - 1).astype(jnp.int32)
    n_used = (end_pad[-1] // MOE_BLOCK).astype(jnp.int32).reshape(1)
    return dest.reshape(MOE_TOP_K, t), row_tok, block_e, n_used


def _split_count(n, want):
    return max(k for k in range(1, want + 1) if n % k == 0)


def _moe(x1, h2, logits, mod_l, wg, wu, wd, layer, final_g, seq_len):
    t = x1.shape[0]
    route, route_t = _routing(logits)
    e = jnp.concatenate([route_t[:, 0, :].reshape(t), route_t[:, 1, :].reshape(t)]).astype(jnp.int32)
    dest, row_tok, block_e, n_used = _dispatch_plan(e, t)
    total_blocks = row_tok.shape[0] // MOE_BLOCK
    n_slices = _split_count(total_blocks, MOE_SLICES)
    per = total_blocks // n_slices
    ys = None
    for c in range(n_slices):
        rows = row_tok[c * per * MOE_BLOCK:(c + 1) * per * MOE_BLOCK]
        ys = _experts(h2[rows], block_e, n_used, wg, wu, wd, layer, ys, c * per, total_blocks)
    final = layer == wg.shape[0] - 1
    tm = min(WIDE_TILE, seq_len)
    n_tiles = x1.shape[0] // tm
    n_slices = _split_count(n_tiles, MOE_SLICES)
    per = n_tiles // n_slices
    x = x1
    for c in range(n_slices):
        d_c = dest[:, c * per * tm:(c + 1) * per * tm]
        x = _combine(x, ys[d_c[0]], ys[d_c[1]], route, mod_l, final_g, final, seq_len, c * per)
    return x


def _pad_cols(w, n):
    return jnp.pad(w, ((0, 0), (0, n - w.shape[1])))


def _pad_rows(w, n):
    return jnp.pad(w, ((0, n - w.shape[0]), (0, 0)))


def _rwkv_params(j, mu, wr, wk, wv, w0, w1, w2, a0, a1, a2, v0, v1, v2, g1, g2, kk, ka, rk,
                 lnw, lnb):
    d = D_MODEL
    p = {
        "mu": mu[j],
        "wr": wr[j].astype(BF16), "wk": wk[j].astype(BF16), "wv": wv[j].astype(BF16),
        "wl1": jnp.concatenate([w1[j, 0], w1[j, 1]], axis=1).astype(BF16),
        "wl2": _block_diag2(w2[j, 0], w2[j, 1]).astype(BF16),
        "al1": jnp.concatenate([a1[j, 0], a1[j, 1]], axis=1).astype(BF16),
        "al2": _block_diag2(a2[j, 0], a2[j, 1]).astype(BF16),
        "g1": g1[j].astype(BF16), "g2": g2[j].astype(BF16),
        "w0": w0[j].reshape(1, 2 * d), "a0": a0[j].reshape(1, 2 * d),
        "kk": kk[j].reshape(1, d), "ka": ka[j].reshape(1, d), "rk": rk[j].reshape(1, d),
        "lnw": lnw[j].reshape(1, d), "lnb": lnb[j].reshape(1, d),
    }
    if j > 0:
        p["v0"] = v0[j - 1].reshape(1, d)
        p["v1"] = _pad_cols(v1[j - 1], LANES).astype(BF16)
        p["v2"] = _pad_rows(v2[j - 1], LANES).astype(BF16)
    return p


def kernel(x_prompt, x_sample, c_prompt, c_sample, ada_w, ada_b, norm_g, final_g, rwkv_mu, rwkv_wr, rwkv_wk, rwkv_wv, rwkv_wo, rwkv_w0, rwkv_w1, rwkv_w2, rwkv_a0, rwkv_a1, rwkv_a2, rwkv_v0, rwkv_v1, rwkv_v2, rwkv_g1, rwkv_g2, rwkv_kk, rwkv_ka, rwkv_rk, rwkv_lnw, rwkv_lnb, sgu_win, sgu_bin, sgu_ng, sgu_nb, sgu_ws, sgu_bs, sgu_wout, sgu_bout, diff_wqkv, diff_wo, diff_lq1, diff_lk1, diff_lq2, diff_lk2, diff_subln, rel_bias, moe_wrg, moe_brg, moe_wre, moe_bre, moe_wg, moe_wu, moe_wd):
    d = D_MODEL
    nb_p, seq_len, _ = x_prompt.shape
    assert x_sample.shape[1] == seq_len
    t_p = nb_p * seq_len
    x = jnp.concatenate([x_prompt.reshape(-1, d), x_sample.reshape(-1, d)], axis=0)
    c = jnp.concatenate([c_prompt, c_sample], axis=0)
    mod = _modulation(c, ada_w, ada_b)
    zero_bias = jnp.zeros((1, d), F32)
    v_first = None
    for i in range(DEPTH):
        j = i // N_MIXERS
        mod_l = mod[i]
        if i % N_MIXERS == 0:
            p = _rwkv_params(j, rwkv_mu, rwkv_wr, rwkv_wk, rwkv_wv, rwkv_w0, rwkv_w1, rwkv_w2,
                             rwkv_a0, rwkv_a1, rwkv_a2, rwkv_v0, rwkv_v1, rwkv_v2, rwkv_g1, rwkv_g2,
                             rwkv_kk, rwkv_ka, rwkv_rk, rwkv_lnw, rwkv_lnb)
            r, k, v, g, kk, wlf, wlb, af, ab, bonus = _rwkv_pre(x, mod_l, norm_g[i, 0], p, v_first,
                                                                seq_len)
            if v_first is None:
                v_first = v
            y = _wkv(r, k, v, kk, wlf, wlb, af, ab, p["ka"], seq_len)
            pre = _rwkv_mid(y, v, g, bonus, p)
            w_out, b_out = rwkv_wo[j].astype(BF16), zero_bias
        elif i % N_MIXERS == 1:
            p = {"win": sgu_win[j].astype(BF16), "bin": sgu_bin[j].reshape(1, -1),
                 "ng": sgu_ng[j].reshape(1, -1), "nb": sgu_nb[j].reshape(1, -1),
                 "ws": sgu_ws[j].astype(BF16),
                 "bs": jnp.repeat(jnp.transpose(sgu_bs[j]), SGU_GC, axis=1)}
            pre = _sgu(x, mod_l, norm_g[i, 0], p, seq_len)
            w_out, b_out = sgu_wout[j].astype(BF16), sgu_bout[j].reshape(1, d)
        else:
            lambda_init = 0.8 - 0.6 * math.exp(-0.3 * i)
            lam = (jnp.exp(jnp.sum(diff_lq1[j] * diff_lk1[j])) - jnp.exp(jnp.sum(diff_lq2[j] * diff_lk2[j]))
                   + lambda_init)
            q, k, v = _qkv(x, mod_l, norm_g[i, 0], diff_wqkv[j].astype(BF16), seq_len)
            windows = _bias_windows(rel_bias, seq_len, min(ATTN_TQ, seq_len))
            pre = _diff_attention(q, k, v, windows, lam, diff_subln[j], lambda_init, seq_len)
            w_out, b_out = diff_wo[j].astype(BF16), zero_bias
        w_router = _pad_cols(jnp.concatenate([moe_wrg[i], moe_wre[i]], axis=1), ROUTER_COLS)
        b_router = _pad_cols(jnp.concatenate([moe_brg[i], moe_bre[i]])[None, :], ROUTER_COLS)
        x1, h2, logits = _post(pre, w_out, b_out, x, mod_l, norm_g[i, 1], w_router, b_router, seq_len)
        x = _moe(x1, h2, logits, mod_l, moe_wg, moe_wu, moe_wd, i, final_g, seq_len)
    return (x[:t_p].reshape(x_prompt.shape), x[t_p:].reshape(x_sample.shape))
```

```python
import functools
import math

import numpy as np
import jax
import jax.numpy as jnp
from jax import lax
from jax.experimental import pallas as pl
from jax.experimental.pallas import tpu as pltpu

F32 = jnp.float32
BF16 = jnp.bfloat16
HIGHEST = lax.Precision.HIGHEST

D_MODEL = 1024
DEPTH = 4
N_MIXERS = 3
NORM_EPS = 1e-6
LANES = 128
VMEM_LIMIT_BYTES = 56 * 1024 * 1024

RWKV_HEAD = 64
RWKV_HEADS = D_MODEL // RWKV_HEAD
RWKV_PAIRS = D_MODEL // LANES
RWKV_GN_EPS = 64e-5
WKV_CHUNK = 64
WKV_PAIRS_PER_STEP = 2
WKV_CHUNKS_PER_STEP = 4

SGU_CHUNK = 128
SGU_DIM = 2 * D_MODEL
SGU_GROUPS = 8
SGU_GC = SGU_DIM // SGU_GROUPS

DIFF_HEADS = 8
DIFF_HEAD = 64
NUM_BUCKETS = 32
MAX_DISTANCE = 128
ATTN_TQ = 512
ATTN_ROW_PARTS = 2

MOE_GROUPS = 4
EXPERTS_PER_GROUP = 8
N_EXPERTS = MOE_GROUPS * EXPERTS_PER_GROUP
MOE_TOP_K = 2
EXPERT_DIM = D_MODEL // 2
MOE_BLOCK = 512
ROUTER_COLS = LANES
ROUTE_TILE = 2048
RANK_TILE = 512
RANK_ROWS = 8
MOE_SLICES = 4

TOKEN_TILE = 256
WIDE_TILE = 512


def _params(*semantics):
    return pltpu.CompilerParams(dimension_semantics=semantics, vmem_limit_bytes=VMEM_LIMIT_BYTES)


def _resident(shape):
    zeros = (0,) * len(shape)
    return pl.BlockSpec(shape, lambda *_: zeros, pipeline_mode=pl.Buffered(1))


def _dot(a, b, precision=None):
    return jnp.dot(a, b, preferred_element_type=F32, precision=precision)


def _dot_nt(a, b, precision=None):
    return lax.dot_general(a, b, (((1,), (1,)), ((), ())), preferred_element_type=F32,
                           precision=precision)


def _dot_tn(a, b, precision=None):
    return lax.dot_general(a, b, (((0,), (0,)), ((), ())), preferred_element_type=F32,
                           precision=precision)


def _split_bf16(x):
    hi = x.astype(BF16)
    return hi, (x - hi.astype(F32)).astype(BF16)


def _head_sums(x, hsum, hexp):
    hi, lo = _split_bf16(x)
    shi, slo = _split_bf16(_dot(hi, hsum) + _dot(lo, hsum))
    return _dot(shi, hexp) + _dot(slo, hexp)


def _sigmoid(x):
    return 1.0 / (1.0 + jnp.exp(-x))


def _norm_mod(x, g, scale, shift, eps=NORM_EPS):
    y = x * lax.rsqrt(jnp.mean(x * x, axis=-1, keepdims=True) + eps)
    return (y * g) * (1.0 + scale) + shift


def _mod_kernel(c_ref, w_ref, b_ref, o_ref):
    c = c_ref[...]
    o_ref[...] = _dot(c * _sigmoid(c), w_ref[...], HIGHEST) + b_ref[...]


def _modulation(c, ada_w, ada_b):
    nb = c.shape[0]
    depth, d, n = ada_w.shape
    out = pl.pallas_call(
        _mod_kernel,
        grid=(depth, n // d),
        in_specs=[pl.BlockSpec((nb, d), lambda l, j: (0, 0)),
                  pl.BlockSpec((None, d, d), lambda l, j: (l, 0, j)),
                  pl.BlockSpec((None, 1, d), lambda l, j: (l, 0, j))],
        out_specs=pl.BlockSpec((None, nb, d), lambda l, j: (l, 0, j)),
        out_shape=jax.ShapeDtypeStruct((depth, nb, n), F32),
        compiler_params=_params("parallel", "parallel"),
        name="adaln_modulation",
    )(c, ada_w, ada_b.reshape(depth, 1, n))
    return out.reshape(depth, nb, n // d, d)


def _rwkv_pre_kernel(*refs, tm, tiles_per_seq, has_vres):
    if has_vres:
        (x_ref, xp_ref, xn_ref, mod_ref, ng_ref, mu_ref, wr_ref, wk_ref, wv_ref, wl1_ref, wl2_ref,
         al1_ref, al2_ref, g1_ref, g2_ref, w0_ref, a0_ref, kscale_ref, ka_ref, rk_ref, hsum_ref,
         hexp_ref, vf_ref, v0_ref, v1_ref, v2_ref,
         r_ref, k_ref, v_ref, g_ref, kk_ref, wlf_ref, wlb_ref, af_ref, ab_ref, bonus_ref) = refs
    else:
        (x_ref, xp_ref, xn_ref, mod_ref, ng_ref, mu_ref, wr_ref, wk_ref, wv_ref, wl1_ref, wl2_ref,
         al1_ref, al2_ref, g1_ref, g2_ref, w0_ref, a0_ref, kscale_ref, ka_ref, rk_ref, hsum_ref,
         hexp_ref,
         r_ref, k_ref, v_ref, g_ref, kk_ref, wlf_ref, wlb_ref, af_ref, ab_ref, bonus_ref) = refs
    d = D_MODEL
    i = pl.program_id(0)
    pos = i % tiles_per_seq
    shift, scale = mod_ref[0:1, :], mod_ref[1:2, :]
    ng = ng_ref[...]
    h = _norm_mod(x_ref[...], ng, scale, shift)
    h_prev = jnp.where(pos == 0, 0.0, _norm_mod(xp_ref[7:8, :], ng, scale, shift))
    h_next = jnp.where(pos == tiles_per_seq - 1, 0.0, _norm_mod(xn_ref[0:1, :], ng, scale, shift))
    row = lax.broadcasted_iota(jnp.int32, (tm, 1), 0)
    h_up = jnp.where(row == 0, h_prev, pltpu.roll(h, 1, 0))
    h_dn = jnp.where(row == tm - 1, h_next, pltpu.roll(h, tm - 1, 0))
    xx = 0.5 * (h_up + h_dn) - h

    def mix(n):
        return (h + xx * mu_ref[n:n + 1, :]).astype(BF16)

    xr, xw, xk, xv, xa, xg = [mix(n) for n in range(6)]
    r = _dot(xr, wr_ref[...])
    r_ref[...] = r.astype(BF16)
    k = _dot(xk, wk_ref[...])
    k_ref[...] = k.astype(BF16)
    kkr = k * kscale_ref[...]
    norm = jnp.sqrt(_head_sums(kkr * kkr, hsum_ref[...], hexp_ref[...]))
    kk_ref[...] = (kkr / jnp.maximum(norm, 1e-12)).astype(BF16)
    v = _dot(xv, wv_ref[...])
    if has_vres:
        lv = _dot(_dot(xv, v1_ref[...]).astype(BF16), v2_ref[...])
        v = v + (vf_ref[...].astype(F32) - v) * _sigmoid(v0_ref[...] + lv)
    v_ref[...] = v.astype(BF16)
    g_ref[...] = _dot(_sigmoid(_dot(xg, g1_ref[...])).astype(BF16), g2_ref[...]).astype(BF16)
    wl = w0_ref[...] + _dot(jnp.tanh(_dot(xw, wl1_ref[...])).astype(BF16), wl2_ref[...])
    logw = -math.exp(-0.5) * _sigmoid(wl)
    wlf_ref[...] = logw[:, :d]
    wlb_ref[...] = logw[:, d:]
    a = _sigmoid(a0_ref[...] + _dot(_dot(xa, al1_ref[...]).astype(BF16), al2_ref[...]))
    af_ref[...] = a[:, :d].astype(BF16)
    ab_ref[...] = a[:, d:].astype(BF16)
    k_bonus = k * (1.0 + (0.5 * (a[:, :d] + a[:, d:]) - 1.0) * ka_ref[...])
    hi, lo = _split_bf16(r * k_bonus * rk_ref[...])
    bonus_ref[...] = _dot(hi, hsum_ref[...]) + _dot(lo, hsum_ref[...])


def _head_sum_matrix():
    head_of_lane = np.arange(D_MODEL) // RWKV_HEAD
    return jnp.asarray(head_of_lane[:, None] == np.arange(RWKV_HEADS)[None, :], BF16)


def _block_diag2(m0, m1):
    z = jnp.zeros_like(m0)
    return jnp.concatenate([jnp.concatenate([m0, z], 1), jnp.concatenate([z, m1], 1)], 0)


def _rwkv_pre(x, mod_l, norm_g, p, v_first, seq_len):
    t, d = x.shape
    tm = min(WIDE_TILE, seq_len)
    tiles_per_seq = seq_len // tm
    n_tiles = t // tm
    has_vres = v_first is not None
    tile = pl.BlockSpec((tm, d), lambda i: (i, 0))
    rows8 = tm // 8
    last8 = t // 8 - 1
    in_specs = [
        tile,
        pl.BlockSpec((8, d), lambda i: (jnp.maximum(i * rows8 - 1, 0), 0)),
        pl.BlockSpec((8, d), lambda i: (jnp.minimum((i + 1) * rows8, last8), 0)),
        pl.BlockSpec((None, 6, d), lambda i: (i // tiles_per_seq, 0, 0)),
        _resident((1, d)), _resident((6, d)),
        _resident((d, d)), _resident((d, d)), _resident((d, d)),
        _resident((d, LANES)), _resident((LANES, 2 * d)),
        _resident((d, LANES)), _resident((LANES, 2 * d)),
        _resident((d, LANES)), _resident((LANES, d)),
        _resident((1, 2 * d)), _resident((1, 2 * d)),
        _resident((1, d)), _resident((1, d)), _resident((1, d)),
        _resident((d, RWKV_HEADS)), _resident((RWKV_HEADS, d)),
    ]
    hsum = _head_sum_matrix()
    args = [x, x, x, mod_l, norm_g.reshape(1, d), p["mu"],
            p["wr"], p["wk"], p["wv"], p["wl1"], p["wl2"], p["al1"], p["al2"], p["g1"], p["g2"],
            p["w0"], p["a0"], p["kk"], p["ka"], p["rk"], hsum, hsum.T]
    if has_vres:
        in_specs += [tile, _resident((1, d)), _resident((d, LANES)), _resident((LANES, d))]
        args += [v_first, p["v0"], p["v1"], p["v2"]]
    dtypes = [BF16] * 5 + [F32] * 2 + [BF16] * 2
    outs = pl.pallas_call(
        functools.partial(_rwkv_pre_kernel, tm=tm, tiles_per_seq=tiles_per_seq, has_vres=has_vres),
        grid=(n_tiles,),
        in_specs=in_specs,
        out_specs=[tile] * 9 + [pl.BlockSpec((tm, RWKV_HEADS), lambda i: (i, 0))],
        out_shape=[jax.ShapeDtypeStruct((t, d), dt) for dt in dtypes]
        + [jax.ShapeDtypeStruct((t, RWKV_HEADS), F32)],
        compiler_params=_params("parallel"),
        name="rwkv_pre",
    )(*args)
    return outs


def _wkv_kernel(r_ref, k_ref, v_ref, kk_ref, wlf_ref, wlb_ref, af_ref, ab_ref, ka_ref,
                lvl_ref, y_ref, state_ref, g_ref, sp_ref, yp_ref, xr_ref, dec_ref,
                *, seq_len, chunk, pairs, group):
    n_chunks = seq_len // chunk
    n_levels = int(math.log2(chunk))
    two = 2 * chunk
    lane = lax.broadcasted_iota(jnp.int32, (1, LANES), 1)
    head0 = lane < RWKV_HEAD
    trow = lax.broadcasted_iota(jnp.int32, (chunk, 1), 0)
    si = lax.broadcasted_iota(jnp.int32, (two, two), 0)
    sj = lax.broadcasted_iota(jnp.int32, (two, two), 1)
    same_head = (si < chunk) == (sj < chunk)
    st, su = si & (chunk - 1), sj & (chunk - 1)
    eye = (si == sj).astype(F32)
    chain_defs = [(pair, reverse) for pair in range(pairs) for reverse in (False, True)]

    def rows_of(idx, reverse):
        pos = (n_chunks - 1 - idx) if reverse else idx
        return pl.ds(pl.multiple_of(pos * chunk, chunk), chunk)

    def stack(x):
        return jnp.concatenate([jnp.where(head0, x, 0.0), jnp.where(head0, 0.0, x)], axis=0)

    def a_load(q, g, idx, slot):
        pair, reverse = chain_defs[q]
        rows = rows_of(idx, reverse)
        lanes = slice(pair * LANES, (pair + 1) * LANES)
        a = dict(q=q, g=g, slot=slot, rows=rows, lanes=lanes, reverse=reverse)
        a["lw"] = lw = (wlb_ref if reverse else wlf_ref)[rows, lanes]
        a["a"] = (ab_ref if reverse else af_ref)[rows, lanes].astype(F32)
        logp = lw
        step = 1
        while step < chunk:
            if reverse:
                logp = logp + jnp.where(trow < chunk - step, pltpu.roll(logp, chunk - step, 0), 0.0)
            else:
                logp = logp + jnp.where(trow >= step, pltpu.roll(logp, step, 0), 0.0)
            step *= 2
        a["logp"] = logp
        return a

    def a_scale(a):
        rows, lanes, logp, lw, av = a["rows"], a["lanes"], a["logp"], a["lw"], a["a"]
        k = k_ref[rows, lanes].astype(F32)
        kk = kk_ref[rows, lanes].astype(F32)
        kd = k * (1.0 + (av - 1.0) * ka_ref[:, lanes])
        p_in = jnp.exp(logp)
        p_out = jnp.exp(-logp)
        a["xa"] = stack(-kk * jnp.exp(logp - lw)).astype(BF16)
        a["xr"] = stack(r_ref[rows, lanes].astype(F32) * p_in).astype(BF16)
        a["yb"] = stack(kk * av * p_out).astype(BF16)
        a["yk"] = stack(kd * p_out).astype(BF16)
        a["vs"] = stack(v_ref[rows, lanes])
        a["decay"] = jnp.exp(jnp.sum(lw, axis=0, keepdims=True))

    def a_scores(a):
        if a["reverse"]:
            strict, incl = same_head & (st < su), same_head & (st <= su)
        else:
            strict, incl = same_head & (st > su), same_head & (st >= su)
        sc = _dot_nt(jnp.concatenate([a["xa"], a["xr"]], 0),
                     jnp.concatenate([a["yb"], a["yk"]], 0))
        a["m"] = sc[:two, :two].astype(BF16)
        m_ak = jnp.where(strict, sc[:two, two:], 0.0).astype(BF16)
        n_rk = jnp.where(incl, sc[two:, two:], 0.0).astype(BF16)
        a["n_rb"] = jnp.where(incl, sc[two:, :two], 0.0).astype(BF16)
        a["inv"] = eye + (a["m"] * lvl_ref[int(a["reverse"]), 0]).astype(F32)
        a["zp"] = _dot(m_ak, a["vs"]).astype(BF16)
        a["yp"] = _dot(n_rk, a["vs"])
        a["sp"] = _dot_tn(a["vs"], a["yk"])

    def a_double(a, level):
        t = a["inv"].astype(BF16)
        m_n = a["m"] * lvl_ref[int(a["reverse"]), level]
        a["inv"] = a["inv"] + _dot(_dot(t, m_n).astype(BF16), t)

    def a_solve(a):
        t = a["inv"].astype(BF16)
        a["w"] = _dot(t, a["xa"]).astype(BF16)
        a["u0"] = _dot(t, a["zp"]).astype(BF16)

    def a_fold(a):
        at = (a["slot"], a["g"], a["q"])
        g_ref[at] = _dot_tn(a["w"], a["yb"]).astype(BF16)
        sp_ref[at] = a["sp"] + _dot_tn(a["u0"], a["yb"])
        yp_ref[at] = a["yp"] + _dot(a["n_rb"], a["u0"])
        xr_ref[at] = (a["xr"].astype(F32) + _dot(a["n_rb"], a["w"])).astype(BF16)
        dec_ref[at] = a["decay"]

    a_stage_list = ([a_scale, a_scores]
                    + [functools.partial(a_double, level=level) for level in range(1, n_levels)]
                    + [a_solve, a_fold])

    def b_state(b):
        at = b["at"]
        s0 = state_ref[b["q"]]
        b["s0b"] = s0.astype(BF16)
        state_ref[b["q"]] = (s0 + _dot(b["s0b"], g_ref[at]) + sp_ref[at]) * dec_ref[at]

    def b_out(b):
        at = b["at"]
        pair, reverse = chain_defs[b["q"]]
        ys = _dot_nt(xr_ref[at], b["s0b"]) + yp_ref[at]
        y_ref[rows_of(b["idx"], reverse), pair * LANES:(pair + 1) * LANES] += ys[:chunk] + ys[chunk:]

    b_stage_list = [b_state, b_out]

    y_ref[...] = jnp.zeros_like(y_ref)
    state_ref[...] = jnp.zeros_like(state_ref)
    n_chains = len(chain_defs)
    first = [a_load(q, g, g, 0) for g in range(group) for q in range(n_chains)]
    for stage in a_stage_list:
        for a in first:
            stage(a)

    def run_step(step, prepare_next):
        slot = step & 1
        base = step * group
        b_work = []
        for g in range(group):
            now = [dict(q=q, at=(slot, g, q), idx=base + g) for q in range(n_chains)]
            b_work += [(stage, now) for stage in b_stage_list]
        a_work = []
        if prepare_next:
            ahead = [a_load(q, g, base + group + g, 1 - slot)
                     for g in range(group) for q in range(n_chains)]
            a_work = [(stage, ahead) for stage in a_stage_list]
        per_a = -(-len(b_work) // max(len(a_work), 1))
        while a_work or b_work:
            for stage, items in b_work[:per_a]:
                for b in items:
                    stage(b)
            b_work = b_work[per_a:]
            if a_work:
                stage, items = a_work.pop(0)
                for a in items:
                    stage(a)

    n_steps = n_chunks // group

    def body(step, carry):
        run_step(step, True)
        return carry

    lax.fori_loop(0, n_steps - 1, body, 0)
    run_step(n_steps - 1, False)


def _wkv_level_masks(chunk):
    s = np.arange(2 * chunk)
    head, t = s // chunk, s % chunk
    n_levels = int(math.log2(chunk))
    masks = np.zeros((2, n_levels, 2 * chunk, 2 * chunk), np.float32)
    for d in range(2):
        pos = t if d == 0 else chunk - 1 - t
        for j in range(n_levels):
            blk = pos >> j
            masks[d, j] = ((head[:, None] == head[None, :]) & (blk[:, None] % 2 == 1)
                           & (blk[None, :] == blk[:, None] - 1))
    return masks


def _wkv(r, k, v, kk, wlf, wlb, af, ab, k_a, seq_len):
    t, d = r.shape
    nb = t // seq_len
    pairs = WKV_PAIRS_PER_STEP
    width = pairs * LANES
    n_chains = 2 * pairs
    seq = pl.BlockSpec((seq_len, width), lambda b, p: (b, p))
    par = pl.BlockSpec((1, width), lambda b, p: (0, p))
    levels = jnp.asarray(_wkv_level_masks(WKV_CHUNK), BF16)
    group = WKV_CHUNKS_PER_STEP
    parked_f32 = pltpu.VMEM((2, group, n_chains, LANES, LANES), F32)
    parked_bf16 = pltpu.VMEM((2, group, n_chains, LANES, LANES), BF16)
    return pl.pallas_call(
        functools.partial(_wkv_kernel, seq_len=seq_len, chunk=WKV_CHUNK, pairs=pairs, group=group),
        grid=(nb, d // width),
        in_specs=[seq] * 8 + [par, _resident(levels.shape)],
        out_specs=seq,
        out_shape=jax.ShapeDtypeStruct((t, d), F32),
        scratch_shapes=[pltpu.VMEM((n_chains, LANES, LANES), F32), parked_bf16, parked_f32,
                        parked_f32, parked_bf16, pltpu.VMEM((2, group, n_chains, 1, LANES), F32)],
        compiler_params=_params("parallel", "parallel"),
        name="wkv7_chunked",
    )(r, k, v, kk, wlf, wlb, af, ab, k_a.reshape(1, d), levels)


def _rwkv_mid_kernel(y_ref, v_ref, g_ref, bonus_ref, lnw_ref, lnb_ref, hsum_ref, hexp_ref, o_ref):
    hsum = hsum_ref[...]
    hexp = hexp_ref[...]

    def head_sum(x):
        return _head_sums(x, hsum, hexp)

    y = y_ref[...]
    yc = y - head_sum(y) * (1.0 / RWKV_HEAD)
    var = head_sum(yc * yc) * (1.0 / RWKV_HEAD)
    yn = yc * lax.rsqrt(var + RWKV_GN_EPS) * lnw_ref[...] + lnb_ref[...]
    b_hi, b_lo = _split_bf16(bonus_ref[...])
    bonus = (_dot(b_hi, hexp) + _dot(b_lo, hexp)) * v_ref[...].astype(F32)
    o_ref[...] = ((yn + bonus) * g_ref[...].astype(F32)).astype(BF16)


def _rwkv_mid(y, v, g, bonus, p):
    t, d = y.shape
    tm = TOKEN_TILE
    tile = pl.BlockSpec((tm, d), lambda i: (i, 0))
    vec = _resident((1, d))
    hsum = _head_sum_matrix()
    return pl.pallas_call(
        _rwkv_mid_kernel,
        grid=(t // tm,),
        in_specs=[tile] * 3 + [pl.BlockSpec((tm, RWKV_HEADS), lambda i: (i, 0))] + [vec] * 2
        + [_resident((d, RWKV_HEADS)), _resident((RWKV_HEADS, d))],
        out_specs=tile,
        out_shape=jax.ShapeDtypeStruct((t, d), BF16),
        compiler_params=_params("parallel"),
        name="rwkv_mid",
    )(y, v, g, bonus, p["lnw"], p["lnb"], hsum, hsum.T)


def _sgu_kernel(x_ref, mod_ref, ng_ref, win_ref, bin_ref, lg_ref, lb_ref, ws_ref, bs_ref, o_ref,
                *, tm):
    h = _norm_mod(x_ref[...], ng_ref[...], mod_ref[1:2, :], mod_ref[0:1, :]).astype(BF16)
    z = _dot(h, win_ref[...]) + bin_ref[...]
    z = 0.5 * z * (1.0 + lax.erf(z * (1.0 / math.sqrt(2.0))))
    u = z[:, :SGU_DIM]
    v = z[:, SGU_DIM:]
    vc = v - jnp.mean(v, axis=-1, keepdims=True)
    vn = vc * lax.rsqrt(jnp.mean(vc * vc, axis=-1, keepdims=True) + 1e-5) * lg_ref[...] + lb_ref[...]
    vb = vn.astype(BF16)
    for c in range(tm // SGU_CHUNK):
        rows = slice(c * SGU_CHUNK, (c + 1) * SGU_CHUNK)
        for g in range(SGU_GROUPS):
            cols = slice(g * SGU_GC, (g + 1) * SGU_GC)
            s = _dot(ws_ref[g], vb[rows, cols]) + bs_ref[:, cols]
            o_ref[rows, cols] = (u[rows, cols] * s).astype(BF16)


def _sgu(x, mod_l, norm_g, p, seq_len):
    t, d = x.shape
    tm = min(WIDE_TILE, seq_len)
    tiles_per_seq = seq_len // tm
    return pl.pallas_call(
        functools.partial(_sgu_kernel, tm=tm),
        grid=(t // tm,),
        in_specs=[pl.BlockSpec((tm, d), lambda i: (i, 0)),
                  pl.BlockSpec((None, 6, d), lambda i: (i // tiles_per_seq, 0, 0)),
                  _resident((1, d)),
                  _resident((d, 2 * SGU_DIM)), _resident((1, 2 * SGU_DIM)),
                  _resident((1, SGU_DIM)), _resident((1, SGU_DIM)),
                  _resident((SGU_GROUPS, SGU_CHUNK, SGU_CHUNK)), _resident((SGU_CHUNK, SGU_DIM))],
        out_specs=pl.BlockSpec((tm, SGU_DIM), lambda i: (i, 0)),
        out_shape=jax.ShapeDtypeStruct((t, SGU_DIM), BF16),
        compiler_params=_params("parallel"),
        name="sgu",
    )(x, mod_l, norm_g.reshape(1, d), p["win"], p["bin"], p["ng"], p["nb"], p["ws"], p["bs"])


def _qkv_kernel(x_ref, mod_ref, ng_ref, w_ref, q_ref, k_ref, v_ref):
    d = D_MODEL
    h = _norm_mod(x_ref[...], ng_ref[...], mod_ref[1:2, :], mod_ref[0:1, :]).astype(BF16)
    qkv = _dot(h, w_ref[...])
    q_ref[...] = (qkv[:, :d] * (DIFF_HEAD ** -0.5)).astype(BF16)
    k_ref[...] = qkv[:, d:2 * d].astype(BF16)
    v_ref[...] = qkv[:, 2 * d:].astype(BF16)


def _qkv(x, mod_l, norm_g, w_qkv, seq_len):
    t, d = x.shape
    tm = TOKEN_TILE
    tiles_per_seq = seq_len // tm
    tile = pl.BlockSpec((tm, d), lambda i: (i, 0))
    return pl.pallas_call(
        _qkv_kernel,
        grid=(t // tm,),
        in_specs=[tile, pl.BlockSpec((None, 6, d), lambda i: (i // tiles_per_seq, 0, 0)),
                  _resident((1, d)), _resident((d, 3 * d))],
        out_specs=[tile] * 3,
        out_shape=[jax.ShapeDtypeStruct((t, d), BF16)] * 3,
        compiler_params=_params("parallel"),
        name="diff_qkv",
    )(x, mod_l, norm_g.reshape(1, d), w_qkv)


def _attn_kernel(q_ref, k_ref, v_ref, win_ref, lam_ref, sg_ref, o_ref, vext_ref, *, tq, seq_len,
                 out_scale):
    @pl.when(pl.program_id(2) == 0)
    def _():
        col = lax.broadcasted_iota(jnp.int32, (seq_len, LANES), 1)
        vext_ref[:, :LANES] = v_ref[...]
        vext_ref[:, LANES:] = jnp.where(col == 0, 1.0, 0.0).astype(BF16)

    lane = lax.broadcasted_iota(jnp.int32, (1, LANES), 1)
    width = seq_len + tq
    window = jnp.broadcast_to(win_ref[...], (tq, width))
    bias = pltpu.roll(window, width - tq + 1, 1, stride=1, stride_axis=0)[:, :seq_len]

    def unnormalised(x):
        e = jnp.exp(x - jnp.max(x, axis=-1, keepdims=True)).astype(BF16)
        pv = _dot(e, vext_ref[...])
        return pv[:, :LANES], pv[:, LANES:LANES + 1]

    part = tq // ATTN_ROW_PARTS
    for r in range(ATTN_ROW_PARTS):
        rows = slice(r * part, (r + 1) * part)
        q = q_ref[rows, :]
        zero = jnp.zeros_like(q)
        qs = jnp.concatenate([jnp.where(lane < DIFF_HEAD, q, zero),
                              jnp.where(lane < DIFF_HEAD, zero, q)], axis=0)
        s = _dot_nt(qs, k_ref[...])
        o1, l1 = unnormalised(s[:part] + bias[rows])
        o2, l2 = unnormalised(s[part:] + bias[rows])
        o = o1 / l1 - lam_ref[...] * (o2 / l2)
        o = o * lax.rsqrt(jnp.mean(o * o, axis=-1, keepdims=True) + 1e-5) * sg_ref[...]
        o_ref[rows, :] = (o * out_scale).astype(BF16)


def _t5_bucket(rel):
    nb = NUM_BUCKETS // 2
    max_exact = nb // 2
    ret = jnp.where(rel > 0, nb, 0)
    n = jnp.abs(rel)
    nf = jnp.maximum(n, 1).astype(F32)
    large = max_exact + (jnp.log(nf / max_exact) / math.log(MAX_DISTANCE / max_exact)
                         * (nb - max_exact)).astype(jnp.int32)
    large = jnp.minimum(large, nb - 1)
    return ret + jnp.where(n < max_exact, n, large)


def _bias_windows(rel_bias, seq_len, tq):
    nqb = seq_len // tq
    j = jnp.arange(seq_len + tq, dtype=jnp.int32)[None, :]
    q_hi = (jnp.arange(nqb, dtype=jnp.int32)[:, None] + 1) * tq
    bucket = _t5_bucket(j - q_hi + 1)
    return jnp.transpose(rel_bias[bucket], (0, 2, 1))[:, :, None, :].astype(F32)


def _diff_attention(q, k, v, windows, lam, subln_g, lambda_init, seq_len):
    t, d = q.shape
    nb = t // seq_len
    tq = min(ATTN_TQ, seq_len)
    nqb = seq_len // tq
    kv = pl.BlockSpec((seq_len, LANES), lambda b, h, i: (b, h))
    qo = pl.BlockSpec((tq, LANES), lambda b, h, i: (b * nqb + i, h))
    return pl.pallas_call(
        functools.partial(_attn_kernel, tq=tq, seq_len=seq_len, out_scale=1.0 - lambda_init),
        grid=(nb, DIFF_HEADS, nqb),
        in_specs=[qo, kv, kv,
                  pl.BlockSpec((None, None, 1, seq_len + tq), lambda b, h, i: (i, h, 0, 0)),
                  pl.BlockSpec((1, 1), lambda b, h, i: (0, 0)),
                  pl.BlockSpec((1, LANES), lambda b, h, i: (0, 0))],
        out_specs=qo,
        out_shape=jax.ShapeDtypeStruct((t, d), BF16),
        scratch_shapes=[pltpu.VMEM((seq_len, 2 * LANES), BF16)],
        compiler_params=_params("parallel", "parallel", "arbitrary"),
        name="diff_attention",
    )(q, k, v, windows, lam.reshape(1, 1), subln_g.reshape(1, LANES))


def _post_kernel(pre_ref, w_ref, b_ref, x_ref, mod_ref, ng_ref, wr_ref, br_ref,
                 x1_ref, h2_ref, lg_ref):
    out = _dot(pre_ref[...], w_ref[...]) + b_ref[...]
    x1 = x_ref[...] + mod_ref[2:3, :] * out
    x1_ref[...] = x1
    h2 = _norm_mod(x1, ng_ref[...], mod_ref[4:5, :], mod_ref[3:4, :])
    h2_hi, h2_lo = _split_bf16(h2)
    h2_ref[...] = h2_hi
    wide = _dot(h2_hi, wr_ref[...])
    lg_ref[...] = (wide[:, :ROUTER_COLS] + (wide[:, ROUTER_COLS:] + _dot(h2_lo, wr_ref[:, :ROUTER_COLS]))
                   + br_ref[...])


def _route(lg):
    lane = lax.broadcasted_iota(jnp.int32, lg.shape, 1)
    first_e, end_e = MOE_GROUPS, MOE_GROUPS + N_EXPERTS
    neg = -jnp.inf

    def max_and_first(x):
        m = jnp.max(x, axis=-1, keepdims=True)
        return m, jnp.min(jnp.where(x == m, lane, LANES), axis=-1, keepdims=True)

    is_group = lane < first_e
    g_max, grp = max_and_first(jnp.where(is_group, lg, neg))
    p_grp = 1.0 / jnp.sum(jnp.where(is_group, jnp.exp(lg - g_max), 0.0), axis=-1, keepdims=True)
    lane_grp = (lane - first_e) >> int(math.log2(EXPERTS_PER_GROUP))
    in_grp = (lane >= first_e) & (lane < end_e) & (lane_grp == grp)
    cand = jnp.where(in_grp, lg, neg)
    v1, i1 = max_and_first(cand)
    v2, i2 = max_and_first(jnp.where(lane == i1, neg, cand))
    p2 = jnp.exp(v2 - v1)
    gate1 = p_grp / (1.0 + p2)
    gate2 = gate1 * p2
    e1 = (i1 - first_e).astype(F32)
    e2 = (i2 - first_e).astype(F32)
    return jnp.where(lane == 0, e1, jnp.where(lane == 1, e2, jnp.where(lane == 2, gate1,
                                                                    jnp.where(lane == 3, gate2, 0.0))))


def _route_kernel(lg_ref, route_ref, route_t_ref):
    route = _route(lg_ref[...])
    route_ref[...] = route
    route_t_ref[...] = route.T[:8, :]


def _routing(logits):
    t = logits.shape[0]
    tr = max(k for k in range(LANES, ROUTE_TILE + 1, LANES) if t % k == 0)
    return pl.pallas_call(
        _route_kernel,
        grid=(t // tr,),
        in_specs=[pl.BlockSpec((tr, ROUTER_COLS), lambda i: (i, 0))],
        out_specs=[pl.BlockSpec((tr, ROUTER_COLS), lambda i: (i, 0)),
                   pl.BlockSpec((None, 8, tr), lambda i: (i, 0, 0))],
        out_shape=[jax.ShapeDtypeStruct((t, ROUTER_COLS), F32),
                   jax.ShapeDtypeStruct((t // tr, 8, tr), F32)],
        compiler_params=_params("parallel"),
        name="moe_route",
    )(logits)


def _post(pre, w, b, x, mod_l, norm_g2, w_router, b_router, seq_len):
    t, d = x.shape
    din = pre.shape[1]
    tm = min(WIDE_TILE, seq_len)
    tiles_per_seq = seq_len // tm
    tile = pl.BlockSpec((tm, d), lambda i: (i, 0))
    return pl.pallas_call(
        _post_kernel,
        grid=(t // tm,),
        in_specs=[pl.BlockSpec((tm, din), lambda i: (i, 0)), _resident((din, d)), _resident((1, d)),
                  tile, pl.BlockSpec((None, 6, d), lambda i: (i // tiles_per_seq, 0, 0)),
                  _resident((1, d)), _resident((d, 2 * ROUTER_COLS)),
                  _resident((1, ROUTER_COLS))],
        out_specs=[tile, tile, pl.BlockSpec((tm, ROUTER_COLS), lambda i: (i, 0))],
        out_shape=[jax.ShapeDtypeStruct((t, d), F32), jax.ShapeDtypeStruct((t, d), BF16),
                   jax.ShapeDtypeStruct((t, ROUTER_COLS), F32)],
        compiler_params=_params("parallel"),
        name="post_router",
    )(pre, w, b, x, mod_l, norm_g2.reshape(1, d), jnp.concatenate(_split_bf16(w_router), axis=1), b_router)


def _expert_kernel(be_ref, nu_ref, x_ref, wg_ref, wu_ref, wd_ref, *rest, first_block):
    o_ref, wgb_ref, wub_ref, wdb_ref = rest[-4:]
    i = pl.program_id(0)
    blk = first_block + i
    used = blk < nu_ref[0]
    new_expert = (i == 0) | (be_ref[blk] != be_ref[jnp.maximum(blk - 1, 0)])

    @pl.when(used & new_expert)
    def _():
        wgb_ref[...] = wg_ref[...].astype(BF16)
        wub_ref[...] = wu_ref[...].astype(BF16)
        wdb_ref[...] = wd_ref[...].astype(BF16)

    @pl.when(used)
    def _():
        x = x_ref[...]
        hg = _dot(x, wgb_ref[...])
        hu = _dot(x, wub_ref[...])
        act = hg * _sigmoid(hg) * hu
        o_ref[...] = _dot(act.astype(BF16), wdb_ref[...]).astype(BF16)

    @pl.when(jnp.logical_not(used))
    def _():
        o_ref[...] = jnp.zeros_like(o_ref)


def _experts(xs, block_e, n_used, wg, wu, wd, layer, ys_buf, first_block, total_blocks):
    p_rows, d = xs.shape
    n_blocks = p_rows // MOE_BLOCK

    def weight(shape):
        return pl.BlockSpec((None, None) + shape, lambda i, be, nu: (layer, be[first_block + i], 0, 0))

    in_specs = [pl.BlockSpec((MOE_BLOCK, d), lambda i, be, nu: (i, 0)),
                weight((d, EXPERT_DIM)), weight((d, EXPERT_DIM)), weight((EXPERT_DIM, d))]
    args = [block_e, n_used, xs, wg, wu, wd]
    aliases = {}
    if ys_buf is not None:
        in_specs.append(pl.BlockSpec(memory_space=pl.ANY))
        aliases = {len(args): 0}
        args.append(ys_buf)
    grid_spec = pltpu.PrefetchScalarGridSpec(
        num_scalar_prefetch=2,
        grid=(n_blocks,),
        in_specs=in_specs,
        out_specs=pl.BlockSpec((MOE_BLOCK, d), lambda i, be, nu: (first_block + i, 0)),
        scratch_shapes=[pltpu.VMEM((d, EXPERT_DIM), BF16), pltpu.VMEM((d, EXPERT_DIM), BF16),
                        pltpu.VMEM((EXPERT_DIM, d), BF16)],
    )
    return pl.pallas_call(
        functools.partial(_expert_kernel, first_block=first_block),
        grid_spec=grid_spec,
        out_shape=jax.ShapeDtypeStruct((total_blocks * MOE_BLOCK, d), BF16),
        input_output_aliases=aliases,
        compiler_params=_params("arbitrary"),
        name="moe_experts",
    )(*args)


def _combine_kernel(x_ref, y0_ref, y1_ref, route_ref, mod_ref, fg_ref, o_ref, *, final):
    route = route_ref[...]
    moe = route[:, 2:3] * y0_ref[...].astype(F32) + route[:, 3:4] * y1_ref[...].astype(F32)
    x = x_ref[...] + mod_ref[5:6, :] * moe
    if final:
        x = x * lax.rsqrt(jnp.mean(x * x, axis=-1, keepdims=True) + NORM_EPS) * fg_ref[...]
    o_ref[...] = x


def _combine(x_buf, y0, y1, route, mod_l, final_g, final, seq_len, first_tile):
    t, d = x_buf.shape
    tm = min(WIDE_TILE, seq_len)
    tiles_per_seq = seq_len // tm
    here = pl.BlockSpec((tm, d), lambda i: (first_tile + i, 0))
    local = pl.BlockSpec((tm, d), lambda i: (i, 0))
    return pl.pallas_call(
        functools.partial(_combine_kernel, final=final),
        grid=(y0.shape[0] // tm,),
        in_specs=[here, local, local, pl.BlockSpec((tm, ROUTER_COLS), lambda i: (first_tile + i, 0)),
                  pl.BlockSpec((None, 6, d), lambda i: ((first_tile + i) // tiles_per_seq, 0, 0)),
                  _resident((1, d))],
        out_specs=here,
        out_shape=jax.ShapeDtypeStruct((t, d), F32),
        input_output_aliases={0: 0},
        compiler_params=_params("parallel"),
        name="moe_combine",
    )(x_buf, y0, y1, route, mod_l, final_g.reshape(1, d))


def _rank_kernel(e_ref, rank_ref, cnt_ref, carry_ref):
    @pl.when(pl.program_id(0) == 0)
    def _():
        carry_ref[...] = jnp.zeros_like(carry_ref)

    expert_id = lax.broadcasted_iota(jnp.int32, (N_EXPERTS, RANK_TILE), 0)
    ri = lax.broadcasted_iota(jnp.int32, (RANK_TILE, RANK_TILE), 0)
    ci = lax.broadcasted_iota(jnp.int32, (RANK_TILE, RANK_TILE), 1)
    upper = (ri <= ci).astype(BF16)
    carry = carry_ref[...]
    for r in range(e_ref.shape[0]):
        e = e_ref[r:r + 1, :]
        onehot = jnp.where(e == expert_id, 1.0, 0.0)
        prefix = _dot(onehot.astype(BF16), upper)
        rank = jnp.sum(onehot * (prefix + carry), axis=0, keepdims=True) - 1.0
        rank_ref[r:r + 1, :] = rank.astype(jnp.int32)
        carry = carry + jnp.sum(onehot, axis=1, keepdims=True)
    carry_ref[...] = carry
    cnt_ref[...] = jnp.broadcast_to(carry, cnt_ref.shape).astype(jnp.int32)


def _rank(e):
    a = e.shape[0]
    n_tiles = a // RANK_TILE
    rows = _split_count(n_tiles, RANK_ROWS)
    tile = pl.BlockSpec((None, rows, RANK_TILE), lambda i: (i, 0, 0))
    rank, counts = pl.pallas_call(
        _rank_kernel,
        grid=(n_tiles // rows,),
        in_specs=[tile],
        out_specs=[tile, pl.BlockSpec((N_EXPERTS, LANES), lambda i: (0, 0))],
        out_shape=[jax.ShapeDtypeStruct((n_tiles // rows, rows, RANK_TILE), jnp.int32),
                   jax.ShapeDtypeStruct((N_EXPERTS, LANES), jnp.int32)],
        scratch_shapes=[pltpu.VMEM((N_EXPERTS, 1), F32)],
        compiler_params=_params("arbitrary"),
        name="moe_rank",
    )(e.reshape(n_tiles // rows, rows, RANK_TILE))
    return rank.reshape(a), counts[:, 0]


def _dispatch_plan(e, t):
    a = t * MOE_TOP_K
    rank, counts = _rank(e)
    padded = (counts + MOE_BLOCK - 1) // MOE_BLOCK * MOE_BLOCK
    end_pad = jnp.cumsum(padded)
    start_pad = end_pad - padded
    dest = (start_pad[e] + rank).astype(jnp.int32)
    n_blocks = (a + N_EXPERTS * (MOE_BLOCK - 1) + MOE_BLOCK - 1) // MOE_BLOCK
    token = jnp.tile(jnp.arange(t, dtype=jnp.int32), MOE_TOP_K)
    filler = jnp.arange(n_blocks * MOE_BLOCK, dtype=jnp.int32) % t
    row_tok = filler + jnp.zeros_like(filler).at[dest].add(token - dest % t, unique_indices=True,
                                                           mode="promise_in_bounds")
    block_start = jnp.arange(n_blocks, dtype=jnp.int32) * MOE_BLOCK
    block_e = jnp.sum((end_pad[None, :] <= block_start[:, None]).astype(jnp.int32), axis=1)
    block_e = jnp.minimum(block_e, N_EXPERTS - 1).astype(jnp.int32)
    n_used = (end_pad[-1] // MOE_BLOCK).astype(jnp.int32).reshape(1)
    return dest.reshape(MOE_TOP_K, t), row_tok, block_e, n_used


def _split_count(n, want):
    return max(k for k in range(1, want + 1) if n % k == 0)


def _moe(x1, h2, logits, mod_l, wg, wu, wd, layer, final_g, seq_len):
    t = x1.shape[0]
    route, route_t = _routing(logits)
    e = jnp.concatenate([route_t[:, 0, :].reshape(t), route_t[:, 1, :].reshape(t)]).astype(jnp.int32)
    dest, row_tok, block_e, n_used = _dispatch_plan(e, t)
    total_blocks = row_tok.shape[0] // MOE_BLOCK
    n_slices = _split_count(total_blocks, MOE_SLICES)
    per = total_blocks // n_slices
    ys = None
    for c in range(n_slices):
        rows = row_tok[c * per * MOE_BLOCK:(c + 1) * per * MOE_BLOCK]
        ys = _experts(h2[rows], block_e, n_used, wg, wu, wd, layer, ys, c * per, total_blocks)
    final = layer == wg.shape[0] - 1
    tm = min(WIDE_TILE, seq_len)
    n_tiles = x1.shape[0] // tm
    n_slices = _split_count(n_tiles, MOE_SLICES)
    per = n_tiles // n_slices
    x = x1
    for c in range(n_slices):
        d_c = dest[:, c * per * tm:(c + 1) * per * tm]
        x = _combine(x, ys[d_c[0]], ys[d_c[1]], route, mod_l, final_g, final, seq_len, c * per)
    return x


def _pad_cols(w, n):
    return jnp.pad(w, ((0, 0), (0, n - w.shape[1])))


def _pad_rows(w, n):
    return jnp.pad(w, ((0, n - w.shape[0]), (0, 0)))


def _rwkv_params(j, mu, wr, wk, wv, w0, w1, w2, a0, a1, a2, v0, v1, v2, g1, g2, kk, ka, rk,
                 lnw, lnb):
    d = D_MODEL
    p = {
        "mu": mu[j],
        "wr": wr[j].astype(BF16), "wk": wk[j].astype(BF16), "wv": wv[j].astype(BF16),
        "wl1": jnp.concatenate([w1[j, 0], w1[j, 1]], axis=1).astype(BF16),
        "wl2": _block_diag2(w2[j, 0], w2[j, 1]).astype(BF16),
        "al1": jnp.concatenate([a1[j, 0], a1[j, 1]], axis=1).astype(BF16),
        "al2": _block_diag2(a2[j, 0], a2[j, 1]).astype(BF16),
        "g1": g1[j].astype(BF16), "g2": g2[j].astype(BF16),
        "w0": w0[j].reshape(1, 2 * d), "a0": a0[j].reshape(1, 2 * d),
        "kk": kk[j].reshape(1, d), "ka": ka[j].reshape(1, d), "rk": rk[j].reshape(1, d),
        "lnw": lnw[j].reshape(1, d), "lnb": lnb[j].reshape(1, d),
    }
    if j > 0:
        p["v0"] = v0[j - 1].reshape(1, d)
        p["v1"] = _pad_cols(v1[j - 1], LANES).astype(BF16)
        p["v2"] = _pad_rows(v2[j - 1], LANES).astype(BF16)
    return p


def kernel(x_prompt, x_sample, c_prompt, c_sample, ada_w, ada_b, norm_g, final_g, rwkv_mu, rwkv_wr, rwkv_wk, rwkv_wv, rwkv_wo, rwkv_w0, rwkv_w1, rwkv_w2, rwkv_a0, rwkv_a1, rwkv_a2, rwkv_v0, rwkv_v1, rwkv_v2, rwkv_g1, rwkv_g2, rwkv_kk, rwkv_ka, rwkv_rk, rwkv_lnw, rwkv_lnb, sgu_win, sgu_bin, sgu_ng, sgu_nb, sgu_ws, sgu_bs, sgu_wout, sgu_bout, diff_wqkv, diff_wo, diff_lq1, diff_lk1, diff_lq2, diff_lk2, diff_subln, rel_bias, moe_wrg, moe_brg, moe_wre, moe_bre, moe_wg, moe_wu, moe_wd):
    d = D_MODEL
    nb_p, seq_len, _ = x_prompt.shape
    assert x_sample.shape[1] == seq_len
    t_p = nb_p * seq_len
    x = jnp.concatenate([x_prompt.reshape(-1, d), x_sample.reshape(-1, d)], axis=0)
    c = jnp.concatenate([c_prompt, c_sample], axis=0)
    mod = _modulation(c, ada_w, ada_b)
    zero_bias = jnp.zeros((1, d), F32)
    v_first = None
    for i in range(DEPTH):
        j = i // N_MIXERS
        mod_l = mod[i]
        if i % N_MIXERS == 0:
            p = _rwkv_params(j, rwkv_mu, rwkv_wr, rwkv_wk, rwkv_wv, rwkv_w0, rwkv_w1, rwkv_w2,
                             rwkv_a0, rwkv_a1, rwkv_a2, rwkv_v0, rwkv_v1, rwkv_v2, rwkv_g1, rwkv_g2,
                             rwkv_kk, rwkv_ka, rwkv_rk, rwkv_lnw, rwkv_lnb)
            r, k, v, g, kk, wlf, wlb, af, ab, bonus = _rwkv_pre(x, mod_l, norm_g[i, 0], p, v_first,
                                                                seq_len)
            if v_first is None:
                v_first = v
            y = _wkv(r, k, v, kk, wlf, wlb, af, ab, p["ka"], seq_len)
            pre = _rwkv_mid(y, v, g, bonus, p)
            w_out, b_out = rwkv_wo[j].astype(BF16), zero_bias
        elif i % N_MIXERS == 1:
            p = {"win": sgu_win[j].astype(BF16), "bin": sgu_bin[j].reshape(1, -1),
                 "ng": sgu_ng[j].reshape(1, -1), "nb": sgu_nb[j].reshape(1, -1),
                 "ws": sgu_ws[j].astype(BF16),
                 "bs": jnp.repeat(jnp.transpose(sgu_bs[j]), SGU_GC, axis=1)}
            pre = _sgu(x, mod_l, norm_g[i, 0], p, seq_len)
            w_out, b_out = sgu_wout[j].astype(BF16), sgu_bout[j].reshape(1, d)
        else:
            lambda_init = 0.8 - 0.6 * math.exp(-0.3 * i)
            lam = (jnp.exp(jnp.sum(diff_lq1[j] * diff_lk1[j])) - jnp.exp(jnp.sum(diff_lq2[j] * diff_lk2[j]))
                   + lambda_init)
            q, k, v = _qkv(x, mod_l, norm_g[i, 0], diff_wqkv[j].astype(BF16), seq_len)
            windows = _bias_windows(rel_bias, seq_len, min(ATTN_TQ, seq_len))
            pre = _diff_attention(q, k, v, windows, lam, diff_subln[j], lambda_init, seq_len)
            w_out, b_out = diff_wo[j].astype(BF16), zero_bias
        w_router = _pad_cols(jnp.concatenate([moe_wrg[i], moe_wre[i]], axis=1), ROUTER_COLS)
        b_router = _pad_cols(jnp.concatenate([moe_brg[i], moe_bre[i]])[None, :], ROUTER_COLS)
        x1, h2, logits = _post(pre, w_out, b_out, x, mod_l, norm_g[i, 1], w_router, b_router, seq_len)
        x = _moe(x1, h2, logits, mod_l, moe_wg, moe_wu, moe_wd, i, final_g, seq_len)
    return (x[:t_p].reshape(x_prompt.shape), x[t_p:].reshape(x_sample.shape))
```
